```python
import jax, jax.numpy as jnp
from jax import lax
import numpy as np

D_MODEL = 2048
BATCH = 8
SEQ = 2048
DEPTH = 4

HEAD_DIM = 128
DIL_GROUPS = ((128, 1), (512, 4), (2048, 16))
HEADS_PER_DIL = 4
N_DIL_HEADS = HEADS_PER_DIL * len(DIL_GROUPS)
N_FOX_HEADS = 4
N_HEADS = N_DIL_HEADS + N_FOX_HEADS
ATTN_WIDTH = N_HEADS * HEAD_DIM
BRANCH_A_WIDTH = HEADS_PER_DIL * HEAD_DIM
BRANCH_B_WIDTH = N_FOX_HEADS * HEAD_DIM
IN_COLS = 3 * ATTN_WIDTH + N_FOX_HEADS
D_FF = 4 * D_MODEL
PLE_DIM = 256
ROPE_THETA = 500000.0
ROPE_DIM = HEAD_DIM // 4
BLOCK = 128
NORM_EPS = 1e-6

kernel_name = "hybrid_dilated_fox_gated_block"


def rms_norm(x, g):
    xf = x.astype(jnp.float32)
    y = xf * lax.rsqrt(jnp.mean(xf * xf, axis=-1, keepdims=True) + NORM_EPS)
    return (y * g.astype(jnp.float32)).astype(x.dtype)


def partial_rope(x):
    S = x.shape[1]
    half = ROPE_DIM // 2
    inv = ROPE_THETA ** (-jnp.arange(half, dtype=jnp.float32) / half)
    ang = jnp.arange(S, dtype=jnp.float32)[:, None] * inv[None, :]
    cos = jnp.cos(ang)[None, :, None, :]
    sin = jnp.sin(ang)[None, :, None, :]
    xr = x[..., :ROPE_DIM].astype(jnp.float32)
    x1, x2 = xr[..., :half], xr[..., half:]
    rot = jnp.concatenate([x1 * cos - x2 * sin, x2 * cos + x1 * sin], axis=-1).astype(x.dtype)
    return jnp.concatenate([rot, x[..., ROPE_DIM:]], axis=-1)


def dilated_group_attention(q, k, v, window, dilation):
    B, S, H, Dh = q.shape
    span = window // dilation
    L = S // dilation
    nb = -(-L // BLOCK)
    Lp = nb * BLOCK

    def to_blocks(t):
        t = t.reshape(B, L, dilation, H, Dh).transpose(0, 2, 1, 3, 4)
        t = jnp.pad(t, ((0, 0), (0, 0), (0, Lp - L), (0, 0), (0, 0)))
        return t.reshape(B, dilation, nb, BLOCK, H, Dh)

    def with_prev(t):
        prev = jnp.pad(t[:, :, :-1], ((0, 0), (0, 0), (1, 0), (0, 0), (0, 0), (0, 0)))
        return jnp.concatenate([prev, t], axis=3)

    qb = to_blocks(q)
    kb = with_prev(to_blocks(k))
    vb = with_prev(to_blocks(v))
    s = jnp.einsum("brnqhd,brnkhd->brnhqk", qb, kb).astype(jnp.float32) * (Dh ** -0.5)
    qi = jnp.arange(BLOCK)[:, None]
    ki = jnp.arange(2 * BLOCK)[None, :]
    dist = BLOCK + qi - ki
    band = (dist >= 0) & (dist <= span)
    key_exists = (jnp.arange(nb) > 0)[:, None, None] | (ki >= BLOCK)[None]
    mask = (band[None] & key_exists)[:, None]
    s = jnp.where(mask, s, -jnp.inf)
    lse = jax.nn.logsumexp(s, axis=-1)
    prob = jnp.exp(s - lse[..., None]).astype(v.dtype)
    o = jnp.einsum("brnhqk,brnkhd->brnqhd", prob, vb)
    o = o.reshape(B, dilation, Lp, H, Dh)[:, :, :L].transpose(0, 2, 1, 3, 4).reshape(B, S, H, Dh)
    lse = lse.transpose(0, 1, 2, 4, 3).reshape(B, dilation, Lp, H)[:, :, :L]
    lse = lse.transpose(0, 2, 1, 3).reshape(B, S, H)
    return o, lse


def dilated_mixture(q, k, v):
    outs, lses = [], []
    for g, (window, dilation) in enumerate(DIL_GROUPS):
        sl = slice(g * HEADS_PER_DIL, (g + 1) * HEADS_PER_DIL)
        o, l = dilated_group_attention(q[:, :, sl], k[:, :, sl], v[:, :, sl], window, dilation)
        outs.append(o)
        lses.append(l)
    o = jnp.stack(outs, axis=0)
    w = jax.nn.softmax(jnp.stack(lses, axis=0), axis=0)
    return jnp.sum(w[..., None].astype(o.dtype) * o, axis=0)


def forgetting_attention(q, k, v, f_logit):
    B, S, H, Dh = q.shape
    nb = S // BLOCK
    c = jnp.cumsum(jax.nn.log_sigmoid(f_logit.astype(jnp.float32)), axis=1)
    c_keys = c.transpose(0, 2, 1)[:, :, None, :]
    qb = q.reshape(B, nb, BLOCK, H, Dh).transpose(1, 0, 2, 3, 4)
    cb = c.reshape(B, nb, BLOCK, H).transpose(1, 0, 3, 2)
    kpos = jnp.arange(S)
    scale = Dh ** -0.5

    def one_block(args):
        j, qj, cj = args
        s = jnp.einsum("bqhd,bkhd->bhqk", qj, k).astype(jnp.float32) * scale
        s = s + cj[..., None] - c_keys
        qpos = j * BLOCK + jnp.arange(BLOCK)
        s = jnp.where(kpos[None, :] <= qpos[:, None], s, -jnp.inf)
        prob = jax.nn.softmax(s, axis=-1).astype(v.dtype)
        return jnp.einsum("bhqk,bkhd->bqhd", prob, v)

    out = lax.map(one_block, (jnp.arange(nb), qb, cb))
    return out.transpose(1, 0, 2, 3, 4).reshape(B, S, H, Dh)


def hybrid_layer(h, p_i, g_mix, w_in, b_f, w_gate, b_gate, w_br_a, w_br_b, w_o,
                 g_mlp, w_up, w_down, g_ple, w_ple, w_ple_gate):
    B, S, _ = h.shape
    u = rms_norm(h, g_mix)
    z = u @ w_in
    q = z[..., :ATTN_WIDTH].reshape(B, S, N_HEADS, HEAD_DIM)
    k = z[..., ATTN_WIDTH:2 * ATTN_WIDTH].reshape(B, S, N_HEADS, HEAD_DIM)
    v = z[..., 2 * ATTN_WIDTH:3 * ATTN_WIDTH].reshape(B, S, N_HEADS, HEAD_DIM)
    f_logit = z[..., 3 * ATTN_WIDTH:] + b_f

    ya = dilated_mixture(partial_rope(q[:, :, :N_DIL_HEADS]), partial_rope(k[:, :, :N_DIL_HEADS]),
                         v[:, :, :N_DIL_HEADS])
    ya = ya.reshape(B, S, BRANCH_A_WIDTH) @ w_br_a
    yb = forgetting_attention(q[:, :, N_DIL_HEADS:], k[:, :, N_DIL_HEADS:], v[:, :, N_DIL_HEADS:], f_logit)
    yb = yb.reshape(B, S, BRANCH_B_WIDTH) @ w_br_b

    gates = jax.nn.sigmoid(u @ w_gate + b_gate)
    merged = gates[..., :D_MODEL] * ya + gates[..., D_MODEL:] * yb
    h = h + merged @ w_o

    m = rms_norm(h, g_mlp)
    h = h + jnp.square(jax.nn.relu(m @ w_up)) @ w_down

    ple_gate = jax.nn.sigmoid(rms_norm(h, g_ple) @ w_ple_gate)
    h = h + ple_gate * (p_i @ w_ple)
    return h


def _fwd_setup_inputs(seed: int = 0) -> dict:
    key = jax.random.key(seed)
    ks = jax.random.split(key, 20)
    f32 = jnp.float32

    def w(k, shape, fan_in):
        return jax.random.normal(k, shape, f32) * (fan_in ** -0.5)

    def gain(k, shape):
        return 1.0 + 0.02 * jax.random.normal(k, shape, f32)

    return {
        "x": jax.random.normal(ks[0], (BATCH, SEQ, D_MODEL), f32),
        "p": jax.random.normal(ks[1], (DEPTH, BATCH, SEQ, PLE_DIM), f32),
        "g_mix": gain(ks[2], (DEPTH, D_MODEL)),
        "w_in": w(ks[3], (DEPTH, D_MODEL, IN_COLS), D_MODEL),
        "b_f": 3.0 + 0.1 * jax.random.normal(ks[4], (DEPTH, N_FOX_HEADS), f32),
        "w_gate": w(ks[5], (DEPTH, D_MODEL, 2 * D_MODEL), D_MODEL),
        "b_gate": 0.1 * jax.random.normal(ks[6], (DEPTH, 2 * D_MODEL), f32),
        "w_br_a": w(ks[7], (DEPTH, BRANCH_A_WIDTH, D_MODEL), BRANCH_A_WIDTH),
        "w_br_b": w(ks[8], (DEPTH, BRANCH_B_WIDTH, D_MODEL), BRANCH_B_WIDTH),
        "w_o": w(ks[9], (DEPTH, D_MODEL, D_MODEL), D_MODEL),
        "g_mlp": gain(ks[10], (DEPTH, D_MODEL)),
        "w_up": w(ks[11], (DEPTH, D_MODEL, D_FF), D_MODEL),
        "w_down": w(ks[12], (DEPTH, D_FF, D_MODEL), D_FF),
        "g_ple": gain(ks[13], (DEPTH, D_MODEL)),
        "w_ple": w(ks[14], (DEPTH, PLE_DIM, D_MODEL), PLE_DIM),
        "w_ple_gate": w(ks[15], (DEPTH, D_MODEL, D_MODEL), D_MODEL),
        "g_final": gain(ks[16], (D_MODEL,)),
    }


def _fwd_reference(x, p, g_mix, w_in, b_f, w_gate, b_gate, w_br_a, w_br_b, w_o,
              g_mlp, w_up, w_down, g_ple, w_ple, w_ple_gate, g_final):
    h = x
    for i in range(DEPTH):
        h = hybrid_layer(h, p[i], g_mix[i], w_in[i], b_f[i], w_gate[i], b_gate[i], w_br_a[i], w_br_b[i],
                         w_o[i], g_mlp[i], w_up[i], w_down[i], g_ple[i], w_ple[i], w_ple_gate[i])
    return rms_norm(h, g_final)


import jax as _jax
import jax.numpy as _jnp

TWIN_FORMAT = 'train_step'
FWD_PARAMS = ['x', 'p', 'g_mix', 'w_in', 'b_f', 'w_gate', 'b_gate', 'w_br_a', 'w_br_b', 'w_o', 'g_mlp', 'w_up', 'w_down', 'g_ple', 'w_ple', 'w_ple_gate', 'g_final']
TWIN_WEIGHTS = ['g_mix', 'w_in', 'b_f', 'w_gate', 'b_gate', 'w_br_a', 'w_br_b', 'w_o', 'g_mlp', 'w_up', 'w_down', 'g_ple', 'w_ple', 'w_ple_gate', 'g_final']
TWIN_DIFF_INPUT = 'x'
TWIN_INPUTS = ['x', 'p', 'g_mix', 'w_in', 'b_f', 'w_gate', 'b_gate', 'w_br_a', 'w_br_b', 'w_o', 'g_mlp', 'w_up', 'w_down', 'g_ple', 'w_ple', 'w_ple_gate', 'g_final', 'loss_target', 'm_g_mix', 'm_w_in', 'm_b_f', 'm_w_gate', 'm_b_gate', 'm_w_br_a', 'm_w_br_b', 'm_w_o', 'm_g_mlp', 'm_w_up', 'm_w_down', 'm_g_ple', 'm_w_ple', 'm_w_ple_gate', 'm_g_final', 'v_g_mix', 'v_w_in', 'v_b_f', 'v_w_gate', 'v_b_gate', 'v_w_br_a', 'v_w_br_b', 'v_w_o', 'v_g_mlp', 'v_w_up', 'v_w_down', 'v_g_ple', 'v_w_ple', 'v_w_ple_gate', 'v_g_final']
TWIN_OUTPUTS = ['loss', 'grad_x', 'grad_g_mix', 'grad_w_in', 'grad_b_f', 'grad_w_gate', 'grad_b_gate', 'grad_w_br_a', 'grad_w_br_b', 'grad_w_o', 'grad_g_mlp', 'grad_w_up', 'grad_w_down', 'grad_g_ple', 'grad_w_ple', 'grad_w_ple_gate', 'grad_g_final', 'delta_g_mix', 'delta_w_in', 'delta_b_f', 'delta_w_gate', 'delta_b_gate', 'delta_w_br_a', 'delta_w_br_b', 'delta_w_o', 'delta_g_mlp', 'delta_w_up', 'delta_w_down', 'delta_g_ple', 'delta_w_ple', 'delta_w_ple_gate', 'delta_g_final', 'new_m_g_mix', 'new_m_w_in', 'new_m_b_f', 'new_m_w_gate', 'new_m_b_gate', 'new_m_w_br_a', 'new_m_w_br_b', 'new_m_w_o', 'new_m_g_mlp', 'new_m_w_up', 'new_m_w_down', 'new_m_g_ple', 'new_m_w_ple', 'new_m_w_ple_gate', 'new_m_g_final', 'new_v_g_mix', 'new_v_w_in', 'new_v_b_f', 'new_v_w_gate', 'new_v_b_gate', 'new_v_w_br_a', 'new_v_w_br_b', 'new_v_w_o', 'new_v_g_mlp', 'new_v_w_up', 'new_v_w_down', 'new_v_g_ple', 'new_v_w_ple', 'new_v_w_ple_gate', 'new_v_g_final']
TWIN_LEAF_KINDS = {'loss': 'loss', 'grad_x': 'grad_x', 'grad_g_mix': 'grad_w', 'grad_w_in': 'grad_w', 'grad_b_f': 'grad_w', 'grad_w_gate': 'grad_w', 'grad_b_gate': 'grad_w', 'grad_w_br_a': 'grad_w', 'grad_w_br_b': 'grad_w', 'grad_w_o': 'grad_w', 'grad_g_mlp': 'grad_w', 'grad_w_up': 'grad_w', 'grad_w_down': 'grad_w', 'grad_g_ple': 'grad_w', 'grad_w_ple': 'grad_w', 'grad_w_ple_gate': 'grad_w', 'grad_g_final': 'grad_w', 'delta_g_mix': 'delta_w', 'delta_w_in': 'delta_w', 'delta_b_f': 'delta_w', 'delta_w_gate': 'delta_w', 'delta_b_gate': 'delta_w', 'delta_w_br_a': 'delta_w', 'delta_w_br_b': 'delta_w', 'delta_w_o': 'delta_w', 'delta_g_mlp': 'delta_w', 'delta_w_up': 'delta_w', 'delta_w_down': 'delta_w', 'delta_g_ple': 'delta_w', 'delta_w_ple': 'delta_w', 'delta_w_ple_gate': 'delta_w', 'delta_g_final': 'delta_w', 'new_m_g_mix': 'new_m', 'new_m_w_in': 'new_m', 'new_m_b_f': 'new_m', 'new_m_w_gate': 'new_m', 'new_m_b_gate': 'new_m', 'new_m_w_br_a': 'new_m', 'new_m_w_br_b': 'new_m', 'new_m_w_o': 'new_m', 'new_m_g_mlp': 'new_m', 'new_m_w_up': 'new_m', 'new_m_w_down': 'new_m', 'new_m_g_ple': 'new_m', 'new_m_w_ple': 'new_m', 'new_m_w_ple_gate': 'new_m', 'new_m_g_final': 'new_m', 'new_v_g_mix': 'new_v', 'new_v_w_in': 'new_v', 'new_v_b_f': 'new_v', 'new_v_w_gate': 'new_v', 'new_v_b_gate': 'new_v', 'new_v_w_br_a': 'new_v', 'new_v_w_br_b': 'new_v', 'new_v_w_o': 'new_v', 'new_v_g_mlp': 'new_v', 'new_v_w_up': 'new_v', 'new_v_w_down': 'new_v', 'new_v_g_ple': 'new_v', 'new_v_w_ple': 'new_v', 'new_v_w_ple_gate': 'new_v', 'new_v_g_final': 'new_v'}


def _forward(args):
    return _fwd_reference(*[args[k] for k in FWD_PARAMS])


def _output_shape():
    out = _jax.eval_shape(lambda: _forward(_fwd_setup_inputs(0)))
    return out.shape, out.dtype

N_MICROBATCH = 1
ADAM_LR = 0.001
ADAM_B1 = 0.9
ADAM_B2 = 0.999
ADAM_EPS = 1e-08
ADAM_WD = 0.01
ADAM_STEP = 10
PER_EXAMPLE_BATCH_AXIS = {'x': 0, 'p': 1, 'loss_target': 0}
SHARED_INPUTS = []
_WEIGHT_DTYPES = {'g_mix': _jnp.float32, 'w_in': _jnp.float32, 'b_f': _jnp.float32, 'w_gate': _jnp.float32, 'b_gate': _jnp.float32, 'w_br_a': _jnp.float32, 'w_br_b': _jnp.float32, 'w_o': _jnp.float32, 'g_mlp': _jnp.float32, 'w_up': _jnp.float32, 'w_down': _jnp.float32, 'g_ple': _jnp.float32, 'w_ple': _jnp.float32, 'w_ple_gate': _jnp.float32, 'g_final': _jnp.float32}
MOMENT_SCALE = {'g_mix': 1.548904e-02, 'w_in': 8.604801e-03, 'b_f': 2.998523e-01, 'w_gate': 2.944716e-03, 'b_gate': 3.028894e-03, 'w_br_a': 5.977164e-03, 'w_br_b': 9.704437e-03, 'w_o': 1.103679e-02, 'g_mlp': 4.405600e-02, 'w_up': 2.167653e-02, 'w_down': 4.231588e-02, 'g_ple': 6.266008e-03, 'w_ple': 1.613240e-02, 'w_ple_gate': 6.378489e-03, 'g_final': 8.218083e+00}


def _to_microbatches(a, axis):
    t = _jnp.moveaxis(a, axis, 0)
    t = t.reshape((N_MICROBATCH, t.shape[0] // N_MICROBATCH) + t.shape[1:])
    return _jnp.moveaxis(t, 1, axis + 1)


def setup_inputs(seed: int = 0) -> dict:
    inp = _fwd_setup_inputs(seed)
    key = _jax.random.fold_in(_jax.random.key(seed), 7919)
    shape, _ = _output_shape()
    out = dict(inp)
    out["loss_target"] = _jax.random.normal(_jax.random.fold_in(key, 0), shape, _jnp.float32)
    for i, name in enumerate(TWIN_WEIGHTS):
        w = inp[name].astype(_jnp.float32)
        if MOMENT_SCALE is None:
            s = _jnp.sqrt(_jnp.mean(_jnp.square(w)) + 1e-30)
        else:
            s = MOMENT_SCALE[name]
        km, kv = _jax.random.split(_jax.random.fold_in(key, i + 1))
        out[name] = w
        out["m_" + name] = s * _jax.random.normal(km, w.shape, _jnp.float32)
        out["v_" + name] = (s * s) * _jax.random.uniform(kv, w.shape, _jnp.float32, 0.5, 1.5)
    if N_MICROBATCH > 1:
        for name, axis in PER_EXAMPLE_BATCH_AXIS.items():
            out[name] = _to_microbatches(out[name], axis)
    return {'x': out['x'], 'p': out['p'], 'g_mix': out['g_mix'], 'w_in': out['w_in'], 'b_f': out['b_f'], 'w_gate': out['w_gate'], 'b_gate': out['b_gate'], 'w_br_a': out['w_br_a'], 'w_br_b': out['w_br_b'], 'w_o': out['w_o'], 'g_mlp': out['g_mlp'], 'w_up': out['w_up'], 'w_down': out['w_down'], 'g_ple': out['g_ple'], 'w_ple': out['w_ple'], 'w_ple_gate': out['w_ple_gate'], 'g_final': out['g_final'], 'loss_target': out['loss_target'], 'm_g_mix': out['m_g_mix'], 'm_w_in': out['m_w_in'], 'm_b_f': out['m_b_f'], 'm_w_gate': out['m_w_gate'], 'm_b_gate': out['m_b_gate'], 'm_w_br_a': out['m_w_br_a'], 'm_w_br_b': out['m_w_br_b'], 'm_w_o': out['m_w_o'], 'm_g_mlp': out['m_g_mlp'], 'm_w_up': out['m_w_up'], 'm_w_down': out['m_w_down'], 'm_g_ple': out['m_g_ple'], 'm_w_ple': out['m_w_ple'], 'm_w_ple_gate': out['m_w_ple_gate'], 'm_g_final': out['m_g_final'], 'v_g_mix': out['v_g_mix'], 'v_w_in': out['v_w_in'], 'v_b_f': out['v_b_f'], 'v_w_gate': out['v_w_gate'], 'v_b_gate': out['v_b_gate'], 'v_w_br_a': out['v_w_br_a'], 'v_w_br_b': out['v_w_br_b'], 'v_w_o': out['v_w_o'], 'v_g_mlp': out['v_g_mlp'], 'v_w_up': out['v_w_up'], 'v_w_down': out['v_w_down'], 'v_g_ple': out['v_g_ple'], 'v_w_ple': out['v_w_ple'], 'v_w_ple_gate': out['v_w_ple_gate'], 'v_g_final': out['v_g_final']}


def _loss(weights, diff, rest, loss_target):
    with _jax.named_scope("forward"):
        args = {**rest, TWIN_DIFF_INPUT: diff, **{k: w.astype(_WEIGHT_DTYPES[k]) for k, w in weights.items()}}
        y = _forward(args)
    with _jax.named_scope("loss_head"):
        err = _jnp.square(y.astype(_jnp.float32) - loss_target)
        return 0.5 * _jnp.sum(_jnp.mean(err, axis=-1)) if err.ndim else 0.5 * err


def _adamw(w, g, m, v):
    m = ADAM_B1 * m + (1.0 - ADAM_B1) * g
    v = ADAM_B2 * v + (1.0 - ADAM_B2) * _jnp.square(g)
    m_hat = m / (1.0 - ADAM_B1 ** ADAM_STEP)
    v_hat = v / (1.0 - ADAM_B2 ** ADAM_STEP)
    delta = -ADAM_LR * (m_hat / (_jnp.sqrt(v_hat) + ADAM_EPS) + ADAM_WD * w)
    return delta, m, v


def reference(x, p, g_mix, w_in, b_f, w_gate, b_gate, w_br_a, w_br_b, w_o, g_mlp, w_up, w_down, g_ple, w_ple, w_ple_gate, g_final, loss_target, m_g_mix, m_w_in, m_b_f, m_w_gate, m_b_gate, m_w_br_a, m_w_br_b, m_w_o, m_g_mlp, m_w_up, m_w_down, m_g_ple, m_w_ple, m_w_ple_gate, m_g_final, v_g_mix, v_w_in, v_b_f, v_w_gate, v_b_gate, v_w_br_a, v_w_br_b, v_w_o, v_g_mlp, v_w_up, v_w_down, v_g_ple, v_w_ple, v_w_ple_gate, v_g_final):
    given = dict(x=x, p=p, g_mix=g_mix, w_in=w_in, b_f=b_f, w_gate=w_gate, b_gate=b_gate, w_br_a=w_br_a, w_br_b=w_br_b, w_o=w_o, g_mlp=g_mlp, w_up=w_up, w_down=w_down, g_ple=g_ple, w_ple=w_ple, w_ple_gate=w_ple_gate, g_final=g_final, loss_target=loss_target, m_g_mix=m_g_mix, m_w_in=m_w_in, m_b_f=m_b_f, m_w_gate=m_w_gate, m_b_gate=m_b_gate, m_w_br_a=m_w_br_a, m_w_br_b=m_w_br_b, m_w_o=m_w_o, m_g_mlp=m_g_mlp, m_w_up=m_w_up, m_w_down=m_w_down, m_g_ple=m_g_ple, m_w_ple=m_w_ple, m_w_ple_gate=m_w_ple_gate, m_g_final=m_g_final, v_g_mix=v_g_mix, v_w_in=v_w_in, v_b_f=v_b_f, v_w_gate=v_w_gate, v_b_gate=v_b_gate, v_w_br_a=v_w_br_a, v_w_br_b=v_w_br_b, v_w_o=v_w_o, v_g_mlp=v_g_mlp, v_w_up=v_w_up, v_w_down=v_w_down, v_g_ple=v_g_ple, v_w_ple=v_w_ple, v_w_ple_gate=v_w_ple_gate, v_g_final=v_g_final)
    weights = {n: given[n] for n in TWIN_WEIGHTS}
    shared = {n: given[n] for n in SHARED_INPUTS}
    per_example = {n: given[n] for n in ['x', 'p']}
    grad_fn = _jax.value_and_grad(_loss, argnums=(0, 1))

    def one_microbatch(ex, loss_target):
        ex = dict(ex)
        diff = ex.pop(TWIN_DIFF_INPUT)
        return grad_fn(weights, diff, {**shared, **ex}, loss_target)

    if N_MICROBATCH == 1:
        loss, (grad_w, grad_x) = one_microbatch(per_example, given["loss_target"])
    else:
        def body(carry, xs):
            loss_sum, grad_sum = carry
            l_k, (gw_k, gx_k) = one_microbatch(xs[0], xs[1])
            with _jax.named_scope("update"):
                return (loss_sum + l_k, _jax.tree.map(_jnp.add, grad_sum, gw_k)), gx_k

        init = (_jnp.zeros((), _jnp.float32), _jax.tree.map(_jnp.zeros_like, weights))
        (loss, grad_w), grad_x = _jax.lax.scan(body, init, (per_example, given["loss_target"]))
    with _jax.named_scope("update"):
        delta_w, new_m, new_v = {}, {}, {}
        for n in TWIN_WEIGHTS:
            delta_w[n], new_m[n], new_v[n] = _adamw(weights[n], grad_w[n], given["m_" + n], given["v_" + n])
    return (loss, grad_x, *[grad_w[n] for n in TWIN_WEIGHTS], *[delta_w[n] for n in TWIN_WEIGHTS],
            *[new_m[n] for n in TWIN_WEIGHTS], *[new_v[n] for n in TWIN_WEIGHTS])
```

```python
import jax
import jax.numpy as jnp
from jax import lax
from jax.experimental import pallas as pl
from jax.experimental.pallas import tpu as pltpu

F32, BF16 = jnp.float32, jnp.bfloat16
SDS = jax.ShapeDtypeStruct
BS = pl.BlockSpec
MESH = pl.DeviceIdType.MESH
HI = lax.Precision.HIGHEST

HEAD_DIM = 128
N_HEADS = 16
ATTN = N_HEADS * HEAD_DIM
DIL_GROUPS = ((128, 1), (512, 4), (2048, 16))
HEADS_PER_GROUP = 4
FOX_HEAD0 = 12
N_FOX = 4
BRANCH = HEADS_PER_GROUP * HEAD_DIM
BLOCK = 128
ROPE_DIM = HEAD_DIM // 4
ROPE_HALF = ROPE_DIM // 2
ROPE_THETA = 500000.0
NORM_EPS = 1e-6
SCALE = HEAD_DIM ** -0.5
NDEV = 8
LANES = 128

ADAM_LR, ADAM_B1, ADAM_B2, ADAM_EPS, ADAM_WD, ADAM_STEP = 0.001, 0.9, 0.999, 1e-08, 0.01, 10

V7X_VMEM_BYTES = 64 * 1024 * 1024
VMEM_LIMIT = (V7X_VMEM_BYTES * 3) // 4

NN = (((1,), (0,)), ((), ()))
NT = (((1,), (1,)), ((), ()))
TN = (((0,), (0,)), ((), ()))

BIG = ("in", "gate", "bra", "brb", "o", "up", "down", "ple", "pg")
ROW_SHARDED = ("in", "o", "down", "pg")


def _tile(n, pref):
    t = min(n, pref)
    while n % t:
        t -= LANES
    return t


def _cparams(sem):
    return pltpu.CompilerParams(dimension_semantics=sem, vmem_limit_bytes=VMEM_LIMIT)


def _sigmoid(x):
    return 1.0 / (1.0 + jnp.exp(-x))


def _mm(name, grid, a, a_spec, b, b_spec, dims, outs, acc_shape, extras=(), epilogue=None, carry=None):
    nk = grid[2]
    n_ex, n_out = len(extras), len(outs)
    n_carry = 0 if carry is None else 1

    def body(*refs):
        a_ref, b_ref = refs[0], refs[1]
        ex_refs = refs[2:2 + n_ex]
        out_refs = refs[2 + n_ex + n_carry:2 + n_ex + n_carry + n_out]
        acc_ref = refs[-1]
        k = pl.program_id(2)
        prod = lax.dot_general(a_ref[...], b_ref[...], dims, preferred_element_type=F32)

        @pl.when(k == 0)
        def _():
            acc_ref[...] = prod

        @pl.when(k > 0)
        def _():
            acc_ref[...] += prod

        @pl.when(k == nk - 1)
        def _():
            acc = acc_ref[...]
            res = (acc,) if epilogue is None else epilogue(acc, *[e[...] for e in ex_refs])
            for o_ref, r in zip(out_refs, res):
                o_ref[...] = r.astype(o_ref.dtype)

    arrays = [a, b] + [e[0] for e in extras]
    specs = [a_spec, b_spec] + [e[1] for e in extras]
    alias = {}
    if carry is not None:
        arrays.append(carry[0])
        specs.append(BS(memory_space=pl.ANY))
        alias = {len(arrays) - 1: carry[1]}
    return pl.pallas_call(
        body, out_shape=[o[0] for o in outs], grid=grid, in_specs=specs, out_specs=[o[1] for o in outs],
        scratch_shapes=[pltpu.VMEM(acc_shape, F32)], input_output_aliases=alias, name=name,
        compiler_params=_cparams(("parallel", "parallel", "arbitrary")))(*arrays)


def _mm_nn(name, a, w, out_dtypes, *, tm, joff=0, ncols=None, tn=None, extras=(), epilogue=None, out_specs=None,
           out_shapes=None):
    M, K = a.shape
    J, _, n = w.shape
    tn = tn or _tile(n, 512)
    tk = _tile(K, 512)
    per = n // tn
    ncols = ncols or J * n
    grid = (M // tm, ncols // tn, K // tk)
    a_spec = BS((tm, tk), lambda i, j, k: (i, k))
    w_spec = BS((None, tk, tn), lambda i, j, k: ((j + joff) // per, k, (j + joff) % per))
    if out_specs is None:
        out_specs = [BS((tm, tn), lambda i, j, k: (i, j))] * len(out_dtypes)
        out_shapes = [(M, ncols)] * len(out_dtypes)
    outs = [(SDS(s, d), sp) for s, d, sp in zip(out_shapes, out_dtypes, out_specs)]
    return _mm(name, grid, a, a_spec, w, w_spec, NN, outs, (tm, tn), extras, epilogue)


def _mm_nt(name, dy, w, out_dtype, *, tm, coff=0, ncontract=None, tc=None, dy_spec=None, extras=(), epilogue=None):
    J, K, n = w.shape
    M = dy.shape[-2]
    tc = tc or _tile(n, 512)
    tko = _tile(K, 512)
    per = n // tc
    ncontract = ncontract or J * n
    grid = (M // tm, K // tko, ncontract // tc)
    if dy_spec is None:
        dy_spec = BS((tm, tc), lambda i, ko, cc: (i, cc))
    w_spec = BS((None, tko, tc), lambda i, ko, cc: ((cc + coff) // per, ko, (cc + coff) % per))
    outs = [(SDS((M, K), out_dtype), BS((tm, tko), lambda i, ko, cc: (i, ko)))]
    return _mm(name, grid, dy, dy_spec, w, w_spec, NT, outs, (tm, tko), extras, epilogue)[0]


def _mm_tn(name, x, dy, wshape, *, ts, joff=0, ncols=None, tn=None, dy_spec=None, carry=None):
    J, K, n = wshape
    S = x.shape[0]
    tn = tn or _tile(n, 512)
    tkm = _tile(K, 512)
    per = n // tn
    ncols = ncols or J * n
    grid = (K // tkm, ncols // tn, S // ts)
    x_spec = BS((ts, tkm), lambda ki, j, sc: (sc, ki))
    if dy_spec is None:
        dy_spec = BS((ts, tn), lambda ki, j, sc: (sc, j))
    outs = [(SDS(wshape, BF16), BS((None, tkm, tn), lambda ki, j, sc: ((j + joff) // per, ki, (j + joff) % per)))]
    return _mm(name, grid, x, x_spec, dy, dy_spec, TN, outs, (tkm, tn), carry=None if carry is None else (carry, 0))[0]


def _rows(name, body, ins, outs, S, tr, sequential=False):
    def spec(shape, kind):
        if kind == "row":
            return BS((tr, shape[1]), lambda i: (i, 0))
        return BS(tuple(shape), lambda i: (0,) * len(shape))
    return pl.pallas_call(
        body, out_shape=[o[0] for o in outs], grid=(S // tr,),
        in_specs=[spec(a.shape, k) for a, k in ins], out_specs=[spec(o[0].shape, o[1]) for o in outs],
        name=name, compiler_params=_cparams(("arbitrary" if sequential else "parallel",)))(*[a for a, _ in ins])


def _rms_fwd(h, g, tr=256):
    S, D = h.shape

    def body(h_ref, g_ref, u_ref):
        x = h_ref[...]
        r = lax.rsqrt(jnp.mean(x * x, axis=-1, keepdims=True) + NORM_EPS)
        u_ref[...] = ((x * r) * g_ref[...]).astype(BF16)

    return _rows("rms_fwd", body, [(h, "row"), (g, "full")], [(SDS((S, D), BF16), "row")], S, tr)[0]


def _rms_bwd(h, g, dy, dres, tr=256):
    S, D = h.shape

    def body(h_ref, g_ref, dy_ref, dres_ref, dh_ref, dhb_ref, dg_ref):
        x = h_ref[...]
        r = lax.rsqrt(jnp.mean(x * x, axis=-1, keepdims=True) + NORM_EPS)
        xhat = x * r
        dyv = dy_ref[...]
        gy = dyv * g_ref[...]
        dx = r * (gy - xhat * jnp.mean(gy * xhat, axis=-1, keepdims=True))
        dh = dres_ref[...] + dx
        dh_ref[...] = dh
        dhb_ref[...] = dh.astype(BF16)
        part = jnp.sum(dyv * xhat, axis=0, keepdims=True)

        @pl.when(pl.program_id(0) == 0)
        def _():
            dg_ref[...] = part

        @pl.when(pl.program_id(0) > 0)
        def _():
            dg_ref[...] += part

    return _rows("rms_bwd", body, [(h, "row"), (g, "full"), (dy, "row"), (dres, "row")],
                 [(SDS((S, D), F32), "row"), (SDS((S, D), BF16), "row"), (SDS((1, D), F32), "full")], S, tr, True)


def _final(h, g, target, tr=256):
    S, D = h.shape

    def body(h_ref, g_ref, t_ref, dh_ref, dg_ref, loss_ref):
        x = h_ref[...]
        r = lax.rsqrt(jnp.mean(x * x, axis=-1, keepdims=True) + NORM_EPS)
        xhat = x * r
        gv = g_ref[...]
        err = xhat * gv - t_ref[...]
        lpart = 0.5 * jnp.sum(jnp.mean(err * err, axis=-1, keepdims=True), axis=0, keepdims=True)
        dyv = err * (1.0 / D)
        gy = dyv * gv
        dh_ref[...] = r * (gy - xhat * jnp.mean(gy * xhat, axis=-1, keepdims=True))
        part = jnp.sum(dyv * xhat, axis=0, keepdims=True)
        lrow = jnp.broadcast_to(lpart, (1, LANES))

        @pl.when(pl.program_id(0) == 0)
        def _():
            dg_ref[...] = part
            loss_ref[...] = lrow

        @pl.when(pl.program_id(0) > 0)
        def _():
            dg_ref[...] += part
            loss_ref[...] += lrow

    return _rows("final_head", body, [(h, "row"), (g, "full"), (target, "row")],
                 [(SDS((S, D), F32), "row"), (SDS((1, D), F32), "full"), (SDS((1, LANES), F32), "full")], S, tr, True)


def _ple_bwd(dh, pg, pp, tr=256):
    S, D = dh.shape

    def body(dh_ref, pg_ref, pp_ref, dpp_ref, dpgz_ref):
        d, g, q = dh_ref[...], pg_ref[...], pp_ref[...]
        dpp_ref[...] = (d * g).astype(BF16)
        dpgz_ref[...] = (d * q * g * (1.0 - g)).astype(BF16)

    return _rows("ple_bwd", body, [(dh, "row"), (pg, "row"), (pp, "row")],
                 [(SDS((S, D), BF16), "row"), (SDS((S, D), BF16), "row")], S, tr)


def _merge_bwd(dmerged, ya, yb, gates, tr=256):
    S, D = dmerged.shape

    def body(dm_ref, ya_ref, yb_ref, g_ref, dya_ref, dyb_ref, dgz_ref, dbg_ref):
        dm, a, b = dm_ref[...], ya_ref[...], yb_ref[...]
        g1, g2 = g_ref[:, :D], g_ref[:, D:]
        dya_ref[...] = (dm * g1).astype(BF16)
        dyb_ref[...] = (dm * g2).astype(BF16)
        dz1 = dm * a * g1 * (1.0 - g1)
        dz2 = dm * b * g2 * (1.0 - g2)
        dgz_ref[:, :D] = dz1.astype(BF16)
        dgz_ref[:, D:] = dz2.astype(BF16)
        p1 = jnp.sum(dz1, axis=0, keepdims=True)
        p2 = jnp.sum(dz2, axis=0, keepdims=True)

        @pl.when(pl.program_id(0) == 0)
        def _():
            dbg_ref[:, :D] = p1
            dbg_ref[:, D:] = p2

        @pl.when(pl.program_id(0) > 0)
        def _():
            dbg_ref[:, :D] += p1
            dbg_ref[:, D:] += p2

    return _rows("merge_bwd", body, [(dmerged, "row"), (ya, "row"), (yb, "row"), (gates, "row")],
                 [(SDS((S, D), BF16), "row"), (SDS((S, D), BF16), "row"), (SDS((S, 2 * D), BF16), "row"),
                  (SDS((1, 2 * D), F32), "full")], S, tr, True)


def _mix_weights(l0, l1, l2):
    mx = jnp.maximum(jnp.maximum(l0, l1), l2)
    e0, e1, e2 = jnp.exp(l0 - mx), jnp.exp(l1 - mx), jnp.exp(l2 - mx)
    inv = 1.0 / (e0 + e1 + e2)
    return e0 * inv, e1 * inv, e2 * inv


def _mix_fwd(os_, ls_, tr=512):
    S = os_[0].shape[0]

    def body(o0, o1, o2, l0, l1, l2, y_ref):
        w0, w1, w2 = _mix_weights(l0[...], l1[...], l2[...])
        y_ref[...] = (w0 * o0[...] + w1 * o1[...] + w2 * o2[...]).astype(BF16)

    return _rows("mix_fwd", body, [(a, "row") for a in (*os_, *ls_)], [(SDS((S, BRANCH), BF16), "row")], S, tr)[0]


def _mix_bwd(dy, os_, ls_, tr=512):
    S = dy.shape[0]

    def head_sums(t):
        return jnp.concatenate(
            [jnp.broadcast_to(jnp.sum(t[:, j * HEAD_DIM:(j + 1) * HEAD_DIM], axis=-1, keepdims=True), (tr, HEAD_DIM))
             for j in range(HEADS_PER_GROUP)], axis=1)

    def body(dy_ref, o0, o1, o2, l0, l1, l2, d0, d1, d2, p0, p1, p2):
        ws = _mix_weights(l0[...], l1[...], l2[...])
        d = dy_ref[...]
        es = [head_sums(d * o[...]) for o in (o0, o1, o2)]
        ebar = ws[0] * es[0] + ws[1] * es[1] + ws[2] * es[2]
        for w, e, d_ref, p_ref in zip(ws, es, (d0, d1, d2), (p0, p1, p2)):
            d_ref[...] = (w * d).astype(BF16)
            p_ref[...] = w * (e - ebar)

    outs = [(SDS((S, BRANCH), BF16), "row")] * 3 + [(SDS((S, BRANCH), F32), "row")] * 3
    return _rows("mix_bwd", body, [(a, "row") for a in (dy, *os_, *ls_)], outs, S, tr)


def _iota2(shape, dim):
    return lax.broadcasted_iota(jnp.int32, shape, dim)


def _fgate_fwd(zf, bf):
    S = zf.shape[0]
    nblk = S // BLOCK

    def body(zf_ref, bf_ref, cb_ref, ct_ref, flt_ref):
        tri = (_iota2((BLOCK, BLOCK), 0) >= _iota2((BLOCK, BLOCK), 1)).astype(F32)
        sel8 = (_iota2((8, LANES), 0) == _iota2((8, LANES), 1)).astype(F32)
        carry = jnp.zeros((1, LANES), F32)
        for blk in range(nblk):
            rows = pl.ds(blk * BLOCK, BLOCK)
            fl = zf_ref[rows, :] + bf_ref[...]
            ls = jnp.minimum(fl, 0.0) - jnp.log(1.0 + jnp.exp(-jnp.abs(fl)))
            c = jnp.dot(tri, ls, precision=HI, preferred_element_type=F32) + carry
            carry = carry + jnp.sum(ls, axis=0, keepdims=True)
            for hd in range(N_FOX):
                e_h = (_iota2((LANES, LANES), 0) == hd).astype(F32)
                cb_ref[rows, hd * LANES:(hd + 1) * LANES] = jnp.dot(c, e_h, precision=HI, preferred_element_type=F32)
            ct_ref[:, rows] = lax.dot_general(sel8, c, NT, precision=HI, preferred_element_type=F32)
            flt_ref[:, rows] = lax.dot_general(sel8, fl, NT, precision=HI, preferred_element_type=F32)

    vm = BS(memory_space=pltpu.VMEM)
    return pl.pallas_call(
        body, out_shape=[SDS((S, N_FOX * LANES), F32), SDS((8, S), F32), SDS((8, S), F32)],
        in_specs=[vm, vm], out_specs=[vm, vm, vm], name="fgate_fwd",
        compiler_params=pltpu.CompilerParams(vmem_limit_bytes=VMEM_LIMIT))(zf, bf)


def _fgate_bwd(dct, flt):
    S = dct.shape[1]
    nblk = S // BLOCK

    def body(dct_ref, flt_ref, dzf_ref, dbf_ref):
        rowid = _iota2((8, BLOCK), 0)
        tri_r = (_iota2((BLOCK, BLOCK), 0) >= _iota2((BLOCK, BLOCK), 1)).astype(F32)
        carry = jnp.zeros((8, 1), F32)
        dbf = jnp.zeros((8, 1), F32)
        for blk in reversed(range(nblk)):
            cols = pl.ds(blk * BLOCK, BLOCK)
            dc = jnp.where(rowid < N_FOX, dct_ref[:, cols], 0.0)
            dls = jnp.dot(dc, tri_r, precision=HI, preferred_element_type=F32) + carry
            carry = carry + jnp.sum(dc, axis=1, keepdims=True)
            dfl = jnp.where(rowid < N_FOX, dls * _sigmoid(-flt_ref[:, cols]), 0.0)
            dbf = dbf + jnp.sum(dfl, axis=1, keepdims=True)
            sq = jnp.concatenate([dfl, jnp.zeros((BLOCK - 8, BLOCK), F32)], axis=0)
            dzf_ref[cols, :] = sq.T.astype(BF16)
        dbf_ref[...] = jnp.broadcast_to(dbf, (8, LANES))

    vm = BS(memory_space=pltpu.VMEM)
    return pl.pallas_call(
        body, out_shape=[SDS((S, LANES), BF16), SDS((8, LANES), F32)],
        in_specs=[vm, vm], out_specs=[vm, vm], name="fgate_bwd",
        compiler_params=pltpu.CompilerParams(vmem_limit_bytes=VMEM_LIMIT))(dct, flt)


def _rope_tables(S):
    inv = ROPE_THETA ** (-jnp.arange(ROPE_HALF, dtype=F32) / ROPE_HALF)
    ang = jnp.arange(S, dtype=F32)[:, None] * inv[None, :]
    cos, sin = jnp.cos(ang), jnp.sin(ang)
    pad = HEAD_DIM - ROPE_DIM
    tc = jnp.concatenate([cos, cos, jnp.ones((S, pad), F32)], axis=1)
    ta = jnp.concatenate([-sin, jnp.zeros((S, HEAD_DIM - ROPE_HALF), F32)], axis=1)
    tb = jnp.concatenate([jnp.zeros((S, ROPE_HALF), F32), sin, jnp.zeros((S, pad), F32)], axis=1)
    return tc, ta, tb


def _rope(x, c, a, b):
    return x * c + pltpu.roll(x, HEAD_DIM - ROPE_HALF, 1) * a + pltpu.roll(x, ROPE_HALF, 1) * b


def _rope_t(g, c, a, b):
    return g * c - pltpu.roll(g, ROPE_HALF, 1) * b - pltpu.roll(g, HEAD_DIM - ROPE_HALF, 1) * a


def _band_mask(n, w):
    qi = _iota2((BLOCK, w), 0)
    ki = _iota2((BLOCK, w), 1)
    if n == 0:
        return ki <= qi
    dist = BLOCK + qi - ki
    return (dist >= 0) & (dist <= BLOCK)


def _wide(t, w):
    return t if w == LANES else jnp.concatenate([t] * (w // LANES), axis=1)


def _dil_fwd(z3, tabs, grp, S):
    _, d = DIL_GROUPS[grp]
    L = S // d
    nb = L // BLOCK
    zv = z3.reshape(3, L, d * ATTN)
    tv = [t.reshape(L, d * HEAD_DIM) for t in tabs]

    def body(qkv_ref, c_ref, a_ref, b_ref, o_ref, lse_ref, qs, ks):
        c, a, b = c_ref[...], a_ref[...], b_ref[...]
        qs[...] = _rope(qkv_ref[0].astype(F32), c, a, b).astype(BF16)
        ks[...] = _rope(qkv_ref[1].astype(F32), c, a, b).astype(BF16)
        for n in range(nb):
            lo, hi = max(n - 1, 0) * BLOCK, (n + 1) * BLOCK
            w = hi - lo
            rows = pl.ds(n * BLOCK, BLOCK)
            s = lax.dot_general(qs[rows, :], ks[lo:hi, :], NT, preferred_element_type=F32) * SCALE
            s = jnp.where(_band_mask(n, w), s, -jnp.inf)
            m = jnp.max(s, axis=-1, keepdims=True)
            p = jnp.exp(s - m)
            l = jnp.sum(p, axis=-1, keepdims=True)
            o = jnp.dot(p.astype(BF16), qkv_ref[2, lo:hi, :], preferred_element_type=F32)
            o_ref[rows, :] = o / l
            lse_ref[rows, :] = jnp.broadcast_to(m + jnp.log(l), (BLOCK, HEAD_DIM))

    hh0 = grp * HEADS_PER_GROUP
    tspec = BS((L, HEAD_DIM), lambda j, r: (0, r))
    ospec = BS((L, HEAD_DIM), lambda j, r: (0, r * HEADS_PER_GROUP + j))
    o, lse = pl.pallas_call(
        body, out_shape=[SDS((L, d * BRANCH), F32)] * 2, grid=(HEADS_PER_GROUP, d),
        in_specs=[BS((3, L, HEAD_DIM), lambda j, r: (0, 0, r * N_HEADS + hh0 + j)), tspec, tspec, tspec],
        out_specs=[ospec, ospec], scratch_shapes=[pltpu.VMEM((L, HEAD_DIM), BF16)] * 2,
        name="dil_fwd", compiler_params=_cparams(("parallel", "parallel")))(zv, *tv)
    return o.reshape(S, BRANCH), lse.reshape(S, BRANCH)


def _dil_bwd(z3, tabs, grp, S, do, lse, dpl, dz3):
    _, d = DIL_GROUPS[grp]
    L = S // d
    nb = L // BLOCK
    zv = z3.reshape(3, L, d * ATTN)
    tv = [t.reshape(L, d * HEAD_DIM) for t in tabs]
    dov, lsev, dplv = [t.reshape(L, d * BRANCH) for t in (do, lse, dpl)]
    n_carry = 0 if dz3 is None else 1

    def body(*refs):
        qkv_ref, c_ref, a_ref, b_ref, do_ref, lse_ref, dpl_ref = refs[:7]
        dz_ref = refs[7 + n_carry]
        qs, ks, dq_acc, dk_acc, dv_acc = refs[8 + n_carry:]
        c, a, b = c_ref[...], a_ref[...], b_ref[...]
        qs[...] = _rope(qkv_ref[0].astype(F32), c, a, b).astype(BF16)
        ks[...] = _rope(qkv_ref[1].astype(F32), c, a, b).astype(BF16)
        dk_acc[...] = jnp.zeros_like(dk_acc)
        dv_acc[...] = jnp.zeros_like(dv_acc)
        for n in range(nb):
            lo, hi = max(n - 1, 0) * BLOCK, (n + 1) * BLOCK
            w = hi - lo
            rows = pl.ds(n * BLOCK, BLOCK)
            q, k, v = qs[rows, :], ks[lo:hi, :], qkv_ref[2, lo:hi, :]
            s = lax.dot_general(q, k, NT, preferred_element_type=F32) * SCALE
            p = jnp.where(_band_mask(n, w), jnp.exp(s - _wide(lse_ref[rows, :], w)), 0.0)
            dob = do_ref[rows, :]
            dp = lax.dot_general(dob, v, NT, preferred_element_type=F32)
            delta = jnp.sum(p * dp, axis=-1, keepdims=True)
            ds = (p * (dp - delta + _wide(dpl_ref[rows, :], w))).astype(BF16)
            dq_acc[rows, :] = jnp.dot(ds, k, preferred_element_type=F32) * SCALE
            dk_acc[lo:hi, :] += lax.dot_general(ds, q, TN, preferred_element_type=F32) * SCALE
            dv_acc[lo:hi, :] += lax.dot_general(p.astype(BF16), dob, TN, preferred_element_type=F32)
        dz_ref[0] = _rope_t(dq_acc[...], c, a, b).astype(BF16)
        dz_ref[1] = _rope_t(dk_acc[...], c, a, b).astype(BF16)
        dz_ref[2] = dv_acc[...].astype(BF16)

    hh0 = grp * HEADS_PER_GROUP
    tspec = BS((L, HEAD_DIM), lambda j, r: (0, r))
    bspec = BS((L, HEAD_DIM), lambda j, r: (0, r * HEADS_PER_GROUP + j))
    zspec = BS((3, L, HEAD_DIM), lambda j, r: (0, 0, r * N_HEADS + hh0 + j))
    arrays = [zv, *tv, dov, lsev, dplv]
    specs = [zspec, tspec, tspec, tspec, bspec, bspec, bspec]
    alias = {}
    if dz3 is not None:
        arrays.append(dz3.reshape(3, L, d * ATTN))
        specs.append(BS(memory_space=pl.ANY))
        alias = {len(arrays) - 1: 0}
    out = pl.pallas_call(
        body, out_shape=SDS((3, L, d * ATTN), BF16), grid=(HEADS_PER_GROUP, d), in_specs=specs, out_specs=zspec,
        scratch_shapes=[pltpu.VMEM((L, HEAD_DIM), BF16)] * 2 + [pltpu.VMEM((L, HEAD_DIM), F32)] * 3,
        input_output_aliases=alias, name="dil_bwd", compiler_params=_cparams(("parallel", "parallel")))(*arrays)
    return out.reshape(3, S, ATTN)


FOX_TQ = 256


def _fox_scores(q_ref, k_ref, cb_ref, ct_ref, i, ext):
    rows = pl.ds(i * FOX_TQ, FOX_TQ)
    s = lax.dot_general(q_ref[0, rows, :], k_ref[0, 0:ext, :], NT, preferred_element_type=F32) * SCALE
    s = s + _wide(cb_ref[rows, :], ext) - ct_ref[:, 0:ext]
    qpos = i * FOX_TQ + _iota2((FOX_TQ, ext), 0)
    kpos = _iota2((FOX_TQ, ext), 1)
    return jnp.where(kpos <= qpos, s, -jnp.inf)


def _fox_specs(S):
    qs = BS((1, S, HEAD_DIM), lambda h: (0, 0, FOX_HEAD0 + h))
    ks = BS((1, S, HEAD_DIM), lambda h: (1, 0, FOX_HEAD0 + h))
    vs = BS((1, S, HEAD_DIM), lambda h: (2, 0, FOX_HEAD0 + h))
    hb = BS((S, HEAD_DIM), lambda h: (0, h))
    ct = BS((None, 1, S), lambda h: (h, 0, 0))
    return qs, ks, vs, hb, ct


def _fox_fwd(z3, cb, ct, S):
    nq = S // FOX_TQ

    def body(q_ref, k_ref, v_ref, cb_ref, ct_ref, y_ref, lse_ref):
        for i in range(nq):
            ext = (i + 1) * FOX_TQ
            rows = pl.ds(i * FOX_TQ, FOX_TQ)
            s = _fox_scores(q_ref, k_ref, cb_ref, ct_ref, i, ext)
            m = jnp.max(s, axis=-1, keepdims=True)
            p = jnp.exp(s - m)
            l = jnp.sum(p, axis=-1, keepdims=True)
            o = jnp.dot(p.astype(BF16), v_ref[0, 0:ext, :], preferred_element_type=F32)
            y_ref[rows, :] = (o / l).astype(BF16)
            lse_ref[rows, :] = jnp.broadcast_to(m + jnp.log(l), (FOX_TQ, HEAD_DIM))

    qs, ks, vs, hb, cts = _fox_specs(S)
    return pl.pallas_call(
        body, out_shape=[SDS((S, BRANCH), BF16), SDS((S, BRANCH), F32)], grid=(N_FOX,),
        in_specs=[qs, ks, vs, hb, cts], out_specs=[hb, hb], name="fox_fwd",
        compiler_params=_cparams(("parallel",)))(z3, z3, z3, cb, ct.reshape(8, 1, S))


def _fox_bwd(z3, cb, ct, S, dy, lse, dz3):
    nq = S // FOX_TQ

    def body(q_ref, k_ref, v_ref, cb_ref, ct_ref, dy_ref, lse_ref, _, dz_ref, dct_ref, dk_acc, dv_acc):
        dk_acc[...] = jnp.zeros_like(dk_acc)
        dv_acc[...] = jnp.zeros_like(dv_acc)
        dct_ref[...] = jnp.zeros_like(dct_ref)
        for i in range(nq):
            ext = (i + 1) * FOX_TQ
            rows = pl.ds(i * FOX_TQ, FOX_TQ)
            s = _fox_scores(q_ref, k_ref, cb_ref, ct_ref, i, ext)
            p = jnp.exp(s - _wide(lse_ref[rows, :], ext))
            dob = dy_ref[rows, :]
            dp = lax.dot_general(dob, v_ref[0, 0:ext, :], NT, preferred_element_type=F32)
            ds = p * (dp - jnp.sum(p * dp, axis=-1, keepdims=True))
            dsb = ds.astype(BF16)
            dz_ref[0, rows, :] = (jnp.dot(dsb, k_ref[0, 0:ext, :], preferred_element_type=F32) * SCALE).astype(BF16)
            dk_acc[0:ext, :] += lax.dot_general(dsb, q_ref[0, rows, :], TN, preferred_element_type=F32) * SCALE
            dv_acc[0:ext, :] += lax.dot_general(p.astype(BF16), dob, TN, preferred_element_type=F32)
            dct_ref[:, 0:ext] -= jnp.sum(ds, axis=0, keepdims=True)
        dz_ref[1] = dk_acc[...].astype(BF16)
        dz_ref[2] = dv_acc[...].astype(BF16)

    qs, ks, vs, hb, cts = _fox_specs(S)
    zspec = BS((3, S, HEAD_DIM), lambda h: (0, 0, FOX_HEAD0 + h))
    dz, dct = pl.pallas_call(
        body, out_shape=[SDS((3, S, ATTN), BF16), SDS((8, 1, S), F32)], grid=(N_FOX,),
        in_specs=[qs, ks, vs, hb, cts, hb, hb, BS(memory_space=pl.ANY)], out_specs=[zspec, cts],
        scratch_shapes=[pltpu.VMEM((S, HEAD_DIM), F32)] * 2, input_output_aliases={7: 0}, name="fox_bwd",
        compiler_params=_cparams(("parallel",)))(z3, z3, z3, cb, ct.reshape(8, 1, S), dy, lse, dz3)
    return dz, dct.reshape(8, S)


TM = 1024


def _layer_fwd(h, p_bf, sm, W, tabs):
    S, D = h.shape
    qkv_cols = 3 * ATTN
    u = _rms_fwd(h, sm["g_mix"])
    tn = 512
    per = ATTN // tn
    z3 = _mm_nn("mm_qkv", u, W["in"], [BF16], tm=TM, ncols=qkv_cols, tn=tn, out_shapes=[(3, S, ATTN)],
                out_specs=[BS((None, TM, tn), lambda i, j, k: (j // per, i, j % per))])[0]
    zf = _mm_nn("mm_f", u, W["in"], [F32], tm=TM, joff=qkv_cols // LANES, ncols=LANES, tn=LANES)[0]
    cb, ct, flt = _fgate_fwd(zf, sm["b_f"])
    os_, ls_ = [], []
    for grp in range(len(DIL_GROUPS)):
        o, l = _dil_fwd(z3, tabs, grp, S)
        os_.append(o)
        ls_.append(l)
    ya_h = _mix_fwd(os_, ls_)
    yb_h, lse_fox = _fox_fwd(z3, cb, ct, S)

    tg = _tile(W["gate"].shape[2], 512)
    gates = _mm_nn("mm_gate", u, W["gate"], [F32], tm=TM,
                   extras=[(sm["b_gate"], BS((1, tg), lambda i, j, k: (0, j)))],
                   epilogue=lambda acc, bias: (_sigmoid(acc + bias),))[0]
    ya = _mm_nn("mm_bra", ya_h, W["bra"], [F32], tm=TM)[0]
    tb = _tile(W["brb"].shape[2], 512)
    g2off = D // tb
    yb, merged = _mm_nn(
        "mm_brb", yb_h, W["brb"], [F32, BF16], tm=TM,
        extras=[(ya, BS((TM, tb), lambda i, j, k: (i, j))), (gates, BS((TM, tb), lambda i, j, k: (i, j))),
                (gates, BS((TM, tb), lambda i, j, k: (i, j + g2off)))],
        epilogue=lambda acc, a, g1, g2: (acc, g1 * a + g2 * acc))
    to = _tile(D, 512)
    res_spec = BS((TM, to), lambda i, j, k: (i, j))
    h1 = _mm_nn("mm_o", merged, W["o"], [F32], tm=TM, extras=[(h, res_spec)], epilogue=lambda acc, r: (r + acc,))[0]

    m = _rms_fwd(h1, sm["g_mlp"])

    def up_epi(acc):
        r = jnp.maximum(acc, 0.0)
        return r, r * r

    relu_a, act = _mm_nn("mm_up", m, W["up"], [BF16, BF16], tm=TM, epilogue=up_epi)
    h2 = _mm_nn("mm_down", act, W["down"], [F32], tm=TM, extras=[(h1, res_spec)], epilogue=lambda acc, r: (r + acc,))[0]

    nrm = _rms_fwd(h2, sm["g_ple"])
    pp = _mm_nn("mm_ple", p_bf, W["ple"], [F32], tm=TM)[0]

    def pg_epi(acc, r, q):
        g = _sigmoid(acc)
        return g, r + g * q

    pg, h3 = _mm_nn("mm_pg", nrm, W["pg"], [F32, F32], tm=TM, extras=[(h2, res_spec), (pp, res_spec)], epilogue=pg_epi)
    saved = dict(h=h, u=u, z3=z3, cb=cb, ct=ct, flt=flt, os=os_, ls=ls_, ya_h=ya_h, yb_h=yb_h, lse_fox=lse_fox,
                 gates=gates, ya=ya, yb=yb, merged=merged, h1=h1, m=m, relu_a=relu_a, act=act, h2=h2, nrm=nrm,
                 pg=pg, pp=pp, p_bf=p_bf)
    return h3, saved


def _layer_bwd(dh3, sv, sm, W, tabs):
    S, D = dh3.shape
    dW, dsm = {}, {}

    dpp, dpgz = _ple_bwd(dh3, sv["pg"], sv["pp"])
    dW["ple"] = _mm_tn("tn_ple", sv["p_bf"], dpp, W["ple"].shape, ts=TM)
    dW["pg"] = _mm_tn("tn_pg", sv["nrm"], dpgz, W["pg"].shape, ts=TM)
    dn = _mm_nt("nt_pg", dpgz, W["pg"], F32, tm=TM)
    dh2, dh2b, dsm["g_ple"] = _rms_bwd(sv["h2"], sm["g_ple"], dn, dh3)

    ta = _tile(W["up"].shape[1], 512)
    tff = _tile(W["down"].shape[1], 512)
    da = _mm_nt("nt_down", dh2b, W["down"], BF16, tm=TM,
                extras=[(sv["relu_a"], BS((TM, tff), lambda i, ko, cc: (i, ko)))],
                epilogue=lambda acc, r: (acc * (2.0 * r.astype(F32)),))
    dW["down"] = _mm_tn("tn_down", sv["act"], dh2b, W["down"].shape, ts=TM)
    dm = _mm_nt("nt_up", da, W["up"], F32, tm=TM)
    dW["up"] = _mm_tn("tn_up", sv["m"], da, W["up"].shape, ts=TM)
    dh1, dh1b, dsm["g_mlp"] = _rms_bwd(sv["h1"], sm["g_mlp"], dm, dh2)

    dmerged = _mm_nt("nt_o", dh1b, W["o"], F32, tm=TM)
    dW["o"] = _mm_tn("tn_o", sv["merged"], dh1b, W["o"].shape, ts=TM)
    dya, dyb, dgz, dsm["b_gate"] = _merge_bwd(dmerged, sv["ya"], sv["yb"], sv["gates"])
    dya_h = _mm_nt("nt_bra", dya, W["bra"], F32, tm=TM)
    dyb_h = _mm_nt("nt_brb", dyb, W["brb"], BF16, tm=TM)
    dW["bra"] = _mm_tn("tn_bra", sv["ya_h"], dya, W["bra"].shape, ts=TM)
    dW["brb"] = _mm_tn("tn_brb", sv["yb_h"], dyb, W["brb"].shape, ts=TM)
    dW["gate"] = _mm_tn("tn_gate", sv["u"], dgz, W["gate"].shape, ts=TM)
    du = _mm_nt("nt_gate", dgz, W["gate"], F32, tm=TM)

    dos_, dpls_ = [], []
    mixed = _mix_bwd(dya_h, sv["os"], sv["ls"])
    dos_, dpls_ = mixed[:3], mixed[3:]
    dz3 = None
    for grp in range(len(DIL_GROUPS)):
        dz3 = _dil_bwd(sv["z3"], tabs, grp, S, dos_[grp], sv["ls"][grp], dpls_[grp], dz3)
    dz3, dct = _fox_bwd(sv["z3"], sv["cb"], sv["ct"], S, dyb_h, sv["lse_fox"], dz3)
    dzf, dsm["b_f"] = _fgate_bwd(dct, sv["flt"])

    tn = 512
    per = ATTN // tn
    qkv_cols = 3 * ATTN
    dwin = _mm_tn("tn_qkv", sv["u"], dz3, W["in"].shape, ts=TM, ncols=qkv_cols, tn=tn,
                  dy_spec=BS((None, TM, tn), lambda ki, j, sc: (j // per, sc, j % per)))
    dW["in"] = _mm_tn("tn_f", sv["u"], dzf, W["in"].shape, ts=TM, joff=qkv_cols // LANES, ncols=LANES, tn=LANES,
                      carry=dwin)
    tu = _tile(D, 512)
    prev_spec = BS((TM, tu), lambda i, ko, cc: (i, ko))
    du = _mm_nt("nt_qkv", dz3, W["in"], F32, tm=TM, ncontract=qkv_cols, tc=tn,
                dy_spec=BS((None, TM, tn), lambda i, ko, cc: (cc // per, i, cc % per)),
                extras=[(du, prev_spec)], epilogue=lambda acc, prev: (prev + acc,))
    du = _mm_nt("nt_f", dzf, W["in"], F32, tm=TM, coff=qkv_cols // LANES, ncontract=LANES, tc=LANES,
                extras=[(du, prev_spec)], epilogue=lambda acc, prev: (prev + acc,))
    dh0, _, dsm["g_mix"] = _rms_bwd(sv["h"], sm["g_mix"], du, dh1)
    return dh0, dW, dsm


def _shard_tiles(R, Cp):
    tc = Cp if Cp <= 2048 else _tile(Cp, 1024)
    tr = R
    while tr * tc > 256 * 1024 and tr % 16 == 0:
        tr //= 2
    return tr, tc


def _padded_cols(C):
    return -(-C // LANES) * LANES


def _cast_bf16(name, w):
    Lr, R, C = w.shape
    Cp = _padded_cols(C)
    tr, tc = _shard_tiles(R, Cp)

    def body(w_ref, o_ref):
        x = w_ref[...]
        if Cp != C:
            col = pl.program_id(2) * tc + _iota2((tr, tc), 1)
            x = jnp.where(col < C, x, 0.0)
        o_ref[...] = x.astype(BF16)

    spec = BS((None, tr, tc), lambda l, i, j: (l, i, j))
    return pl.pallas_call(
        body, out_shape=SDS((Lr, R, Cp), BF16), grid=(Lr, R // tr, Cp // tc), in_specs=[spec], out_specs=spec,
        name="cast_" + name, compiler_params=_cparams(("parallel", "parallel", "parallel")))(w)


def _adam_math(w, g, m, v):
    m = ADAM_B1 * m + (1.0 - ADAM_B1) * g
    v = ADAM_B2 * v + (1.0 - ADAM_B2) * (g * g)
    m_hat = m / (1.0 - ADAM_B1 ** ADAM_STEP)
    v_hat = v / (1.0 - ADAM_B2 ** ADAM_STEP)
    delta = -ADAM_LR * (m_hat / (jnp.sqrt(v_hat) + ADAM_EPS) + ADAM_WD * w)
    return delta, m, v


def _adamw(name, w, m, v, recv):
    Lr, R, C = w.shape
    Cp = recv.shape[3]
    tr, tc = _shard_tiles(R, Cp)

    def body(w_ref, m_ref, v_ref, r_ref, g_out, d_out, m_out, v_out):
        g = r_ref[0].astype(F32)
        for s in range(1, NDEV):
            g = g + r_ref[s].astype(F32)
        delta, mn, vn = _adam_math(w_ref[...], g, m_ref[...], v_ref[...])
        g_out[...] = g
        d_out[...] = delta
        m_out[...] = mn
        v_out[...] = vn

    spec = BS((None, tr, tc), lambda l, i, j: (l, i, j))
    rspec = BS((None, NDEV, tr, tc), lambda l, i, j: (l, 0, i, j))
    return pl.pallas_call(
        body, out_shape=[SDS(w.shape, F32)] * 4, grid=(Lr, R // tr, Cp // tc), in_specs=[spec, spec, spec, rspec],
        out_specs=[spec] * 4, name="adamw_" + name,
        compiler_params=_cparams(("parallel", "parallel", "parallel")))(w, m, v, recv)


def _adamw_small(w, g, m, v):
    vm = BS(memory_space=pltpu.VMEM)

    def body(w_ref, g_ref, m_ref, v_ref, d_out, m_out, v_out):
        delta, mn, vn = _adam_math(w_ref[...], g_ref[...], m_ref[...], v_ref[...])
        d_out[...] = delta
        m_out[...] = mn
        v_out[...] = vn

    return pl.pallas_call(body, out_shape=[SDS(w.shape, F32)] * 3, in_specs=[vm] * 4, out_specs=[vm] * 3,
                          name="adamw_small")(w, g, m, v)


def _place():
    return lax.axis_index("x"), lax.axis_index("y"), lax.axis_index("c")


def _slot(px, py, pc):
    return 4 * px + 2 * py + pc


def _allgather_layer(ws, layer):
    n = len(ws)

    def body(*refs):
        w_refs, o_refs = refs[:n], refs[n:2 * n]
        send_sems, recv_sems, local_sems = refs[2 * n:]
        x, y, c = _place()
        me, sib = (x, y, c), (x, y, 1 - c)
        chips = [(1 - x, y), (x, 1 - y), (1 - x, 1 - y)]

        def copy(i, k, block, to, own=False):
            dst = o_refs[i].at[_slot(*block)]
            return pltpu.make_async_remote_copy(
                src_ref=w_refs[i].at[layer] if own else dst, dst_ref=dst, send_sem=send_sems.at[i, k],
                recv_sem=recv_sems.at[i, k], device_id=to, device_id_type=MESH)

        mine = [pltpu.make_async_copy(w_refs[i].at[layer], o_refs[i].at[_slot(*me)], local_sems.at[i])
                for i in range(n)]
        for cp in mine:
            cp.start()
        started = []
        for i in range(n):
            first = [copy(i, 0, me, sib, own=True)]
            first += [copy(i, 1 + j, me, (*chip, c), own=True) for j, chip in enumerate(chips)]
            for cp in first:
                cp.start()
            started += first
        for i in range(n):
            for j, chip in enumerate(chips):
                copy(i, 1 + j, (*chip, c), me).wait_recv()
                relay = copy(i, 4 + j, (*chip, c), sib)
                relay.start()
                started.append(relay)
        for i in range(n):
            copy(i, 0, sib, me).wait_recv()
            for j, chip in enumerate(chips):
                copy(i, 4 + j, (*chip, 1 - c), me).wait_recv()
        for cp in started:
            cp.wait_send()
        for cp in mine:
            cp.wait()

    hbm = BS(memory_space=pl.ANY)
    return pl.pallas_call(
        body, out_shape=[SDS((NDEV,) + w.shape[1:], w.dtype) for w in ws], in_specs=[hbm] * n, out_specs=[hbm] * n,
        scratch_shapes=[pltpu.SemaphoreType.DMA((n, 7)), pltpu.SemaphoreType.DMA((n, 7)),
                        pltpu.SemaphoreType.DMA((n,))],
        name="allgather_weights")(*ws)


def _reduce_scatter_layer(dws, recvs, layer):
    n = len(dws)

    def body(*refs):
        dw_refs, r_refs = refs[:n], refs[2 * n:3 * n]
        send_sems, recv_sems, local_sems = refs[3 * n:]
        x, y, c = _place()
        me = _slot(x, y, c)
        mine = [pltpu.make_async_copy(dw_refs[i].at[me], r_refs[i].at[layer, me], local_sems.at[i]) for i in range(n)]
        for cp in mine:
            cp.start()
        started = []
        for i in range(n):
            for rel in range(1, NDEV):
                peer = (1 - x if rel & 4 else x, 1 - y if rel & 2 else y, 1 - c if rel & 1 else c)
                cp = pltpu.make_async_remote_copy(
                    src_ref=dw_refs[i].at[_slot(*peer)], dst_ref=r_refs[i].at[layer, me],
                    send_sem=send_sems.at[i, rel - 1], recv_sem=recv_sems.at[i, rel - 1], device_id=peer,
                    device_id_type=MESH)
                cp.start()
                started.append(cp)
        for cp in started:
            cp.wait()
        for cp in mine:
            cp.wait()

    hbm = BS(memory_space=pl.ANY)
    return pl.pallas_call(
        body, out_shape=[SDS(r.shape, r.dtype) for r in recvs], in_specs=[hbm] * (2 * n), out_specs=[hbm] * n,
        scratch_shapes=[pltpu.SemaphoreType.DMA((n, 7)), pltpu.SemaphoreType.DMA((n, 7)),
                        pltpu.SemaphoreType.DMA((n,))],
        input_output_aliases={n + i: i for i in range(n)}, name="reduce_scatter_grads")(*dws, *recvs)


def _allreduce_small(v):
    R, Wd = v.shape

    def body(v_ref, o_ref, buf, send_sems, recv_sems):
        x, y, c = _place()
        me = _slot(x, y, c)
        buf[me] = v_ref[...]
        started = []
        for rel in range(1, NDEV):
            peer = (1 - x if rel & 4 else x, 1 - y if rel & 2 else y, 1 - c if rel & 1 else c)
            cp = pltpu.make_async_remote_copy(
                src_ref=v_ref, dst_ref=buf.at[me], send_sem=send_sems.at[rel - 1], recv_sem=recv_sems.at[rel - 1],
                device_id=peer, device_id_type=MESH)
            cp.start()
            started.append(cp)
        for cp in started:
            cp.wait()
        acc = buf[0]
        for j in range(1, NDEV):
            acc = acc + buf[j]
        o_ref[...] = acc

    vm = BS(memory_space=pltpu.VMEM)
    return pl.pallas_call(
        body, out_shape=SDS((R, Wd), F32), in_specs=[vm], out_specs=vm,
        scratch_shapes=[pltpu.VMEM((NDEV, R, Wd), F32), pltpu.SemaphoreType.DMA((7,)), pltpu.SemaphoreType.DMA((7,))],
        name="allreduce_small")(v)


def _as_matrix(name, g):
    if name in ROW_SHARDED:
        return g.reshape(1, NDEV * g.shape[1], g.shape[2])
    return g


def _pad8(a):
    return jnp.pad(a, ((0, -a.shape[0] % 8), (0, 0)))


def _pack_small(g_mix, g_mlp, g_ple, b_gate, g_final, b_f, extra_row=None):
    Lr, D = g_mix.shape
    rows = [g_mix, g_mlp, g_ple, b_gate.reshape(2 * Lr, D), g_final.reshape(1, D),
            jnp.pad(b_f.reshape(1, -1), ((0, 0), (0, D - b_f.size)))]
    rows.append(jnp.zeros((1, D), F32) if extra_row is None else extra_row)
    return jnp.concatenate([_pad8(r) for r in rows], axis=0)


def _unpack_small(pk, Lr, D, nf):
    o = 0
    out = []
    for rows, shape in ((Lr, (Lr, D)), (Lr, (Lr, D)), (Lr, (Lr, D)), (2 * Lr, (Lr, 2 * D)), (1, (D,))):
        out.append(pk[o:o + rows].reshape(shape))
        o += rows + (-rows % 8)
    out.append(pk[o, :Lr * nf].reshape(Lr, nf))
    return out, pk[o + 8, 0]


def kernel(x, p, g_mix, w_in, b_f, w_gate, b_gate, w_br_a, w_br_b, w_o, g_mlp, w_up, w_down, g_ple, w_ple, w_ple_gate, g_final, loss_target, m_g_mix, m_w_in, m_b_f, m_w_gate, m_b_gate, m_w_br_a, m_w_br_b, m_w_o, m_g_mlp, m_w_up, m_w_down, m_g_ple, m_w_ple, m_w_ple_gate, m_g_final, v_g_mix, v_w_in, v_b_f, v_w_gate, v_b_gate, v_w_br_a, v_w_br_b, v_w_o, v_g_mlp, v_w_up, v_w_down, v_g_ple, v_w_ple, v_w_ple_gate, v_g_final):
    Lr, D = g_mix.shape
    S = x.shape[1]
    nf = b_f.shape[1]
    big_w = dict(zip(BIG, (w_in, w_gate, w_br_a, w_br_b, w_o, w_up, w_down, w_ple, w_ple_gate)))
    big_m = dict(zip(BIG, (m_w_in, m_w_gate, m_w_br_a, m_w_br_b, m_w_o, m_w_up, m_w_down, m_w_ple, m_w_ple_gate)))
    big_v = dict(zip(BIG, (v_w_in, v_w_gate, v_w_br_a, v_w_br_b, v_w_o, v_w_up, v_w_down, v_w_ple, v_w_ple_gate)))

    wbf = {k: _cast_bf16(k, big_w[k]) for k in BIG}
    tabs = _rope_tables(S)
    bf_pad = jnp.pad(b_f, ((0, 0), (0, LANES - nf)))

    Ws, smalls = [], []
    for l in range(Lr):
        gathered = _allgather_layer([wbf[k] for k in BIG], l)
        Ws.append({k: _as_matrix(k, g) for k, g in zip(BIG, gathered)})
        smalls.append(dict(g_mix=g_mix[l:l + 1], b_f=bf_pad[l:l + 1], b_gate=b_gate[l:l + 1], g_mlp=g_mlp[l:l + 1],
                           g_ple=g_ple[l:l + 1]))

    h = x.reshape(S, D)
    saves = []
    for l in range(Lr):
        h, sv = _layer_fwd(h, p[l, 0].astype(BF16), smalls[l], Ws[l], tabs)
        saves.append(sv)

    dh, dg_final, loss_part = _final(h, g_final.reshape(1, D), loss_target.reshape(S, D))

    recvs = {k: lax.empty((Lr, NDEV) + wbf[k].shape[1:], BF16) for k in BIG}
    dsmalls = [None] * Lr
    for l in reversed(range(Lr)):
        dh, dW, dsmalls[l] = _layer_bwd(dh, saves[l], smalls[l], Ws[l], tabs)
        dws = [dW[k].reshape((NDEV,) + wbf[k].shape[1:]) for k in BIG]
        out = _reduce_scatter_layer(dws, [recvs[k] for k in BIG], l)
        recvs = dict(zip(BIG, out))

    big_out = {k: _adamw(k, big_w[k], big_m[k], big_v[k], recvs[k]) for k in BIG}

    cat = lambda key: jnp.concatenate([_pad8(d[key]) for d in dsmalls], axis=0)[::8]
    dbf = jnp.concatenate([d["b_f"] for d in dsmalls], axis=0).reshape(Lr, 8, LANES)[:, :nf, 0]
    loss_row = jnp.pad(loss_part, ((0, 0), (0, D - LANES)))
    g_pack = _allreduce_small(_pack_small(cat("g_mix"), cat("g_mlp"), cat("g_ple"), cat("b_gate"), dg_final, dbf,
                                          loss_row))
    w_pack = _pack_small(g_mix, g_mlp, g_ple, b_gate, g_final, b_f)
    m_pack = _pack_small(m_g_mix, m_g_mlp, m_g_ple, m_b_gate, m_g_final, m_b_f)
    v_pack = _pack_small(v_g_mix, v_g_mlp, v_g_ple, v_b_gate, v_g_final, v_b_f)
    d_pack, mn_pack, vn_pack = _adamw_small(w_pack, g_pack, m_pack, v_pack)

    small_names = ("g_mix", "g_mlp", "g_ple", "b_gate", "g_final", "b_f")
    sg, loss = _unpack_small(g_pack, Lr, D, nf)
    small_out = {}
    for kind, pk in (("grad", None), ("delta", d_pack), ("m", mn_pack), ("v", vn_pack)):
        vals = sg if pk is None else _unpack_small(pk, Lr, D, nf)[0]
        small_out[kind] = dict(zip(small_names, vals))

    order = ("g_mix", "in", "b_f", "gate", "b_gate", "bra", "brb", "o", "g_mlp", "up", "down", "g_ple", "ple", "pg",
             "g_final")
    outs = [loss, dh.reshape(x.shape)]
    for idx, kind in enumerate(("grad", "delta", "m", "v")):
        for name in order:
            outs.append(big_out[name][idx] if name in big_out else small_out[kind][name])
    return tuple(outs)
```

```python
import jax
import jax.numpy as jnp
from jax import lax
from jax.experimental import pallas as pl
from jax.experimental.pallas import tpu as pltpu

F32, BF16 = jnp.float32, jnp.bfloat16
SDS = jax.ShapeDtypeStruct
BS = pl.BlockSpec
MESH = pl.DeviceIdType.MESH
HI = lax.Precision.HIGHEST

HEAD_DIM = 128
N_HEADS = 16
ATTN = N_HEADS * HEAD_DIM
DIL_GROUPS = ((128, 1), (512, 4), (2048, 16))
HEADS_PER_GROUP = 4
FOX_HEAD0 = 12
N_FOX = 4
BRANCH = HEADS_PER_GROUP * HEAD_DIM
BLOCK = 128
ROPE_DIM = HEAD_DIM // 4
ROPE_HALF = ROPE_DIM // 2
ROPE_THETA = 500000.0
NORM_EPS = 1e-6
SCALE = HEAD_DIM ** -0.5
NDEV = 8
LANES = 128

ADAM_LR, ADAM_B1, ADAM_B2, ADAM_EPS, ADAM_WD, ADAM_STEP = 0.001, 0.9, 0.999, 1e-08, 0.01, 10

V7X_VMEM_BYTES = 64 * 1024 * 1024
VMEM_LIMIT = (V7X_VMEM_BYTES * 3) // 4

NN = (((1,), (0,)), ((), ()))
NT = (((1,), (1,)), ((), ()))
TN = (((0,), (0,)), ((), ()))

BIG = ("in", "gate", "bra", "brb", "o", "up", "down", "ple", "pg")
ROW_SHARDED = ("in", "o", "down", "pg")


def _tile(n, pref):
    t = min(n, pref)
    while n % t:
        t -= LANES
    return t


def _cparams(sem):
    return pltpu.CompilerParams(dimension_semantics=sem, vmem_limit_bytes=VMEM_LIMIT)


def _sigmoid(x):
    return 1.0 / (1.0 + jnp.exp(-x))


def _mm(name, grid, a, a_spec, b, b_spec, dims, outs, acc_shape, extras=(), epilogue=None, carry=None):
    nk = grid[2]
    n_ex, n_out = len(extras), len(outs)
    n_carry = 0 if carry is None else 1

    def body(*refs):
        a_ref, b_ref = refs[0], refs[1]
        ex_refs = refs[2:2 + n_ex]
        out_refs = refs[2 + n_ex + n_carry:2 + n_ex + n_carry + n_out]

        def product():
            return lax.dot_general(a_ref[...], b_ref[...], dims, preferred_element_type=F32)

        def finish(acc):
            res = (acc,) if epilogue is None else epilogue(acc, *[e[...] for e in ex_refs])
            for o_ref, r in zip(out_refs, res):
                o_ref[...] = r.astype(o_ref.dtype)

        if nk == 1:
            finish(product())
            return
        acc_ref = refs[-1]
        k = pl.program_id(2)

        @pl.when(k == 0)
        def _():
            acc_ref[...] = jnp.zeros_like(acc_ref)

        acc_ref[...] += product()

        @pl.when(k == nk - 1)
        def _():
            finish(acc_ref[...])

    arrays = [a, b] + [e[0] for e in extras]
    specs = [a_spec, b_spec] + [e[1] for e in extras]
    alias = {}
    if carry is not None:
        arrays.append(carry[0])
        specs.append(BS(memory_space=pl.ANY))
        alias = {len(arrays) - 1: carry[1]}
    return pl.pallas_call(
        body, out_shape=[o[0] for o in outs], grid=grid, in_specs=specs, out_specs=[o[1] for o in outs],
        scratch_shapes=[pltpu.VMEM(acc_shape, F32)] if nk > 1 else [], input_output_aliases=alias, name=name,
        compiler_params=_cparams(("parallel", "parallel", "arbitrary")))(*arrays)


def _mm_nn(name, a, w, out_dtypes, *, tm, joff=0, ncols=None, tn=None, extras=(), epilogue=None, out_specs=None,
           out_shapes=None):
    M, K = a.shape
    J, _, n = w.shape
    tn = tn or _tile(n, 512)
    tk = _tile(K, 2048)
    per = n // tn
    ncols = ncols or J * n
    grid = (M // tm, ncols // tn, K // tk)
    a_spec = BS((tm, tk), lambda i, j, k: (i, k))
    w_spec = BS((None, tk, tn), lambda i, j, k: ((j + joff) // per, k, (j + joff) % per))
    if out_specs is None:
        out_specs = [BS((tm, tn), lambda i, j, k: (i, j))] * len(out_dtypes)
        out_shapes = [(M, ncols)] * len(out_dtypes)
    outs = [(SDS(s, d), sp) for s, d, sp in zip(out_shapes, out_dtypes, out_specs)]
    return _mm(name, grid, a, a_spec, w, w_spec, NN, outs, (tm, tn), extras, epilogue)


def _mm_nt(name, dy, w, out_dtype, *, tm, coff=0, ncontract=None, tc=None, dy_spec=None, extras=(), epilogue=None):
    J, K, n = w.shape
    M = dy.shape[-2]
    tc = tc or _tile(n, 2048)
    tko = _tile(K, 512)
    per = n // tc
    ncontract = ncontract or J * n
    grid = (M // tm, K // tko, ncontract // tc)
    if dy_spec is None:
        dy_spec = BS((tm, tc), lambda i, ko, cc: (i, cc))
    w_spec = BS((None, tko, tc), lambda i, ko, cc: ((cc + coff) // per, ko, (cc + coff) % per))
    outs = [(SDS((M, K), out_dtype), BS((tm, tko), lambda i, ko, cc: (i, ko)))]
    return _mm(name, grid, dy, dy_spec, w, w_spec, NT, outs, (tm, tko), extras, epilogue)[0]


def _mm_tn(name, x, dy, wshape, *, ts, joff=0, ncols=None, tn=None, dy_spec=None, carry=None):
    J, K, n = wshape
    S = x.shape[0]
    tn = tn or _tile(n, 1024)
    tkm = _tile(K, 512)
    per = n // tn
    ncols = ncols or J * n
    grid = (K // tkm, ncols // tn, S // ts)
    x_spec = BS((ts, tkm), lambda ki, j, sc: (sc, ki))
    if dy_spec is None:
        dy_spec = BS((ts, tn), lambda ki, j, sc: (sc, j))
    outs = [(SDS(wshape, BF16), BS((None, tkm, tn), lambda ki, j, sc: ((j + joff) // per, ki, (j + joff) % per)))]
    return _mm(name, grid, x, x_spec, dy, dy_spec, TN, outs, (tkm, tn), carry=None if carry is None else (carry, 0))[0]


def _rows(name, body, ins, outs, S, tr, sequential=False):
    def spec(shape, kind):
        if kind == "row":
            return BS((tr, shape[1]), lambda i: (i, 0))
        if kind == "order":
            return BS(memory_space=pl.ANY)
        return BS(tuple(shape), lambda i: (0,) * len(shape))
    return pl.pallas_call(
        body, out_shape=[o[0] for o in outs], grid=(S // tr,),
        in_specs=[spec(a.shape, k) for a, k in ins], out_specs=[spec(o[0].shape, o[1]) for o in outs],
        name=name, compiler_params=_cparams(("arbitrary" if sequential else "parallel",)))(*[a for a, _ in ins])


def _rms_fwd(h, g, tr=256, after=()):
    S, D = h.shape
    n_after = len(after)

    def body(h_ref, g_ref, *rest):
        u_ref = rest[n_after]
        x = h_ref[...]
        r = lax.rsqrt(jnp.mean(x * x, axis=-1, keepdims=True) + NORM_EPS)
        u_ref[...] = ((x * r) * g_ref[...]).astype(BF16)

    ins = [(h, "row"), (g, "full")] + [(t, "order") for t in after]
    return _rows("rms_fwd", body, ins, [(SDS((S, D), BF16), "row")], S, tr)[0]


def _rms_bwd(h, g, dy, dres, tr=256):
    S, D = h.shape

    def body(h_ref, g_ref, dy_ref, dres_ref, dh_ref, dhb_ref, dg_ref):
        x = h_ref[...]
        r = lax.rsqrt(jnp.mean(x * x, axis=-1, keepdims=True) + NORM_EPS)
        xhat = x * r
        dyv = dy_ref[...]
        gy = dyv * g_ref[...]
        dx = r * (gy - xhat * jnp.mean(gy * xhat, axis=-1, keepdims=True))
        dh = dres_ref[...] + dx
        dh_ref[...] = dh
        dhb_ref[...] = dh.astype(BF16)
        part = jnp.sum(dyv * xhat, axis=0, keepdims=True)

        @pl.when(pl.program_id(0) == 0)
        def _():
            dg_ref[...] = part

        @pl.when(pl.program_id(0) > 0)
        def _():
            dg_ref[...] += part

    return _rows("rms_bwd", body, [(h, "row"), (g, "full"), (dy, "row"), (dres, "row")],
                 [(SDS((S, D), F32), "row"), (SDS((S, D), BF16), "row"), (SDS((1, D), F32), "full")], S, tr, True)


def _final(h, g, target, tr=256):
    S, D = h.shape

    def body(h_ref, g_ref, t_ref, dh_ref, dg_ref, loss_ref):
        x = h_ref[...]
        r = lax.rsqrt(jnp.mean(x * x, axis=-1, keepdims=True) + NORM_EPS)
        xhat = x * r
        gv = g_ref[...]
        err = xhat * gv - t_ref[...]
        lpart = 0.5 * jnp.sum(jnp.mean(err * err, axis=-1, keepdims=True), axis=0, keepdims=True)
        dyv = err * (1.0 / D)
        gy = dyv * gv
        dh_ref[...] = r * (gy - xhat * jnp.mean(gy * xhat, axis=-1, keepdims=True))
        part = jnp.sum(dyv * xhat, axis=0, keepdims=True)
        lrow = jnp.broadcast_to(lpart, (1, LANES))

        @pl.when(pl.program_id(0) == 0)
        def _():
            dg_ref[...] = part
            loss_ref[...] = lrow

        @pl.when(pl.program_id(0) > 0)
        def _():
            dg_ref[...] += part
            loss_ref[...] += lrow

    return _rows("final_head", body, [(h, "row"), (g, "full"), (target, "row")],
                 [(SDS((S, D), F32), "row"), (SDS((1, D), F32), "full"), (SDS((1, LANES), F32), "full")], S, tr, True)


def _ple_bwd(dh, pg, pp, tr=256, after=()):
    S, D = dh.shape
    n_after = len(after)

    def body(dh_ref, pg_ref, pp_ref, *rest):
        dpp_ref, dpgz_ref = rest[n_after:]
        d, g, q = dh_ref[...], pg_ref[...], pp_ref[...]
        dpp_ref[...] = (d * g).astype(BF16)
        dpgz_ref[...] = (d * q * g * (1.0 - g)).astype(BF16)

    ins = [(dh, "row"), (pg, "row"), (pp, "row")] + [(t, "order") for t in after]
    return _rows("ple_bwd", body, ins, [(SDS((S, D), BF16), "row"), (SDS((S, D), BF16), "row")], S, tr)


def _merge_bwd(dmerged, ya, yb, gates, tr=256):
    S, D = dmerged.shape

    def body(dm_ref, ya_ref, yb_ref, g_ref, dya_ref, dyb_ref, dgz_ref, dbg_ref):
        dm, a, b = dm_ref[...], ya_ref[...], yb_ref[...]
        g1, g2 = g_ref[:, :D], g_ref[:, D:]
        dya_ref[...] = (dm * g1).astype(BF16)
        dyb_ref[...] = (dm * g2).astype(BF16)
        dz1 = dm * a * g1 * (1.0 - g1)
        dz2 = dm * b * g2 * (1.0 - g2)
        dgz_ref[:, :D] = dz1.astype(BF16)
        dgz_ref[:, D:] = dz2.astype(BF16)
        p1 = jnp.sum(dz1, axis=0, keepdims=True)
        p2 = jnp.sum(dz2, axis=0, keepdims=True)

        @pl.when(pl.program_id(0) == 0)
        def _():
            dbg_ref[:, :D] = p1
            dbg_ref[:, D:] = p2

        @pl.when(pl.program_id(0) > 0)
        def _():
            dbg_ref[:, :D] += p1
            dbg_ref[:, D:] += p2

    return _rows("merge_bwd", body, [(dmerged, "row"), (ya, "row"), (yb, "row"), (gates, "row")],
                 [(SDS((S, D), BF16), "row"), (SDS((S, D), BF16), "row"), (SDS((S, 2 * D), BF16), "row"),
                  (SDS((1, 2 * D), F32), "full")], S, tr, True)


def _mix_weights(l0, l1, l2):
    mx = jnp.maximum(jnp.maximum(l0, l1), l2)
    e0, e1, e2 = jnp.exp(l0 - mx), jnp.exp(l1 - mx), jnp.exp(l2 - mx)
    inv = 1.0 / (e0 + e1 + e2)
    return e0 * inv, e1 * inv, e2 * inv


def _mix_fwd(os_, ls_, tr=512):
    S = os_[0].shape[0]

    def body(o0, o1, o2, l0, l1, l2, y_ref):
        w0, w1, w2 = _mix_weights(l0[...], l1[...], l2[...])
        y_ref[...] = (w0 * o0[...] + w1 * o1[...] + w2 * o2[...]).astype(BF16)

    return _rows("mix_fwd", body, [(a, "row") for a in (*os_, *ls_)], [(SDS((S, BRANCH), BF16), "row")], S, tr)[0]


def _mix_bwd(dy, os_, ls_, tr=512):
    S = dy.shape[0]

    def head_sums(t):
        return jnp.concatenate(
            [jnp.broadcast_to(jnp.sum(t[:, j * HEAD_DIM:(j + 1) * HEAD_DIM], axis=-1, keepdims=True), (tr, HEAD_DIM))
             for j in range(HEADS_PER_GROUP)], axis=1)

    def body(dy_ref, o0, o1, o2, l0, l1, l2, d0, d1, d2, p0, p1, p2):
        ws = _mix_weights(l0[...], l1[...], l2[...])
        d = dy_ref[...]
        es = [head_sums(d * o[...]) for o in (o0, o1, o2)]
        ebar = ws[0] * es[0] + ws[1] * es[1] + ws[2] * es[2]
        for w, e, d_ref, p_ref in zip(ws, es, (d0, d1, d2), (p0, p1, p2)):
            d_ref[...] = (w * d).astype(BF16)
            p_ref[...] = w * (e - ebar)

    outs = [(SDS((S, BRANCH), BF16), "row")] * 3 + [(SDS((S, BRANCH), F32), "row")] * 3
    return _rows("mix_bwd", body, [(a, "row") for a in (dy, *os_, *ls_)], outs, S, tr)


def _iota2(shape, dim):
    return lax.broadcasted_iota(jnp.int32, shape, dim)


def _fgate_fwd(zf, bf):
    S = zf.shape[0]
    nblk = S // BLOCK

    def body(zf_ref, bf_ref, cb_ref, ct_ref, flt_ref):
        tri = (_iota2((BLOCK, BLOCK), 0) >= _iota2((BLOCK, BLOCK), 1)).astype(F32)
        sel8 = (_iota2((8, LANES), 0) == _iota2((8, LANES), 1)).astype(F32)
        carry = jnp.zeros((1, LANES), F32)
        for blk in range(nblk):
            rows = pl.ds(blk * BLOCK, BLOCK)
            fl = zf_ref[rows, :] + bf_ref[...]
            ls = jnp.minimum(fl, 0.0) - jnp.log(1.0 + jnp.exp(-jnp.abs(fl)))
            c = jnp.dot(tri, ls, precision=HI, preferred_element_type=F32) + carry
            carry = carry + jnp.sum(ls, axis=0, keepdims=True)
            for hd in range(N_FOX):
                e_h = (_iota2((LANES, LANES), 0) == hd).astype(F32)
                cb_ref[rows, hd * LANES:(hd + 1) * LANES] = jnp.dot(c, e_h, precision=HI, preferred_element_type=F32)
            ct_ref[:, rows] = lax.dot_general(sel8, c, NT, precision=HI, preferred_element_type=F32)
            flt_ref[:, rows] = lax.dot_general(sel8, fl, NT, precision=HI, preferred_element_type=F32)

    vm = BS(memory_space=pltpu.VMEM)
    return pl.pallas_call(
        body, out_shape=[SDS((S, N_FOX * LANES), F32), SDS((8, S), F32), SDS((8, S), F32)],
        in_specs=[vm, vm], out_specs=[vm, vm, vm], name="fgate_fwd",
        compiler_params=pltpu.CompilerParams(vmem_limit_bytes=VMEM_LIMIT))(zf, bf)


def _fgate_bwd(dct, flt):
    S = dct.shape[1]
    nblk = S // BLOCK

    def body(dct_ref, flt_ref, dzf_ref, dbf_ref):
        rowid = _iota2((8, BLOCK), 0)
        tri_r = (_iota2((BLOCK, BLOCK), 0) >= _iota2((BLOCK, BLOCK), 1)).astype(F32)
        carry = jnp.zeros((8, 1), F32)
        dbf = jnp.zeros((8, 1), F32)
        for blk in reversed(range(nblk)):
            cols = pl.ds(blk * BLOCK, BLOCK)
            dc = jnp.where(rowid < N_FOX, dct_ref[:, cols], 0.0)
            dls = jnp.dot(dc, tri_r, precision=HI, preferred_element_type=F32) + carry
            carry = carry + jnp.sum(dc, axis=1, keepdims=True)
            dfl = jnp.where(rowid < N_FOX, dls * _sigmoid(-flt_ref[:, cols]), 0.0)
            dbf = dbf + jnp.sum(dfl, axis=1, keepdims=True)
            sq = jnp.concatenate([dfl, jnp.zeros((BLOCK - 8, BLOCK), F32)], axis=0)
            dzf_ref[cols, :] = sq.T.astype(BF16)
        dbf_ref[...] = jnp.broadcast_to(dbf, (8, LANES))

    vm = BS(memory_space=pltpu.VMEM)
    return pl.pallas_call(
        body, out_shape=[SDS((S, LANES), BF16), SDS((8, LANES), F32)],
        in_specs=[vm, vm], out_specs=[vm, vm], name="fgate_bwd",
        compiler_params=pltpu.CompilerParams(vmem_limit_bytes=VMEM_LIMIT))(dct, flt)


def _rope_tables(S):
    inv = ROPE_THETA ** (-jnp.arange(ROPE_HALF, dtype=F32) / ROPE_HALF)
    ang = jnp.arange(S, dtype=F32)[:, None] * inv[None, :]
    cos, sin = jnp.cos(ang), jnp.sin(ang)
    pad = HEAD_DIM - ROPE_DIM
    tc = jnp.concatenate([cos, cos, jnp.ones((S, pad), F32)], axis=1)
    ta = jnp.concatenate([-sin, jnp.zeros((S, HEAD_DIM - ROPE_HALF), F32)], axis=1)
    tb = jnp.concatenate([jnp.zeros((S, ROPE_HALF), F32), sin, jnp.zeros((S, pad), F32)], axis=1)
    return tc, ta, tb


def _rope(x, c, a, b):
    return x * c + pltpu.roll(x, HEAD_DIM - ROPE_HALF, 1) * a + pltpu.roll(x, ROPE_HALF, 1) * b


def _rope_t(g, c, a, b):
    return g * c - pltpu.roll(g, ROPE_HALF, 1) * b - pltpu.roll(g, HEAD_DIM - ROPE_HALF, 1) * a


def _band_mask(n, w):
    qi = _iota2((BLOCK, w), 0)
    ki = _iota2((BLOCK, w), 1)
    if n == 0:
        return ki <= qi
    dist = BLOCK + qi - ki
    return (dist >= 0) & (dist <= BLOCK)


def _wide(t, w):
    return t if w == LANES else jnp.concatenate([t] * (w // LANES), axis=1)


def _dil_fwd(z3, tabs, grp, S):
    _, d = DIL_GROUPS[grp]
    L = S // d
    nb = L // BLOCK
    zv = z3.reshape(3, L, d * ATTN)
    tv = [t.reshape(L, d * HEAD_DIM) for t in tabs]

    def body(qkv_ref, c_ref, a_ref, b_ref, o_ref, lse_ref, qs, ks):
        c, a, b = c_ref[...], a_ref[...], b_ref[...]
        qs[...] = _rope(qkv_ref[0].astype(F32), c, a, b).astype(BF16)
        ks[...] = _rope(qkv_ref[1].astype(F32), c, a, b).astype(BF16)
        for n in range(nb):
            lo, hi = max(n - 1, 0) * BLOCK, (n + 1) * BLOCK
            w = hi - lo
            rows = pl.ds(n * BLOCK, BLOCK)
            s = lax.dot_general(qs[rows, :], ks[lo:hi, :], NT, preferred_element_type=F32) * SCALE
            s = jnp.where(_band_mask(n, w), s, -jnp.inf)
            m = jnp.max(s, axis=-1, keepdims=True)
            p = jnp.exp(s - m)
            l = jnp.sum(p, axis=-1, keepdims=True)
            o = jnp.dot(p.astype(BF16), qkv_ref[2, lo:hi, :], preferred_element_type=F32)
            o_ref[rows, :] = o / l
            lse_ref[rows, :] = jnp.broadcast_to(m + jnp.log(l), (BLOCK, HEAD_DIM))

    hh0 = grp * HEADS_PER_GROUP
    tspec = BS((L, HEAD_DIM), lambda j, r: (0, r))
    ospec = BS((L, HEAD_DIM), lambda j, r: (0, r * HEADS_PER_GROUP + j))
    o, lse = pl.pallas_call(
        body, out_shape=[SDS((L, d * BRANCH), F32)] * 2, grid=(HEADS_PER_GROUP, d),
        in_specs=[BS((3, L, HEAD_DIM), lambda j, r: (0, 0, r * N_HEADS + hh0 + j)), tspec, tspec, tspec],
        out_specs=[ospec, ospec], scratch_shapes=[pltpu.VMEM((L, HEAD_DIM), BF16)] * 2,
        name="dil_fwd", compiler_params=_cparams(("parallel", "parallel")))(zv, *tv)
    return o.reshape(S, BRANCH), lse.reshape(S, BRANCH)


def _dil_bwd(z3, tabs, grp, S, do, lse, dpl, dz3):
    _, d = DIL_GROUPS[grp]
    L = S // d
    nb = L // BLOCK
    zv = z3.reshape(3, L, d * ATTN)
    tv = [t.reshape(L, d * HEAD_DIM) for t in tabs]
    dov, lsev, dplv = [t.reshape(L, d * BRANCH) for t in (do, lse, dpl)]
    n_carry = 0 if dz3 is None else 1

    def body(*refs):
        qkv_ref, c_ref, a_ref, b_ref, do_ref, lse_ref, dpl_ref = refs[:7]
        dz_ref = refs[7 + n_carry]
        qs, ks, dq_acc, dk_acc, dv_acc = refs[8 + n_carry:]
        c, a, b = c_ref[...], a_ref[...], b_ref[...]
        qs[...] = _rope(qkv_ref[0].astype(F32), c, a, b).astype(BF16)
        ks[...] = _rope(qkv_ref[1].astype(F32), c, a, b).astype(BF16)
        dk_acc[...] = jnp.zeros_like(dk_acc)
        dv_acc[...] = jnp.zeros_like(dv_acc)
        for n in range(nb):
            lo, hi = max(n - 1, 0) * BLOCK, (n + 1) * BLOCK
            w = hi - lo
            rows = pl.ds(n * BLOCK, BLOCK)
            q, k, v = qs[rows, :], ks[lo:hi, :], qkv_ref[2, lo:hi, :]
            s = lax.dot_general(q, k, NT, preferred_element_type=F32) * SCALE
            p = jnp.where(_band_mask(n, w), jnp.exp(s - _wide(lse_ref[rows, :], w)), 0.0)
            dob = do_ref[rows, :]
            dp = lax.dot_general(dob, v, NT, preferred_element_type=F32)
            delta = jnp.sum(p * dp, axis=-1, keepdims=True)
            ds = (p * (dp - delta + _wide(dpl_ref[rows, :], w))).astype(BF16)
            dq_acc[rows, :] = jnp.dot(ds, k, preferred_element_type=F32) * SCALE
            dk_acc[lo:hi, :] += lax.dot_general(ds, q, TN, preferred_element_type=F32) * SCALE
            dv_acc[lo:hi, :] += lax.dot_general(p.astype(BF16), dob, TN, preferred_element_type=F32)
        dz_ref[0] = _rope_t(dq_acc[...], c, a, b).astype(BF16)
        dz_ref[1] = _rope_t(dk_acc[...], c, a, b).astype(BF16)
        dz_ref[2] = dv_acc[...].astype(BF16)

    hh0 = grp * HEADS_PER_GROUP
    tspec = BS((L, HEAD_DIM), lambda j, r: (0, r))
    bspec = BS((L, HEAD_DIM), lambda j, r: (0, r * HEADS_PER_GROUP + j))
    zspec = BS((3, L, HEAD_DIM), lambda j, r: (0, 0, r * N_HEADS + hh0 + j))
    arrays = [zv, *tv, dov, lsev, dplv]
    specs = [zspec, tspec, tspec, tspec, bspec, bspec, bspec]
    alias = {}
    if dz3 is not None:
        arrays.append(dz3.reshape(3, L, d * ATTN))
        specs.append(BS(memory_space=pl.ANY))
        alias = {len(arrays) - 1: 0}
    out = pl.pallas_call(
        body, out_shape=SDS((3, L, d * ATTN), BF16), grid=(HEADS_PER_GROUP, d), in_specs=specs, out_specs=zspec,
        scratch_shapes=[pltpu.VMEM((L, HEAD_DIM), BF16)] * 2 + [pltpu.VMEM((L, HEAD_DIM), F32)] * 3,
        input_output_aliases=alias, name="dil_bwd", compiler_params=_cparams(("parallel", "parallel")))(*arrays)
    return out.reshape(3, S, ATTN)


FOX_TQ = 256


def _fox_scores(q_ref, k_ref, cb_ref, ct_ref, i, ext):
    rows = pl.ds(i * FOX_TQ, FOX_TQ)
    s = lax.dot_general(q_ref[0, rows, :], k_ref[0, 0:ext, :], NT, preferred_element_type=F32) * SCALE
    s = s + _wide(cb_ref[rows, :], ext) - ct_ref[:, 0:ext]
    qpos = i * FOX_TQ + _iota2((FOX_TQ, ext), 0)
    kpos = _iota2((FOX_TQ, ext), 1)
    return jnp.where(kpos <= qpos, s, -jnp.inf)


def _fox_specs(S):
    qs = BS((1, S, HEAD_DIM), lambda h: (0, 0, FOX_HEAD0 + h))
    ks = BS((1, S, HEAD_DIM), lambda h: (1, 0, FOX_HEAD0 + h))
    vs = BS((1, S, HEAD_DIM), lambda h: (2, 0, FOX_HEAD0 + h))
    hb = BS((S, HEAD_DIM), lambda h: (0, h))
    ct = BS((None, 1, S), lambda h: (h, 0, 0))
    return qs, ks, vs, hb, ct


def _fox_fwd(z3, cb, ct, S):
    nq = S // FOX_TQ

    def body(q_ref, k_ref, v_ref, cb_ref, ct_ref, y_ref, lse_ref):
        for i in range(nq):
            ext = (i + 1) * FOX_TQ
            rows = pl.ds(i * FOX_TQ, FOX_TQ)
            s = _fox_scores(q_ref, k_ref, cb_ref, ct_ref, i, ext)
            m = jnp.max(s, axis=-1, keepdims=True)
            p = jnp.exp(s - m)
            l = jnp.sum(p, axis=-1, keepdims=True)
            o = jnp.dot(p.astype(BF16), v_ref[0, 0:ext, :], preferred_element_type=F32)
            y_ref[rows, :] = (o / l).astype(BF16)
            lse_ref[rows, :] = jnp.broadcast_to(m + jnp.log(l), (FOX_TQ, HEAD_DIM))

    qs, ks, vs, hb, cts = _fox_specs(S)
    return pl.pallas_call(
        body, out_shape=[SDS((S, BRANCH), BF16), SDS((S, BRANCH), F32)], grid=(N_FOX,),
        in_specs=[qs, ks, vs, hb, cts], out_specs=[hb, hb], name="fox_fwd",
        compiler_params=_cparams(("parallel",)))(z3, z3, z3, cb, ct.reshape(8, 1, S))


def _fox_bwd(z3, cb, ct, S, dy, lse, dz3):
    nq = S // FOX_TQ

    def body(q_ref, k_ref, v_ref, cb_ref, ct_ref, dy_ref, lse_ref, _, dz_ref, dct_ref, dk_acc, dv_acc):
        dk_acc[...] = jnp.zeros_like(dk_acc)
        dv_acc[...] = jnp.zeros_like(dv_acc)
        dct_ref[...] = jnp.zeros_like(dct_ref)
        for i in range(nq):
            ext = (i + 1) * FOX_TQ
            rows = pl.ds(i * FOX_TQ, FOX_TQ)
            s = _fox_scores(q_ref, k_ref, cb_ref, ct_ref, i, ext)
            p = jnp.exp(s - _wide(lse_ref[rows, :], ext))
            dob = dy_ref[rows, :]
            dp = lax.dot_general(dob, v_ref[0, 0:ext, :], NT, preferred_element_type=F32)
            ds = p * (dp - jnp.sum(p * dp, axis=-1, keepdims=True))
            dsb = ds.astype(BF16)
            dz_ref[0, rows, :] = (jnp.dot(dsb, k_ref[0, 0:ext, :], preferred_element_type=F32) * SCALE).astype(BF16)
            dk_acc[0:ext, :] += lax.dot_general(dsb, q_ref[0, rows, :], TN, preferred_element_type=F32) * SCALE
            dv_acc[0:ext, :] += lax.dot_general(p.astype(BF16), dob, TN, preferred_element_type=F32)
            dct_ref[:, 0:ext] -= jnp.sum(ds, axis=0, keepdims=True)
        dz_ref[1] = dk_acc[...].astype(BF16)
        dz_ref[2] = dv_acc[...].astype(BF16)

    qs, ks, vs, hb, cts = _fox_specs(S)
    zspec = BS((3, S, HEAD_DIM), lambda h: (0, 0, FOX_HEAD0 + h))
    dz, dct = pl.pallas_call(
        body, out_shape=[SDS((3, S, ATTN), BF16), SDS((8, 1, S), F32)], grid=(N_FOX,),
        in_specs=[qs, ks, vs, hb, cts, hb, hb, BS(memory_space=pl.ANY)], out_specs=[zspec, cts],
        scratch_shapes=[pltpu.VMEM((S, HEAD_DIM), F32)] * 2, input_output_aliases={7: 0}, name="fox_bwd",
        compiler_params=_cparams(("parallel",)))(z3, z3, z3, cb, ct.reshape(8, 1, S), dy, lse, dz3)
    return dz, dct.reshape(8, S)


TM = 1024


def _layer_fwd(h, p_bf, sm, W, tabs, after=()):
    S, D = h.shape
    qkv_cols = 3 * ATTN
    u = _rms_fwd(h, sm["g_mix"], after=after)
    tn = 512
    per = ATTN // tn
    z3 = _mm_nn("mm_qkv", u, W["in"], [BF16], tm=TM, ncols=qkv_cols, tn=tn, out_shapes=[(3, S, ATTN)],
                out_specs=[BS((None, TM, tn), lambda i, j, k: (j // per, i, j % per))])[0]
    zf = _mm_nn("mm_f", u, W["in"], [F32], tm=TM, joff=qkv_cols // LANES, ncols=LANES, tn=LANES)[0]
    cb, ct, flt = _fgate_fwd(zf, sm["b_f"])
    os_, ls_ = [], []
    for grp in range(len(DIL_GROUPS)):
        o, l = _dil_fwd(z3, tabs, grp, S)
        os_.append(o)
        ls_.append(l)
    ya_h = _mix_fwd(os_, ls_)
    yb_h, lse_fox = _fox_fwd(z3, cb, ct, S)

    tg = _tile(W["gate"].shape[2], 512)
    gates = _mm_nn("mm_gate", u, W["gate"], [F32], tm=TM,
                   extras=[(sm["b_gate"], BS((1, tg), lambda i, j, k: (0, j)))],
                   epilogue=lambda acc, bias: (_sigmoid(acc + bias),))[0]
    ya = _mm_nn("mm_bra", ya_h, W["bra"], [F32], tm=TM)[0]
    tb = _tile(W["brb"].shape[2], 512)
    g2off = D // tb
    yb, merged = _mm_nn(
        "mm_brb", yb_h, W["brb"], [F32, BF16], tm=TM,
        extras=[(ya, BS((TM, tb), lambda i, j, k: (i, j))), (gates, BS((TM, tb), lambda i, j, k: (i, j))),
                (gates, BS((TM, tb), lambda i, j, k: (i, j + g2off)))],
        epilogue=lambda acc, a, g1, g2: (acc, g1 * a + g2 * acc))
    to = _tile(D, 512)
    res_spec = BS((TM, to), lambda i, j, k: (i, j))
    h1 = _mm_nn("mm_o", merged, W["o"], [F32], tm=TM, extras=[(h, res_spec)], epilogue=lambda acc, r: (r + acc,))[0]

    m = _rms_fwd(h1, sm["g_mlp"])

    def up_epi(acc):
        r = jnp.maximum(acc, 0.0)
        return r, r * r

    relu_a, act = _mm_nn("mm_up", m, W["up"], [BF16, BF16], tm=TM, epilogue=up_epi)
    h2 = _mm_nn("mm_down", act, W["down"], [F32], tm=TM, extras=[(h1, res_spec)], epilogue=lambda acc, r: (r + acc,))[0]

    nrm = _rms_fwd(h2, sm["g_ple"])
    pp = _mm_nn("mm_ple", p_bf, W["ple"], [F32], tm=TM)[0]

    def pg_epi(acc, r, q):
        g = _sigmoid(acc)
        return g, r + g * q

    pg, h3 = _mm_nn("mm_pg", nrm, W["pg"], [F32, F32], tm=TM, extras=[(h2, res_spec), (pp, res_spec)], epilogue=pg_epi)
    saved = dict(h=h, u=u, z3=z3, cb=cb, ct=ct, flt=flt, os=os_, ls=ls_, ya_h=ya_h, yb_h=yb_h, lse_fox=lse_fox,
                 gates=gates, ya=ya, yb=yb, merged=merged, h1=h1, m=m, relu_a=relu_a, act=act, h2=h2, nrm=nrm,
                 pg=pg, pp=pp, p_bf=p_bf)
    return h3, saved


def _layer_bwd(dh3, sv, sm, W, tabs, after=()):
    S, D = dh3.shape
    dW, dsm = {}, {}

    dpp, dpgz = _ple_bwd(dh3, sv["pg"], sv["pp"], after=after)
    dW["ple"] = _mm_tn("tn_ple", sv["p_bf"], dpp, W["ple"].shape, ts=S)
    dW["pg"] = _mm_tn("tn_pg", sv["nrm"], dpgz, W["pg"].shape, ts=S)
    dn = _mm_nt("nt_pg", dpgz, W["pg"], F32, tm=TM)
    dh2, dh2b, dsm["g_ple"] = _rms_bwd(sv["h2"], sm["g_ple"], dn, dh3)

    ta = _tile(W["up"].shape[1], 512)
    tff = _tile(W["down"].shape[1], 512)
    da = _mm_nt("nt_down", dh2b, W["down"], BF16, tm=TM,
                extras=[(sv["relu_a"], BS((TM, tff), lambda i, ko, cc: (i, ko)))],
                epilogue=lambda acc, r: (acc * (2.0 * r.astype(F32)),))
    dW["down"] = _mm_tn("tn_down", sv["act"], dh2b, W["down"].shape, ts=S)
    dm = _mm_nt("nt_up", da, W["up"], F32, tm=TM)
    dW["up"] = _mm_tn("tn_up", sv["m"], da, W["up"].shape, ts=S)
    dh1, dh1b, dsm["g_mlp"] = _rms_bwd(sv["h1"], sm["g_mlp"], dm, dh2)

    dmerged = _mm_nt("nt_o", dh1b, W["o"], F32, tm=TM)
    dW["o"] = _mm_tn("tn_o", sv["merged"], dh1b, W["o"].shape, ts=S)
    dya, dyb, dgz, dsm["b_gate"] = _merge_bwd(dmerged, sv["ya"], sv["yb"], sv["gates"])
    dya_h = _mm_nt("nt_bra", dya, W["bra"], F32, tm=TM)
    dyb_h = _mm_nt("nt_brb", dyb, W["brb"], BF16, tm=TM)
    dW["bra"] = _mm_tn("tn_bra", sv["ya_h"], dya, W["bra"].shape, ts=S)
    dW["brb"] = _mm_tn("tn_brb", sv["yb_h"], dyb, W["brb"].shape, ts=S)
    dW["gate"] = _mm_tn("tn_gate", sv["u"], dgz, W["gate"].shape, ts=S)
    du = _mm_nt("nt_gate", dgz, W["gate"], F32, tm=TM)

    dos_, dpls_ = [], []
    mixed = _mix_bwd(dya_h, sv["os"], sv["ls"])
    dos_, dpls_ = mixed[:3], mixed[3:]
    dz3 = None
    for grp in range(len(DIL_GROUPS)):
        dz3 = _dil_bwd(sv["z3"], tabs, grp, S, dos_[grp], sv["ls"][grp], dpls_[grp], dz3)
    dz3, dct = _fox_bwd(sv["z3"], sv["cb"], sv["ct"], S, dyb_h, sv["lse_fox"], dz3)
    dzf, dsm["b_f"] = _fgate_bwd(dct, sv["flt"])

    tn = 512
    per = ATTN // tn
    qkv_cols = 3 * ATTN
    dwin = _mm_tn("tn_qkv", sv["u"], dz3, W["in"].shape, ts=S, ncols=qkv_cols, tn=tn,
                  dy_spec=BS((None, S, tn), lambda ki, j, sc: (j // per, sc, j % per)))
    dW["in"] = _mm_tn("tn_f", sv["u"], dzf, W["in"].shape, ts=S, joff=qkv_cols // LANES, ncols=LANES, tn=LANES,
                      carry=dwin)
    tu = _tile(D, 512)
    prev_spec = BS((TM, tu), lambda i, ko, cc: (i, ko))
    du = _mm_nt("nt_qkv", dz3, W["in"], F32, tm=TM, ncontract=qkv_cols, tc=ATTN,
                dy_spec=BS((None, TM, ATTN), lambda i, ko, cc: (cc, i, 0)),
                extras=[(du, prev_spec)], epilogue=lambda acc, prev: (prev + acc,))
    du = _mm_nt("nt_f", dzf, W["in"], F32, tm=TM, coff=qkv_cols // LANES, ncontract=LANES, tc=LANES,
                extras=[(du, prev_spec)], epilogue=lambda acc, prev: (prev + acc,))
    dh0, _, dsm["g_mix"] = _rms_bwd(sv["h"], sm["g_mix"], du, dh1)
    return dh0, dW, dsm


def _shard_tiles(R, Cp):
    tc = Cp if Cp <= 2048 else _tile(Cp, 1024)
    tr = R
    while tr * tc > 256 * 1024 and tr % 16 == 0:
        tr //= 2
    return tr, tc


def _padded_cols(C):
    return -(-C // LANES) * LANES


def _cast_bf16(name, w):
    Lr, R, C = w.shape
    Cp = _padded_cols(C)
    tr, tc = _shard_tiles(R, Cp)

    def body(w_ref, *o_refs):
        for l in range(Lr):
            x = w_ref[l]
            if Cp != C:
                col = pl.program_id(1) * tc + _iota2((tr, tc), 1)
                x = jnp.where(col < C, x, 0.0)
            o_refs[l][...] = x.astype(BF16)

    return pl.pallas_call(
        body, out_shape=[SDS((R, Cp), BF16)] * Lr, grid=(R // tr, Cp // tc),
        in_specs=[BS((Lr, tr, tc), lambda i, j: (0, i, j))], out_specs=[BS((tr, tc), lambda i, j: (i, j))] * Lr,
        name="cast_" + name, compiler_params=_cparams(("parallel", "parallel")))(w)


def _adam_math(w, g, m, v):
    m = ADAM_B1 * m + (1.0 - ADAM_B1) * g
    v = ADAM_B2 * v + (1.0 - ADAM_B2) * (g * g)
    m_hat = m / (1.0 - ADAM_B1 ** ADAM_STEP)
    v_hat = v / (1.0 - ADAM_B2 ** ADAM_STEP)
    delta = -ADAM_LR * (m_hat / (jnp.sqrt(v_hat) + ADAM_EPS) + ADAM_WD * w)
    return delta, m, v


def _adamw(name, layer, w, m, v, recv, carried):
    Lr, R, C = w.shape
    Cp = recv.shape[2]
    tr, tc = _shard_tiles(R, Cp)
    n_carry = 0 if carried is None else 4

    def body(w_ref, m_ref, v_ref, r_ref, *rest):
        g_out, d_out, m_out, v_out = rest[n_carry:]
        g = r_ref[0].astype(F32)
        for s in range(1, NDEV):
            g = g + r_ref[s].astype(F32)
        delta, mn, vn = _adam_math(w_ref[...], g, m_ref[...], v_ref[...])
        g_out[...] = g
        d_out[...] = delta
        m_out[...] = mn
        v_out[...] = vn

    spec = BS((None, tr, tc), lambda i, j: (layer, i, j))
    rspec = BS((NDEV, tr, tc), lambda i, j: (0, i, j))
    arrays, specs, alias = [w, m, v, recv], [spec, spec, spec, rspec], {}
    if carried is not None:
        arrays += list(carried)
        specs += [BS(memory_space=pl.ANY)] * 4
        alias = {4 + k: k for k in range(4)}
    return pl.pallas_call(
        body, out_shape=[SDS(w.shape, F32)] * 4, grid=(R // tr, Cp // tc), in_specs=specs, out_specs=[spec] * 4,
        input_output_aliases=alias, name="adamw_" + name,
        compiler_params=_cparams(("parallel", "parallel")))(*arrays)


def _adamw_small(w, g, m, v):
    vm = BS(memory_space=pltpu.VMEM)

    def body(w_ref, g_ref, m_ref, v_ref, d_out, m_out, v_out):
        delta, mn, vn = _adam_math(w_ref[...], g_ref[...], m_ref[...], v_ref[...])
        d_out[...] = delta
        m_out[...] = mn
        v_out[...] = vn

    return pl.pallas_call(body, out_shape=[SDS(w.shape, F32)] * 3, in_specs=[vm] * 4, out_specs=[vm] * 3,
                          name="adamw_small")(w, g, m, v)


def _place():
    return lax.axis_index("x"), lax.axis_index("y"), lax.axis_index("c")


def _slot(px, py, pc):
    return 4 * px + 2 * py + pc


def _other_chips(x, y):
    return [(1 - x, y), (x, 1 - y), (1 - x, 1 - y)]


def _peer(x, y, c, rel):
    return (1 - x if rel & 4 else x, 1 - y if rel & 2 else y, 1 - c if rel & 1 else c)


def _plan_gather_ici(src, land):
    x, y, c = _place()
    dst = land.at[_slot(x, y, c)]
    return [(src, dst, (x, y, 1 - c))] + [(src, dst, (*chip, c)) for chip in _other_chips(x, y)]


def _plan_gather_relay(src, land):
    x, y, c = _place()
    return [(land.at[_slot(*chip, c)], land.at[_slot(*chip, c)], (x, y, 1 - c)) for chip in _other_chips(x, y)]


def _plan_scatter(src, land):
    x, y, c = _place()
    me = _slot(x, y, c)
    peers = [_peer(x, y, c, rel) for rel in range(1, NDEV)]
    return [(src.at[_slot(*peer)], land.at[me], peer) for peer in peers]


EFFECT = pltpu.SideEffectType.DATAFLOW_SIDE_EFFECTING
HBM_SPEC = BS(memory_space=pltpu.HBM)
SEM_SPEC = BS(memory_space=pltpu.SEMAPHORE)
ORDER_SPEC = BS(memory_space=pl.ANY)


def _exchange_start(name, plan, ncopy, lands, srcs=None, after=()):
    bufs = ([] if srcs is None else list(srcs)) + list(lands)
    n, nb, ns = len(lands), len(bufs), len(bufs) - len(lands)
    n_after = len(after)

    def body(*refs):
        src_refs = refs[:ns] if ns else [None] * n
        land_refs = refs[ns:nb]
        send_sems, recv_sems = refs[nb + n_after], refs[nb + n_after + 1]
        token = refs[-1]
        for i in range(n):
            for k, (src, dst, to) in enumerate(plan(src_refs[i], land_refs[i])):
                pltpu.make_async_remote_copy(src_ref=src, dst_ref=dst, send_sem=send_sems.at[i * ncopy + k],
                                             recv_sem=recv_sems.at[i * ncopy + k], device_id=to,
                                             device_id_type=MESH).start()
        token[...] = jnp.zeros_like(token)

    out_shape = [pltpu.SemaphoreType.DMA((n * ncopy,)), pltpu.SemaphoreType.DMA((n * ncopy,))]
    out_shape += [pltpu.HBM(b.shape, b.dtype) for b in bufs] + [SDS((8, LANES), F32)]
    out = pl.pallas_call(
        body, name=name, out_shape=out_shape, in_specs=[HBM_SPEC] * nb + [ORDER_SPEC] * n_after,
        out_specs=[SEM_SPEC, SEM_SPEC] + [HBM_SPEC] * nb + [BS(memory_space=pltpu.VMEM)],
        input_output_aliases={i: 2 + i for i in range(nb)},
        compiler_params=pltpu.CompilerParams(has_side_effects=EFFECT))(
            *[pltpu.with_memory_space_constraint(b, pltpu.HBM) for b in bufs], *after)
    return out[0], out[1], out[2:2 + ns], out[2 + ns:2 + nb], out[-1]


def _exchange_wait(name, plan, send_sems, recv_sems, lands, srcs=None, after=()):
    bufs = ([] if srcs is None else list(srcs)) + list(lands)
    n, nb, ns = len(lands), len(bufs), len(bufs) - len(lands)

    def body(*refs):
        src_refs = refs[:ns] if ns else [None] * n
        land_refs = refs[ns:nb]
        send_ref, recv_ref = refs[nb], refs[nb + 1]
        for i in range(n):
            copies = plan(src_refs[i], land_refs[i])
            for k, (src, dst, to) in enumerate(copies):
                cp = pltpu.make_async_remote_copy(src_ref=src, dst_ref=dst, send_sem=send_ref.at[i * len(copies) + k],
                                                  recv_sem=recv_ref.at[i * len(copies) + k], device_id=to,
                                                  device_id_type=MESH)
                cp.wait_send()
                cp.wait_recv()

    out = pl.pallas_call(
        body, name=name, out_shape=[pltpu.HBM(b.shape, b.dtype) for b in bufs],
        in_specs=[HBM_SPEC] * nb + [SEM_SPEC, SEM_SPEC] + [ORDER_SPEC] * len(after), out_specs=[HBM_SPEC] * nb,
        input_output_aliases={i: i for i in range(nb)},
        compiler_params=pltpu.CompilerParams(has_side_effects=EFFECT))(*bufs, send_sems, recv_sems, *after)
    return out[:ns], out[ns:]


def _place_own(name, srcs, own_block, after=()):
    n = len(srcs)

    def body(*refs):
        src_refs, land_refs, sems = refs[:n], refs[n + len(after):2 * n + len(after)], refs[-1]
        x, y, c = _place()
        me = _slot(x, y, c)
        cps = [pltpu.make_async_copy(own_block(src_refs[i], me), land_refs[i].at[me], sems.at[i]) for i in range(n)]
        for cp in cps:
            cp.start()
        for cp in cps:
            cp.wait()

    outs = []
    for s in srcs:
        blk = s.shape if s.ndim == 2 else s.shape[1:]
        outs.append(SDS((NDEV,) + tuple(blk), s.dtype))
    return pl.pallas_call(
        body, name=name, out_shape=outs, in_specs=[ORDER_SPEC] * (n + len(after)), out_specs=[ORDER_SPEC] * n,
        scratch_shapes=[pltpu.SemaphoreType.DMA((n,))])(*srcs, *after)


def _allreduce_small(v):
    R, Wd = v.shape

    def body(v_ref, o_ref, buf, send_sems, recv_sems):
        x, y, c = _place()
        me = _slot(x, y, c)
        buf[me] = v_ref[...]
        started = []
        for rel in range(1, NDEV):
            peer = (1 - x if rel & 4 else x, 1 - y if rel & 2 else y, 1 - c if rel & 1 else c)
            cp = pltpu.make_async_remote_copy(
                src_ref=v_ref, dst_ref=buf.at[me], send_sem=send_sems.at[rel - 1], recv_sem=recv_sems.at[rel - 1],
                device_id=peer, device_id_type=MESH)
            cp.start()
            started.append(cp)
        for cp in started:
            cp.wait()
        acc = buf[0]
        for j in range(1, NDEV):
            acc = acc + buf[j]
        o_ref[...] = acc

    vm = BS(memory_space=pltpu.VMEM)
    return pl.pallas_call(
        body, out_shape=SDS((R, Wd), F32), in_specs=[vm], out_specs=vm,
        scratch_shapes=[pltpu.VMEM((NDEV, R, Wd), F32), pltpu.SemaphoreType.DMA((7,)), pltpu.SemaphoreType.DMA((7,))],
        name="allreduce_small")(v)


def _as_matrix(name, g):
    if name in ROW_SHARDED:
        return g.reshape(1, NDEV * g.shape[1], g.shape[2])
    return g


def _pad8(a):
    return jnp.pad(a, ((0, -a.shape[0] % 8), (0, 0)))


def _pack_small(g_mix, g_mlp, g_ple, b_gate, g_final, b_f, extra_row=None):
    Lr, D = g_mix.shape
    rows = [g_mix, g_mlp, g_ple, b_gate.reshape(2 * Lr, D), g_final.reshape(1, D),
            jnp.pad(b_f.reshape(1, -1), ((0, 0), (0, D - b_f.size)))]
    rows.append(jnp.zeros((1, D), F32) if extra_row is None else extra_row)
    return jnp.concatenate([_pad8(r) for r in rows], axis=0)


def _unpack_small(pk, Lr, D, nf):
    o = 0
    out = []
    for rows, shape in ((Lr, (Lr, D)), (Lr, (Lr, D)), (Lr, (Lr, D)), (2 * Lr, (Lr, 2 * D)), (1, (D,))):
        out.append(pk[o:o + rows].reshape(shape))
        o += rows + (-rows % 8)
    out.append(pk[o, :Lr * nf].reshape(Lr, nf))
    return out, pk[o + 8, 0]


def kernel(x, p, g_mix, w_in, b_f, w_gate, b_gate, w_br_a, w_br_b, w_o, g_mlp, w_up, w_down, g_ple, w_ple, w_ple_gate, g_final, loss_target, m_g_mix, m_w_in, m_b_f, m_w_gate, m_b_gate, m_w_br_a, m_w_br_b, m_w_o, m_g_mlp, m_w_up, m_w_down, m_g_ple, m_w_ple, m_w_ple_gate, m_g_final, v_g_mix, v_w_in, v_b_f, v_w_gate, v_b_gate, v_w_br_a, v_w_br_b, v_w_o, v_g_mlp, v_w_up, v_w_down, v_g_ple, v_w_ple, v_w_ple_gate, v_g_final):
    Lr, D = g_mix.shape
    S = x.shape[1]
    nf = b_f.shape[1]
    big_w = dict(zip(BIG, (w_in, w_gate, w_br_a, w_br_b, w_o, w_up, w_down, w_ple, w_ple_gate)))
    big_m = dict(zip(BIG, (m_w_in, m_w_gate, m_w_br_a, m_w_br_b, m_w_o, m_w_up, m_w_down, m_w_ple, m_w_ple_gate)))
    big_v = dict(zip(BIG, (v_w_in, v_w_gate, v_w_br_a, v_w_br_b, v_w_o, v_w_up, v_w_down, v_w_ple, v_w_ple_gate)))

    wbf = {k: _cast_bf16(k, big_w[k]) for k in BIG}
    tabs = _rope_tables(S)
    bf_pad = jnp.pad(b_f, ((0, 0), (0, LANES - nf)))

    smalls = [dict(g_mix=g_mix[l:l + 1], b_f=bf_pad[l:l + 1], b_gate=b_gate[l:l + 1], g_mlp=g_mlp[l:l + 1],
                   g_ple=g_ple[l:l + 1]) for l in range(Lr)]

    def gather_start(l, after=()):
        srcs = [wbf[k][l] for k in BIG]
        lands = _place_own(f"gather_own_{l}", srcs, lambda ref, me: ref)
        return _exchange_start(f"gather_ici_start_{l}", _plan_gather_ici, 4, lands, srcs, after)

    h = x.reshape(S, D)
    Ws, saves = [], []
    started = gather_start(0)
    for l in range(Lr):
        send_sems, recv_sems, srcs, lands, _ = started
        _, lands = _exchange_wait(f"gather_ici_wait_{l}", _plan_gather_ici, send_sems, recv_sems, lands, srcs, (h,))
        send_sems, recv_sems, _, lands, relay_token = _exchange_start(
            f"gather_relay_start_{l}", _plan_gather_relay, 3, lands)
        after = ()
        if l + 1 < Lr:
            started = gather_start(l + 1, (relay_token,))
            after = (started[4],)
        _, lands = _exchange_wait(f"gather_relay_wait_{l}", _plan_gather_relay, send_sems, recv_sems, lands)
        Ws.append({k: _as_matrix(k, g) for k, g in zip(BIG, lands)})
        h, sv = _layer_fwd(h, p[l, 0].astype(BF16), smalls[l], Ws[l], tabs, after)
        saves.append(sv)

    dh, dg_final, loss_part = _final(h, g_final.reshape(1, D), loss_target.reshape(S, D))

    big_out = {k: None for k in BIG}
    dsmalls = [None] * Lr

    def finish_scatter(pending, after):
        l, send_sems, recv_sems, dws, lands = pending
        _, landed = _exchange_wait(f"scatter_wait_{l}", _plan_scatter, send_sems, recv_sems, lands, dws, after)
        for k, recv in zip(BIG, landed):
            big_out[k] = _adamw(k, l, big_w[k], big_m[k], big_v[k], recv, big_out[k])
        return landed[0]

    pending, after = None, ()
    for l in reversed(range(Lr)):
        dh, dW, dsmalls[l] = _layer_bwd(dh, saves[l], smalls[l], Ws[l], tabs, after)
        order = () if pending is None else (finish_scatter(pending, (dh,)),)
        dws = [dW[k].reshape((NDEV,) + wbf[k][l].shape) for k in BIG]
        lands = _place_own(f"scatter_own_{l}", dws, lambda ref, me: ref.at[me], order)
        send_sems, recv_sems, dws, lands, token = _exchange_start(f"scatter_start_{l}", _plan_scatter, 7, lands, dws)
        pending, after = (l, send_sems, recv_sems, dws, lands), (token,)
    finish_scatter(pending, ())

    cat = lambda key: jnp.concatenate([_pad8(d[key]) for d in dsmalls], axis=0)[::8]
    dbf = jnp.concatenate([d["b_f"] for d in dsmalls], axis=0).reshape(Lr, 8, LANES)[:, :nf, 0]
    loss_row = jnp.concatenate([loss_part, g_final.reshape(1, D)[:, LANES:]], axis=1)
    g_pack = _allreduce_small(_pack_small(cat("g_mix"), cat("g_mlp"), cat("g_ple"), cat("b_gate"), dg_final, dbf,
                                          loss_row))
    w_pack = _pack_small(g_mix, g_mlp, g_ple, b_gate, g_final, b_f)
    m_pack = _pack_small(m_g_mix, m_g_mlp, m_g_ple, m_b_gate, m_g_final, m_b_f)
    v_pack = _pack_small(v_g_mix, v_g_mlp, v_g_ple, v_b_gate, v_g_final, v_b_f)
    d_pack, mn_pack, vn_pack = _adamw_small(w_pack, g_pack, m_pack, v_pack)

    small_names = ("g_mix", "g_mlp", "g_ple", "b_gate", "g_final", "b_f")
    sg, loss = _unpack_small(g_pack, Lr, D, nf)
    small_out = {}
    for kind, pk in (("grad", None), ("delta", d_pack), ("m", mn_pack), ("v", vn_pack)):
        vals = sg if pk is None else _unpack_small(pk, Lr, D, nf)[0]
        small_out[kind] = dict(zip(small_names, vals))

    order = ("g_mix", "in", "b_f", "gate", "b_gate", "bra", "brb", "o", "g_mlp", "up", "down", "g_ple", "ple", "pg",
             "g_final")
    outs = [loss, dh.reshape(x.shape)]
    for idx, kind in enumerate(("grad", "delta", "m", "v")):
        for name in order:
            outs.append(big_out[name][idx] if name in big_out else small_out[kind][name])
    return tuple(outs)
```

```python
import jax
import jax.numpy as jnp
from jax import lax
from jax.experimental import pallas as pl
from jax.experimental.pallas import tpu as pltpu

F32, BF16 = jnp.float32, jnp.bfloat16
SDS = jax.ShapeDtypeStruct
BS = pl.BlockSpec
MESH = pl.DeviceIdType.MESH
HI = lax.Precision.HIGHEST

HEAD_DIM = 128
N_HEADS = 16
ATTN = N_HEADS * HEAD_DIM
DIL_GROUPS = ((128, 1), (512, 4), (2048, 16))
HEADS_PER_GROUP = 4
FOX_HEAD0 = 12
N_FOX = 4
BRANCH = HEADS_PER_GROUP * HEAD_DIM
BLOCK = 128
ROPE_DIM = HEAD_DIM // 4
ROPE_HALF = ROPE_DIM // 2
ROPE_THETA = 500000.0
NORM_EPS = 1e-6
SCALE = HEAD_DIM ** -0.5
NDEV = 8
LANES = 128

ADAM_LR, ADAM_B1, ADAM_B2, ADAM_EPS, ADAM_WD, ADAM_STEP = 0.001, 0.9, 0.999, 1e-08, 0.01, 10

V7X_VMEM_BYTES = 64 * 1024 * 1024
VMEM_LIMIT = (V7X_VMEM_BYTES * 3) // 4

NN = (((1,), (0,)), ((), ()))
NT = (((1,), (1,)), ((), ()))
TN = (((0,), (0,)), ((), ()))

BIG = ("in", "gate", "bra", "brb", "o", "up", "down", "ple", "pg")
ROW_SHARDED = ("in", "o", "down", "pg")


def _tile(n, pref):
    t = min(n, pref)
    while n % t:
        t -= LANES
    return t


def _cparams(sem):
    return pltpu.CompilerParams(dimension_semantics=sem, vmem_limit_bytes=VMEM_LIMIT)


def _sigmoid(x):
    return 1.0 / (1.0 + jnp.exp(-x))


def _mm(name, grid, a, a_spec, b, b_spec, dims, outs, acc_shape, extras=(), epilogue=None, carry=None):
    nk = grid[2]
    n_ex, n_out = len(extras), len(outs)
    n_carry = 0 if carry is None else 1

    def body(*refs):
        a_ref, b_ref = refs[0], refs[1]
        ex_refs = refs[2:2 + n_ex]
        out_refs = refs[2 + n_ex + n_carry:2 + n_ex + n_carry + n_out]

        def product():
            return lax.dot_general(a_ref[...], b_ref[...], dims, preferred_element_type=F32)

        def finish(acc):
            res = (acc,) if epilogue is None else epilogue(acc, *[e[...] for e in ex_refs])
            for o_ref, r in zip(out_refs, res):
                o_ref[...] = r.astype(o_ref.dtype)

        if nk == 1:
            finish(product())
            return
        acc_ref = refs[-1]
        k = pl.program_id(2)

        @pl.when(k == 0)
        def _():
            acc_ref[...] = jnp.zeros_like(acc_ref)

        acc_ref[...] += product()

        @pl.when(k == nk - 1)
        def _():
            finish(acc_ref[...])

    arrays = [a, b] + [e[0] for e in extras]
    specs = [a_spec, b_spec] + [e[1] for e in extras]
    alias = {}
    if carry is not None:
        arrays.append(carry[0])
        specs.append(BS(memory_space=pl.ANY))
        alias = {len(arrays) - 1: carry[1]}
    return pl.pallas_call(
        body, out_shape=[o[0] for o in outs], grid=grid, in_specs=specs, out_specs=[o[1] for o in outs],
        scratch_shapes=[pltpu.VMEM(acc_shape, F32)] if nk > 1 else [], input_output_aliases=alias, name=name,
        compiler_params=_cparams(("parallel", "parallel", "arbitrary")))(*arrays)


def _mm_nn(name, a, w, out_dtypes, *, tm, joff=0, ncols=None, tn=None, extras=(), epilogue=None, out_specs=None,
           out_shapes=None):
    M, K = a.shape
    J, _, n = w.shape
    tn = tn or _tile(n, 512)
    tk = _tile(K, 2048)
    per = n // tn
    ncols = ncols or J * n
    grid = (M // tm, ncols // tn, K // tk)
    a_spec = BS((tm, tk), lambda i, j, k: (i, k))
    w_spec = BS((None, tk, tn), lambda i, j, k: ((j + joff) // per, k, (j + joff) % per))
    if out_specs is None:
        out_specs = [BS((tm, tn), lambda i, j, k: (i, j))] * len(out_dtypes)
        out_shapes = [(M, ncols)] * len(out_dtypes)
    outs = [(SDS(s, d), sp) for s, d, sp in zip(out_shapes, out_dtypes, out_specs)]
    return _mm(name, grid, a, a_spec, w, w_spec, NN, outs, (tm, tn), extras, epilogue)


def _mm_nt(name, dy, w, out_dtype, *, tm, coff=0, ncontract=None, tc=None, dy_spec=None, extras=(), epilogue=None):
    J, K, n = w.shape
    M = dy.shape[-2]
    tc = tc or _tile(n, 2048)
    tko = _tile(K, 512)
    per = n // tc
    ncontract = ncontract or J * n
    grid = (M // tm, K // tko, ncontract // tc)
    if dy_spec is None:
        dy_spec = BS((tm, tc), lambda i, ko, cc: (i, cc))
    w_spec = BS((None, tko, tc), lambda i, ko, cc: ((cc + coff) // per, ko, (cc + coff) % per))
    outs = [(SDS((M, K), out_dtype), BS((tm, tko), lambda i, ko, cc: (i, ko)))]
    return _mm(name, grid, dy, dy_spec, w, w_spec, NT, outs, (tm, tko), extras, epilogue)[0]


def _mm_tn(name, x, dy, wshape, *, ts, joff=0, ncols=None, tn=None, dy_spec=None, carry=None):
    J, K, n = wshape
    S = x.shape[0]
    tn = tn or _tile(n, 1024)
    tkm = _tile(K, 512)
    per = n // tn
    ncols = ncols or J * n
    grid = (K // tkm, ncols // tn, S // ts)
    x_spec = BS((ts, tkm), lambda ki, j, sc: (sc, ki))
    if dy_spec is None:
        dy_spec = BS((ts, tn), lambda ki, j, sc: (sc, j))
    outs = [(SDS(wshape, BF16), BS((None, tkm, tn), lambda ki, j, sc: ((j + joff) // per, ki, (j + joff) % per)))]
    return _mm(name, grid, x, x_spec, dy, dy_spec, TN, outs, (tkm, tn), carry=None if carry is None else (carry, 0))[0]


def _rows(name, body, ins, outs, S, tr, sequential=False):
    def spec(shape, kind):
        if kind == "row":
            return BS((tr, shape[1]), lambda i: (i, 0))
        if kind == "order":
            return BS(memory_space=pl.ANY)
        return BS(tuple(shape), lambda i: (0,) * len(shape))
    return pl.pallas_call(
        body, out_shape=[o[0] for o in outs], grid=(S // tr,),
        in_specs=[spec(a.shape, k) for a, k in ins], out_specs=[spec(o[0].shape, o[1]) for o in outs],
        name=name, compiler_params=_cparams(("arbitrary" if sequential else "parallel",)))(*[a for a, _ in ins])


def _rms_fwd(h, g, tr=256, after=()):
    S, D = h.shape
    n_after = len(after)

    def body(h_ref, g_ref, *rest):
        u_ref = rest[n_after]
        x = h_ref[...]
        r = lax.rsqrt(jnp.mean(x * x, axis=-1, keepdims=True) + NORM_EPS)
        u_ref[...] = ((x * r) * g_ref[...]).astype(BF16)

    ins = [(h, "row"), (g, "full")] + [(t, "order") for t in after]
    return _rows("rms_fwd", body, ins, [(SDS((S, D), BF16), "row")], S, tr)[0]


def _rms_bwd(h, g, dy, dres, tr=256, after=()):
    S, D = h.shape
    n_after = len(after)

    def body(h_ref, g_ref, dy_ref, dres_ref, *rest):
        dh_ref, dhb_ref, dg_ref = rest[n_after:]
        x = h_ref[...]
        r = lax.rsqrt(jnp.mean(x * x, axis=-1, keepdims=True) + NORM_EPS)
        xhat = x * r
        dyv = dy_ref[...]
        gy = dyv * g_ref[...]
        dx = r * (gy - xhat * jnp.mean(gy * xhat, axis=-1, keepdims=True))
        dh = dres_ref[...] + dx
        dh_ref[...] = dh
        dhb_ref[...] = dh.astype(BF16)
        part = jnp.sum(dyv * xhat, axis=0, keepdims=True)

        @pl.when(pl.program_id(0) == 0)
        def _():
            dg_ref[...] = part

        @pl.when(pl.program_id(0) > 0)
        def _():
            dg_ref[...] += part

    ins = [(h, "row"), (g, "full"), (dy, "row"), (dres, "row")] + [(t, "order") for t in after]
    return _rows("rms_bwd", body, ins,
                 [(SDS((S, D), F32), "row"), (SDS((S, D), BF16), "row"), (SDS((1, D), F32), "full")], S, tr, True)


def _final(h, g, target, tr=256):
    S, D = h.shape

    def body(h_ref, g_ref, t_ref, dh_ref, dg_ref, loss_ref):
        x = h_ref[...]
        r = lax.rsqrt(jnp.mean(x * x, axis=-1, keepdims=True) + NORM_EPS)
        xhat = x * r
        gv = g_ref[...]
        err = xhat * gv - t_ref[...]
        lpart = 0.5 * jnp.sum(jnp.mean(err * err, axis=-1, keepdims=True), axis=0, keepdims=True)
        dyv = err * (1.0 / D)
        gy = dyv * gv
        dh_ref[...] = r * (gy - xhat * jnp.mean(gy * xhat, axis=-1, keepdims=True))
        part = jnp.sum(dyv * xhat, axis=0, keepdims=True)
        lrow = jnp.broadcast_to(lpart, (1, LANES))

        @pl.when(pl.program_id(0) == 0)
        def _():
            dg_ref[...] = part
            loss_ref[...] = lrow

        @pl.when(pl.program_id(0) > 0)
        def _():
            dg_ref[...] += part
            loss_ref[...] += lrow

    return _rows("final_head", body, [(h, "row"), (g, "full"), (target, "row")],
                 [(SDS((S, D), F32), "row"), (SDS((1, D), F32), "full"), (SDS((1, LANES), F32), "full")], S, tr, True)


def _ple_bwd(dh, pg, pp, tr=256, after=()):
    S, D = dh.shape
    n_after = len(after)

    def body(dh_ref, pg_ref, pp_ref, *rest):
        dpp_ref, dpgz_ref = rest[n_after:]
        d, g, q = dh_ref[...], pg_ref[...], pp_ref[...]
        dpp_ref[...] = (d * g).astype(BF16)
        dpgz_ref[...] = (d * q * g * (1.0 - g)).astype(BF16)

    ins = [(dh, "row"), (pg, "row"), (pp, "row")] + [(t, "order") for t in after]
    return _rows("ple_bwd", body, ins, [(SDS((S, D), BF16), "row"), (SDS((S, D), BF16), "row")], S, tr)


def _merge_bwd(dmerged, ya, yb, gates, tr=256):
    S, D = dmerged.shape

    def body(dm_ref, ya_ref, yb_ref, g_ref, dya_ref, dyb_ref, dgz_ref, dbg_ref):
        dm, a, b = dm_ref[...], ya_ref[...], yb_ref[...]
        g1, g2 = g_ref[:, :D], g_ref[:, D:]
        dya_ref[...] = (dm * g1).astype(BF16)
        dyb_ref[...] = (dm * g2).astype(BF16)
        dz1 = dm * a * g1 * (1.0 - g1)
        dz2 = dm * b * g2 * (1.0 - g2)
        dgz_ref[:, :D] = dz1.astype(BF16)
        dgz_ref[:, D:] = dz2.astype(BF16)
        p1 = jnp.sum(dz1, axis=0, keepdims=True)
        p2 = jnp.sum(dz2, axis=0, keepdims=True)

        @pl.when(pl.program_id(0) == 0)
        def _():
            dbg_ref[:, :D] = p1
            dbg_ref[:, D:] = p2

        @pl.when(pl.program_id(0) > 0)
        def _():
            dbg_ref[:, :D] += p1
            dbg_ref[:, D:] += p2

    return _rows("merge_bwd", body, [(dmerged, "row"), (ya, "row"), (yb, "row"), (gates, "row")],
                 [(SDS((S, D), BF16), "row"), (SDS((S, D), BF16), "row"), (SDS((S, 2 * D), BF16), "row"),
                  (SDS((1, 2 * D), F32), "full")], S, tr, True)


def _mix_weights(l0, l1, l2):
    mx = jnp.maximum(jnp.maximum(l0, l1), l2)
    e0, e1, e2 = jnp.exp(l0 - mx), jnp.exp(l1 - mx), jnp.exp(l2 - mx)
    inv = 1.0 / (e0 + e1 + e2)
    return e0 * inv, e1 * inv, e2 * inv


def _mix_fwd(os_, ls_, tr=512):
    S = os_[0].shape[0]

    def body(o0, o1, o2, l0, l1, l2, y_ref):
        w0, w1, w2 = _mix_weights(l0[...], l1[...], l2[...])
        y_ref[...] = (w0 * o0[...] + w1 * o1[...] + w2 * o2[...]).astype(BF16)

    return _rows("mix_fwd", body, [(a, "row") for a in (*os_, *ls_)], [(SDS((S, BRANCH), BF16), "row")], S, tr)[0]


def _mix_bwd(dy, os_, ls_, tr=512):
    S = dy.shape[0]

    def head_sums(t):
        return jnp.concatenate(
            [jnp.broadcast_to(jnp.sum(t[:, j * HEAD_DIM:(j + 1) * HEAD_DIM], axis=-1, keepdims=True), (tr, HEAD_DIM))
             for j in range(HEADS_PER_GROUP)], axis=1)

    def body(dy_ref, o0, o1, o2, l0, l1, l2, d0, d1, d2, p0, p1, p2):
        ws = _mix_weights(l0[...], l1[...], l2[...])
        d = dy_ref[...]
        es = [head_sums(d * o[...]) for o in (o0, o1, o2)]
        ebar = ws[0] * es[0] + ws[1] * es[1] + ws[2] * es[2]
        for w, e, d_ref, p_ref in zip(ws, es, (d0, d1, d2), (p0, p1, p2)):
            d_ref[...] = (w * d).astype(BF16)
            p_ref[...] = w * (e - ebar)

    outs = [(SDS((S, BRANCH), BF16), "row")] * 3 + [(SDS((S, BRANCH), F32), "row")] * 3
    return _rows("mix_bwd", body, [(a, "row") for a in (dy, *os_, *ls_)], outs, S, tr)


def _iota2(shape, dim):
    return lax.broadcasted_iota(jnp.int32, shape, dim)


def _fgate_fwd(zf, bf):
    S = zf.shape[0]
    nblk = S // BLOCK

    def body(zf_ref, bf_ref, cb_ref, ct_ref, flt_ref):
        tri = (_iota2((BLOCK, BLOCK), 0) >= _iota2((BLOCK, BLOCK), 1)).astype(F32)
        sel8 = (_iota2((8, LANES), 0) == _iota2((8, LANES), 1)).astype(F32)
        carry = jnp.zeros((1, LANES), F32)
        for blk in range(nblk):
            rows = pl.ds(blk * BLOCK, BLOCK)
            fl = zf_ref[rows, :] + bf_ref[...]
            ls = jnp.minimum(fl, 0.0) - jnp.log(1.0 + jnp.exp(-jnp.abs(fl)))
            c = jnp.dot(tri, ls, precision=HI, preferred_element_type=F32) + carry
            carry = carry + jnp.sum(ls, axis=0, keepdims=True)
            for hd in range(N_FOX):
                e_h = (_iota2((LANES, LANES), 0) == hd).astype(F32)
                cb_ref[rows, hd * LANES:(hd + 1) * LANES] = jnp.dot(c, e_h, precision=HI, preferred_element_type=F32)
            ct_ref[:, rows] = lax.dot_general(sel8, c, NT, precision=HI, preferred_element_type=F32)
            flt_ref[:, rows] = lax.dot_general(sel8, fl, NT, precision=HI, preferred_element_type=F32)

    vm = BS(memory_space=pltpu.VMEM)
    return pl.pallas_call(
        body, out_shape=[SDS((S, N_FOX * LANES), F32), SDS((8, S), F32), SDS((8, S), F32)],
        in_specs=[vm, vm], out_specs=[vm, vm, vm], name="fgate_fwd",
        compiler_params=pltpu.CompilerParams(vmem_limit_bytes=VMEM_LIMIT))(zf, bf)


def _fgate_bwd(dct, flt):
    S = dct.shape[1]
    nblk = S // BLOCK

    def body(dct_ref, flt_ref, dzf_ref, dbf_ref):
        rowid = _iota2((8, BLOCK), 0)
        tri_r = (_iota2((BLOCK, BLOCK), 0) >= _iota2((BLOCK, BLOCK), 1)).astype(F32)
        carry = jnp.zeros((8, 1), F32)
        dbf = jnp.zeros((8, 1), F32)
        for blk in reversed(range(nblk)):
            cols = pl.ds(blk * BLOCK, BLOCK)
            dc = jnp.where(rowid < N_FOX, dct_ref[:, cols], 0.0)
            dls = jnp.dot(dc, tri_r, precision=HI, preferred_element_type=F32) + carry
            carry = carry + jnp.sum(dc, axis=1, keepdims=True)
            dfl = jnp.where(rowid < N_FOX, dls * _sigmoid(-flt_ref[:, cols]), 0.0)
            dbf = dbf + jnp.sum(dfl, axis=1, keepdims=True)
            sq = jnp.concatenate([dfl, jnp.zeros((BLOCK - 8, BLOCK), F32)], axis=0)
            dzf_ref[cols, :] = sq.T.astype(BF16)
        dbf_ref[...] = jnp.broadcast_to(dbf, (8, LANES))

    vm = BS(memory_space=pltpu.VMEM)
    return pl.pallas_call(
        body, out_shape=[SDS((S, LANES), BF16), SDS((8, LANES), F32)],
        in_specs=[vm, vm], out_specs=[vm, vm], name="fgate_bwd",
        compiler_params=pltpu.CompilerParams(vmem_limit_bytes=VMEM_LIMIT))(dct, flt)


def _rope_tables(S):
    inv = ROPE_THETA ** (-jnp.arange(ROPE_HALF, dtype=F32) / ROPE_HALF)
    ang = jnp.arange(S, dtype=F32)[:, None] * inv[None, :]
    cos, sin = jnp.cos(ang), jnp.sin(ang)
    pad = HEAD_DIM - ROPE_DIM
    tc = jnp.concatenate([cos, cos, jnp.ones((S, pad), F32)], axis=1)
    ta = jnp.concatenate([-sin, jnp.zeros((S, HEAD_DIM - ROPE_HALF), F32)], axis=1)
    tb = jnp.concatenate([jnp.zeros((S, ROPE_HALF), F32), sin, jnp.zeros((S, pad), F32)], axis=1)
    return tc, ta, tb


def _rope(x, c, a, b):
    return x * c + pltpu.roll(x, HEAD_DIM - ROPE_HALF, 1) * a + pltpu.roll(x, ROPE_HALF, 1) * b


def _rope_t(g, c, a, b):
    return g * c - pltpu.roll(g, ROPE_HALF, 1) * b - pltpu.roll(g, HEAD_DIM - ROPE_HALF, 1) * a


def _band_mask(n, w):
    qi = _iota2((BLOCK, w), 0)
    ki = _iota2((BLOCK, w), 1)
    if n == 0:
        return ki <= qi
    dist = BLOCK + qi - ki
    return (dist >= 0) & (dist <= BLOCK)


def _wide(t, w):
    return t if w == LANES else jnp.concatenate([t] * (w // LANES), axis=1)


def _dil_fwd(z3, tabs, grp, S):
    _, d = DIL_GROUPS[grp]
    L = S // d
    nb = L // BLOCK
    zv = z3.reshape(3, L, d * ATTN)
    tv = [t.reshape(L, d * HEAD_DIM) for t in tabs]

    def body(qkv_ref, c_ref, a_ref, b_ref, o_ref, lse_ref, qs, ks):
        c, a, b = c_ref[...], a_ref[...], b_ref[...]
        qs[...] = _rope(qkv_ref[0].astype(F32), c, a, b).astype(BF16)
        ks[...] = _rope(qkv_ref[1].astype(F32), c, a, b).astype(BF16)
        for n in range(nb):
            lo, hi = max(n - 1, 0) * BLOCK, (n + 1) * BLOCK
            w = hi - lo
            rows = pl.ds(n * BLOCK, BLOCK)
            s = lax.dot_general(qs[rows, :], ks[lo:hi, :], NT, preferred_element_type=F32) * SCALE
            s = jnp.where(_band_mask(n, w), s, -jnp.inf)
            m = jnp.max(s, axis=-1, keepdims=True)
            p = jnp.exp(s - m)
            l = jnp.sum(p, axis=-1, keepdims=True)
            o = jnp.dot(p.astype(BF16), qkv_ref[2, lo:hi, :], preferred_element_type=F32)
            o_ref[rows, :] = o / l
            lse_ref[rows, :] = jnp.broadcast_to(m + jnp.log(l), (BLOCK, HEAD_DIM))

    hh0 = grp * HEADS_PER_GROUP
    tspec = BS((L, HEAD_DIM), lambda j, r: (0, r))
    ospec = BS((L, HEAD_DIM), lambda j, r: (0, r * HEADS_PER_GROUP + j))
    o, lse = pl.pallas_call(
        body, out_shape=[SDS((L, d * BRANCH), F32)] * 2, grid=(HEADS_PER_GROUP, d),
        in_specs=[BS((3, L, HEAD_DIM), lambda j, r: (0, 0, r * N_HEADS + hh0 + j)), tspec, tspec, tspec],
        out_specs=[ospec, ospec], scratch_shapes=[pltpu.VMEM((L, HEAD_DIM), BF16)] * 2,
        name="dil_fwd", compiler_params=_cparams(("parallel", "parallel")))(zv, *tv)
    return o.reshape(S, BRANCH), lse.reshape(S, BRANCH)


def _dil_bwd(z3, tabs, grp, S, do, lse, dpl, dz3):
    _, d = DIL_GROUPS[grp]
    L = S // d
    nb = L // BLOCK
    zv = z3.reshape(3, L, d * ATTN)
    tv = [t.reshape(L, d * HEAD_DIM) for t in tabs]
    dov, lsev, dplv = [t.reshape(L, d * BRANCH) for t in (do, lse, dpl)]
    n_carry = 0 if dz3 is None else 1

    def body(*refs):
        qkv_ref, c_ref, a_ref, b_ref, do_ref, lse_ref, dpl_ref = refs[:7]
        dz_ref = refs[7 + n_carry]
        qs, ks, dq_acc, dk_acc, dv_acc = refs[8 + n_carry:]
        c, a, b = c_ref[...], a_ref[...], b_ref[...]
        qs[...] = _rope(qkv_ref[0].astype(F32), c, a, b).astype(BF16)
        ks[...] = _rope(qkv_ref[1].astype(F32), c, a, b).astype(BF16)
        dk_acc[...] = jnp.zeros_like(dk_acc)
        dv_acc[...] = jnp.zeros_like(dv_acc)
        for n in range(nb):
            lo, hi = max(n - 1, 0) * BLOCK, (n + 1) * BLOCK
            w = hi - lo
            rows = pl.ds(n * BLOCK, BLOCK)
            q, k, v = qs[rows, :], ks[lo:hi, :], qkv_ref[2, lo:hi, :]
            s = lax.dot_general(q, k, NT, preferred_element_type=F32) * SCALE
            p = jnp.where(_band_mask(n, w), jnp.exp(s - _wide(lse_ref[rows, :], w)), 0.0)
            dob = do_ref[rows, :]
            dp = lax.dot_general(dob, v, NT, preferred_element_type=F32)
            delta = jnp.sum(p * dp, axis=-1, keepdims=True)
            ds = (p * (dp - delta + _wide(dpl_ref[rows, :], w))).astype(BF16)
            dq_acc[rows, :] = jnp.dot(ds, k, preferred_element_type=F32) * SCALE
            dk_acc[lo:hi, :] += lax.dot_general(ds, q, TN, preferred_element_type=F32) * SCALE
            dv_acc[lo:hi, :] += lax.dot_general(p.astype(BF16), dob, TN, preferred_element_type=F32)
        dz_ref[0] = _rope_t(dq_acc[...], c, a, b).astype(BF16)
        dz_ref[1] = _rope_t(dk_acc[...], c, a, b).astype(BF16)
        dz_ref[2] = dv_acc[...].astype(BF16)

    hh0 = grp * HEADS_PER_GROUP
    tspec = BS((L, HEAD_DIM), lambda j, r: (0, r))
    bspec = BS((L, HEAD_DIM), lambda j, r: (0, r * HEADS_PER_GROUP + j))
    zspec = BS((3, L, HEAD_DIM), lambda j, r: (0, 0, r * N_HEADS + hh0 + j))
    arrays = [zv, *tv, dov, lsev, dplv]
    specs = [zspec, tspec, tspec, tspec, bspec, bspec, bspec]
    alias = {}
    if dz3 is not None:
        arrays.append(dz3.reshape(3, L, d * ATTN))
        specs.append(BS(memory_space=pl.ANY))
        alias = {len(arrays) - 1: 0}
    out = pl.pallas_call(
        body, out_shape=SDS((3, L, d * ATTN), BF16), grid=(HEADS_PER_GROUP, d), in_specs=specs, out_specs=zspec,
        scratch_shapes=[pltpu.VMEM((L, HEAD_DIM), BF16)] * 2 + [pltpu.VMEM((L, HEAD_DIM), F32)] * 3,
        input_output_aliases=alias, name="dil_bwd", compiler_params=_cparams(("parallel", "parallel")))(*arrays)
    return out.reshape(3, S, ATTN)


FOX_TQ = 256


def _fox_scores(q_ref, k_ref, cb_ref, ct_ref, i, ext):
    rows = pl.ds(i * FOX_TQ, FOX_TQ)
    s = lax.dot_general(q_ref[0, rows, :], k_ref[0, 0:ext, :], NT, preferred_element_type=F32) * SCALE
    s = s + _wide(cb_ref[rows, :], ext) - ct_ref[:, 0:ext]
    qpos = i * FOX_TQ + _iota2((FOX_TQ, ext), 0)
    kpos = _iota2((FOX_TQ, ext), 1)
    return jnp.where(kpos <= qpos, s, -jnp.inf)


def _fox_specs(S):
    qs = BS((1, S, HEAD_DIM), lambda h: (0, 0, FOX_HEAD0 + h))
    ks = BS((1, S, HEAD_DIM), lambda h: (1, 0, FOX_HEAD0 + h))
    vs = BS((1, S, HEAD_DIM), lambda h: (2, 0, FOX_HEAD0 + h))
    hb = BS((S, HEAD_DIM), lambda h: (0, h))
    ct = BS((None, 1, S), lambda h: (h, 0, 0))
    return qs, ks, vs, hb, ct


def _fox_fwd(z3, cb, ct, S):
    nq = S // FOX_TQ

    def body(q_ref, k_ref, v_ref, cb_ref, ct_ref, y_ref, lse_ref):
        for i in range(nq):
            ext = (i + 1) * FOX_TQ
            rows = pl.ds(i * FOX_TQ, FOX_TQ)
            s = _fox_scores(q_ref, k_ref, cb_ref, ct_ref, i, ext)
            m = jnp.max(s, axis=-1, keepdims=True)
            p = jnp.exp(s - m)
            l = jnp.sum(p, axis=-1, keepdims=True)
            o = jnp.dot(p.astype(BF16), v_ref[0, 0:ext, :], preferred_element_type=F32)
            y_ref[rows, :] = (o / l).astype(BF16)
            lse_ref[rows, :] = jnp.broadcast_to(m + jnp.log(l), (FOX_TQ, HEAD_DIM))

    qs, ks, vs, hb, cts = _fox_specs(S)
    return pl.pallas_call(
        body, out_shape=[SDS((S, BRANCH), BF16), SDS((S, BRANCH), F32)], grid=(N_FOX,),
        in_specs=[qs, ks, vs, hb, cts], out_specs=[hb, hb], name="fox_fwd",
        compiler_params=_cparams(("parallel",)))(z3, z3, z3, cb, ct.reshape(8, 1, S))


def _fox_bwd(z3, cb, ct, S, dy, lse, dz3):
    nq = S // FOX_TQ

    def body(q_ref, k_ref, v_ref, cb_ref, ct_ref, dy_ref, lse_ref, _, dz_ref, dct_ref, dk_acc, dv_acc):
        dk_acc[...] = jnp.zeros_like(dk_acc)
        dv_acc[...] = jnp.zeros_like(dv_acc)
        dct_ref[...] = jnp.zeros_like(dct_ref)
        for i in range(nq):
            ext = (i + 1) * FOX_TQ
            rows = pl.ds(i * FOX_TQ, FOX_TQ)
            s = _fox_scores(q_ref, k_ref, cb_ref, ct_ref, i, ext)
            p = jnp.exp(s - _wide(lse_ref[rows, :], ext))
            dob = dy_ref[rows, :]
            dp = lax.dot_general(dob, v_ref[0, 0:ext, :], NT, preferred_element_type=F32)
            ds = p * (dp - jnp.sum(p * dp, axis=-1, keepdims=True))
            dsb = ds.astype(BF16)
            dz_ref[0, rows, :] = (jnp.dot(dsb, k_ref[0, 0:ext, :], preferred_element_type=F32) * SCALE).astype(BF16)
            dk_acc[0:ext, :] += lax.dot_general(dsb, q_ref[0, rows, :], TN, preferred_element_type=F32) * SCALE
            dv_acc[0:ext, :] += lax.dot_general(p.astype(BF16), dob, TN, preferred_element_type=F32)
            dct_ref[:, 0:ext] -= jnp.sum(ds, axis=0, keepdims=True)
        dz_ref[1] = dk_acc[...].astype(BF16)
        dz_ref[2] = dv_acc[...].astype(BF16)

    qs, ks, vs, hb, cts = _fox_specs(S)
    zspec = BS((3, S, HEAD_DIM), lambda h: (0, 0, FOX_HEAD0 + h))
    dz, dct = pl.pallas_call(
        body, out_shape=[SDS((3, S, ATTN), BF16), SDS((8, 1, S), F32)], grid=(N_FOX,),
        in_specs=[qs, ks, vs, hb, cts, hb, hb, BS(memory_space=pl.ANY)], out_specs=[zspec, cts],
        scratch_shapes=[pltpu.VMEM((S, HEAD_DIM), F32)] * 2, input_output_aliases={7: 0}, name="fox_bwd",
        compiler_params=_cparams(("parallel",)))(z3, z3, z3, cb, ct.reshape(8, 1, S), dy, lse, dz3)
    return dz, dct.reshape(8, S)


TM = 1024


def _layer_fwd(h, p_bf, sm, W, tabs, after=(), mid=None):
    S, D = h.shape
    qkv_cols = 3 * ATTN
    u = _rms_fwd(h, sm["g_mix"], after=after)
    tn = 512
    per = ATTN // tn
    z3 = _mm_nn("mm_qkv", u, W["in"], [BF16], tm=TM, ncols=qkv_cols, tn=tn, out_shapes=[(3, S, ATTN)],
                out_specs=[BS((None, TM, tn), lambda i, j, k: (j // per, i, j % per))])[0]
    zf = _mm_nn("mm_f", u, W["in"], [F32], tm=TM, joff=qkv_cols // LANES, ncols=LANES, tn=LANES)[0]
    cb, ct, flt = _fgate_fwd(zf, sm["b_f"])
    os_, ls_ = [], []
    for grp in range(len(DIL_GROUPS)):
        o, l = _dil_fwd(z3, tabs, grp, S)
        os_.append(o)
        ls_.append(l)
    ya_h = _mix_fwd(os_, ls_)
    yb_h, lse_fox = _fox_fwd(z3, cb, ct, S)

    tg = _tile(W["gate"].shape[2], 512)
    gates = _mm_nn("mm_gate", u, W["gate"], [F32], tm=TM,
                   extras=[(sm["b_gate"], BS((1, tg), lambda i, j, k: (0, j)))],
                   epilogue=lambda acc, bias: (_sigmoid(acc + bias),))[0]
    ya = _mm_nn("mm_bra", ya_h, W["bra"], [F32], tm=TM)[0]
    tb = _tile(W["brb"].shape[2], 512)
    g2off = D // tb
    yb, merged = _mm_nn(
        "mm_brb", yb_h, W["brb"], [F32, BF16], tm=TM,
        extras=[(ya, BS((TM, tb), lambda i, j, k: (i, j))), (gates, BS((TM, tb), lambda i, j, k: (i, j))),
                (gates, BS((TM, tb), lambda i, j, k: (i, j + g2off)))],
        epilogue=lambda acc, a, g1, g2: (acc, g1 * a + g2 * acc))
    to = _tile(D, 512)
    res_spec = BS((TM, to), lambda i, j, k: (i, j))
    h1 = _mm_nn("mm_o", merged, W["o"], [F32], tm=TM, extras=[(h, res_spec)], epilogue=lambda acc, r: (r + acc,))[0]

    m = _rms_fwd(h1, sm["g_mlp"], after=() if mid is None else mid(h1))

    def up_epi(acc):
        r = jnp.maximum(acc, 0.0)
        return r, r * r

    relu_a, act = _mm_nn("mm_up", m, W["up"], [BF16, BF16], tm=TM, epilogue=up_epi)
    h2 = _mm_nn("mm_down", act, W["down"], [F32], tm=TM, extras=[(h1, res_spec)], epilogue=lambda acc, r: (r + acc,))[0]

    nrm = _rms_fwd(h2, sm["g_ple"])
    pp = _mm_nn("mm_ple", p_bf, W["ple"], [F32], tm=TM)[0]

    def pg_epi(acc, r, q):
        g = _sigmoid(acc)
        return g, r + g * q

    pg, h3 = _mm_nn("mm_pg", nrm, W["pg"], [F32, F32], tm=TM, extras=[(h2, res_spec), (pp, res_spec)], epilogue=pg_epi)
    saved = dict(h=h, u=u, z3=z3, cb=cb, ct=ct, flt=flt, os=os_, ls=ls_, ya_h=ya_h, yb_h=yb_h, lse_fox=lse_fox,
                 gates=gates, ya=ya, yb=yb, merged=merged, h1=h1, m=m, relu_a=relu_a, act=act, h2=h2, nrm=nrm,
                 pg=pg, pp=pp, p_bf=p_bf)
    return h3, saved


def _layer_bwd(dh3, sv, sm, W, tabs, after=(), ready=None):
    S, D = dh3.shape
    dW, dsm = {}, {}

    dpp, dpgz = _ple_bwd(dh3, sv["pg"], sv["pp"], after=after)
    dW["ple"] = _mm_tn("tn_ple", sv["p_bf"], dpp, W["ple"].shape, ts=S)
    dW["pg"] = _mm_tn("tn_pg", sv["nrm"], dpgz, W["pg"].shape, ts=S)
    dn = _mm_nt("nt_pg", dpgz, W["pg"], F32, tm=TM)
    dh2, dh2b, dsm["g_ple"] = _rms_bwd(sv["h2"], sm["g_ple"], dn, dh3)

    ta = _tile(W["up"].shape[1], 512)
    tff = _tile(W["down"].shape[1], 512)
    da = _mm_nt("nt_down", dh2b, W["down"], BF16, tm=TM,
                extras=[(sv["relu_a"], BS((TM, tff), lambda i, ko, cc: (i, ko)))],
                epilogue=lambda acc, r: (acc * (2.0 * r.astype(F32)),))
    dW["down"] = _mm_tn("tn_down", sv["act"], dh2b, W["down"].shape, ts=S)
    dm = _mm_nt("nt_up", da, W["up"], F32, tm=TM)
    dW["up"] = _mm_tn("tn_up", sv["m"], da, W["up"].shape, ts=S)
    dh1, dh1b, dsm["g_mlp"] = _rms_bwd(sv["h1"], sm["g_mlp"], dm, dh2, after=() if ready is None else ready(dW))

    dmerged = _mm_nt("nt_o", dh1b, W["o"], F32, tm=TM)
    dW["o"] = _mm_tn("tn_o", sv["merged"], dh1b, W["o"].shape, ts=S)
    dya, dyb, dgz, dsm["b_gate"] = _merge_bwd(dmerged, sv["ya"], sv["yb"], sv["gates"])
    dya_h = _mm_nt("nt_bra", dya, W["bra"], F32, tm=TM)
    dyb_h = _mm_nt("nt_brb", dyb, W["brb"], BF16, tm=TM)
    dW["bra"] = _mm_tn("tn_bra", sv["ya_h"], dya, W["bra"].shape, ts=S)
    dW["brb"] = _mm_tn("tn_brb", sv["yb_h"], dyb, W["brb"].shape, ts=S)
    dW["gate"] = _mm_tn("tn_gate", sv["u"], dgz, W["gate"].shape, ts=S)
    du = _mm_nt("nt_gate", dgz, W["gate"], F32, tm=TM)

    dos_, dpls_ = [], []
    mixed = _mix_bwd(dya_h, sv["os"], sv["ls"])
    dos_, dpls_ = mixed[:3], mixed[3:]
    dz3 = None
    for grp in range(len(DIL_GROUPS)):
        dz3 = _dil_bwd(sv["z3"], tabs, grp, S, dos_[grp], sv["ls"][grp], dpls_[grp], dz3)
    dz3, dct = _fox_bwd(sv["z3"], sv["cb"], sv["ct"], S, dyb_h, sv["lse_fox"], dz3)
    dzf, dsm["b_f"] = _fgate_bwd(dct, sv["flt"])

    tn = 512
    per = ATTN // tn
    qkv_cols = 3 * ATTN
    dwin = _mm_tn("tn_qkv", sv["u"], dz3, W["in"].shape, ts=S, ncols=qkv_cols, tn=tn,
                  dy_spec=BS((None, S, tn), lambda ki, j, sc: (j // per, sc, j % per)))
    dW["in"] = _mm_tn("tn_f", sv["u"], dzf, W["in"].shape, ts=S, joff=qkv_cols // LANES, ncols=LANES, tn=LANES,
                      carry=dwin)
    tu = _tile(D, 512)
    prev_spec = BS((TM, tu), lambda i, ko, cc: (i, ko))
    du = _mm_nt("nt_qkv", dz3, W["in"], F32, tm=TM, ncontract=qkv_cols, tc=ATTN,
                dy_spec=BS((None, TM, ATTN), lambda i, ko, cc: (cc, i, 0)),
                extras=[(du, prev_spec)], epilogue=lambda acc, prev: (prev + acc,))
    du = _mm_nt("nt_f", dzf, W["in"], F32, tm=TM, coff=qkv_cols // LANES, ncontract=LANES, tc=LANES,
                extras=[(du, prev_spec)], epilogue=lambda acc, prev: (prev + acc,))
    dh0, _, dsm["g_mix"] = _rms_bwd(sv["h"], sm["g_mix"], du, dh1)
    return dh0, dW, dsm


def _shard_tiles(R, Cp):
    tc = Cp if Cp <= 2048 else _tile(Cp, 1024)
    tr = R
    while tr * tc > 256 * 1024 and tr % 16 == 0:
        tr //= 2
    return tr, tc


def _padded_cols(C):
    return -(-C // LANES) * LANES


def _cast_bf16(name, w, me):
    Lr, R, C = w.shape
    Cp = _padded_cols(C)
    tr, tc = _shard_tiles(R, Cp)

    def body(me_ref, w_ref, *o_refs):
        for l in range(Lr):
            x = w_ref[l]
            if Cp != C:
                col = pl.program_id(1) * tc + _iota2((tr, tc), 1)
                x = jnp.where(col < C, x, 0.0)
            o_refs[l][...] = x.astype(BF16)

    grid_spec = pltpu.PrefetchScalarGridSpec(
        num_scalar_prefetch=1, grid=(R // tr, Cp // tc),
        in_specs=[BS((Lr, tr, tc), lambda i, j, me_ref: (0, i, j))],
        out_specs=[BS((None, tr, tc), lambda i, j, me_ref: (me_ref[0], i, j))] * Lr)
    return pl.pallas_call(
        body, out_shape=[SDS((NDEV, R, Cp), BF16)] * Lr, grid_spec=grid_spec,
        name="cast_" + name, compiler_params=_cparams(("parallel", "parallel")))(me, w)


def _adam_math(w, g, m, v):
    m = ADAM_B1 * m + (1.0 - ADAM_B1) * g
    v = ADAM_B2 * v + (1.0 - ADAM_B2) * (g * g)
    m_hat = m / (1.0 - ADAM_B1 ** ADAM_STEP)
    v_hat = v / (1.0 - ADAM_B2 ** ADAM_STEP)
    delta = -ADAM_LR * (m_hat / (jnp.sqrt(v_hat) + ADAM_EPS) + ADAM_WD * w)
    return delta, m, v


def _adamw(name, layer, w, m, v, recv, own, me, carried):
    Lr, R, C = w.shape
    Cp = recv.shape[2]
    tr, tc = _shard_tiles(R, Cp)
    n_carry = 0 if carried is None else 4

    def body(me_ref, w_ref, m_ref, v_ref, r_ref, own_ref, *rest):
        g_out, d_out, m_out, v_out = rest[n_carry:]
        mine = own_ref[...].astype(F32)
        g = None
        for s in range(NDEV):
            part = jnp.where(me_ref[0] == s, mine, r_ref[s].astype(F32))
            g = part if g is None else g + part
        delta, mn, vn = _adam_math(w_ref[...], g, m_ref[...], v_ref[...])
        g_out[...] = g
        d_out[...] = delta
        m_out[...] = mn
        v_out[...] = vn

    spec = BS((None, tr, tc), lambda i, j, me_ref: (layer, i, j))
    rspec = BS((NDEV, tr, tc), lambda i, j, me_ref: (0, i, j))
    ospec = BS((None, tr, tc), lambda i, j, me_ref: (me_ref[0], i, j))
    arrays, specs, alias = [w, m, v, recv, own], [spec, spec, spec, rspec, ospec], {}
    if carried is not None:
        arrays += list(carried)
        specs += [BS(memory_space=pl.ANY)] * 4
        alias = {6 + k: k for k in range(4)}
    grid_spec = pltpu.PrefetchScalarGridSpec(
        num_scalar_prefetch=1, grid=(R // tr, Cp // tc), in_specs=specs, out_specs=[spec] * 4)
    return pl.pallas_call(
        body, out_shape=[SDS(w.shape, F32)] * 4, grid_spec=grid_spec, input_output_aliases=alias,
        name="adamw_" + name, compiler_params=_cparams(("parallel", "parallel")))(me, *arrays)


def _adamw_small(w, g, m, v):
    vm = BS(memory_space=pltpu.VMEM)

    def body(w_ref, g_ref, m_ref, v_ref, d_out, m_out, v_out):
        delta, mn, vn = _adam_math(w_ref[...], g_ref[...], m_ref[...], v_ref[...])
        d_out[...] = delta
        m_out[...] = mn
        v_out[...] = vn

    return pl.pallas_call(body, out_shape=[SDS(w.shape, F32)] * 3, in_specs=[vm] * 4, out_specs=[vm] * 3,
                          name="adamw_small")(w, g, m, v)


def _place():
    return lax.axis_index("x"), lax.axis_index("y"), lax.axis_index("c")


def _slot(px, py, pc):
    return 4 * px + 2 * py + pc


def _other_chips(x, y):
    return [(1 - x, y), (x, 1 - y), (1 - x, 1 - y)]


def _peer(x, y, c, rel):
    return (1 - x if rel & 4 else x, 1 - y if rel & 2 else y, 1 - c if rel & 1 else c)


def _plan_gather_ici(src, land):
    x, y, c = _place()
    own = land.at[_slot(x, y, c)]
    return [(own, own, (x, y, 1 - c))] + [(own, own, (*chip, c)) for chip in _other_chips(x, y)]


def _plan_gather_relay(src, land):
    x, y, c = _place()
    return [(land.at[_slot(*chip, c)], land.at[_slot(*chip, c)], (x, y, 1 - c)) for chip in _other_chips(x, y)]


def _plan_scatter(src, land):
    x, y, c = _place()
    me = _slot(x, y, c)
    peers = [_peer(x, y, c, rel) for rel in range(1, NDEV)]
    return [(src.at[_slot(*peer)], land.at[me], peer) for peer in peers]


EFFECT = pltpu.SideEffectType.DATAFLOW_SIDE_EFFECTING
HBM_SPEC = BS(memory_space=pltpu.HBM)
SEM_SPEC = BS(memory_space=pltpu.SEMAPHORE)
ORDER_SPEC = BS(memory_space=pl.ANY)


def _exchange_start(name, plan, ncopy, lands, srcs=None, after=()):
    bufs = ([] if srcs is None else list(srcs)) + list(lands)
    n, nb, ns = len(lands), len(bufs), len(bufs) - len(lands)
    n_after = len(after)

    def body(*refs):
        src_refs = refs[:ns] if ns else [None] * n
        land_refs = refs[ns:nb]
        send_sems, recv_sems = refs[nb + n_after], refs[nb + n_after + 1]
        token = refs[-1]
        for i in range(n):
            for k, (src, dst, to) in enumerate(plan(src_refs[i], land_refs[i])):
                pltpu.make_async_remote_copy(src_ref=src, dst_ref=dst, send_sem=send_sems.at[i * ncopy + k],
                                             recv_sem=recv_sems.at[i * ncopy + k], device_id=to,
                                             device_id_type=MESH).start()
        token[...] = jnp.zeros_like(token)

    out_shape = [pltpu.SemaphoreType.DMA((n * ncopy,)), pltpu.SemaphoreType.DMA((n * ncopy,))]
    out_shape += [pltpu.HBM(b.shape, b.dtype) for b in bufs] + [SDS((8, LANES), F32)]
    out = pl.pallas_call(
        body, name=name, out_shape=out_shape, in_specs=[HBM_SPEC] * nb + [ORDER_SPEC] * n_after,
        out_specs=[SEM_SPEC, SEM_SPEC] + [HBM_SPEC] * nb + [BS(memory_space=pltpu.VMEM)],
        input_output_aliases={i: 2 + i for i in range(nb)},
        compiler_params=pltpu.CompilerParams(has_side_effects=EFFECT))(
            *[pltpu.with_memory_space_constraint(b, pltpu.HBM) for b in bufs], *after)
    return out[0], out[1], out[2:2 + ns], out[2 + ns:2 + nb], out[-1]


def _exchange_wait(name, plan, send_sems, recv_sems, lands, srcs=None, after=()):
    bufs = ([] if srcs is None else list(srcs)) + list(lands)
    n, nb, ns = len(lands), len(bufs), len(bufs) - len(lands)

    def body(*refs):
        src_refs = refs[:ns] if ns else [None] * n
        land_refs = refs[ns:nb]
        send_ref, recv_ref = refs[nb], refs[nb + 1]
        for i in range(n):
            copies = plan(src_refs[i], land_refs[i])
            for k, (src, dst, to) in enumerate(copies):
                cp = pltpu.make_async_remote_copy(src_ref=src, dst_ref=dst, send_sem=send_ref.at[i * len(copies) + k],
                                                  recv_sem=recv_ref.at[i * len(copies) + k], device_id=to,
                                                  device_id_type=MESH)
                cp.wait_send()
                cp.wait_recv()

    out = pl.pallas_call(
        body, name=name, out_shape=[pltpu.HBM(b.shape, b.dtype) for b in bufs],
        in_specs=[HBM_SPEC] * nb + [SEM_SPEC, SEM_SPEC] + [ORDER_SPEC] * len(after), out_specs=[HBM_SPEC] * nb,
        input_output_aliases={i: i for i in range(nb)},
        compiler_params=pltpu.CompilerParams(has_side_effects=EFFECT))(*bufs, send_sems, recv_sems, *after)
    return out[:ns], out[ns:]


def _allreduce_small(v, after=()):
    R, Wd = v.shape
    n_after = len(after)

    def body(v_ref, *rest):
        o_ref, buf, send_sems, recv_sems = rest[n_after:]
        x, y, c = _place()
        me = _slot(x, y, c)
        buf[me] = v_ref[...]
        started = []
        for rel in range(1, NDEV):
            peer = (1 - x if rel & 4 else x, 1 - y if rel & 2 else y, 1 - c if rel & 1 else c)
            cp = pltpu.make_async_remote_copy(
                src_ref=v_ref, dst_ref=buf.at[me], send_sem=send_sems.at[rel - 1], recv_sem=recv_sems.at[rel - 1],
                device_id=peer, device_id_type=MESH)
            cp.start()
            started.append(cp)
        for cp in started:
            cp.wait()
        acc = buf[0]
        for j in range(1, NDEV):
            acc = acc + buf[j]
        o_ref[...] = acc

    vm = BS(memory_space=pltpu.VMEM)
    return pl.pallas_call(
        body, out_shape=SDS((R, Wd), F32), in_specs=[vm] + [ORDER_SPEC] * n_after, out_specs=vm,
        scratch_shapes=[pltpu.VMEM((NDEV, R, Wd), F32), pltpu.SemaphoreType.DMA((7,)), pltpu.SemaphoreType.DMA((7,))],
        name="allreduce_small")(v, *after)


def _as_matrix(name, g):
    if name in ROW_SHARDED:
        return g.reshape(1, NDEV * g.shape[1], g.shape[2])
    return g


def _pad8(a):
    return jnp.pad(a, ((0, -a.shape[0] % 8), (0, 0)))


def _pack_small(g_mix, g_mlp, g_ple, b_gate, g_final, b_f, extra_row=None):
    Lr, D = g_mix.shape
    rows = [g_mix, g_mlp, g_ple, b_gate.reshape(2 * Lr, D), g_final.reshape(1, D),
            jnp.pad(b_f.reshape(1, -1), ((0, 0), (0, D - b_f.size)))]
    rows.append(jnp.zeros((1, D), F32) if extra_row is None else extra_row)
    return jnp.concatenate([_pad8(r) for r in rows], axis=0)


def _unpack_small(pk, Lr, D, nf):
    o = 0
    out = []
    for rows, shape in ((Lr, (Lr, D)), (Lr, (Lr, D)), (Lr, (Lr, D)), (2 * Lr, (Lr, 2 * D)), (1, (D,))):
        out.append(pk[o:o + rows].reshape(shape))
        o += rows + (-rows % 8)
    out.append(pk[o, :Lr * nf].reshape(Lr, nf))
    return out, pk[o + 8, 0]


def kernel(x, p, g_mix, w_in, b_f, w_gate, b_gate, w_br_a, w_br_b, w_o, g_mlp, w_up, w_down, g_ple, w_ple, w_ple_gate, g_final, loss_target, m_g_mix, m_w_in, m_b_f, m_w_gate, m_b_gate, m_w_br_a, m_w_br_b, m_w_o, m_g_mlp, m_w_up, m_w_down, m_g_ple, m_w_ple, m_w_ple_gate, m_g_final, v_g_mix, v_w_in, v_b_f, v_w_gate, v_b_gate, v_w_br_a, v_w_br_b, v_w_o, v_g_mlp, v_w_up, v_w_down, v_g_ple, v_w_ple, v_w_ple_gate, v_g_final):
    Lr, D = g_mix.shape
    S = x.shape[1]
    nf = b_f.shape[1]
    big_w = dict(zip(BIG, (w_in, w_gate, w_br_a, w_br_b, w_o, w_up, w_down, w_ple, w_ple_gate)))
    big_m = dict(zip(BIG, (m_w_in, m_w_gate, m_w_br_a, m_w_br_b, m_w_o, m_w_up, m_w_down, m_w_ple, m_w_ple_gate)))
    big_v = dict(zip(BIG, (v_w_in, v_w_gate, v_w_br_a, v_w_br_b, v_w_o, v_w_up, v_w_down, v_w_ple, v_w_ple_gate)))

    me = _slot(*_place()).astype(jnp.int32).reshape(1)
    wbf = {k: _cast_bf16(k, big_w[k], me) for k in BIG}
    tabs = _rope_tables(S)
    bf_pad = jnp.pad(b_f, ((0, 0), (0, LANES - nf)))

    smalls = [dict(g_mix=g_mix[l:l + 1], b_f=bf_pad[l:l + 1], b_gate=b_gate[l:l + 1], g_mlp=g_mlp[l:l + 1],
                   g_ple=g_ple[l:l + 1]) for l in range(Lr)]

    def links_start(l, after=()):
        return _exchange_start(f"gather_ici_start_{l}", _plan_gather_ici, 4, [wbf[k][l] for k in BIG], None, after)

    def links_to_relay(l, on_links, after):
        send_sems, recv_sems, _, lands, _ = on_links
        _, lands = _exchange_wait(f"gather_ici_wait_{l}", _plan_gather_ici, send_sems, recv_sems, lands, None, after)
        relay = _exchange_start(f"gather_relay_start_{l}", _plan_gather_relay, 3, lands)
        return relay, (links_start(l + 1, (relay[4],)) if l + 1 < Lr else None)

    def relay_done(l, relay):
        send_sems, recv_sems, _, lands, _ = relay
        _, lands = _exchange_wait(f"gather_relay_wait_{l}", _plan_gather_relay, send_sems, recv_sems, lands)
        return {k: _as_matrix(k, g) for k, g in zip(BIG, lands)}

    h = x.reshape(S, D)
    Ws, saves = [], []
    relay, on_links = links_to_relay(0, links_start(0), (h,))
    for l in range(Lr):
        Ws.append(relay_done(l, relay))
        nxt = {}

        def mid(h1, l=l, on_links=on_links, nxt=nxt):
            if on_links is None:
                return ()
            nxt["relay"], nxt["on_links"] = links_to_relay(l + 1, on_links, (h1,))
            return tuple(t[4] for t in (nxt["relay"], nxt["on_links"]) if t is not None)

        h, sv = _layer_fwd(h, p[l, 0].astype(BF16), smalls[l], Ws[l], tabs,
                           () if on_links is None else (on_links[4],), mid)
        saves.append(sv)
        relay, on_links = nxt.get("relay"), nxt.get("on_links")

    dh, dg_final, loss_part = _final(h, g_final.reshape(1, D), loss_target.reshape(S, D))

    big_out = {k: None for k in BIG}
    dsmalls = [None] * Lr
    mlp_group, mixer_group = ("ple", "pg", "down", "up"), ("o", "bra", "brb", "gate", "in")

    def scatter_start(l, names, dW):
        dws = [dW[k].reshape(wbf[k][l].shape) for k in names]
        lands = [lax.empty(d.shape, BF16) for d in dws]
        return (l, names) + tuple(_exchange_start(f"scatter_start_{l}_{names[0]}", _plan_scatter, 7, lands, dws))

    def scatter_finish(started, after):
        l, names, send_sems, recv_sems, dws, lands, _ = started
        dws, landed = _exchange_wait(f"scatter_wait_{l}_{names[0]}", _plan_scatter, send_sems, recv_sems, lands, dws,
                                     after)
        for k, own, recv in zip(names, dws, landed):
            big_out[k] = _adamw(k, l, big_w[k], big_m[k], big_v[k], recv, own, me, big_out[k])
        return landed[0]

    pending, after, last = [], (), None
    for l in reversed(range(Lr)):
        group = {}

        def ready(dW, l=l, group=group):
            group["mlp"] = scatter_start(l, mlp_group, dW)
            return (group["mlp"][6],)

        dh, dW, dsmalls[l] = _layer_bwd(dh, saves[l], smalls[l], Ws[l], tabs, after, ready)
        mixer = scatter_start(l, mixer_group, dW)
        for started in pending:
            last = scatter_finish(started, (dh,))
        pending, after = [group["mlp"], mixer], (mixer[6],)
    for started in pending:
        last = scatter_finish(started, ())

    cat = lambda key: jnp.concatenate([_pad8(d[key]) for d in dsmalls], axis=0)[::8]
    dbf = jnp.concatenate([d["b_f"] for d in dsmalls], axis=0).reshape(Lr, 8, LANES)[:, :nf, 0]
    loss_row = jnp.concatenate([loss_part, g_final.reshape(1, D)[:, LANES:]], axis=1)
    g_pack = _allreduce_small(_pack_small(cat("g_mix"), cat("g_mlp"), cat("g_ple"), cat("b_gate"), dg_final, dbf,
                                          loss_row), (last,))
    w_pack = _pack_small(g_mix, g_mlp, g_ple, b_gate, g_final, b_f)
    m_pack = _pack_small(m_g_mix, m_g_mlp, m_g_ple, m_b_gate, m_g_final, m_b_f)
    v_pack = _pack_small(v_g_mix, v_g_mlp, v_g_ple, v_b_gate, v_g_final, v_b_f)
    d_pack, mn_pack, vn_pack = _adamw_small(w_pack, g_pack, m_pack, v_pack)

    small_names = ("g_mix", "g_mlp", "g_ple", "b_gate", "g_final", "b_f")
    sg, loss = _unpack_small(g_pack, Lr, D, nf)
    small_out = {}
    for kind, pk in (("grad", None), ("delta", d_pack), ("m", mn_pack), ("v", vn_pack)):
        vals = sg if pk is None else _unpack_small(pk, Lr, D, nf)[0]
        small_out[kind] = dict(zip(small_names, vals))

    order = ("g_mix", "in", "b_f", "gate", "b_gate", "bra", "brb", "o", "g_mlp", "up", "down", "g_ple", "ple", "pg",
             "g_final")
    outs = [loss, dh.reshape(x.shape)]
    for idx, kind in enumerate(("grad", "delta", "m", "v")):
        for name in order:
            outs.append(big_out[name][idx] if name in big_out else small_out[kind][name])
    return tuple(outs)
```

```python
import jax
import jax.numpy as jnp
from jax import lax
from jax.experimental import pallas as pl
from jax.experimental.pallas import tpu as pltpu

F32, BF16 = jnp.float32, jnp.bfloat16
SDS = jax.ShapeDtypeStruct
BS = pl.BlockSpec
MESH = pl.DeviceIdType.MESH
HI = lax.Precision.HIGHEST

HEAD_DIM = 128
N_HEADS = 16
ATTN = N_HEADS * HEAD_DIM
DIL_GROUPS = ((128, 1), (512, 4), (2048, 16))
HEADS_PER_GROUP = 4
FOX_HEAD0 = 12
N_FOX = 4
BRANCH = HEADS_PER_GROUP * HEAD_DIM
BLOCK = 128
ROPE_DIM = HEAD_DIM // 4
ROPE_HALF = ROPE_DIM // 2
ROPE_THETA = 500000.0
NORM_EPS = 1e-6
SCALE = HEAD_DIM ** -0.5
NDEV = 8
LANES = 128

ADAM_LR, ADAM_B1, ADAM_B2, ADAM_EPS, ADAM_WD, ADAM_STEP = 0.001, 0.9, 0.999, 1e-08, 0.01, 10

V7X_VMEM_BYTES = 64 * 1024 * 1024
VMEM_LIMIT = (V7X_VMEM_BYTES * 3) // 4

NN = (((1,), (0,)), ((), ()))
NT = (((1,), (1,)), ((), ()))
TN = (((0,), (0,)), ((), ()))

BIG = ("in", "gate", "bra", "brb", "o", "up", "down", "ple", "pg")
ROW_SHARDED = ("in", "o", "down", "pg")


def _tile(n, pref):
    t = min(n, pref)
    while n % t:
        t -= LANES
    return t


def _cparams(sem):
    return pltpu.CompilerParams(dimension_semantics=sem, vmem_limit_bytes=VMEM_LIMIT)


def _sigmoid(x):
    return 1.0 / (1.0 + jnp.exp(-x))


def _mm(name, grid, a, a_spec, b, b_spec, dims, outs, acc_shape, extras=(), epilogue=None, carry=None):
    nk = grid[2]
    n_ex, n_out = len(extras), len(outs)
    n_carry = 0 if carry is None else 1

    def body(*refs):
        a_ref, b_ref = refs[0], refs[1]
        ex_refs = refs[2:2 + n_ex]
        out_refs = refs[2 + n_ex + n_carry:2 + n_ex + n_carry + n_out]

        def product():
            return lax.dot_general(a_ref[...], b_ref[...], dims, preferred_element_type=F32)

        def finish(acc):
            res = (acc,) if epilogue is None else epilogue(acc, *[e[...] for e in ex_refs])
            for o_ref, r in zip(out_refs, res):
                o_ref[...] = r.astype(o_ref.dtype)

        if nk == 1:
            finish(product())
            return
        acc_ref = refs[-1]
        k = pl.program_id(2)

        @pl.when(k == 0)
        def _():
            acc_ref[...] = jnp.zeros_like(acc_ref)

        acc_ref[...] += product()

        @pl.when(k == nk - 1)
        def _():
            finish(acc_ref[...])

    arrays = [a, b] + [e[0] for e in extras]
    specs = [a_spec, b_spec] + [e[1] for e in extras]
    alias = {}
    if carry is not None:
        arrays.append(carry[0])
        specs.append(BS(memory_space=pl.ANY))
        alias = {len(arrays) - 1: carry[1]}
    return pl.pallas_call(
        body, out_shape=[o[0] for o in outs], grid=grid, in_specs=specs, out_specs=[o[1] for o in outs],
        scratch_shapes=[pltpu.VMEM(acc_shape, F32)] if nk > 1 else [], input_output_aliases=alias, name=name,
        compiler_params=_cparams(("parallel", "parallel", "arbitrary")))(*arrays)


def _mm_nn(name, a, w, out_dtypes, *, tm, joff=0, ncols=None, tn=None, extras=(), epilogue=None, out_specs=None,
           out_shapes=None):
    M, K = a.shape
    J, _, n = w.shape
    tn = tn or _tile(n, 512)
    tk = _tile(K, 2048)
    per = n // tn
    ncols = ncols or J * n
    grid = (M // tm, ncols // tn, K // tk)
    a_spec = BS((tm, tk), lambda i, j, k: (i, k))
    w_spec = BS((None, tk, tn), lambda i, j, k: ((j + joff) // per, k, (j + joff) % per))
    if out_specs is None:
        out_specs = [BS((tm, tn), lambda i, j, k: (i, j))] * len(out_dtypes)
        out_shapes = [(M, ncols)] * len(out_dtypes)
    outs = [(SDS(s, d), sp) for s, d, sp in zip(out_shapes, out_dtypes, out_specs)]
    return _mm(name, grid, a, a_spec, w, w_spec, NN, outs, (tm, tn), extras, epilogue)


def _mm_nt(name, dy, w, out_dtype, *, tm, coff=0, ncontract=None, tc=None, dy_spec=None, extras=(), epilogue=None):
    J, K, n = w.shape
    M = dy.shape[-2]
    tc = tc or _tile(n, 2048)
    tko = _tile(K, 512)
    per = n // tc
    ncontract = ncontract or J * n
    grid = (M // tm, K // tko, ncontract // tc)
    if dy_spec is None:
        dy_spec = BS((tm, tc), lambda i, ko, cc: (i, cc))
    w_spec = BS((None, tko, tc), lambda i, ko, cc: ((cc + coff) // per, ko, (cc + coff) % per))
    outs = [(SDS((M, K), out_dtype), BS((tm, tko), lambda i, ko, cc: (i, ko)))]
    return _mm(name, grid, dy, dy_spec, w, w_spec, NT, outs, (tm, tko), extras, epilogue)[0]


def _mm_tn(name, x, dy, wshape, *, ts, joff=0, ncols=None, tn=None, dy_spec=None, carry=None):
    J, K, n = wshape
    S = x.shape[0]
    tn = tn or _tile(n, 1024)
    tkm = _tile(K, 512)
    per = n // tn
    ncols = ncols or J * n
    grid = (K // tkm, ncols // tn, S // ts)
    x_spec = BS((ts, tkm), lambda ki, j, sc: (sc, ki))
    if dy_spec is None:
        dy_spec = BS((ts, tn), lambda ki, j, sc: (sc, j))
    outs = [(SDS(wshape, BF16), BS((None, tkm, tn), lambda ki, j, sc: ((j + joff) // per, ki, (j + joff) % per)))]
    return _mm(name, grid, x, x_spec, dy, dy_spec, TN, outs, (tkm, tn), carry=None if carry is None else (carry, 0))[0]


def _rows(name, body, ins, outs, S, tr, sequential=False):
    def spec(shape, kind):
        if kind == "row":
            return BS((tr, shape[1]), lambda i: (i, 0))
        if kind == "order":
            return BS(memory_space=pl.ANY)
        return BS(tuple(shape), lambda i: (0,) * len(shape))
    return pl.pallas_call(
        body, out_shape=[o[0] for o in outs], grid=(S // tr,),
        in_specs=[spec(a.shape, k) for a, k in ins], out_specs=[spec(o[0].shape, o[1]) for o in outs],
        name=name, compiler_params=_cparams(("arbitrary" if sequential else "parallel",)))(*[a for a, _ in ins])


def _rms_fwd(h, g, tr=256, after=()):
    S, D = h.shape
    n_after = len(after)

    def body(h_ref, g_ref, *rest):
        u_ref = rest[n_after]
        x = h_ref[...]
        r = lax.rsqrt(jnp.mean(x * x, axis=-1, keepdims=True) + NORM_EPS)
        u_ref[...] = ((x * r) * g_ref[...]).astype(BF16)

    ins = [(h, "row"), (g, "full")] + [(t, "order") for t in after]
    return _rows("rms_fwd", body, ins, [(SDS((S, D), BF16), "row")], S, tr)[0]


def _rms_bwd(h, g, dy, dres, tr=256, after=()):
    S, D = h.shape
    n_after = len(after)

    def body(h_ref, g_ref, dy_ref, dres_ref, *rest):
        dh_ref, dhb_ref, dg_ref = rest[n_after:]
        x = h_ref[...]
        r = lax.rsqrt(jnp.mean(x * x, axis=-1, keepdims=True) + NORM_EPS)
        xhat = x * r
        dyv = dy_ref[...]
        gy = dyv * g_ref[...]
        dx = r * (gy - xhat * jnp.mean(gy * xhat, axis=-1, keepdims=True))
        dh = dres_ref[...] + dx
        dh_ref[...] = dh
        dhb_ref[...] = dh.astype(BF16)
        part = jnp.sum(dyv * xhat, axis=0, keepdims=True)

        @pl.when(pl.program_id(0) == 0)
        def _():
            dg_ref[...] = part

        @pl.when(pl.program_id(0) > 0)
        def _():
            dg_ref[...] += part

    ins = [(h, "row"), (g, "full"), (dy, "row"), (dres, "row")] + [(t, "order") for t in after]
    return _rows("rms_bwd", body, ins,
                 [(SDS((S, D), F32), "row"), (SDS((S, D), BF16), "row"), (SDS((1, D), F32), "full")], S, tr, True)


def _final(h, g, target, tr=256):
    S, D = h.shape

    def body(h_ref, g_ref, t_ref, dh_ref, dg_ref, loss_ref):
        x = h_ref[...]
        r = lax.rsqrt(jnp.mean(x * x, axis=-1, keepdims=True) + NORM_EPS)
        xhat = x * r
        gv = g_ref[...]
        err = xhat * gv - t_ref[...]
        lpart = 0.5 * jnp.sum(jnp.mean(err * err, axis=-1, keepdims=True), axis=0, keepdims=True)
        dyv = err * (1.0 / D)
        gy = dyv * gv
        dh_ref[...] = r * (gy - xhat * jnp.mean(gy * xhat, axis=-1, keepdims=True))
        part = jnp.sum(dyv * xhat, axis=0, keepdims=True)
        lrow = jnp.broadcast_to(lpart, (1, LANES))

        @pl.when(pl.program_id(0) == 0)
        def _():
            dg_ref[...] = part
            loss_ref[...] = lrow

        @pl.when(pl.program_id(0) > 0)
        def _():
            dg_ref[...] += part
            loss_ref[...] += lrow

    return _rows("final_head", body, [(h, "row"), (g, "full"), (target, "row")],
                 [(SDS((S, D), F32), "row"), (SDS((1, D), F32), "full"), (SDS((1, LANES), F32), "full")], S, tr, True)


def _ple_bwd(dh, pg, pp, tr=256, after=()):
    S, D = dh.shape
    n_after = len(after)

    def body(dh_ref, pg_ref, pp_ref, *rest):
        dpp_ref, dpgz_ref = rest[n_after:]
        d, g, q = dh_ref[...], pg_ref[...], pp_ref[...]
        dpp_ref[...] = (d * g).astype(BF16)
        dpgz_ref[...] = (d * q * g * (1.0 - g)).astype(BF16)

    ins = [(dh, "row"), (pg, "row"), (pp, "row")] + [(t, "order") for t in after]
    return _rows("ple_bwd", body, ins, [(SDS((S, D), BF16), "row"), (SDS((S, D), BF16), "row")], S, tr)


def _merge_bwd(dmerged, ya, yb, gates, tr=256):
    S, D = dmerged.shape

    def body(dm_ref, ya_ref, yb_ref, g_ref, dya_ref, dyb_ref, dgz_ref, dbg_ref):
        dm, a, b = dm_ref[...], ya_ref[...], yb_ref[...]
        g1, g2 = g_ref[:, :D], g_ref[:, D:]
        dya_ref[...] = (dm * g1).astype(BF16)
        dyb_ref[...] = (dm * g2).astype(BF16)
        dz1 = dm * a * g1 * (1.0 - g1)
        dz2 = dm * b * g2 * (1.0 - g2)
        dgz_ref[:, :D] = dz1.astype(BF16)
        dgz_ref[:, D:] = dz2.astype(BF16)
        p1 = jnp.sum(dz1, axis=0, keepdims=True)
        p2 = jnp.sum(dz2, axis=0, keepdims=True)

        @pl.when(pl.program_id(0) == 0)
        def _():
            dbg_ref[:, :D] = p1
            dbg_ref[:, D:] = p2

        @pl.when(pl.program_id(0) > 0)
        def _():
            dbg_ref[:, :D] += p1
            dbg_ref[:, D:] += p2

    return _rows("merge_bwd", body, [(dmerged, "row"), (ya, "row"), (yb, "row"), (gates, "row")],
                 [(SDS((S, D), BF16), "row"), (SDS((S, D), BF16), "row"), (SDS((S, 2 * D), BF16), "row"),
                  (SDS((1, 2 * D), F32), "full")], S, tr, True)


def _mix_weights(l0, l1, l2):
    mx = jnp.maximum(jnp.maximum(l0, l1), l2)
    e0, e1, e2 = jnp.exp(l0 - mx), jnp.exp(l1 - mx), jnp.exp(l2 - mx)
    inv = 1.0 / (e0 + e1 + e2)
    return e0 * inv, e1 * inv, e2 * inv


def _mix_fwd(os_, ls_, tr=512):
    S = os_[0].shape[0]

    def body(o0, o1, o2, l0, l1, l2, y_ref):
        w0, w1, w2 = _mix_weights(l0[...], l1[...], l2[...])
        y_ref[...] = (w0 * o0[...] + w1 * o1[...] + w2 * o2[...]).astype(BF16)

    return _rows("mix_fwd", body, [(a, "row") for a in (*os_, *ls_)], [(SDS((S, BRANCH), BF16), "row")], S, tr)[0]


def _mix_bwd(dy, os_, ls_, tr=512, after=()):
    S = dy.shape[0]
    n_after = len(after)

    def head_sums(t):
        return jnp.concatenate(
            [jnp.broadcast_to(jnp.sum(t[:, j * HEAD_DIM:(j + 1) * HEAD_DIM], axis=-1, keepdims=True), (tr, HEAD_DIM))
             for j in range(HEADS_PER_GROUP)], axis=1)

    def body(dy_ref, o0, o1, o2, l0, l1, l2, *rest):
        d0, d1, d2, p0, p1, p2 = rest[n_after:]
        ws = _mix_weights(l0[...], l1[...], l2[...])
        d = dy_ref[...]
        es = [head_sums(d * o[...]) for o in (o0, o1, o2)]
        ebar = ws[0] * es[0] + ws[1] * es[1] + ws[2] * es[2]
        for w, e, d_ref, p_ref in zip(ws, es, (d0, d1, d2), (p0, p1, p2)):
            d_ref[...] = (w * d).astype(BF16)
            p_ref[...] = w * (e - ebar)

    outs = [(SDS((S, BRANCH), BF16), "row")] * 3 + [(SDS((S, BRANCH), F32), "row")] * 3
    ins = [(a, "row") for a in (dy, *os_, *ls_)] + [(t, "order") for t in after]
    return _rows("mix_bwd", body, ins, outs, S, tr)


def _iota2(shape, dim):
    return lax.broadcasted_iota(jnp.int32, shape, dim)


def _fgate_fwd(zf, bf):
    S = zf.shape[0]
    nblk = S // BLOCK

    def body(zf_ref, bf_ref, cb_ref, ct_ref, flt_ref):
        tri = (_iota2((BLOCK, BLOCK), 0) >= _iota2((BLOCK, BLOCK), 1)).astype(F32)
        sel8 = (_iota2((8, LANES), 0) == _iota2((8, LANES), 1)).astype(F32)
        carry = jnp.zeros((1, LANES), F32)
        for blk in range(nblk):
            rows = pl.ds(blk * BLOCK, BLOCK)
            fl = zf_ref[rows, :] + bf_ref[...]
            ls = jnp.minimum(fl, 0.0) - jnp.log(1.0 + jnp.exp(-jnp.abs(fl)))
            c = jnp.dot(tri, ls, precision=HI, preferred_element_type=F32) + carry
            carry = carry + jnp.sum(ls, axis=0, keepdims=True)
            for hd in range(N_FOX):
                e_h = (_iota2((LANES, LANES), 0) == hd).astype(F32)
                cb_ref[rows, hd * LANES:(hd + 1) * LANES] = jnp.dot(c, e_h, precision=HI, preferred_element_type=F32)
            ct_ref[:, rows] = lax.dot_general(sel8, c, NT, precision=HI, preferred_element_type=F32)
            flt_ref[:, rows] = lax.dot_general(sel8, fl, NT, precision=HI, preferred_element_type=F32)

    vm = BS(memory_space=pltpu.VMEM)
    return pl.pallas_call(
        body, out_shape=[SDS((S, N_FOX * LANES), F32), SDS((8, S), F32), SDS((8, S), F32)],
        in_specs=[vm, vm], out_specs=[vm, vm, vm], name="fgate_fwd",
        compiler_params=pltpu.CompilerParams(vmem_limit_bytes=VMEM_LIMIT))(zf, bf)


def _fgate_bwd(dct, flt):
    S = dct.shape[1]
    nblk = S // BLOCK

    def body(dct_ref, flt_ref, dzf_ref, dbf_ref):
        rowid = _iota2((8, BLOCK), 0)
        tri_r = (_iota2((BLOCK, BLOCK), 0) >= _iota2((BLOCK, BLOCK), 1)).astype(F32)
        carry = jnp.zeros((8, 1), F32)
        dbf = jnp.zeros((8, 1), F32)
        for blk in reversed(range(nblk)):
            cols = pl.ds(blk * BLOCK, BLOCK)
            dc = jnp.where(rowid < N_FOX, dct_ref[:, cols], 0.0)
            dls = jnp.dot(dc, tri_r, precision=HI, preferred_element_type=F32) + carry
            carry = carry + jnp.sum(dc, axis=1, keepdims=True)
            dfl = jnp.where(rowid < N_FOX, dls * _sigmoid(-flt_ref[:, cols]), 0.0)
            dbf = dbf + jnp.sum(dfl, axis=1, keepdims=True)
            sq = jnp.concatenate([dfl, jnp.zeros((BLOCK - 8, BLOCK), F32)], axis=0)
            dzf_ref[cols, :] = sq.T.astype(BF16)
        dbf_ref[...] = jnp.broadcast_to(dbf, (8, LANES))

    vm = BS(memory_space=pltpu.VMEM)
    return pl.pallas_call(
        body, out_shape=[SDS((S, LANES), BF16), SDS((8, LANES), F32)],
        in_specs=[vm, vm], out_specs=[vm, vm], name="fgate_bwd",
        compiler_params=pltpu.CompilerParams(vmem_limit_bytes=VMEM_LIMIT))(dct, flt)


def _rope_tables(S):
    inv = ROPE_THETA ** (-jnp.arange(ROPE_HALF, dtype=F32) / ROPE_HALF)
    ang = jnp.arange(S, dtype=F32)[:, None] * inv[None, :]
    cos, sin = jnp.cos(ang), jnp.sin(ang)
    pad = HEAD_DIM - ROPE_DIM
    tc = jnp.concatenate([cos, cos, jnp.ones((S, pad), F32)], axis=1)
    ta = jnp.concatenate([-sin, jnp.zeros((S, HEAD_DIM - ROPE_HALF), F32)], axis=1)
    tb = jnp.concatenate([jnp.zeros((S, ROPE_HALF), F32), sin, jnp.zeros((S, pad), F32)], axis=1)
    return tc, ta, tb


def _rope(x, c, a, b):
    return x * c + pltpu.roll(x, HEAD_DIM - ROPE_HALF, 1) * a + pltpu.roll(x, ROPE_HALF, 1) * b


def _rope_t(g, c, a, b):
    return g * c - pltpu.roll(g, ROPE_HALF, 1) * b - pltpu.roll(g, HEAD_DIM - ROPE_HALF, 1) * a


DIL_TQ = 256


def _dil_extent(i, window):
    return max(0, i * DIL_TQ - window), (i + 1) * DIL_TQ


def _dil_mask(i, lo, ext, window, d):
    diff = (i * DIL_TQ - lo) + _iota2((DIL_TQ, ext), 0) - _iota2((DIL_TQ, ext), 1)
    mask = (diff >= 0) & (diff <= window)
    return mask & ((diff & (d - 1)) == 0) if d > 1 else mask


def _wide(t, w):
    return t if w == LANES else jnp.concatenate([t] * (w // LANES), axis=1)


def _dil_fwd(z3, tabs, grp, S):
    window, d = DIL_GROUPS[grp]
    nq = S // DIL_TQ

    def body(qkv_ref, c_ref, a_ref, b_ref, o_ref, lse_ref, qs, ks):
        c, a, b = c_ref[...], a_ref[...], b_ref[...]
        qs[...] = _rope(qkv_ref[0].astype(F32), c, a, b).astype(BF16)
        ks[...] = _rope(qkv_ref[1].astype(F32), c, a, b).astype(BF16)
        for i in range(nq):
            lo, hi = _dil_extent(i, window)
            rows = pl.ds(i * DIL_TQ, DIL_TQ)
            s = lax.dot_general(qs[rows, :], ks[lo:hi, :], NT, preferred_element_type=F32) * SCALE
            s = jnp.where(_dil_mask(i, lo, hi - lo, window, d), s, -jnp.inf)
            m = jnp.max(s, axis=-1, keepdims=True)
            p = jnp.exp(s - m)
            l = jnp.sum(p, axis=-1, keepdims=True)
            o = jnp.dot(p.astype(BF16), qkv_ref[2, lo:hi, :], preferred_element_type=F32)
            o_ref[rows, :] = o / l
            lse_ref[rows, :] = jnp.broadcast_to(m + jnp.log(l), (DIL_TQ, HEAD_DIM))

    hh0 = grp * HEADS_PER_GROUP
    tspec = BS((S, HEAD_DIM), lambda j: (0, 0))
    ospec = BS((S, HEAD_DIM), lambda j: (0, j))
    return pl.pallas_call(
        body, out_shape=[SDS((S, BRANCH), F32)] * 2, grid=(HEADS_PER_GROUP,),
        in_specs=[BS((3, S, HEAD_DIM), lambda j: (0, 0, hh0 + j)), tspec, tspec, tspec],
        out_specs=[ospec, ospec], scratch_shapes=[pltpu.VMEM((S, HEAD_DIM), BF16)] * 2,
        name="dil_fwd", compiler_params=_cparams(("parallel",)))(z3, *tabs)


def _dil_bwd(z3, tabs, grp, S, do, lse, dpl, dz3):
    window, d = DIL_GROUPS[grp]
    nq = S // DIL_TQ
    n_carry = 0 if dz3 is None else 1

    def body(*refs):
        qkv_ref, c_ref, a_ref, b_ref, do_ref, lse_ref, dpl_ref = refs[:7]
        dz_ref = refs[7 + n_carry]
        qs, ks, dq_acc, dk_acc, dv_acc = refs[8 + n_carry:]
        c, a, b = c_ref[...], a_ref[...], b_ref[...]
        qs[...] = _rope(qkv_ref[0].astype(F32), c, a, b).astype(BF16)
        ks[...] = _rope(qkv_ref[1].astype(F32), c, a, b).astype(BF16)
        dk_acc[...] = jnp.zeros_like(dk_acc)
        dv_acc[...] = jnp.zeros_like(dv_acc)
        for i in range(nq):
            lo, hi = _dil_extent(i, window)
            w = hi - lo
            rows = pl.ds(i * DIL_TQ, DIL_TQ)
            q, k, v = qs[rows, :], ks[lo:hi, :], qkv_ref[2, lo:hi, :]
            s = lax.dot_general(q, k, NT, preferred_element_type=F32) * SCALE
            p = jnp.where(_dil_mask(i, lo, w, window, d), jnp.exp(s - _wide(lse_ref[rows, :], w)), 0.0)
            dob = do_ref[rows, :]
            dp = lax.dot_general(dob, v, NT, preferred_element_type=F32)
            delta = jnp.sum(p * dp, axis=-1, keepdims=True)
            ds = (p * (dp - delta + _wide(dpl_ref[rows, :], w))).astype(BF16)
            dq_acc[rows, :] = jnp.dot(ds, k, preferred_element_type=F32) * SCALE
            dk_acc[lo:hi, :] += lax.dot_general(ds, q, TN, preferred_element_type=F32) * SCALE
            dv_acc[lo:hi, :] += lax.dot_general(p.astype(BF16), dob, TN, preferred_element_type=F32)
        dz_ref[0] = _rope_t(dq_acc[...], c, a, b).astype(BF16)
        dz_ref[1] = _rope_t(dk_acc[...], c, a, b).astype(BF16)
        dz_ref[2] = dv_acc[...].astype(BF16)

    hh0 = grp * HEADS_PER_GROUP
    tspec = BS((S, HEAD_DIM), lambda j: (0, 0))
    bspec = BS((S, HEAD_DIM), lambda j: (0, j))
    zspec = BS((3, S, HEAD_DIM), lambda j: (0, 0, hh0 + j))
    arrays = [z3, *tabs, do, lse, dpl]
    specs = [zspec, tspec, tspec, tspec, bspec, bspec, bspec]
    alias = {}
    if dz3 is not None:
        arrays.append(dz3)
        specs.append(BS(memory_space=pl.ANY))
        alias = {len(arrays) - 1: 0}
    return pl.pallas_call(
        body, out_shape=SDS((3, S, ATTN), BF16), grid=(HEADS_PER_GROUP,), in_specs=specs, out_specs=zspec,
        scratch_shapes=[pltpu.VMEM((S, HEAD_DIM), BF16)] * 2 + [pltpu.VMEM((S, HEAD_DIM), F32)] * 3,
        input_output_aliases=alias, name="dil_bwd", compiler_params=_cparams(("parallel",)))(*arrays)


FOX_TQ = 256


def _fox_scores(q_ref, k_ref, cb_ref, ct_ref, i, ext):
    rows = pl.ds(i * FOX_TQ, FOX_TQ)
    s = lax.dot_general(q_ref[0, rows, :], k_ref[0, 0:ext, :], NT, preferred_element_type=F32) * SCALE
    s = s + _wide(cb_ref[rows, :], ext) - ct_ref[:, 0:ext]
    qpos = i * FOX_TQ + _iota2((FOX_TQ, ext), 0)
    kpos = _iota2((FOX_TQ, ext), 1)
    return jnp.where(kpos <= qpos, s, -jnp.inf)


def _fox_specs(S):
    qs = BS((1, S, HEAD_DIM), lambda h: (0, 0, FOX_HEAD0 + h))
    ks = BS((1, S, HEAD_DIM), lambda h: (1, 0, FOX_HEAD0 + h))
    vs = BS((1, S, HEAD_DIM), lambda h: (2, 0, FOX_HEAD0 + h))
    hb = BS((S, HEAD_DIM), lambda h: (0, h))
    ct = BS((None, 1, S), lambda h: (h, 0, 0))
    return qs, ks, vs, hb, ct


def _fox_fwd(z3, cb, ct, S):
    nq = S // FOX_TQ

    def body(q_ref, k_ref, v_ref, cb_ref, ct_ref, y_ref, lse_ref):
        for i in range(nq):
            ext = (i + 1) * FOX_TQ
            rows = pl.ds(i * FOX_TQ, FOX_TQ)
            s = _fox_scores(q_ref, k_ref, cb_ref, ct_ref, i, ext)
            m = jnp.max(s, axis=-1, keepdims=True)
            p = jnp.exp(s - m)
            l = jnp.sum(p, axis=-1, keepdims=True)
            o = jnp.dot(p.astype(BF16), v_ref[0, 0:ext, :], preferred_element_type=F32)
            y_ref[rows, :] = (o / l).astype(BF16)
            lse_ref[rows, :] = jnp.broadcast_to(m + jnp.log(l), (FOX_TQ, HEAD_DIM))

    qs, ks, vs, hb, cts = _fox_specs(S)
    return pl.pallas_call(
        body, out_shape=[SDS((S, BRANCH), BF16), SDS((S, BRANCH), F32)], grid=(N_FOX,),
        in_specs=[qs, ks, vs, hb, cts], out_specs=[hb, hb], name="fox_fwd",
        compiler_params=_cparams(("parallel",)))(z3, z3, z3, cb, ct.reshape(8, 1, S))


def _fox_bwd(z3, cb, ct, S, dy, lse, dz3):
    nq = S // FOX_TQ

    def body(q_ref, k_ref, v_ref, cb_ref, ct_ref, dy_ref, lse_ref, _, dz_ref, dct_ref, dk_acc, dv_acc):
        dk_acc[...] = jnp.zeros_like(dk_acc)
        dv_acc[...] = jnp.zeros_like(dv_acc)
        dct_ref[...] = jnp.zeros_like(dct_ref)
        for i in range(nq):
            ext = (i + 1) * FOX_TQ
            rows = pl.ds(i * FOX_TQ, FOX_TQ)
            s = _fox_scores(q_ref, k_ref, cb_ref, ct_ref, i, ext)
            p = jnp.exp(s - _wide(lse_ref[rows, :], ext))
            dob = dy_ref[rows, :]
            dp = lax.dot_general(dob, v_ref[0, 0:ext, :], NT, preferred_element_type=F32)
            ds = p * (dp - jnp.sum(p * dp, axis=-1, keepdims=True))
            dsb = ds.astype(BF16)
            dz_ref[0, rows, :] = (jnp.dot(dsb, k_ref[0, 0:ext, :], preferred_element_type=F32) * SCALE).astype(BF16)
            dk_acc[0:ext, :] += lax.dot_general(dsb, q_ref[0, rows, :], TN, preferred_element_type=F32) * SCALE
            dv_acc[0:ext, :] += lax.dot_general(p.astype(BF16), dob, TN, preferred_element_type=F32)
            dct_ref[:, 0:ext] -= jnp.sum(ds, axis=0, keepdims=True)
        dz_ref[1] = dk_acc[...].astype(BF16)
        dz_ref[2] = dv_acc[...].astype(BF16)

    qs, ks, vs, hb, cts = _fox_specs(S)
    zspec = BS((3, S, HEAD_DIM), lambda h: (0, 0, FOX_HEAD0 + h))
    dz, dct = pl.pallas_call(
        body, out_shape=[SDS((3, S, ATTN), BF16), SDS((8, 1, S), F32)], grid=(N_FOX,),
        in_specs=[qs, ks, vs, hb, cts, hb, hb, BS(memory_space=pl.ANY)], out_specs=[zspec, cts],
        scratch_shapes=[pltpu.VMEM((S, HEAD_DIM), F32)] * 2, input_output_aliases={7: 0}, name="fox_bwd",
        compiler_params=_cparams(("parallel",)))(z3, z3, z3, cb, ct.reshape(8, 1, S), dy, lse, dz3)
    return dz, dct.reshape(8, S)


TM = 1024


def _layer_fwd(h, p_bf, sm, W, tabs, after=(), mid=None):
    S, D = h.shape
    qkv_cols = 3 * ATTN
    u = _rms_fwd(h, sm["g_mix"], after=after)
    tn = 512
    per = ATTN // tn
    z3 = _mm_nn("mm_qkv", u, W["in"], [BF16], tm=TM, ncols=qkv_cols, tn=tn, out_shapes=[(3, S, ATTN)],
                out_specs=[BS((None, TM, tn), lambda i, j, k: (j // per, i, j % per))])[0]
    zf = _mm_nn("mm_f", u, W["in"], [F32], tm=TM, joff=qkv_cols // LANES, ncols=LANES, tn=LANES)[0]
    cb, ct, flt = _fgate_fwd(zf, sm["b_f"])
    os_, ls_ = [], []
    for grp in range(len(DIL_GROUPS)):
        o, l = _dil_fwd(z3, tabs, grp, S)
        os_.append(o)
        ls_.append(l)
    ya_h = _mix_fwd(os_, ls_)
    yb_h, lse_fox = _fox_fwd(z3, cb, ct, S)

    tg = _tile(W["gate"].shape[2], 512)
    gates = _mm_nn("mm_gate", u, W["gate"], [F32], tm=TM,
                   extras=[(sm["b_gate"], BS((1, tg), lambda i, j, k: (0, j)))],
                   epilogue=lambda acc, bias: (_sigmoid(acc + bias),))[0]
    ya = _mm_nn("mm_bra", ya_h, W["bra"], [F32], tm=TM)[0]
    tb = _tile(W["brb"].shape[2], 512)
    g2off = D // tb
    yb, merged = _mm_nn(
        "mm_brb", yb_h, W["brb"], [F32, BF16], tm=TM,
        extras=[(ya, BS((TM, tb), lambda i, j, k: (i, j))), (gates, BS((TM, tb), lambda i, j, k: (i, j))),
                (gates, BS((TM, tb), lambda i, j, k: (i, j + g2off)))],
        epilogue=lambda acc, a, g1, g2: (acc, g1 * a + g2 * acc))
    to = _tile(D, 512)
    res_spec = BS((TM, to), lambda i, j, k: (i, j))
    h1 = _mm_nn("mm_o", merged, W["o"], [F32], tm=TM, extras=[(h, res_spec)], epilogue=lambda acc, r: (r + acc,))[0]

    m = _rms_fwd(h1, sm["g_mlp"], after=() if mid is None else mid(h1))

    def up_epi(acc):
        r = jnp.maximum(acc, 0.0)
        return r, r * r

    relu_a, act = _mm_nn("mm_up", m, W["up"], [BF16, BF16], tm=TM, epilogue=up_epi)
    h2 = _mm_nn("mm_down", act, W["down"], [F32], tm=TM, extras=[(h1, res_spec)], epilogue=lambda acc, r: (r + acc,))[0]

    nrm = _rms_fwd(h2, sm["g_ple"])
    pp = _mm_nn("mm_ple", p_bf, W["ple"], [F32], tm=TM)[0]

    def pg_epi(acc, r, q):
        g = _sigmoid(acc)
        return g, r + g * q

    pg, h3 = _mm_nn("mm_pg", nrm, W["pg"], [F32, F32], tm=TM, extras=[(h2, res_spec), (pp, res_spec)], epilogue=pg_epi)
    saved = dict(h=h, u=u, z3=z3, cb=cb, ct=ct, flt=flt, os=os_, ls=ls_, ya_h=ya_h, yb_h=yb_h, lse_fox=lse_fox,
                 gates=gates, ya=ya, yb=yb, merged=merged, h1=h1, m=m, relu_a=relu_a, act=act, h2=h2, nrm=nrm,
                 pg=pg, pp=pp, p_bf=p_bf)
    return h3, saved


def _layer_bwd(dh3, sv, sm, W, tabs, after=(), ready=None):
    S, D = dh3.shape
    dW, dsm = {}, {}

    dpp, dpgz = _ple_bwd(dh3, sv["pg"], sv["pp"], after=after)
    dW["ple"] = _mm_tn("tn_ple", sv["p_bf"], dpp, W["ple"].shape, ts=S)
    dW["pg"] = _mm_tn("tn_pg", sv["nrm"], dpgz, W["pg"].shape, ts=S)
    dn = _mm_nt("nt_pg", dpgz, W["pg"], F32, tm=TM)
    dh2, dh2b, dsm["g_ple"] = _rms_bwd(sv["h2"], sm["g_ple"], dn, dh3)

    ta = _tile(W["up"].shape[1], 512)
    tff = _tile(W["down"].shape[1], 512)
    da = _mm_nt("nt_down", dh2b, W["down"], BF16, tm=TM,
                extras=[(sv["relu_a"], BS((TM, tff), lambda i, ko, cc: (i, ko)))],
                epilogue=lambda acc, r: (acc * (2.0 * r.astype(F32)),))
    dW["down"] = _mm_tn("tn_down", sv["act"], dh2b, W["down"].shape, ts=S)
    dm = _mm_nt("nt_up", da, W["up"], F32, tm=TM)
    dW["up"] = _mm_tn("tn_up", sv["m"], da, W["up"].shape, ts=S)
    dh1, dh1b, dsm["g_mlp"] = _rms_bwd(sv["h1"], sm["g_mlp"], dm, dh2, after=() if ready is None else ready("mlp", dW))

    dmerged = _mm_nt("nt_o", dh1b, W["o"], F32, tm=TM)
    dW["o"] = _mm_tn("tn_o", sv["merged"], dh1b, W["o"].shape, ts=S)
    dya, dyb, dgz, dsm["b_gate"] = _merge_bwd(dmerged, sv["ya"], sv["yb"], sv["gates"])
    dya_h = _mm_nt("nt_bra", dya, W["bra"], F32, tm=TM)
    dyb_h = _mm_nt("nt_brb", dyb, W["brb"], BF16, tm=TM)
    dW["bra"] = _mm_tn("tn_bra", sv["ya_h"], dya, W["bra"].shape, ts=S)
    dW["brb"] = _mm_tn("tn_brb", sv["yb_h"], dyb, W["brb"].shape, ts=S)
    dW["gate"] = _mm_tn("tn_gate", sv["u"], dgz, W["gate"].shape, ts=S)
    du = _mm_nt("nt_gate", dgz, W["gate"], F32, tm=TM)

    dos_, dpls_ = [], []
    mixed = _mix_bwd(dya_h, sv["os"], sv["ls"], after=() if ready is None else ready("merge", dW))
    dos_, dpls_ = mixed[:3], mixed[3:]
    dz3 = None
    for grp in range(len(DIL_GROUPS)):
        dz3 = _dil_bwd(sv["z3"], tabs, grp, S, dos_[grp], sv["ls"][grp], dpls_[grp], dz3)
    dz3, dct = _fox_bwd(sv["z3"], sv["cb"], sv["ct"], S, dyb_h, sv["lse_fox"], dz3)
    dzf, dsm["b_f"] = _fgate_bwd(dct, sv["flt"])

    tn = 512
    per = ATTN // tn
    qkv_cols = 3 * ATTN
    dwin = _mm_tn("tn_qkv", sv["u"], dz3, W["in"].shape, ts=S, ncols=qkv_cols, tn=tn,
                  dy_spec=BS((None, S, tn), lambda ki, j, sc: (j // per, sc, j % per)))
    dW["in"] = _mm_tn("tn_f", sv["u"], dzf, W["in"].shape, ts=S, joff=qkv_cols // LANES, ncols=LANES, tn=LANES,
                      carry=dwin)
    tu = _tile(D, 512)
    prev_spec = BS((TM, tu), lambda i, ko, cc: (i, ko))
    du = _mm_nt("nt_qkv", dz3, W["in"], F32, tm=TM, ncontract=qkv_cols, tc=ATTN,
                dy_spec=BS((None, TM, ATTN), lambda i, ko, cc: (cc, i, 0)),
                extras=[(du, prev_spec)], epilogue=lambda acc, prev: (prev + acc,))
    du = _mm_nt("nt_f", dzf, W["in"], F32, tm=TM, coff=qkv_cols // LANES, ncontract=LANES, tc=LANES,
                extras=[(du, prev_spec)], epilogue=lambda acc, prev: (prev + acc,))
    dh0, _, dsm["g_mix"] = _rms_bwd(sv["h"], sm["g_mix"], du, dh1)
    return dh0, dW, dsm


def _shard_tiles(R, Cp):
    tc = Cp if Cp <= 2048 else _tile(Cp, 1024)
    tr = R
    while tr * tc > 256 * 1024 and tr % 16 == 0:
        tr //= 2
    return tr, tc


def _padded_cols(C):
    return -(-C // LANES) * LANES


def _cast_bf16(name, w, me):
    Lr, R, C = w.shape
    Cp = _padded_cols(C)
    tr, tc = _shard_tiles(R, Cp)

    def body(me_ref, w_ref, *o_refs):
        for l in range(Lr):
            x = w_ref[l]
            if Cp != C:
                col = pl.program_id(1) * tc + _iota2((tr, tc), 1)
                x = jnp.where(col < C, x, 0.0)
            o_refs[l][...] = x.astype(BF16)

    grid_spec = pltpu.PrefetchScalarGridSpec(
        num_scalar_prefetch=1, grid=(R // tr, Cp // tc),
        in_specs=[BS((Lr, tr, tc), lambda i, j, me_ref: (0, i, j))],
        out_specs=[BS((None, tr, tc), lambda i, j, me_ref: (me_ref[0], i, j))] * Lr)
    return pl.pallas_call(
        body, out_shape=[SDS((NDEV, R, Cp), BF16)] * Lr, grid_spec=grid_spec,
        name="cast_" + name, compiler_params=_cparams(("parallel", "parallel")))(me, w)


def _adam_math(w, g, m, v):
    m = ADAM_B1 * m + (1.0 - ADAM_B1) * g
    v = ADAM_B2 * v + (1.0 - ADAM_B2) * (g * g)
    m_hat = m / (1.0 - ADAM_B1 ** ADAM_STEP)
    v_hat = v / (1.0 - ADAM_B2 ** ADAM_STEP)
    delta = -ADAM_LR * (m_hat / (jnp.sqrt(v_hat) + ADAM_EPS) + ADAM_WD * w)
    return delta, m, v


def _adamw(name, layer, w, m, v, recv, own, me, carried):
    Lr, R, C = w.shape
    Cp = recv.shape[2]
    tr, tc = _shard_tiles(R, Cp)
    n_carry = 0 if carried is None else 4

    def body(me_ref, w_ref, m_ref, v_ref, r_ref, own_ref, *rest):
        g_out, d_out, m_out, v_out = rest[n_carry:]
        mine = own_ref[...].astype(F32)
        g = None
        for s in range(NDEV):
            part = jnp.where(me_ref[0] == s, mine, r_ref[s].astype(F32))
            g = part if g is None else g + part
        delta, mn, vn = _adam_math(w_ref[...], g, m_ref[...], v_ref[...])
        g_out[...] = g
        d_out[...] = delta
        m_out[...] = mn
        v_out[...] = vn

    spec = BS((None, tr, tc), lambda i, j, me_ref: (layer, i, j))
    rspec = BS((NDEV, tr, tc), lambda i, j, me_ref: (0, i, j))
    ospec = BS((None, tr, tc), lambda i, j, me_ref: (me_ref[0], i, j))
    arrays, specs, alias = [w, m, v, recv, own], [spec, spec, spec, rspec, ospec], {}
    if carried is not None:
        arrays += list(carried)
        specs += [BS(memory_space=pl.ANY)] * 4
        alias = {6 + k: k for k in range(4)}
    grid_spec = pltpu.PrefetchScalarGridSpec(
        num_scalar_prefetch=1, grid=(R // tr, Cp // tc), in_specs=specs, out_specs=[spec] * 4)
    return pl.pallas_call(
        body, out_shape=[SDS(w.shape, F32)] * 4, grid_spec=grid_spec, input_output_aliases=alias,
        name="adamw_" + name, compiler_params=_cparams(("parallel", "parallel")))(me, *arrays)


def _adamw_small(w, g, m, v):
    vm = BS(memory_space=pltpu.VMEM)

    def body(w_ref, g_ref, m_ref, v_ref, d_out, m_out, v_out):
        delta, mn, vn = _adam_math(w_ref[...], g_ref[...], m_ref[...], v_ref[...])
        d_out[...] = delta
        m_out[...] = mn
        v_out[...] = vn

    return pl.pallas_call(body, out_shape=[SDS(w.shape, F32)] * 3, in_specs=[vm] * 4, out_specs=[vm] * 3,
                          name="adamw_small")(w, g, m, v)


def _place():
    return lax.axis_index("x"), lax.axis_index("y"), lax.axis_index("c")


def _slot(px, py, pc):
    return 4 * px + 2 * py + pc


def _other_chips(x, y):
    return [(1 - x, y), (x, 1 - y), (1 - x, 1 - y)]


def _peer(x, y, c, rel):
    return (1 - x if rel & 4 else x, 1 - y if rel & 2 else y, 1 - c if rel & 1 else c)


def _plan_gather_ici(src, land):
    x, y, c = _place()
    own = land.at[_slot(x, y, c)]
    return [(own, own, (x, y, 1 - c))] + [(own, own, (*chip, c)) for chip in _other_chips(x, y)]


def _plan_gather_relay(src, land):
    x, y, c = _place()
    return [(land.at[_slot(*chip, c)], land.at[_slot(*chip, c)], (x, y, 1 - c)) for chip in _other_chips(x, y)]


def _plan_scatter(src, land):
    x, y, c = _place()
    me = _slot(x, y, c)
    peers = [_peer(x, y, c, rel) for rel in range(1, NDEV)]
    return [(src.at[_slot(*peer)], land.at[me], peer) for peer in peers]


EFFECT = pltpu.SideEffectType.DATAFLOW_SIDE_EFFECTING
HBM_SPEC = BS(memory_space=pltpu.HBM)
SEM_SPEC = BS(memory_space=pltpu.SEMAPHORE)
ORDER_SPEC = BS(memory_space=pl.ANY)


def _exchange_start(name, plan, ncopy, lands, srcs=None, after=()):
    bufs = ([] if srcs is None else list(srcs)) + list(lands)
    n, nb, ns = len(lands), len(bufs), len(bufs) - len(lands)
    n_after = len(after)

    def body(*refs):
        src_refs = refs[:ns] if ns else [None] * n
        land_refs = refs[ns:nb]
        send_sems, recv_sems = refs[nb + n_after], refs[nb + n_after + 1]
        token = refs[-1]
        for i in range(n):
            for k, (src, dst, to) in enumerate(plan(src_refs[i], land_refs[i])):
                pltpu.make_async_remote_copy(src_ref=src, dst_ref=dst, send_sem=send_sems.at[i * ncopy + k],
                                             recv_sem=recv_sems.at[i * ncopy + k], device_id=to,
                                             device_id_type=MESH).start()
        token[...] = jnp.zeros_like(token)

    out_shape = [pltpu.SemaphoreType.DMA((n * ncopy,)), pltpu.SemaphoreType.DMA((n * ncopy,))]
    out_shape += [pltpu.HBM(b.shape, b.dtype) for b in bufs] + [SDS((8, LANES), F32)]
    out = pl.pallas_call(
        body, name=name, out_shape=out_shape, in_specs=[HBM_SPEC] * nb + [ORDER_SPEC] * n_after,
        out_specs=[SEM_SPEC, SEM_SPEC] + [HBM_SPEC] * nb + [BS(memory_space=pltpu.VMEM)],
        input_output_aliases={i: 2 + i for i in range(nb)},
        compiler_params=pltpu.CompilerParams(has_side_effects=EFFECT))(
            *[pltpu.with_memory_space_constraint(b, pltpu.HBM) for b in bufs], *after)
    return out[0], out[1], out[2:2 + ns], out[2 + ns:2 + nb], out[-1]


def _exchange_wait(name, plan, send_sems, recv_sems, lands, srcs=None, after=()):
    bufs = ([] if srcs is None else list(srcs)) + list(lands)
    n, nb, ns = len(lands), len(bufs), len(bufs) - len(lands)

    def body(*refs):
        src_refs = refs[:ns] if ns else [None] * n
        land_refs = refs[ns:nb]
        send_ref, recv_ref = refs[nb], refs[nb + 1]
        for i in range(n):
            copies = plan(src_refs[i], land_refs[i])
            for k, (src, dst, to) in enumerate(copies):
                cp = pltpu.make_async_remote_copy(src_ref=src, dst_ref=dst, send_sem=send_ref.at[i * len(copies) + k],
                                                  recv_sem=recv_ref.at[i * len(copies) + k], device_id=to,
                                                  device_id_type=MESH)
                cp.wait_send()
                cp.wait_recv()

    out = pl.pallas_call(
        body, name=name, out_shape=[pltpu.HBM(b.shape, b.dtype) for b in bufs],
        in_specs=[HBM_SPEC] * nb + [SEM_SPEC, SEM_SPEC] + [ORDER_SPEC] * len(after), out_specs=[HBM_SPEC] * nb,
        input_output_aliases={i: i for i in range(nb)},
        compiler_params=pltpu.CompilerParams(has_side_effects=EFFECT))(*bufs, send_sems, recv_sems, *after)
    return out[:ns], out[ns:]


def _allreduce_small(v, after=()):
    R, Wd = v.shape
    n_after = len(after)

    def body(v_ref, *rest):
        o_ref, buf, send_sems, recv_sems = rest[n_after:]
        x, y, c = _place()
        me = _slot(x, y, c)
        buf[me] = v_ref[...]
        started = []
        for rel in range(1, NDEV):
            peer = (1 - x if rel & 4 else x, 1 - y if rel & 2 else y, 1 - c if rel & 1 else c)
            cp = pltpu.make_async_remote_copy(
                src_ref=v_ref, dst_ref=buf.at[me], send_sem=send_sems.at[rel - 1], recv_sem=recv_sems.at[rel - 1],
                device_id=peer, device_id_type=MESH)
            cp.start()
            started.append(cp)
        for cp in started:
            cp.wait()
        acc = buf[0]
        for j in range(1, NDEV):
            acc = acc + buf[j]
        o_ref[...] = acc

    vm = BS(memory_space=pltpu.VMEM)
    return pl.pallas_call(
        body, out_shape=SDS((R, Wd), F32), in_specs=[vm] + [ORDER_SPEC] * n_after, out_specs=vm,
        scratch_shapes=[pltpu.VMEM((NDEV, R, Wd), F32), pltpu.SemaphoreType.DMA((7,)), pltpu.SemaphoreType.DMA((7,))],
        name="allreduce_small")(v, *after)


def _as_matrix(name, g):
    if name in ROW_SHARDED:
        return g.reshape(1, NDEV * g.shape[1], g.shape[2])
    return g


def _pad8(a):
    return jnp.pad(a, ((0, -a.shape[0] % 8), (0, 0)))


def _pack_small(g_mix, g_mlp, g_ple, b_gate, g_final, b_f, extra_row=None):
    Lr, D = g_mix.shape
    rows = [g_mix, g_mlp, g_ple, b_gate.reshape(2 * Lr, D), g_final.reshape(1, D),
            jnp.pad(b_f.reshape(1, -1), ((0, 0), (0, D - b_f.size)))]
    rows.append(jnp.zeros((1, D), F32) if extra_row is None else extra_row)
    return jnp.concatenate([_pad8(r) for r in rows], axis=0)


def _unpack_small(pk, Lr, D, nf):
    o = 0
    out = []
    for rows, shape in ((Lr, (Lr, D)), (Lr, (Lr, D)), (Lr, (Lr, D)), (2 * Lr, (Lr, 2 * D)), (1, (D,))):
        out.append(pk[o:o + rows].reshape(shape))
        o += rows + (-rows % 8)
    out.append(pk[o, :Lr * nf].reshape(Lr, nf))
    return out, pk[o + 8, 0]


def kernel(x, p, g_mix, w_in, b_f, w_gate, b_gate, w_br_a, w_br_b, w_o, g_mlp, w_up, w_down, g_ple, w_ple, w_ple_gate, g_final, loss_target, m_g_mix, m_w_in, m_b_f, m_w_gate, m_b_gate, m_w_br_a, m_w_br_b, m_w_o, m_g_mlp, m_w_up, m_w_down, m_g_ple, m_w_ple, m_w_ple_gate, m_g_final, v_g_mix, v_w_in, v_b_f, v_w_gate, v_b_gate, v_w_br_a, v_w_br_b, v_w_o, v_g_mlp, v_w_up, v_w_down, v_g_ple, v_w_ple, v_w_ple_gate, v_g_final):
    Lr, D = g_mix.shape
    S = x.shape[1]
    nf = b_f.shape[1]
    big_w = dict(zip(BIG, (w_in, w_gate, w_br_a, w_br_b, w_o, w_up, w_down, w_ple, w_ple_gate)))
    big_m = dict(zip(BIG, (m_w_in, m_w_gate, m_w_br_a, m_w_br_b, m_w_o, m_w_up, m_w_down, m_w_ple, m_w_ple_gate)))
    big_v = dict(zip(BIG, (v_w_in, v_w_gate, v_w_br_a, v_w_br_b, v_w_o, v_w_up, v_w_down, v_w_ple, v_w_ple_gate)))

    me = _slot(*_place()).astype(jnp.int32).reshape(1)
    wbf = {k: _cast_bf16(k, big_w[k], me) for k in BIG}
    tabs = _rope_tables(S)
    bf_pad = jnp.pad(b_f, ((0, 0), (0, LANES - nf)))

    smalls = [dict(g_mix=g_mix[l:l + 1], b_f=bf_pad[l:l + 1], b_gate=b_gate[l:l + 1], g_mlp=g_mlp[l:l + 1],
                   g_ple=g_ple[l:l + 1]) for l in range(Lr)]

    def links_start(l, after=()):
        return _exchange_start(f"gather_ici_start_{l}", _plan_gather_ici, 4, [wbf[k][l] for k in BIG], None, after)

    def links_to_relay(l, on_links, after):
        send_sems, recv_sems, _, lands, _ = on_links
        _, lands = _exchange_wait(f"gather_ici_wait_{l}", _plan_gather_ici, send_sems, recv_sems, lands, None, after)
        relay = _exchange_start(f"gather_relay_start_{l}", _plan_gather_relay, 3, lands)
        return relay, (links_start(l + 1, (relay[4],)) if l + 1 < Lr else None)

    def relay_done(l, relay):
        send_sems, recv_sems, _, lands, _ = relay
        _, lands = _exchange_wait(f"gather_relay_wait_{l}", _plan_gather_relay, send_sems, recv_sems, lands)
        return {k: _as_matrix(k, g) for k, g in zip(BIG, lands)}

    h = x.reshape(S, D)
    Ws, saves = [], []
    relay, on_links = links_to_relay(0, links_start(0), (h,))
    for l in range(Lr):
        Ws.append(relay_done(l, relay))
        nxt = {}

        def mid(h1, l=l, on_links=on_links, nxt=nxt):
            if on_links is None:
                return ()
            nxt["relay"], nxt["on_links"] = links_to_relay(l + 1, on_links, (h1,))
            return tuple(t[4] for t in (nxt["relay"], nxt["on_links"]) if t is not None)

        h, sv = _layer_fwd(h, p[l, 0].astype(BF16), smalls[l], Ws[l], tabs,
                           () if on_links is None else (on_links[4],), mid)
        saves.append(sv)
        relay, on_links = nxt.get("relay"), nxt.get("on_links")

    dh, dg_final, loss_part = _final(h, g_final.reshape(1, D), loss_target.reshape(S, D))

    big_out = {k: None for k in BIG}
    dsmalls = [None] * Lr
    groups = dict(mlp=("ple", "pg", "down", "up"), merge=("o", "bra", "brb", "gate"), proj=("in",))

    def scatter_start(l, names, dW):
        dws = [dW[k].reshape(wbf[k][l].shape) for k in names]
        lands = [lax.empty(d.shape, BF16) for d in dws]
        return (l, names) + tuple(_exchange_start(f"scatter_start_{l}_{names[0]}", _plan_scatter, 7, lands, dws))

    def scatter_finish(started, after):
        l, names, send_sems, recv_sems, dws, lands, _ = started
        dws, landed = _exchange_wait(f"scatter_wait_{l}_{names[0]}", _plan_scatter, send_sems, recv_sems, lands, dws,
                                     after)
        for k, own, recv in zip(names, dws, landed):
            big_out[k] = _adamw(k, l, big_w[k], big_m[k], big_v[k], recv, own, me, big_out[k])
        return landed[0]

    pending, after, last = [], (), None
    for l in reversed(range(Lr)):
        early = []

        def ready(stage, dW, l=l, early=early):
            early.append(scatter_start(l, groups[stage], dW))
            return (early[-1][6],)

        dh, dW, dsmalls[l] = _layer_bwd(dh, saves[l], smalls[l], Ws[l], tabs, after, ready)
        proj = scatter_start(l, groups["proj"], dW)
        for started in pending:
            last = scatter_finish(started, (dh,))
        pending, after = early + [proj], (proj[6],)
    for started in pending:
        last = scatter_finish(started, ())

    cat = lambda key: jnp.concatenate([_pad8(d[key]) for d in dsmalls], axis=0)[::8]
    dbf = jnp.concatenate([d["b_f"] for d in dsmalls], axis=0).reshape(Lr, 8, LANES)[:, :nf, 0]
    loss_row = jnp.concatenate([loss_part, g_final.reshape(1, D)[:, LANES:]], axis=1)
    g_pack = _allreduce_small(_pack_small(cat("g_mix"), cat("g_mlp"), cat("g_ple"), cat("b_gate"), dg_final, dbf,
                                          loss_row), (last,))
    w_pack = _pack_small(g_mix, g_mlp, g_ple, b_gate, g_final, b_f)
    m_pack = _pack_small(m_g_mix, m_g_mlp, m_g_ple, m_b_gate, m_g_final, m_b_f)
    v_pack = _pack_small(v_g_mix, v_g_mlp, v_g_ple, v_b_gate, v_g_final, v_b_f)
    d_pack, mn_pack, vn_pack = _adamw_small(w_pack, g_pack, m_pack, v_pack)

    small_names = ("g_mix", "g_mlp", "g_ple", "b_gate", "g_final", "b_f")
    sg, loss = _unpack_small(g_pack, Lr, D, nf)
    small_out = {}
    for kind, pk in (("grad", None), ("delta", d_pack), ("m", mn_pack), ("v", vn_pack)):
        vals = sg if pk is None else _unpack_small(pk, Lr, D, nf)[0]
        small_out[kind] = dict(zip(small_names, vals))

    order = ("g_mix", "in", "b_f", "gate", "b_gate", "bra", "brb", "o", "g_mlp", "up", "down", "g_ple", "ple", "pg",
             "g_final")
    outs = [loss, dh.reshape(x.shape)]
    for idx, kind in enumerate(("grad", "delta", "m", "v")):
        for name in order:
            outs.append(big_out[name][idx] if name in big_out else small_out[kind][name])
    return tuple(outs)
```

```python
import jax
import jax.numpy as jnp
from jax import lax
from jax.experimental import pallas as pl
from jax.experimental.pallas import tpu as pltpu

F32, BF16 = jnp.float32, jnp.bfloat16
SDS = jax.ShapeDtypeStruct
BS = pl.BlockSpec
MESH = pl.DeviceIdType.MESH
HI = lax.Precision.HIGHEST

HEAD_DIM = 128
N_HEADS = 16
ATTN = N_HEADS * HEAD_DIM
DIL_GROUPS = ((128, 1), (512, 4), (2048, 16))
HEADS_PER_GROUP = 4
FOX_HEAD0 = 12
N_FOX = 4
BRANCH = HEADS_PER_GROUP * HEAD_DIM
BLOCK = 128
ROPE_DIM = HEAD_DIM // 4
ROPE_HALF = ROPE_DIM // 2
ROPE_THETA = 500000.0
NORM_EPS = 1e-6
SCALE = HEAD_DIM ** -0.5
NDEV = 8
LANES = 128

ADAM_LR, ADAM_B1, ADAM_B2, ADAM_EPS, ADAM_WD, ADAM_STEP = 0.001, 0.9, 0.999, 1e-08, 0.01, 10

V7X_VMEM_BYTES = 64 * 1024 * 1024
VMEM_LIMIT = (V7X_VMEM_BYTES * 3) // 4

NN = (((1,), (0,)), ((), ()))
NT = (((1,), (1,)), ((), ()))
TN = (((0,), (0,)), ((), ()))

BIG = ("in", "gate", "bra", "brb", "o", "up", "down", "ple", "pg")
ROW_SHARDED = ("in", "o", "down", "pg")


def _tile(n, pref):
    t = min(n, pref)
    while n % t:
        t -= LANES
    return t


def _cparams(sem):
    return pltpu.CompilerParams(dimension_semantics=sem, vmem_limit_bytes=VMEM_LIMIT)


def _sigmoid(x):
    return 1.0 / (1.0 + jnp.exp(-x))


def _mm(name, grid, a, a_spec, b, b_spec, dims, outs, acc_shape, extras=(), epilogue=None, carry=None):
    nk = grid[2]
    n_ex, n_out = len(extras), len(outs)
    n_carry = 0 if carry is None else 1

    def body(*refs):
        a_ref, b_ref = refs[0], refs[1]
        ex_refs = refs[2:2 + n_ex]
        out_refs = refs[2 + n_ex + n_carry:2 + n_ex + n_carry + n_out]

        def product():
            return lax.dot_general(a_ref[...], b_ref[...], dims, preferred_element_type=F32)

        def finish(acc):
            res = (acc,) if epilogue is None else epilogue(acc, *[e[...] for e in ex_refs])
            for o_ref, r in zip(out_refs, res):
                o_ref[...] = r.astype(o_ref.dtype)

        if nk == 1:
            finish(product())
            return
        acc_ref = refs[-1]
        k = pl.program_id(2)

        @pl.when(k == 0)
        def _():
            acc_ref[...] = jnp.zeros_like(acc_ref)

        acc_ref[...] += product()

        @pl.when(k == nk - 1)
        def _():
            finish(acc_ref[...])

    arrays = [a, b] + [e[0] for e in extras]
    specs = [a_spec, b_spec] + [e[1] for e in extras]
    alias = {}
    if carry is not None:
        arrays.append(carry[0])
        specs.append(BS(memory_space=pl.ANY))
        alias = {len(arrays) - 1: carry[1]}
    return pl.pallas_call(
        body, out_shape=[o[0] for o in outs], grid=grid, in_specs=specs, out_specs=[o[1] for o in outs],
        scratch_shapes=[pltpu.VMEM(acc_shape, F32)] if nk > 1 else [], input_output_aliases=alias, name=name,
        compiler_params=_cparams(("parallel", "parallel", "arbitrary")))(*arrays)


def _mm_nn(name, a, w, out_dtypes, *, tm, joff=0, ncols=None, tn=None, extras=(), epilogue=None, out_specs=None,
           out_shapes=None):
    M, K = a.shape
    J, _, n = w.shape
    tn = tn or _tile(n, 512)
    tk = _tile(K, 2048)
    per = n // tn
    ncols = ncols or J * n
    grid = (M // tm, ncols // tn, K // tk)
    a_spec = BS((tm, tk), lambda i, j, k: (i, k))
    w_spec = BS((None, tk, tn), lambda i, j, k: ((j + joff) // per, k, (j + joff) % per))
    if out_specs is None:
        out_specs = [BS((tm, tn), lambda i, j, k: (i, j))] * len(out_dtypes)
        out_shapes = [(M, ncols)] * len(out_dtypes)
    outs = [(SDS(s, d), sp) for s, d, sp in zip(out_shapes, out_dtypes, out_specs)]
    return _mm(name, grid, a, a_spec, w, w_spec, NN, outs, (tm, tn), extras, epilogue)


def _mm_nt(name, dy, w, out_dtype, *, tm, coff=0, ncontract=None, tc=None, dy_spec=None, extras=(), epilogue=None):
    J, K, n = w.shape
    M = dy.shape[-2]
    tc = tc or _tile(n, 2048)
    tko = _tile(K, 512)
    per = n // tc
    ncontract = ncontract or J * n
    grid = (M // tm, K // tko, ncontract // tc)
    if dy_spec is None:
        dy_spec = BS((tm, tc), lambda i, ko, cc: (i, cc))
    w_spec = BS((None, tko, tc), lambda i, ko, cc: ((cc + coff) // per, ko, (cc + coff) % per))
    outs = [(SDS((M, K), out_dtype), BS((tm, tko), lambda i, ko, cc: (i, ko)))]
    return _mm(name, grid, dy, dy_spec, w, w_spec, NT, outs, (tm, tko), extras, epilogue)[0]


def _mm_tn(name, x, dy, wshape, *, ts, joff=0, ncols=None, tn=None, dy_spec=None, carry=None):
    J, K, n = wshape
    S = x.shape[0]
    tn = tn or _tile(n, 1024)
    tkm = _tile(K, 512)
    per = n // tn
    ncols = ncols or J * n
    grid = (K // tkm, ncols // tn, S // ts)
    x_spec = BS((ts, tkm), lambda ki, j, sc: (sc, ki))
    if dy_spec is None:
        dy_spec = BS((ts, tn), lambda ki, j, sc: (sc, j))
    outs = [(SDS(wshape, BF16), BS((None, tkm, tn), lambda ki, j, sc: ((j + joff) // per, ki, (j + joff) % per)))]
    return _mm(name, grid, x, x_spec, dy, dy_spec, TN, outs, (tkm, tn), carry=None if carry is None else (carry, 0))[0]


def _rows(name, body, ins, outs, S, tr, sequential=False):
    def spec(shape, kind):
        if kind == "row":
            return BS((tr, shape[1]), lambda i: (i, 0))
        if kind == "order":
            return BS(memory_space=pl.ANY)
        return BS(tuple(shape), lambda i: (0,) * len(shape))
    return pl.pallas_call(
        body, out_shape=[o[0] for o in outs], grid=(S // tr,),
        in_specs=[spec(a.shape, k) for a, k in ins], out_specs=[spec(o[0].shape, o[1]) for o in outs],
        name=name, compiler_params=_cparams(("arbitrary" if sequential else "parallel",)))(*[a for a, _ in ins])


def _rms_fwd(h, g, tr=256, after=()):
    S, D = h.shape
    n_after = len(after)

    def body(h_ref, g_ref, *rest):
        u_ref = rest[n_after]
        x = h_ref[...]
        r = lax.rsqrt(jnp.mean(x * x, axis=-1, keepdims=True) + NORM_EPS)
        u_ref[...] = ((x * r) * g_ref[...]).astype(BF16)

    ins = [(h, "row"), (g, "full")] + [(t, "order") for t in after]
    return _rows("rms_fwd", body, ins, [(SDS((S, D), BF16), "row")], S, tr)[0]


def _rms_bwd(h, g, dy, dres, tr=256, after=()):
    S, D = h.shape
    n_after = len(after)

    def body(h_ref, g_ref, dy_ref, dres_ref, *rest):
        dh_ref, dhb_ref, dg_ref = rest[n_after:]
        x = h_ref[...]
        r = lax.rsqrt(jnp.mean(x * x, axis=-1, keepdims=True) + NORM_EPS)
        xhat = x * r
        dyv = dy_ref[...]
        gy = dyv * g_ref[...]
        dx = r * (gy - xhat * jnp.mean(gy * xhat, axis=-1, keepdims=True))
        dh = dres_ref[...] + dx
        dh_ref[...] = dh
        dhb_ref[...] = dh.astype(BF16)
        part = jnp.sum(dyv * xhat, axis=0, keepdims=True)

        @pl.when(pl.program_id(0) == 0)
        def _():
            dg_ref[...] = part

        @pl.when(pl.program_id(0) > 0)
        def _():
            dg_ref[...] += part

    ins = [(h, "row"), (g, "full"), (dy, "row"), (dres, "row")] + [(t, "order") for t in after]
    return _rows("rms_bwd", body, ins,
                 [(SDS((S, D), F32), "row"), (SDS((S, D), BF16), "row"), (SDS((1, D), F32), "full")], S, tr, True)


def _final(h, g, target, tr=256):
    S, D = h.shape

    def body(h_ref, g_ref, t_ref, dh_ref, dg_ref, loss_ref):
        x = h_ref[...]
        r = lax.rsqrt(jnp.mean(x * x, axis=-1, keepdims=True) + NORM_EPS)
        xhat = x * r
        gv = g_ref[...]
        err = xhat * gv - t_ref[...]
        lpart = 0.5 * jnp.sum(jnp.mean(err * err, axis=-1, keepdims=True), axis=0, keepdims=True)
        dyv = err * (1.0 / D)
        gy = dyv * gv
        dh_ref[...] = r * (gy - xhat * jnp.mean(gy * xhat, axis=-1, keepdims=True))
        part = jnp.sum(dyv * xhat, axis=0, keepdims=True)
        lrow = jnp.broadcast_to(lpart, (1, LANES))

        @pl.when(pl.program_id(0) == 0)
        def _():
            dg_ref[...] = part
            loss_ref[...] = lrow

        @pl.when(pl.program_id(0) > 0)
        def _():
            dg_ref[...] += part
            loss_ref[...] += lrow

    return _rows("final_head", body, [(h, "row"), (g, "full"), (target, "row")],
                 [(SDS((S, D), F32), "row"), (SDS((1, D), F32), "full"), (SDS((1, LANES), F32), "full")], S, tr, True)


def _ple_bwd(dh, pg, pp, tr=256, after=()):
    S, D = dh.shape
    n_after = len(after)

    def body(dh_ref, pg_ref, pp_ref, *rest):
        dpp_ref, dpgz_ref = rest[n_after:]
        d, g, q = dh_ref[...], pg_ref[...], pp_ref[...]
        dpp_ref[...] = (d * g).astype(BF16)
        dpgz_ref[...] = (d * q * g * (1.0 - g)).astype(BF16)

    ins = [(dh, "row"), (pg, "row"), (pp, "row")] + [(t, "order") for t in after]
    return _rows("ple_bwd", body, ins, [(SDS((S, D), BF16), "row"), (SDS((S, D), BF16), "row")], S, tr)


def _merge_bwd(dmerged, ya, yb, gates, tr=256):
    S, D = dmerged.shape

    def body(dm_ref, ya_ref, yb_ref, g_ref, dya_ref, dyb_ref, dgz_ref, dbg_ref):
        dm, a, b = dm_ref[...], ya_ref[...], yb_ref[...]
        g1, g2 = g_ref[:, :D], g_ref[:, D:]
        dya_ref[...] = (dm * g1).astype(BF16)
        dyb_ref[...] = (dm * g2).astype(BF16)
        dz1 = dm * a * g1 * (1.0 - g1)
        dz2 = dm * b * g2 * (1.0 - g2)
        dgz_ref[:, :D] = dz1.astype(BF16)
        dgz_ref[:, D:] = dz2.astype(BF16)
        p1 = jnp.sum(dz1, axis=0, keepdims=True)
        p2 = jnp.sum(dz2, axis=0, keepdims=True)

        @pl.when(pl.program_id(0) == 0)
        def _():
            dbg_ref[:, :D] = p1
            dbg_ref[:, D:] = p2

        @pl.when(pl.program_id(0) > 0)
        def _():
            dbg_ref[:, :D] += p1
            dbg_ref[:, D:] += p2

    return _rows("merge_bwd", body, [(dmerged, "row"), (ya, "row"), (yb, "row"), (gates, "row")],
                 [(SDS((S, D), BF16), "row"), (SDS((S, D), BF16), "row"), (SDS((S, 2 * D), BF16), "row"),
                  (SDS((1, 2 * D), F32), "full")], S, tr, True)


def _mix_weights(l0, l1, l2):
    mx = jnp.maximum(jnp.maximum(l0, l1), l2)
    e0, e1, e2 = jnp.exp(l0 - mx), jnp.exp(l1 - mx), jnp.exp(l2 - mx)
    inv = 1.0 / (e0 + e1 + e2)
    return e0 * inv, e1 * inv, e2 * inv


def _mix_fwd(os_, ls_, tr=512):
    S = os_[0].shape[0]

    def body(o0, o1, o2, l0, l1, l2, y_ref):
        w0, w1, w2 = _mix_weights(l0[...], l1[...], l2[...])
        y_ref[...] = (w0 * o0[...] + w1 * o1[...] + w2 * o2[...]).astype(BF16)

    return _rows("mix_fwd", body, [(a, "row") for a in (*os_, *ls_)], [(SDS((S, BRANCH), BF16), "row")], S, tr)[0]


def _mix_bwd(dy, os_, ls_, tr=512, after=()):
    S = dy.shape[0]
    n_after = len(after)

    def head_sums(t):
        return jnp.concatenate(
            [jnp.broadcast_to(jnp.sum(t[:, j * HEAD_DIM:(j + 1) * HEAD_DIM], axis=-1, keepdims=True), (tr, HEAD_DIM))
             for j in range(HEADS_PER_GROUP)], axis=1)

    def body(dy_ref, o0, o1, o2, l0, l1, l2, *rest):
        d0, d1, d2, p0, p1, p2 = rest[n_after:]
        ws = _mix_weights(l0[...], l1[...], l2[...])
        d = dy_ref[...]
        es = [head_sums(d * o[...]) for o in (o0, o1, o2)]
        ebar = ws[0] * es[0] + ws[1] * es[1] + ws[2] * es[2]
        for w, e, d_ref, p_ref in zip(ws, es, (d0, d1, d2), (p0, p1, p2)):
            d_ref[...] = (w * d).astype(BF16)
            p_ref[...] = w * (e - ebar)

    outs = [(SDS((S, BRANCH), BF16), "row")] * 3 + [(SDS((S, BRANCH), F32), "row")] * 3
    ins = [(a, "row") for a in (dy, *os_, *ls_)] + [(t, "order") for t in after]
    return _rows("mix_bwd", body, ins, outs, S, tr)


def _iota2(shape, dim):
    return lax.broadcasted_iota(jnp.int32, shape, dim)


def _fgate_fwd(zf, bf):
    S = zf.shape[0]
    nblk = S // BLOCK

    def body(zf_ref, bf_ref, cb_ref, ct_ref, flt_ref):
        tri = (_iota2((BLOCK, BLOCK), 0) >= _iota2((BLOCK, BLOCK), 1)).astype(F32)
        sel8 = (_iota2((8, LANES), 0) == _iota2((8, LANES), 1)).astype(F32)
        carry = jnp.zeros((1, LANES), F32)
        for blk in range(nblk):
            rows = pl.ds(blk * BLOCK, BLOCK)
            fl = zf_ref[rows, :] + bf_ref[...]
            ls = jnp.minimum(fl, 0.0) - jnp.log(1.0 + jnp.exp(-jnp.abs(fl)))
            c = jnp.dot(tri, ls, precision=HI, preferred_element_type=F32) + carry
            carry = carry + jnp.sum(ls, axis=0, keepdims=True)
            for hd in range(N_FOX):
                e_h = (_iota2((LANES, LANES), 0) == hd).astype(F32)
                cb_ref[rows, hd * LANES:(hd + 1) * LANES] = jnp.dot(c, e_h, precision=HI, preferred_element_type=F32)
            ct_ref[:, rows] = lax.dot_general(sel8, c, NT, precision=HI, preferred_element_type=F32)
            flt_ref[:, rows] = lax.dot_general(sel8, fl, NT, precision=HI, preferred_element_type=F32)

    vm = BS(memory_space=pltpu.VMEM)
    return pl.pallas_call(
        body, out_shape=[SDS((S, N_FOX * LANES), F32), SDS((8, S), F32), SDS((8, S), F32)],
        in_specs=[vm, vm], out_specs=[vm, vm, vm], name="fgate_fwd",
        compiler_params=pltpu.CompilerParams(vmem_limit_bytes=VMEM_LIMIT))(zf, bf)


def _fgate_bwd(dct, flt):
    S = dct.shape[1]
    nblk = S // BLOCK

    def body(dct_ref, flt_ref, dzf_ref, dbf_ref):
        rowid = _iota2((8, BLOCK), 0)
        tri_r = (_iota2((BLOCK, BLOCK), 0) >= _iota2((BLOCK, BLOCK), 1)).astype(F32)
        carry = jnp.zeros((8, 1), F32)
        dbf = jnp.zeros((8, 1), F32)
        for blk in reversed(range(nblk)):
            cols = pl.ds(blk * BLOCK, BLOCK)
            dc = jnp.where(rowid < N_FOX, dct_ref[:, cols], 0.0)
            dls = jnp.dot(dc, tri_r, precision=HI, preferred_element_type=F32) + carry
            carry = carry + jnp.sum(dc, axis=1, keepdims=True)
            dfl = jnp.where(rowid < N_FOX, dls * _sigmoid(-flt_ref[:, cols]), 0.0)
            dbf = dbf + jnp.sum(dfl, axis=1, keepdims=True)
            sq = jnp.concatenate([dfl, jnp.zeros((BLOCK - 8, BLOCK), F32)], axis=0)
            dzf_ref[cols, :] = sq.T.astype(BF16)
        dbf_ref[...] = jnp.broadcast_to(dbf, (8, LANES))

    vm = BS(memory_space=pltpu.VMEM)
    return pl.pallas_call(
        body, out_shape=[SDS((S, LANES), BF16), SDS((8, LANES), F32)],
        in_specs=[vm, vm], out_specs=[vm, vm], name="fgate_bwd",
        compiler_params=pltpu.CompilerParams(vmem_limit_bytes=VMEM_LIMIT))(dct, flt)


def _rope_tables(S):
    inv = ROPE_THETA ** (-jnp.arange(ROPE_HALF, dtype=F32) / ROPE_HALF)
    ang = jnp.arange(S, dtype=F32)[:, None] * inv[None, :]
    cos, sin = jnp.cos(ang), jnp.sin(ang)
    pad = HEAD_DIM - ROPE_DIM
    tc = jnp.concatenate([cos, cos, jnp.ones((S, pad), F32)], axis=1)
    ta = jnp.concatenate([-sin, jnp.zeros((S, HEAD_DIM - ROPE_HALF), F32)], axis=1)
    tb = jnp.concatenate([jnp.zeros((S, ROPE_HALF), F32), sin, jnp.zeros((S, pad), F32)], axis=1)
    return tc, ta, tb


def _rope(x, c, a, b):
    return x * c + pltpu.roll(x, HEAD_DIM - ROPE_HALF, 1) * a + pltpu.roll(x, ROPE_HALF, 1) * b


def _rope_t(g, c, a, b):
    return g * c - pltpu.roll(g, ROPE_HALF, 1) * b - pltpu.roll(g, HEAD_DIM - ROPE_HALF, 1) * a


DIL_TQ = 256


def _dil_extent(i, window):
    return max(0, i * DIL_TQ - window), (i + 1) * DIL_TQ


def _dil_mask(i, lo, ext, window, d):
    diff = (i * DIL_TQ - lo) + _iota2((DIL_TQ, ext), 0) - _iota2((DIL_TQ, ext), 1)
    mask = (diff >= 0) & (diff <= window)
    return mask & ((diff & (d - 1)) == 0) if d > 1 else mask


def _wide(t, w):
    return t if w == LANES else jnp.concatenate([t] * (w // LANES), axis=1)


def _dil_fwd(z3, tabs, grp, S):
    window, d = DIL_GROUPS[grp]
    nq = S // DIL_TQ

    def body(qkv_ref, c_ref, a_ref, b_ref, o_ref, lse_ref, qs, ks):
        c, a, b = c_ref[...], a_ref[...], b_ref[...]
        qs[...] = _rope(qkv_ref[0].astype(F32), c, a, b).astype(BF16)
        ks[...] = _rope(qkv_ref[1].astype(F32), c, a, b).astype(BF16)
        for i in range(nq):
            lo, hi = _dil_extent(i, window)
            rows = pl.ds(i * DIL_TQ, DIL_TQ)
            s = lax.dot_general(qs[rows, :], ks[lo:hi, :], NT, preferred_element_type=F32) * SCALE
            s = jnp.where(_dil_mask(i, lo, hi - lo, window, d), s, -jnp.inf)
            m = jnp.max(s, axis=-1, keepdims=True)
            p = jnp.exp(s - m)
            l = jnp.sum(p, axis=-1, keepdims=True)
            o = jnp.dot(p.astype(BF16), qkv_ref[2, lo:hi, :], preferred_element_type=F32)
            o_ref[rows, :] = o / l
            lse_ref[rows, :] = jnp.broadcast_to(m + jnp.log(l), (DIL_TQ, HEAD_DIM))

    hh0 = grp * HEADS_PER_GROUP
    tspec = BS((S, HEAD_DIM), lambda j: (0, 0))
    ospec = BS((S, HEAD_DIM), lambda j: (0, j))
    return pl.pallas_call(
        body, out_shape=[SDS((S, BRANCH), F32)] * 2, grid=(HEADS_PER_GROUP,),
        in_specs=[BS((3, S, HEAD_DIM), lambda j: (0, 0, hh0 + j)), tspec, tspec, tspec],
        out_specs=[ospec, ospec], scratch_shapes=[pltpu.VMEM((S, HEAD_DIM), BF16)] * 2,
        name="dil_fwd", compiler_params=_cparams(("parallel",)))(z3, *tabs)


def _dil_bwd(z3, tabs, grp, S, do, lse, dpl, dz3):
    window, d = DIL_GROUPS[grp]
    nq = S // DIL_TQ
    n_carry = 0 if dz3 is None else 1

    def body(*refs):
        qkv_ref, c_ref, a_ref, b_ref, do_ref, lse_ref, dpl_ref = refs[:7]
        dz_ref = refs[7 + n_carry]
        qs, ks, dq_acc, dk_acc, dv_acc = refs[8 + n_carry:]
        c, a, b = c_ref[...], a_ref[...], b_ref[...]
        qs[...] = _rope(qkv_ref[0].astype(F32), c, a, b).astype(BF16)
        ks[...] = _rope(qkv_ref[1].astype(F32), c, a, b).astype(BF16)
        dk_acc[...] = jnp.zeros_like(dk_acc)
        dv_acc[...] = jnp.zeros_like(dv_acc)
        for i in range(nq):
            lo, hi = _dil_extent(i, window)
            w = hi - lo
            rows = pl.ds(i * DIL_TQ, DIL_TQ)
            q, k, v = qs[rows, :], ks[lo:hi, :], qkv_ref[2, lo:hi, :]
            s = lax.dot_general(q, k, NT, preferred_element_type=F32) * SCALE
            p = jnp.where(_dil_mask(i, lo, w, window, d), jnp.exp(s - _wide(lse_ref[rows, :], w)), 0.0)
            dob = do_ref[rows, :]
            dp = lax.dot_general(dob, v, NT, preferred_element_type=F32)
            delta = jnp.sum(p * dp, axis=-1, keepdims=True)
            ds = (p * (dp - delta + _wide(dpl_ref[rows, :], w))).astype(BF16)
            dq_acc[rows, :] = jnp.dot(ds, k, preferred_element_type=F32) * SCALE
            dk_acc[lo:hi, :] += lax.dot_general(ds, q, TN, preferred_element_type=F32) * SCALE
            dv_acc[lo:hi, :] += lax.dot_general(p.astype(BF16), dob, TN, preferred_element_type=F32)
        dz_ref[0] = _rope_t(dq_acc[...], c, a, b).astype(BF16)
        dz_ref[1] = _rope_t(dk_acc[...], c, a, b).astype(BF16)
        dz_ref[2] = dv_acc[...].astype(BF16)

    hh0 = grp * HEADS_PER_GROUP
    tspec = BS((S, HEAD_DIM), lambda j: (0, 0))
    bspec = BS((S, HEAD_DIM), lambda j: (0, j))
    zspec = BS((3, S, HEAD_DIM), lambda j: (0, 0, hh0 + j))
    arrays = [z3, *tabs, do, lse, dpl]
    specs = [zspec, tspec, tspec, tspec, bspec, bspec, bspec]
    alias = {}
    if dz3 is not None:
        arrays.append(dz3)
        specs.append(BS(memory_space=pl.ANY))
        alias = {len(arrays) - 1: 0}
    return pl.pallas_call(
        body, out_shape=SDS((3, S, ATTN), BF16), grid=(HEADS_PER_GROUP,), in_specs=specs, out_specs=zspec,
        scratch_shapes=[pltpu.VMEM((S, HEAD_DIM), BF16)] * 2 + [pltpu.VMEM((S, HEAD_DIM), F32)] * 3,
        input_output_aliases=alias, name="dil_bwd", compiler_params=_cparams(("parallel",)))(*arrays)


FOX_TQ = 256


def _fox_scores(q_ref, k_ref, cb_ref, ct_ref, i, ext):
    rows = pl.ds(i * FOX_TQ, FOX_TQ)
    s = lax.dot_general(q_ref[0, rows, :], k_ref[0, 0:ext, :], NT, preferred_element_type=F32) * SCALE
    s = s + _wide(cb_ref[rows, :], ext) - ct_ref[:, 0:ext]
    qpos = i * FOX_TQ + _iota2((FOX_TQ, ext), 0)
    kpos = _iota2((FOX_TQ, ext), 1)
    return jnp.where(kpos <= qpos, s, -jnp.inf)


def _fox_specs(S):
    qs = BS((1, S, HEAD_DIM), lambda h: (0, 0, FOX_HEAD0 + h))
    ks = BS((1, S, HEAD_DIM), lambda h: (1, 0, FOX_HEAD0 + h))
    vs = BS((1, S, HEAD_DIM), lambda h: (2, 0, FOX_HEAD0 + h))
    hb = BS((S, HEAD_DIM), lambda h: (0, h))
    ct = BS((None, 1, S), lambda h: (h, 0, 0))
    return qs, ks, vs, hb, ct


def _fox_fwd(z3, cb, ct, S):
    nq = S // FOX_TQ

    def body(q_ref, k_ref, v_ref, cb_ref, ct_ref, y_ref, lse_ref):
        for i in range(nq):
            ext = (i + 1) * FOX_TQ
            rows = pl.ds(i * FOX_TQ, FOX_TQ)
            s = _fox_scores(q_ref, k_ref, cb_ref, ct_ref, i, ext)
            m = jnp.max(s, axis=-1, keepdims=True)
            p = jnp.exp(s - m)
            l = jnp.sum(p, axis=-1, keepdims=True)
            o = jnp.dot(p.astype(BF16), v_ref[0, 0:ext, :], preferred_element_type=F32)
            y_ref[rows, :] = (o / l).astype(BF16)
            lse_ref[rows, :] = jnp.broadcast_to(m + jnp.log(l), (FOX_TQ, HEAD_DIM))

    qs, ks, vs, hb, cts = _fox_specs(S)
    return pl.pallas_call(
        body, out_shape=[SDS((S, BRANCH), BF16), SDS((S, BRANCH), F32)], grid=(N_FOX,),
        in_specs=[qs, ks, vs, hb, cts], out_specs=[hb, hb], name="fox_fwd",
        compiler_params=_cparams(("parallel",)))(z3, z3, z3, cb, ct.reshape(8, 1, S))


def _fox_bwd(z3, cb, ct, S, dy, lse, dz3):
    nq = S // FOX_TQ

    def body(q_ref, k_ref, v_ref, cb_ref, ct_ref, dy_ref, lse_ref, _, dz_ref, dct_ref, dk_acc, dv_acc):
        dk_acc[...] = jnp.zeros_like(dk_acc)
        dv_acc[...] = jnp.zeros_like(dv_acc)
        dct_ref[...] = jnp.zeros_like(dct_ref)
        for i in range(nq):
            ext = (i + 1) * FOX_TQ
            rows = pl.ds(i * FOX_TQ, FOX_TQ)
            s = _fox_scores(q_ref, k_ref, cb_ref, ct_ref, i, ext)
            p = jnp.exp(s - _wide(lse_ref[rows, :], ext))
            dob = dy_ref[rows, :]
            dp = lax.dot_general(dob, v_ref[0, 0:ext, :], NT, preferred_element_type=F32)
            ds = p * (dp - jnp.sum(p * dp, axis=-1, keepdims=True))
            dsb = ds.astype(BF16)
            dz_ref[0, rows, :] = (jnp.dot(dsb, k_ref[0, 0:ext, :], preferred_element_type=F32) * SCALE).astype(BF16)
            dk_acc[0:ext, :] += lax.dot_general(dsb, q_ref[0, rows, :], TN, preferred_element_type=F32) * SCALE
            dv_acc[0:ext, :] += lax.dot_general(p.astype(BF16), dob, TN, preferred_element_type=F32)
            dct_ref[:, 0:ext] -= jnp.sum(ds, axis=0, keepdims=True)
        dz_ref[1] = dk_acc[...].astype(BF16)
        dz_ref[2] = dv_acc[...].astype(BF16)

    qs, ks, vs, hb, cts = _fox_specs(S)
    zspec = BS((3, S, HEAD_DIM), lambda h: (0, 0, FOX_HEAD0 + h))
    dz, dct = pl.pallas_call(
        body, out_shape=[SDS((3, S, ATTN), BF16), SDS((8, 1, S), F32)], grid=(N_FOX,),
        in_specs=[qs, ks, vs, hb, cts, hb, hb, BS(memory_space=pl.ANY)], out_specs=[zspec, cts],
        scratch_shapes=[pltpu.VMEM((S, HEAD_DIM), F32)] * 2, input_output_aliases={7: 0}, name="fox_bwd",
        compiler_params=_cparams(("parallel",)))(z3, z3, z3, cb, ct.reshape(8, 1, S), dy, lse, dz3)
    return dz, dct.reshape(8, S)


TM = 1024


def _layer_fwd(h, p_bf, sm, W, tabs, after=(), mid=None):
    S, D = h.shape
    qkv_cols = 3 * ATTN
    u = _rms_fwd(h, sm["g_mix"], after=after)
    tn = 512
    per = ATTN // tn
    z3 = _mm_nn("mm_qkv", u, W["in"], [BF16], tm=TM, ncols=qkv_cols, tn=tn, out_shapes=[(3, S, ATTN)],
                out_specs=[BS((None, TM, tn), lambda i, j, k: (j // per, i, j % per))])[0]
    zf = _mm_nn("mm_f", u, W["in"], [F32], tm=TM, joff=qkv_cols // LANES, ncols=LANES, tn=LANES)[0]
    cb, ct, flt = _fgate_fwd(zf, sm["b_f"])
    os_, ls_ = [], []
    for grp in range(len(DIL_GROUPS)):
        o, l = _dil_fwd(z3, tabs, grp, S)
        os_.append(o)
        ls_.append(l)
    ya_h = _mix_fwd(os_, ls_)
    yb_h, lse_fox = _fox_fwd(z3, cb, ct, S)

    tg = _tile(W["gate"].shape[2], 512)
    gates = _mm_nn("mm_gate", u, W["gate"], [F32], tm=TM,
                   extras=[(sm["b_gate"], BS((1, tg), lambda i, j, k: (0, j)))],
                   epilogue=lambda acc, bias: (_sigmoid(acc + bias),))[0]
    ya = _mm_nn("mm_bra", ya_h, W["bra"], [F32], tm=TM)[0]
    tb = _tile(W["brb"].shape[2], 512)
    g2off = D // tb
    yb, merged = _mm_nn(
        "mm_brb", yb_h, W["brb"], [F32, BF16], tm=TM,
        extras=[(ya, BS((TM, tb), lambda i, j, k: (i, j))), (gates, BS((TM, tb), lambda i, j, k: (i, j))),
                (gates, BS((TM, tb), lambda i, j, k: (i, j + g2off)))],
        epilogue=lambda acc, a, g1, g2: (acc, g1 * a + g2 * acc))
    to = _tile(D, 512)
    res_spec = BS((TM, to), lambda i, j, k: (i, j))
    h1 = _mm_nn("mm_o", merged, W["o"], [F32], tm=TM, extras=[(h, res_spec)], epilogue=lambda acc, r: (r + acc,))[0]

    m = _rms_fwd(h1, sm["g_mlp"], after=() if mid is None else mid(h1))

    def up_epi(acc):
        r = jnp.maximum(acc, 0.0)
        return r, r * r

    relu_a, act = _mm_nn("mm_up", m, W["up"], [BF16, BF16], tm=TM, epilogue=up_epi)
    h2 = _mm_nn("mm_down", act, W["down"], [F32], tm=TM, extras=[(h1, res_spec)], epilogue=lambda acc, r: (r + acc,))[0]

    nrm = _rms_fwd(h2, sm["g_ple"])
    pp = _mm_nn("mm_ple", p_bf, W["ple"], [F32], tm=TM)[0]

    def pg_epi(acc, r, q):
        g = _sigmoid(acc)
        return g, r + g * q

    pg, h3 = _mm_nn("mm_pg", nrm, W["pg"], [F32, F32], tm=TM, extras=[(h2, res_spec), (pp, res_spec)], epilogue=pg_epi)
    saved = dict(h=h, u=u, z3=z3, cb=cb, ct=ct, flt=flt, os=os_, ls=ls_, ya_h=ya_h, yb_h=yb_h, lse_fox=lse_fox,
                 gates=gates, ya=ya, yb=yb, merged=merged, h1=h1, m=m, relu_a=relu_a, act=act, h2=h2, nrm=nrm,
                 pg=pg, pp=pp, p_bf=p_bf)
    return h3, saved


def _layer_bwd(dh3, sv, sm, W, tabs, after=(), ready=None):
    S, D = dh3.shape
    dW, dsm = {}, {}

    dpp, dpgz = _ple_bwd(dh3, sv["pg"], sv["pp"], after=after)
    dW["ple"] = _mm_tn("tn_ple", sv["p_bf"], dpp, W["ple"].shape, ts=S)
    dW["pg"] = _mm_tn("tn_pg", sv["nrm"], dpgz, W["pg"].shape, ts=S)
    dn = _mm_nt("nt_pg", dpgz, W["pg"], F32, tm=TM)
    dh2, dh2b, dsm["g_ple"] = _rms_bwd(sv["h2"], sm["g_ple"], dn, dh3)

    ta = _tile(W["up"].shape[1], 512)
    tff = _tile(W["down"].shape[1], 512)
    da = _mm_nt("nt_down", dh2b, W["down"], BF16, tm=TM,
                extras=[(sv["relu_a"], BS((TM, tff), lambda i, ko, cc: (i, ko)))],
                epilogue=lambda acc, r: (acc * (2.0 * r.astype(F32)),))
    dW["down"] = _mm_tn("tn_down", sv["act"], dh2b, W["down"].shape, ts=S)
    dm = _mm_nt("nt_up", da, W["up"], F32, tm=TM)
    dW["up"] = _mm_tn("tn_up", sv["m"], da, W["up"].shape, ts=S)
    dh1, dh1b, dsm["g_mlp"] = _rms_bwd(sv["h1"], sm["g_mlp"], dm, dh2, after=() if ready is None else ready("mlp", dW))

    dmerged = _mm_nt("nt_o", dh1b, W["o"], F32, tm=TM)
    dW["o"] = _mm_tn("tn_o", sv["merged"], dh1b, W["o"].shape, ts=S)
    dya, dyb, dgz, dsm["b_gate"] = _merge_bwd(dmerged, sv["ya"], sv["yb"], sv["gates"])
    dya_h = _mm_nt("nt_bra", dya, W["bra"], F32, tm=TM)
    dyb_h = _mm_nt("nt_brb", dyb, W["brb"], BF16, tm=TM)
    dW["bra"] = _mm_tn("tn_bra", sv["ya_h"], dya, W["bra"].shape, ts=S)
    dW["brb"] = _mm_tn("tn_brb", sv["yb_h"], dyb, W["brb"].shape, ts=S)
    dW["gate"] = _mm_tn("tn_gate", sv["u"], dgz, W["gate"].shape, ts=S)
    du = _mm_nt("nt_gate", dgz, W["gate"], F32, tm=TM)

    dos_, dpls_ = [], []
    mixed = _mix_bwd(dya_h, sv["os"], sv["ls"], after=() if ready is None else ready("merge", dW))
    dos_, dpls_ = mixed[:3], mixed[3:]
    dz3 = None
    for grp in range(len(DIL_GROUPS)):
        dz3 = _dil_bwd(sv["z3"], tabs, grp, S, dos_[grp], sv["ls"][grp], dpls_[grp], dz3)
    dz3, dct = _fox_bwd(sv["z3"], sv["cb"], sv["ct"], S, dyb_h, sv["lse_fox"], dz3)
    dzf, dsm["b_f"] = _fgate_bwd(dct, sv["flt"])

    tn = 512
    per = ATTN // tn
    qkv_cols = 3 * ATTN
    dwin = _mm_tn("tn_qkv", sv["u"], dz3, W["in"].shape, ts=S, ncols=qkv_cols, tn=tn,
                  dy_spec=BS((None, S, tn), lambda ki, j, sc: (j // per, sc, j % per)))
    dW["in"] = _mm_tn("tn_f", sv["u"], dzf, W["in"].shape, ts=S, joff=qkv_cols // LANES, ncols=LANES, tn=LANES,
                      carry=dwin)
    tu = _tile(D, 512)
    prev_spec = BS((TM, tu), lambda i, ko, cc: (i, ko))
    du = _mm_nt("nt_qkv", dz3, W["in"], F32, tm=TM, ncontract=qkv_cols, tc=ATTN,
                dy_spec=BS((None, TM, ATTN), lambda i, ko, cc: (cc, i, 0)),
                extras=[(du, prev_spec)], epilogue=lambda acc, prev: (prev + acc,))
    du = _mm_nt("nt_f", dzf, W["in"], F32, tm=TM, coff=qkv_cols // LANES, ncontract=LANES, tc=LANES,
                extras=[(du, prev_spec)], epilogue=lambda acc, prev: (prev + acc,))
    dh0, _, dsm["g_mix"] = _rms_bwd(sv["h"], sm["g_mix"], du, dh1)
    return dh0, dW, dsm


def _shard_tiles(R, Cp):
    tc = Cp if Cp <= 2048 else _tile(Cp, 1024)
    tr = R
    while tr * tc > 256 * 1024 and tr % 16 == 0:
        tr //= 2
    return tr, tc


def _padded_cols(C):
    return -(-C // LANES) * LANES


def _cast_bf16(name, w, me):
    Lr, R, C = w.shape
    Cp = _padded_cols(C)
    tr, tc = _shard_tiles(R, Cp)

    def body(me_ref, w_ref, *o_refs):
        for l in range(Lr):
            x = w_ref[l]
            if Cp != C:
                col = pl.program_id(1) * tc + _iota2((tr, tc), 1)
                x = jnp.where(col < C, x, 0.0)
            o_refs[l][...] = x.astype(BF16)

    grid_spec = pltpu.PrefetchScalarGridSpec(
        num_scalar_prefetch=1, grid=(R // tr, Cp // tc),
        in_specs=[BS((Lr, tr, tc), lambda i, j, me_ref: (0, i, j))],
        out_specs=[BS((None, tr, tc), lambda i, j, me_ref: (me_ref[0], i, j))] * Lr)
    return pl.pallas_call(
        body, out_shape=[SDS((NDEV, R, Cp), BF16)] * Lr, grid_spec=grid_spec,
        name="cast_" + name, compiler_params=_cparams(("parallel", "parallel")))(me, w)


def _adam_math(w, g, m, v):
    m = ADAM_B1 * m + (1.0 - ADAM_B1) * g
    v = ADAM_B2 * v + (1.0 - ADAM_B2) * (g * g)
    m_hat = m / (1.0 - ADAM_B1 ** ADAM_STEP)
    v_hat = v / (1.0 - ADAM_B2 ** ADAM_STEP)
    delta = -ADAM_LR * (m_hat / (jnp.sqrt(v_hat) + ADAM_EPS) + ADAM_WD * w)
    return delta, m, v


def _pair_sum(name, dw, sib, core):
    _, R, Cp = dw.shape
    tr, tc = _shard_tiles(R, Cp)

    def body(core_ref, a_ref, b_ref, o_ref):
        o_ref[...] = (a_ref[...].astype(F32) + b_ref[...].astype(F32)).astype(BF16)

    grid_spec = pltpu.PrefetchScalarGridSpec(
        num_scalar_prefetch=1, grid=(4, R // tr, Cp // tc),
        in_specs=[BS((None, tr, tc), lambda q, i, j, core_ref: (2 * q + core_ref[0], i, j)),
                  BS((None, tr, tc), lambda q, i, j, core_ref: (q, i, j))],
        out_specs=BS((None, tr, tc), lambda q, i, j, core_ref: (q, i, j)))
    return pl.pallas_call(
        body, out_shape=SDS((4, R, Cp), BF16), grid_spec=grid_spec, name="pair_sum_" + name,
        compiler_params=_cparams(("parallel", "parallel", "parallel")))(core, dw, sib)


def _adamw(name, layer, w, m, v, recv, own, me, carried):
    Lr, R, C = w.shape
    nslot, _, Cp = recv.shape
    tr, tc = _shard_tiles(R, Cp)
    n_carry = 0 if carried is None else 4

    def body(me_ref, w_ref, m_ref, v_ref, r_ref, own_ref, *rest):
        g_out, d_out, m_out, v_out = rest[n_carry:]
        mine = own_ref[...].astype(F32)
        g = None
        for s in range(nslot):
            part = jnp.where(me_ref[0] == s, mine, r_ref[s].astype(F32))
            g = part if g is None else g + part
        delta, mn, vn = _adam_math(w_ref[...], g, m_ref[...], v_ref[...])
        g_out[...] = g
        d_out[...] = delta
        m_out[...] = mn
        v_out[...] = vn

    spec = BS((None, tr, tc), lambda i, j, me_ref: (layer, i, j))
    rspec = BS((nslot, tr, tc), lambda i, j, me_ref: (0, i, j))
    ospec = BS((None, tr, tc), lambda i, j, me_ref: (me_ref[0], i, j))
    arrays, specs, alias = [w, m, v, recv, own], [spec, spec, spec, rspec, ospec], {}
    if carried is not None:
        arrays += list(carried)
        specs += [BS(memory_space=pl.ANY)] * 4
        alias = {6 + k: k for k in range(4)}
    grid_spec = pltpu.PrefetchScalarGridSpec(
        num_scalar_prefetch=1, grid=(R // tr, Cp // tc), in_specs=specs, out_specs=[spec] * 4)
    return pl.pallas_call(
        body, out_shape=[SDS(w.shape, F32)] * 4, grid_spec=grid_spec, input_output_aliases=alias,
        name="adamw_" + name, compiler_params=_cparams(("parallel", "parallel")))(me, *arrays)


def _adamw_small(w, g, m, v):
    vm = BS(memory_space=pltpu.VMEM)

    def body(w_ref, g_ref, m_ref, v_ref, d_out, m_out, v_out):
        delta, mn, vn = _adam_math(w_ref[...], g_ref[...], m_ref[...], v_ref[...])
        d_out[...] = delta
        m_out[...] = mn
        v_out[...] = vn

    return pl.pallas_call(body, out_shape=[SDS(w.shape, F32)] * 3, in_specs=[vm] * 4, out_specs=[vm] * 3,
                          name="adamw_small")(w, g, m, v)


def _place():
    return lax.axis_index("x"), lax.axis_index("y"), lax.axis_index("c")


def _slot(px, py, pc):
    return 4 * px + 2 * py + pc


def _other_chips(x, y):
    return [(1 - x, y), (x, 1 - y), (1 - x, 1 - y)]


def _peer(x, y, c, rel):
    return (1 - x if rel & 4 else x, 1 - y if rel & 2 else y, 1 - c if rel & 1 else c)


def _plan_gather_ici(src, land):
    x, y, c = _place()
    own = land.at[_slot(x, y, c)]
    return [(own, own, (x, y, 1 - c))] + [(own, own, (*chip, c)) for chip in _other_chips(x, y)]


def _plan_gather_relay(src, land):
    x, y, c = _place()
    return [(land.at[_slot(*chip, c)], land.at[_slot(*chip, c)], (x, y, 1 - c)) for chip in _other_chips(x, y)]


def _plan_pair(src, land):
    x, y, c = _place()
    return [(src.at[2 * q + (1 - c)], land.at[q], (x, y, 1 - c)) for q in range(4)]


def _plan_scatter_chips(src, land):
    x, y, c = _place()
    return [(src.at[2 * cx + cy], land.at[2 * x + y], (cx, cy, c)) for cx, cy in _other_chips(x, y)]


EFFECT = pltpu.SideEffectType.DATAFLOW_SIDE_EFFECTING
HBM_SPEC = BS(memory_space=pltpu.HBM)
SEM_SPEC = BS(memory_space=pltpu.SEMAPHORE)
ORDER_SPEC = BS(memory_space=pl.ANY)


def _exchange_start(name, plan, ncopy, lands, srcs=None, after=()):
    bufs = ([] if srcs is None else list(srcs)) + list(lands)
    n, nb, ns = len(lands), len(bufs), len(bufs) - len(lands)
    n_after = len(after)

    def body(*refs):
        src_refs = refs[:ns] if ns else [None] * n
        land_refs = refs[ns:nb]
        send_sems, recv_sems = refs[nb + n_after], refs[nb + n_after + 1]
        token = refs[-1]
        for i in range(n):
            for k, (src, dst, to) in enumerate(plan(src_refs[i], land_refs[i])):
                pltpu.make_async_remote_copy(src_ref=src, dst_ref=dst, send_sem=send_sems.at[i * ncopy + k],
                                             recv_sem=recv_sems.at[i * ncopy + k], device_id=to,
                                             device_id_type=MESH).start()
        token[...] = jnp.zeros_like(token)

    out_shape = [pltpu.SemaphoreType.DMA((n * ncopy,)), pltpu.SemaphoreType.DMA((n * ncopy,))]
    out_shape += [pltpu.HBM(b.shape, b.dtype) for b in bufs] + [SDS((8, LANES), F32)]
    out = pl.pallas_call(
        body, name=name, out_shape=out_shape, in_specs=[HBM_SPEC] * nb + [ORDER_SPEC] * n_after,
        out_specs=[SEM_SPEC, SEM_SPEC] + [HBM_SPEC] * nb + [BS(memory_space=pltpu.VMEM)],
        input_output_aliases={i: 2 + i for i in range(nb)},
        compiler_params=pltpu.CompilerParams(has_side_effects=EFFECT))(
            *[pltpu.with_memory_space_constraint(b, pltpu.HBM) for b in bufs], *after)
    return out[0], out[1], out[2:2 + ns], out[2 + ns:2 + nb], out[-1]


def _exchange_wait(name, plan, send_sems, recv_sems, lands, srcs=None, after=()):
    bufs = ([] if srcs is None else list(srcs)) + list(lands)
    n, nb, ns = len(lands), len(bufs), len(bufs) - len(lands)

    def body(*refs):
        src_refs = refs[:ns] if ns else [None] * n
        land_refs = refs[ns:nb]
        send_ref, recv_ref = refs[nb], refs[nb + 1]
        for i in range(n):
            copies = plan(src_refs[i], land_refs[i])
            for k, (src, dst, to) in enumerate(copies):
                cp = pltpu.make_async_remote_copy(src_ref=src, dst_ref=dst, send_sem=send_ref.at[i * len(copies) + k],
                                                  recv_sem=recv_ref.at[i * len(copies) + k], device_id=to,
                                                  device_id_type=MESH)
                cp.wait_send()
                cp.wait_recv()

    out = pl.pallas_call(
        body, name=name, out_shape=[pltpu.HBM(b.shape, b.dtype) for b in bufs],
        in_specs=[HBM_SPEC] * nb + [SEM_SPEC, SEM_SPEC] + [ORDER_SPEC] * len(after), out_specs=[HBM_SPEC] * nb,
        input_output_aliases={i: i for i in range(nb)},
        compiler_params=pltpu.CompilerParams(has_side_effects=EFFECT))(*bufs, send_sems, recv_sems, *after)
    return out[:ns], out[ns:]


def _allreduce_small(v, after=()):
    R, Wd = v.shape
    n_after = len(after)

    def body(v_ref, *rest):
        o_ref, buf, send_sems, recv_sems = rest[n_after:]
        x, y, c = _place()
        me = _slot(x, y, c)
        buf[me] = v_ref[...]
        started = []
        for rel in range(1, NDEV):
            peer = _peer(x, y, c, rel)
            cp = pltpu.make_async_remote_copy(
                src_ref=v_ref, dst_ref=buf.at[me], send_sem=send_sems.at[rel - 1], recv_sem=recv_sems.at[rel - 1],
                device_id=peer, device_id_type=MESH)
            cp.start()
            started.append(cp)
        for cp in started:
            cp.wait()
        acc = buf[0]
        for j in range(1, NDEV):
            acc = acc + buf[j]
        o_ref[...] = acc

    vm = BS(memory_space=pltpu.VMEM)
    return pl.pallas_call(
        body, out_shape=SDS((R, Wd), F32), in_specs=[vm] + [ORDER_SPEC] * n_after, out_specs=vm,
        scratch_shapes=[pltpu.VMEM((NDEV, R, Wd), F32), pltpu.SemaphoreType.DMA((7,)), pltpu.SemaphoreType.DMA((7,))],
        name="allreduce_small")(v, *after)


def _as_matrix(name, g):
    if name in ROW_SHARDED:
        return g.reshape(1, NDEV * g.shape[1], g.shape[2])
    return g


def _pad8(a):
    return jnp.pad(a, ((0, -a.shape[0] % 8), (0, 0)))


def _pack_small(g_mix, g_mlp, g_ple, b_gate, g_final, b_f, extra_row=None):
    Lr, D = g_mix.shape
    rows = [g_mix, g_mlp, g_ple, b_gate.reshape(2 * Lr, D), g_final.reshape(1, D),
            jnp.pad(b_f.reshape(1, -1), ((0, 0), (0, D - b_f.size)))]
    rows.append(jnp.zeros((1, D), F32) if extra_row is None else extra_row)
    return jnp.concatenate([_pad8(r) for r in rows], axis=0)


def _unpack_small(pk, Lr, D, nf):
    o = 0
    out = []
    for rows, shape in ((Lr, (Lr, D)), (Lr, (Lr, D)), (Lr, (Lr, D)), (2 * Lr, (Lr, 2 * D)), (1, (D,))):
        out.append(pk[o:o + rows].reshape(shape))
        o += rows + (-rows % 8)
    out.append(pk[o, :Lr * nf].reshape(Lr, nf))
    return out, pk[o + 8, 0]


def kernel(x, p, g_mix, w_in, b_f, w_gate, b_gate, w_br_a, w_br_b, w_o, g_mlp, w_up, w_down, g_ple, w_ple, w_ple_gate, g_final, loss_target, m_g_mix, m_w_in, m_b_f, m_w_gate, m_b_gate, m_w_br_a, m_w_br_b, m_w_o, m_g_mlp, m_w_up, m_w_down, m_g_ple, m_w_ple, m_w_ple_gate, m_g_final, v_g_mix, v_w_in, v_b_f, v_w_gate, v_b_gate, v_w_br_a, v_w_br_b, v_w_o, v_g_mlp, v_w_up, v_w_down, v_g_ple, v_w_ple, v_w_ple_gate, v_g_final):
    Lr, D = g_mix.shape
    S = x.shape[1]
    nf = b_f.shape[1]
    big_w = dict(zip(BIG, (w_in, w_gate, w_br_a, w_br_b, w_o, w_up, w_down, w_ple, w_ple_gate)))
    big_m = dict(zip(BIG, (m_w_in, m_w_gate, m_w_br_a, m_w_br_b, m_w_o, m_w_up, m_w_down, m_w_ple, m_w_ple_gate)))
    big_v = dict(zip(BIG, (v_w_in, v_w_gate, v_w_br_a, v_w_br_b, v_w_o, v_w_up, v_w_down, v_w_ple, v_w_ple_gate)))

    x_pos, y_pos, c_pos = _place()
    me = _slot(x_pos, y_pos, c_pos).astype(jnp.int32).reshape(1)
    core = c_pos.astype(jnp.int32).reshape(1)
    chip = (2 * x_pos + y_pos).astype(jnp.int32).reshape(1)
    wbf = {k: _cast_bf16(k, big_w[k], me) for k in BIG}
    tabs = _rope_tables(S)
    bf_pad = jnp.pad(b_f, ((0, 0), (0, LANES - nf)))

    smalls = [dict(g_mix=g_mix[l:l + 1], b_f=bf_pad[l:l + 1], b_gate=b_gate[l:l + 1], g_mlp=g_mlp[l:l + 1],
                   g_ple=g_ple[l:l + 1]) for l in range(Lr)]

    gather_groups = (("in", "gate", "bra", "brb", "o"), ("up", "down", "ple", "pg"))
    order = [(l, gi) for l in range(Lr) for gi in range(len(gather_groups))]
    on_links = {}

    def links_start(k, after=()):
        l, gi = order[k]
        on_links[k] = _exchange_start(f"gather_ici_start_{l}_{gi}", _plan_gather_ici, 4,
                                      [wbf[n][l] for n in gather_groups[gi]], None, after)
        return on_links[k][4]

    def arrive(k, after):
        l, gi = order[k]
        send_sems, recv_sems, _, lands, _ = on_links.pop(k)
        _, lands = _exchange_wait(f"gather_ici_wait_{l}_{gi}", _plan_gather_ici, send_sems, recv_sems, lands, None,
                                  after)
        send_sems, recv_sems, _, lands, token = _exchange_start(
            f"gather_relay_start_{l}_{gi}", _plan_gather_relay, 3, lands)
        tokens = [token]
        if k + 2 < len(order):
            tokens.append(links_start(k + 2, (token,)))
        _, lands = _exchange_wait(f"gather_relay_wait_{l}_{gi}", _plan_gather_relay, send_sems, recv_sems, lands)
        return {n: _as_matrix(n, g) for n, g in zip(gather_groups[gi], lands)}, tuple(tokens)

    h = x.reshape(S, D)
    Ws, saves = [], []
    links_start(1, (links_start(0),))
    for l in range(Lr):
        W, tokens = arrive(2 * l, (h,))

        def mid(h1, l=l, W=W):
            more, mid_tokens = arrive(2 * l + 1, (h1,))
            W.update(more)
            return mid_tokens

        h, sv = _layer_fwd(h, p[l, 0].astype(BF16), smalls[l], W, tabs, tokens, mid)
        Ws.append(W)
        saves.append(sv)

    dh, dg_final, loss_part = _final(h, g_final.reshape(1, D), loss_target.reshape(S, D))

    big_out = {k: None for k in BIG}
    dsmalls = [None] * Lr
    groups = dict(mlp=("ple", "pg", "down", "up"), merge=("o", "bra", "brb", "gate"), proj=("in",))
    in_flight = []
    links_steps = 3

    def pair_start(l, stage, dW):
        dws = [dW[k].reshape(wbf[k][l].shape) for k in groups[stage]]
        lands = [lax.empty((4,) + d.shape[1:], BF16) for d in dws]
        started = _exchange_start(f"pair_start_{l}_{stage}", _plan_pair, 4, lands, dws)
        in_flight.append(dict(tag=f"{l}_{stage}", layer=l, names=groups[stage], on_links=False, started=started,
                              age=0))
        return started[4]

    def advance(after, drain=False):
        tokens = []
        for ex in list(in_flight):
            send_sems, recv_sems, srcs, lands, _ = ex["started"]
            if not ex["on_links"]:
                dws, sib = _exchange_wait(f"pair_wait_{ex['tag']}", _plan_pair, send_sems, recv_sems, lands, srcs,
                                          after)
                sums = [_pair_sum(k, d, s, core) for k, d, s in zip(ex["names"], dws, sib)]
                ex["started"] = _exchange_start(f"scatter_start_{ex['tag']}", _plan_scatter_chips, 3,
                                                [lax.empty(s.shape, BF16) for s in sums], sums)
                ex["on_links"], ex["age"] = True, 0
                tokens.append(ex["started"][4])
            elif drain or ex["age"] + 1 >= links_steps:
                sums, landed = _exchange_wait(f"scatter_wait_{ex['tag']}", _plan_scatter_chips, send_sems, recv_sems,
                                              lands, srcs, after)
                for k, own, recv in zip(ex["names"], sums, landed):
                    big_out[k] = _adamw(k, ex["layer"], big_w[k], big_m[k], big_v[k], recv, own, chip, big_out[k])
                in_flight.remove(ex)
                tokens.append(landed[0])
            else:
                ex["age"] += 1
        return tokens

    after = ()
    for l in reversed(range(Lr)):
        def ready(stage, dW, l=l):
            tokens = advance((dW[groups[stage][-1]],))
            return tuple(tokens + [pair_start(l, stage, dW)])

        dh, dW, dsmalls[l] = _layer_bwd(dh, saves[l], smalls[l], Ws[l], tabs, after, ready)
        after = tuple(advance((dh,)) + [pair_start(l, "proj", dW)])
    last = after[-1]
    while in_flight:
        last = advance((last,), drain=True)[-1]

    cat = lambda key: jnp.concatenate([_pad8(d[key]) for d in dsmalls], axis=0)[::8]
    dbf = jnp.concatenate([d["b_f"] for d in dsmalls], axis=0).reshape(Lr, 8, LANES)[:, :nf, 0]
    loss_row = jnp.concatenate([loss_part, g_final.reshape(1, D)[:, LANES:]], axis=1)
    g_pack = _allreduce_small(_pack_small(cat("g_mix"), cat("g_mlp"), cat("g_ple"), cat("b_gate"), dg_final, dbf,
                                          loss_row), (last,))
    w_pack = _pack_small(g_mix, g_mlp, g_ple, b_gate, g_final, b_f)
    m_pack = _pack_small(m_g_mix, m_g_mlp, m_g_ple, m_b_gate, m_g_final, m_b_f)
    v_pack = _pack_small(v_g_mix, v_g_mlp, v_g_ple, v_b_gate, v_g_final, v_b_f)
    d_pack, mn_pack, vn_pack = _adamw_small(w_pack, g_pack, m_pack, v_pack)

    small_names = ("g_mix", "g_mlp", "g_ple", "b_gate", "g_final", "b_f")
    sg, loss = _unpack_small(g_pack, Lr, D, nf)
    small_out = {}
    for kind, pk in (("grad", None), ("delta", d_pack), ("m", mn_pack), ("v", vn_pack)):
        vals = sg if pk is None else _unpack_small(pk, Lr, D, nf)[0]
        small_out[kind] = dict(zip(small_names, vals))

    order = ("g_mix", "in", "b_f", "gate", "b_gate", "bra", "brb", "o", "g_mlp", "up", "down", "g_ple", "ple", "pg",
             "g_final")
    outs = [loss, dh.reshape(x.shape)]
    for idx, kind in enumerate(("grad", "delta", "m", "v")):
        for name in order:
            outs.append(big_out[name][idx] if name in big_out else small_out[kind][name])
    return tuple(outs)
```

```python
import jax
import jax.numpy as jnp
from jax import lax
from jax.experimental import pallas as pl
from jax.experimental.pallas import tpu as pltpu

F32, BF16 = jnp.float32, jnp.bfloat16
SDS = jax.ShapeDtypeStruct
BS = pl.BlockSpec
MESH = pl.DeviceIdType.MESH
HI = lax.Precision.HIGHEST

HEAD_DIM = 128
N_HEADS = 16
ATTN = N_HEADS * HEAD_DIM
DIL_GROUPS = ((128, 1), (512, 4), (2048, 16))
HEADS_PER_GROUP = 4
FOX_HEAD0 = 12
N_FOX = 4
BRANCH = HEADS_PER_GROUP * HEAD_DIM
BLOCK = 128
ROPE_DIM = HEAD_DIM // 4
ROPE_HALF = ROPE_DIM // 2
ROPE_THETA = 500000.0
NORM_EPS = 1e-6
SCALE = HEAD_DIM ** -0.5
NDEV = 8
LANES = 128

ADAM_LR, ADAM_B1, ADAM_B2, ADAM_EPS, ADAM_WD, ADAM_STEP = 0.001, 0.9, 0.999, 1e-08, 0.01, 10

V7X_VMEM_BYTES = 64 * 1024 * 1024
VMEM_LIMIT = (V7X_VMEM_BYTES * 3) // 4

NT_CONTRACT = 2048
NN = (((1,), (0,)), ((), ()))
NT = (((1,), (1,)), ((), ()))
TN = (((0,), (0,)), ((), ()))

BIG = ("in", "gate", "bra", "brb", "o", "up", "down", "ple", "pg")
ROW_SHARDED = ("in", "o", "down", "pg")


def _tile(n, pref):
    t = min(n, pref)
    while n % t:
        t -= LANES
    return t


def _cparams(sem):
    return pltpu.CompilerParams(dimension_semantics=sem, vmem_limit_bytes=VMEM_LIMIT)


def _sigmoid(x):
    return 1.0 / (1.0 + jnp.exp(-x))


def _mm(name, grid, a, a_spec, b, b_spec, dims, outs, acc_shape, extras=(), epilogue=None, carry=None, blocks=1):
    nk = grid[2]
    n_ex, n_out = len(extras), len(outs)
    n_carry = 0 if carry is None else 1

    def body(*refs):
        a_ref, b_ref = refs[0], refs[1]
        ex_refs = refs[2:2 + n_ex]
        out_refs = refs[2 + n_ex + n_carry:2 + n_ex + n_carry + n_out]

        def product():
            if blocks == 1:
                return lax.dot_general(a_ref[...], b_ref[...], dims, preferred_element_type=F32)
            n = b_ref.shape[2]
            total = None
            for jb in range(blocks):
                part = lax.dot_general(a_ref[:, jb * n:(jb + 1) * n], b_ref[jb], dims, preferred_element_type=F32)
                total = part if total is None else total + part
            return total

        def finish(acc):
            res = (acc,) if epilogue is None else epilogue(acc, *[e[...] for e in ex_refs])
            for o_ref, r in zip(out_refs, res):
                o_ref[...] = r.astype(o_ref.dtype)

        if nk == 1:
            finish(product())
            return
        acc_ref = refs[-1]
        k = pl.program_id(2)

        @pl.when(k == 0)
        def _():
            acc_ref[...] = jnp.zeros_like(acc_ref)

        acc_ref[...] += product()

        @pl.when(k == nk - 1)
        def _():
            finish(acc_ref[...])

    arrays = [a, b] + [e[0] for e in extras]
    specs = [a_spec, b_spec] + [e[1] for e in extras]
    alias = {}
    if carry is not None:
        arrays.append(carry[0])
        specs.append(BS(memory_space=pl.ANY))
        alias = {len(arrays) - 1: carry[1]}
    return pl.pallas_call(
        body, out_shape=[o[0] for o in outs], grid=grid, in_specs=specs, out_specs=[o[1] for o in outs],
        scratch_shapes=[pltpu.VMEM(acc_shape, F32)] if nk > 1 else [], input_output_aliases=alias, name=name,
        compiler_params=_cparams(("parallel", "parallel", "arbitrary")))(*arrays)


def _mm_nn(name, a, w, out_dtypes, *, tm, joff=0, ncols=None, tn=None, tk=None, extras=(), epilogue=None,
           out_specs=None, out_shapes=None):
    M, K = a.shape
    J, _, n = w.shape
    tn = tn or _tile(n, 512)
    tk = tk or _tile(K, 2048)
    per = n // tn
    ncols = ncols or J * n
    grid = (M // tm, ncols // tn, K // tk)
    a_spec = BS((tm, tk), lambda i, j, k: (i, k))
    w_spec = BS((None, tk, tn), lambda i, j, k: ((j + joff) // per, k, (j + joff) % per))
    if out_specs is None:
        out_specs = [BS((tm, tn), lambda i, j, k: (i, j))] * len(out_dtypes)
        out_shapes = [(M, ncols)] * len(out_dtypes)
    outs = [(SDS(s, d), sp) for s, d, sp in zip(out_shapes, out_dtypes, out_specs)]
    return _mm(name, grid, a, a_spec, w, w_spec, NN, outs, (tm, tn), extras, epilogue)


def _mm_nt(name, dy, w, out_dtype, *, tm, coff=0, ncontract=None, tc=None, dy_spec=None, extras=(), epilogue=None):
    J, K, n = w.shape
    M = dy.shape[-2]
    tko = _tile(K, 512)
    outs = [(SDS((M, K), out_dtype), BS((tm, tko), lambda i, ko, cc: (i, ko)))]
    if J > 1 and tc is None and dy_spec is None:
        blocks = max(1, min(J, NT_CONTRACT // n))
        grid = (M // tm, K // tko, J // blocks)
        dy_spec = BS((tm, blocks * n), lambda i, ko, cc: (i, cc))
        w_spec = BS((blocks, tko, n), lambda i, ko, cc: (cc, ko, 0))
        return _mm(name, grid, dy, dy_spec, w, w_spec, NT, outs, (tm, tko), extras, epilogue, blocks=blocks)[0]
    tc = tc or _tile(n, NT_CONTRACT)
    per = n // tc
    ncontract = ncontract or J * n
    grid = (M // tm, K // tko, ncontract // tc)
    if dy_spec is None:
        dy_spec = BS((tm, tc), lambda i, ko, cc: (i, cc))
    w_spec = BS((None, tko, tc), lambda i, ko, cc: ((cc + coff) // per, ko, (cc + coff) % per))
    return _mm(name, grid, dy, dy_spec, w, w_spec, NT, outs, (tm, tko), extras, epilogue)[0]


def _mm_tn(name, x, dy, wshape, *, ts, joff=0, ncols=None, tn=None, dy_spec=None, carry=None):
    J, K, n = wshape
    S = x.shape[0]
    tn = tn or _tile(n, 1024)
    tkm = _tile(K, 512)
    per = n // tn
    ncols = ncols or J * n
    grid = (K // tkm, ncols // tn, S // ts)
    x_spec = BS((ts, tkm), lambda ki, j, sc: (sc, ki))
    if dy_spec is None:
        dy_spec = BS((ts, tn), lambda ki, j, sc: (sc, j))
    outs = [(SDS(wshape, BF16), BS((None, tkm, tn), lambda ki, j, sc: ((j + joff) // per, ki, (j + joff) % per)))]
    return _mm(name, grid, x, x_spec, dy, dy_spec, TN, outs, (tkm, tn), carry=None if carry is None else (carry, 0))[0]


def _rows(name, body, ins, outs, S, tr, sequential=False):
    def spec(shape, kind):
        if kind == "row":
            return BS((tr, shape[1]), lambda i: (i, 0))
        if kind == "order":
            return BS(memory_space=pl.ANY)
        return BS(tuple(shape), lambda i: (0,) * len(shape))
    return pl.pallas_call(
        body, out_shape=[o[0] for o in outs], grid=(S // tr,),
        in_specs=[spec(a.shape, k) for a, k in ins], out_specs=[spec(o[0].shape, o[1]) for o in outs],
        name=name, compiler_params=_cparams(("arbitrary" if sequential else "parallel",)))(*[a for a, _ in ins])


def _rms_fwd(h, g, tr=256, after=()):
    S, D = h.shape
    n_after = len(after)

    def body(h_ref, g_ref, *rest):
        u_ref = rest[n_after]
        x = h_ref[...]
        r = lax.rsqrt(jnp.mean(x * x, axis=-1, keepdims=True) + NORM_EPS)
        u_ref[...] = ((x * r) * g_ref[...]).astype(BF16)

    ins = [(h, "row"), (g, "full")] + [(t, "order") for t in after]
    return _rows("rms_fwd", body, ins, [(SDS((S, D), BF16), "row")], S, tr)[0]


def _rms_bwd(h, g, dy, dres, tr=256, after=()):
    S, D = h.shape
    n_after = len(after)

    def body(h_ref, g_ref, dy_ref, dres_ref, *rest):
        dh_ref, dhb_ref, dg_ref = rest[n_after:]
        x = h_ref[...]
        r = lax.rsqrt(jnp.mean(x * x, axis=-1, keepdims=True) + NORM_EPS)
        xhat = x * r
        dyv = dy_ref[...]
        gy = dyv * g_ref[...]
        dx = r * (gy - xhat * jnp.mean(gy * xhat, axis=-1, keepdims=True))
        dh = dres_ref[...] + dx
        dh_ref[...] = dh
        dhb_ref[...] = dh.astype(BF16)
        part = jnp.sum(dyv * xhat, axis=0, keepdims=True)

        @pl.when(pl.program_id(0) == 0)
        def _():
            dg_ref[...] = part

        @pl.when(pl.program_id(0) > 0)
        def _():
            dg_ref[...] += part

    ins = [(h, "row"), (g, "full"), (dy, "row"), (dres, "row")] + [(t, "order") for t in after]
    return _rows("rms_bwd", body, ins,
                 [(SDS((S, D), F32), "row"), (SDS((S, D), BF16), "row"), (SDS((1, D), F32), "full")], S, tr, True)


def _final(h, g, target, tr=256):
    S, D = h.shape

    def body(h_ref, g_ref, t_ref, dh_ref, dg_ref, loss_ref):
        x = h_ref[...]
        r = lax.rsqrt(jnp.mean(x * x, axis=-1, keepdims=True) + NORM_EPS)
        xhat = x * r
        gv = g_ref[...]
        err = xhat * gv - t_ref[...]
        lpart = 0.5 * jnp.sum(jnp.mean(err * err, axis=-1, keepdims=True), axis=0, keepdims=True)
        dyv = err * (1.0 / D)
        gy = dyv * gv
        dh_ref[...] = r * (gy - xhat * jnp.mean(gy * xhat, axis=-1, keepdims=True))
        part = jnp.sum(dyv * xhat, axis=0, keepdims=True)
        lrow = jnp.broadcast_to(lpart, (1, LANES))

        @pl.when(pl.program_id(0) == 0)
        def _():
            dg_ref[...] = part
            loss_ref[...] = lrow

        @pl.when(pl.program_id(0) > 0)
        def _():
            dg_ref[...] += part
            loss_ref[...] += lrow

    return _rows("final_head", body, [(h, "row"), (g, "full"), (target, "row")],
                 [(SDS((S, D), F32), "row"), (SDS((1, D), F32), "full"), (SDS((1, LANES), F32), "full")], S, tr, True)


def _ple_bwd(dh, pg, pp, tr=256, after=()):
    S, D = dh.shape
    n_after = len(after)

    def body(dh_ref, pg_ref, pp_ref, *rest):
        dpp_ref, dpgz_ref = rest[n_after:]
        d, g, q = dh_ref[...], pg_ref[...], pp_ref[...]
        dpp_ref[...] = (d * g).astype(BF16)
        dpgz_ref[...] = (d * q * g * (1.0 - g)).astype(BF16)

    ins = [(dh, "row"), (pg, "row"), (pp, "row")] + [(t, "order") for t in after]
    return _rows("ple_bwd", body, ins, [(SDS((S, D), BF16), "row"), (SDS((S, D), BF16), "row")], S, tr)


def _merge_bwd(dmerged, ya, yb, gates, tr=256):
    S, D = dmerged.shape

    def body(dm_ref, ya_ref, yb_ref, g_ref, dya_ref, dyb_ref, dgz_ref, dbg_ref):
        dm, a, b = dm_ref[...], ya_ref[...], yb_ref[...]
        g1, g2 = g_ref[:, :D], g_ref[:, D:]
        dya_ref[...] = (dm * g1).astype(BF16)
        dyb_ref[...] = (dm * g2).astype(BF16)
        dz1 = dm * a * g1 * (1.0 - g1)
        dz2 = dm * b * g2 * (1.0 - g2)
        dgz_ref[:, :D] = dz1.astype(BF16)
        dgz_ref[:, D:] = dz2.astype(BF16)
        p1 = jnp.sum(dz1, axis=0, keepdims=True)
        p2 = jnp.sum(dz2, axis=0, keepdims=True)

        @pl.when(pl.program_id(0) == 0)
        def _():
            dbg_ref[:, :D] = p1
            dbg_ref[:, D:] = p2

        @pl.when(pl.program_id(0) > 0)
        def _():
            dbg_ref[:, :D] += p1
            dbg_ref[:, D:] += p2

    return _rows("merge_bwd", body, [(dmerged, "row"), (ya, "row"), (yb, "row"), (gates, "row")],
                 [(SDS((S, D), BF16), "row"), (SDS((S, D), BF16), "row"), (SDS((S, 2 * D), BF16), "row"),
                  (SDS((1, 2 * D), F32), "full")], S, tr, True)


def _mix_weights(l0, l1, l2):
    mx = jnp.maximum(jnp.maximum(l0, l1), l2)
    e0, e1, e2 = jnp.exp(l0 - mx), jnp.exp(l1 - mx), jnp.exp(l2 - mx)
    inv = 1.0 / (e0 + e1 + e2)
    return e0 * inv, e1 * inv, e2 * inv


def _mix_fwd(os_, ls_, tr=512):
    S = os_[0].shape[0]

    def body(o0, o1, o2, l0, l1, l2, y_ref):
        w0, w1, w2 = _mix_weights(l0[...], l1[...], l2[...])
        y_ref[...] = (w0 * o0[...] + w1 * o1[...] + w2 * o2[...]).astype(BF16)

    return _rows("mix_fwd", body, [(a, "row") for a in (*os_, *ls_)], [(SDS((S, BRANCH), BF16), "row")], S, tr)[0]


def _mix_bwd(dy, os_, ls_, tr=512, after=()):
    S = dy.shape[0]
    n_after = len(after)

    def head_sums(t):
        return jnp.concatenate(
            [jnp.broadcast_to(jnp.sum(t[:, j * HEAD_DIM:(j + 1) * HEAD_DIM], axis=-1, keepdims=True), (tr, HEAD_DIM))
             for j in range(HEADS_PER_GROUP)], axis=1)

    def body(dy_ref, o0, o1, o2, l0, l1, l2, *rest):
        d0, d1, d2, p0, p1, p2 = rest[n_after:]
        ws = _mix_weights(l0[...], l1[...], l2[...])
        d = dy_ref[...]
        es = [head_sums(d * o[...]) for o in (o0, o1, o2)]
        ebar = ws[0] * es[0] + ws[1] * es[1] + ws[2] * es[2]
        for w, e, d_ref, p_ref in zip(ws, es, (d0, d1, d2), (p0, p1, p2)):
            d_ref[...] = (w * d).astype(BF16)
            p_ref[...] = w * (e - ebar)

    outs = [(SDS((S, BRANCH), BF16), "row")] * 3 + [(SDS((S, BRANCH), F32), "row")] * 3
    ins = [(a, "row") for a in (dy, *os_, *ls_)] + [(t, "order") for t in after]
    return _rows("mix_bwd", body, ins, outs, S, tr)


def _iota2(shape, dim):
    return lax.broadcasted_iota(jnp.int32, shape, dim)


def _fgate_fwd(zf, bf):
    S = zf.shape[0]
    nblk = S // BLOCK

    def body(zf_ref, bf_ref, cb_ref, ct_ref, flt_ref):
        tri = (_iota2((BLOCK, BLOCK), 0) >= _iota2((BLOCK, BLOCK), 1)).astype(F32)
        sel8 = (_iota2((8, LANES), 0) == _iota2((8, LANES), 1)).astype(F32)
        carry = jnp.zeros((1, LANES), F32)
        for blk in range(nblk):
            rows = pl.ds(blk * BLOCK, BLOCK)
            fl = zf_ref[rows, :] + bf_ref[...]
            ls = jnp.minimum(fl, 0.0) - jnp.log(1.0 + jnp.exp(-jnp.abs(fl)))
            c = jnp.dot(tri, ls, precision=HI, preferred_element_type=F32) + carry
            carry = carry + jnp.sum(ls, axis=0, keepdims=True)
            for hd in range(N_FOX):
                e_h = (_iota2((LANES, LANES), 0) == hd).astype(F32)
                cb_ref[rows, hd * LANES:(hd + 1) * LANES] = jnp.dot(c, e_h, precision=HI, preferred_element_type=F32)
            ct_ref[:, rows] = lax.dot_general(sel8, c, NT, precision=HI, preferred_element_type=F32)
            flt_ref[:, rows] = lax.dot_general(sel8, fl, NT, precision=HI, preferred_element_type=F32)

    vm = BS(memory_space=pltpu.VMEM)
    return pl.pallas_call(
        body, out_shape=[SDS((S, N_FOX * LANES), F32), SDS((8, S), F32), SDS((8, S), F32)],
        in_specs=[vm, vm], out_specs=[vm, vm, vm], name="fgate_fwd",
        compiler_params=pltpu.CompilerParams(vmem_limit_bytes=VMEM_LIMIT))(zf, bf)


def _fgate_bwd(dct, flt):
    S = dct.shape[1]
    nblk = S // BLOCK

    def body(dct_ref, flt_ref, dzf_ref, dbf_ref):
        rowid = _iota2((8, BLOCK), 0)
        tri_r = (_iota2((BLOCK, BLOCK), 0) >= _iota2((BLOCK, BLOCK), 1)).astype(F32)
        carry = jnp.zeros((8, 1), F32)
        dbf = jnp.zeros((8, 1), F32)
        for blk in reversed(range(nblk)):
            cols = pl.ds(blk * BLOCK, BLOCK)
            dc = jnp.where(rowid < N_FOX, dct_ref[:, cols], 0.0)
            dls = jnp.dot(dc, tri_r, precision=HI, preferred_element_type=F32) + carry
            carry = carry + jnp.sum(dc, axis=1, keepdims=True)
            dfl = jnp.where(rowid < N_FOX, dls * _sigmoid(-flt_ref[:, cols]), 0.0)
            dbf = dbf + jnp.sum(dfl, axis=1, keepdims=True)
            sq = jnp.concatenate([dfl, jnp.zeros((BLOCK - 8, BLOCK), F32)], axis=0)
            dzf_ref[cols, :] = sq.T.astype(BF16)
        dbf_ref[...] = jnp.broadcast_to(dbf, (8, LANES))

    vm = BS(memory_space=pltpu.VMEM)
    return pl.pallas_call(
        body, out_shape=[SDS((S, LANES), BF16), SDS((8, LANES), F32)],
        in_specs=[vm, vm], out_specs=[vm, vm], name="fgate_bwd",
        compiler_params=pltpu.CompilerParams(vmem_limit_bytes=VMEM_LIMIT))(dct, flt)


def _rope_tables(S):
    inv = ROPE_THETA ** (-jnp.arange(ROPE_HALF, dtype=F32) / ROPE_HALF)
    ang = jnp.arange(S, dtype=F32)[:, None] * inv[None, :]
    cos, sin = jnp.cos(ang), jnp.sin(ang)
    pad = HEAD_DIM - ROPE_DIM
    tc = jnp.concatenate([cos, cos, jnp.ones((S, pad), F32)], axis=1)
    ta = jnp.concatenate([-sin, jnp.zeros((S, HEAD_DIM - ROPE_HALF), F32)], axis=1)
    tb = jnp.concatenate([jnp.zeros((S, ROPE_HALF), F32), sin, jnp.zeros((S, pad), F32)], axis=1)
    return tc, ta, tb


def _rope(x, c, a, b):
    return x * c + pltpu.roll(x, HEAD_DIM - ROPE_HALF, 1) * a + pltpu.roll(x, ROPE_HALF, 1) * b


def _rope_t(g, c, a, b):
    return g * c - pltpu.roll(g, ROPE_HALF, 1) * b - pltpu.roll(g, HEAD_DIM - ROPE_HALF, 1) * a


DIL_TQ = 256


def _dil_extent(i, window):
    return max(0, i * DIL_TQ - window), (i + 1) * DIL_TQ


def _dil_mask(i, lo, ext, window, d):
    diff = (i * DIL_TQ - lo) + _iota2((DIL_TQ, ext), 0) - _iota2((DIL_TQ, ext), 1)
    mask = (diff >= 0) & (diff <= window)
    return mask & ((diff & (d - 1)) == 0) if d > 1 else mask


def _wide(t, w):
    return t if w == LANES else jnp.concatenate([t] * (w // LANES), axis=1)


def _dil_fwd(z3, tabs, grp, S):
    window, d = DIL_GROUPS[grp]
    nq = S // DIL_TQ

    def body(qkv_ref, c_ref, a_ref, b_ref, o_ref, lse_ref, qs, ks):
        c, a, b = c_ref[...], a_ref[...], b_ref[...]
        qs[...] = _rope(qkv_ref[0].astype(F32), c, a, b).astype(BF16)
        ks[...] = _rope(qkv_ref[1].astype(F32), c, a, b).astype(BF16)
        for i in range(nq):
            lo, hi = _dil_extent(i, window)
            rows = pl.ds(i * DIL_TQ, DIL_TQ)
            s = lax.dot_general(qs[rows, :], ks[lo:hi, :], NT, preferred_element_type=F32) * SCALE
            s = jnp.where(_dil_mask(i, lo, hi - lo, window, d), s, -jnp.inf)
            m = jnp.max(s, axis=-1, keepdims=True)
            p = jnp.exp(s - m)
            l = jnp.sum(p, axis=-1, keepdims=True)
            o = jnp.dot(p.astype(BF16), qkv_ref[2, lo:hi, :], preferred_element_type=F32)
            o_ref[rows, :] = o / l
            lse_ref[rows, :] = jnp.broadcast_to(m + jnp.log(l), (DIL_TQ, HEAD_DIM))

    hh0 = grp * HEADS_PER_GROUP
    tspec = BS((S, HEAD_DIM), lambda j: (0, 0))
    ospec = BS((S, HEAD_DIM), lambda j: (0, j))
    return pl.pallas_call(
        body, out_shape=[SDS((S, BRANCH), F32)] * 2, grid=(HEADS_PER_GROUP,),
        in_specs=[BS((3, S, HEAD_DIM), lambda j: (0, 0, hh0 + j)), tspec, tspec, tspec],
        out_specs=[ospec, ospec], scratch_shapes=[pltpu.VMEM((S, HEAD_DIM), BF16)] * 2,
        name="dil_fwd", compiler_params=_cparams(("parallel",)))(z3, *tabs)


def _dil_bwd(z3, tabs, grp, S, do, lse, dpl, dz3):
    window, d = DIL_GROUPS[grp]
    nq = S // DIL_TQ
    n_carry = 0 if dz3 is None else 1

    def body(*refs):
        qkv_ref, c_ref, a_ref, b_ref, do_ref, lse_ref, dpl_ref = refs[:7]
        dz_ref = refs[7 + n_carry]
        qs, ks, dq_acc, dk_acc, dv_acc = refs[8 + n_carry:]
        c, a, b = c_ref[...], a_ref[...], b_ref[...]
        qs[...] = _rope(qkv_ref[0].astype(F32), c, a, b).astype(BF16)
        ks[...] = _rope(qkv_ref[1].astype(F32), c, a, b).astype(BF16)
        dk_acc[...] = jnp.zeros_like(dk_acc)
        dv_acc[...] = jnp.zeros_like(dv_acc)
        for i in range(nq):
            lo, hi = _dil_extent(i, window)
            w = hi - lo
            rows = pl.ds(i * DIL_TQ, DIL_TQ)
            q, k, v = qs[rows, :], ks[lo:hi, :], qkv_ref[2, lo:hi, :]
            s = lax.dot_general(q, k, NT, preferred_element_type=F32) * SCALE
            p = jnp.where(_dil_mask(i, lo, w, window, d), jnp.exp(s - _wide(lse_ref[rows, :], w)), 0.0)
            dob = do_ref[rows, :]
            dp = lax.dot_general(dob, v, NT, preferred_element_type=F32)
            delta = jnp.sum(p * dp, axis=-1, keepdims=True)
            ds = (p * (dp - delta + _wide(dpl_ref[rows, :], w))).astype(BF16)
            dq_acc[rows, :] = jnp.dot(ds, k, preferred_element_type=F32) * SCALE
            dk_acc[lo:hi, :] += lax.dot_general(ds, q, TN, preferred_element_type=F32) * SCALE
            dv_acc[lo:hi, :] += lax.dot_general(p.astype(BF16), dob, TN, preferred_element_type=F32)
        dz_ref[0] = _rope_t(dq_acc[...], c, a, b).astype(BF16)
        dz_ref[1] = _rope_t(dk_acc[...], c, a, b).astype(BF16)
        dz_ref[2] = dv_acc[...].astype(BF16)

    hh0 = grp * HEADS_PER_GROUP
    tspec = BS((S, HEAD_DIM), lambda j: (0, 0))
    bspec = BS((S, HEAD_DIM), lambda j: (0, j))
    zspec = BS((3, S, HEAD_DIM), lambda j: (0, 0, hh0 + j))
    arrays = [z3, *tabs, do, lse, dpl]
    specs = [zspec, tspec, tspec, tspec, bspec, bspec, bspec]
    alias = {}
    if dz3 is not None:
        arrays.append(dz3)
        specs.append(BS(memory_space=pl.ANY))
        alias = {len(arrays) - 1: 0}
    return pl.pallas_call(
        body, out_shape=SDS((3, S, ATTN), BF16), grid=(HEADS_PER_GROUP,), in_specs=specs, out_specs=zspec,
        scratch_shapes=[pltpu.VMEM((S, HEAD_DIM), BF16)] * 2 + [pltpu.VMEM((S, HEAD_DIM), F32)] * 3,
        input_output_aliases=alias, name="dil_bwd", compiler_params=_cparams(("parallel",)))(*arrays)


FOX_TQ = 256


def _fox_scores(q_ref, k_ref, cb_ref, ct_ref, i, ext):
    rows = pl.ds(i * FOX_TQ, FOX_TQ)
    s = lax.dot_general(q_ref[0, rows, :], k_ref[0, 0:ext, :], NT, preferred_element_type=F32) * SCALE
    s = s + _wide(cb_ref[rows, :], ext) - ct_ref[:, 0:ext]
    qpos = i * FOX_TQ + _iota2((FOX_TQ, ext), 0)
    kpos = _iota2((FOX_TQ, ext), 1)
    return jnp.where(kpos <= qpos, s, -jnp.inf)


def _fox_specs(S):
    qs = BS((1, S, HEAD_DIM), lambda h: (0, 0, FOX_HEAD0 + h))
    ks = BS((1, S, HEAD_DIM), lambda h: (1, 0, FOX_HEAD0 + h))
    vs = BS((1, S, HEAD_DIM), lambda h: (2, 0, FOX_HEAD0 + h))
    hb = BS((S, HEAD_DIM), lambda h: (0, h))
    ct = BS((None, 1, S), lambda h: (h, 0, 0))
    return qs, ks, vs, hb, ct


def _fox_fwd(z3, cb, ct, S):
    nq = S // FOX_TQ

    def body(q_ref, k_ref, v_ref, cb_ref, ct_ref, y_ref, lse_ref):
        for i in range(nq):
            ext = (i + 1) * FOX_TQ
            rows = pl.ds(i * FOX_TQ, FOX_TQ)
            s = _fox_scores(q_ref, k_ref, cb_ref, ct_ref, i, ext)
            m = jnp.max(s, axis=-1, keepdims=True)
            p = jnp.exp(s - m)
            l = jnp.sum(p, axis=-1, keepdims=True)
            o = jnp.dot(p.astype(BF16), v_ref[0, 0:ext, :], preferred_element_type=F32)
            y_ref[rows, :] = (o / l).astype(BF16)
            lse_ref[rows, :] = jnp.broadcast_to(m + jnp.log(l), (FOX_TQ, HEAD_DIM))

    qs, ks, vs, hb, cts = _fox_specs(S)
    return pl.pallas_call(
        body, out_shape=[SDS((S, BRANCH), BF16), SDS((S, BRANCH), F32)], grid=(N_FOX,),
        in_specs=[qs, ks, vs, hb, cts], out_specs=[hb, hb], name="fox_fwd",
        compiler_params=_cparams(("parallel",)))(z3, z3, z3, cb, ct.reshape(8, 1, S))


def _fox_bwd(z3, cb, ct, S, dy, lse, dz3):
    nq = S // FOX_TQ

    def body(q_ref, k_ref, v_ref, cb_ref, ct_ref, dy_ref, lse_ref, _, dz_ref, dct_ref, dk_acc, dv_acc):
        dk_acc[...] = jnp.zeros_like(dk_acc)
        dv_acc[...] = jnp.zeros_like(dv_acc)
        dct_ref[...] = jnp.zeros_like(dct_ref)
        for i in range(nq):
            ext = (i + 1) * FOX_TQ
            rows = pl.ds(i * FOX_TQ, FOX_TQ)
            s = _fox_scores(q_ref, k_ref, cb_ref, ct_ref, i, ext)
            p = jnp.exp(s - _wide(lse_ref[rows, :], ext))
            dob = dy_ref[rows, :]
            dp = lax.dot_general(dob, v_ref[0, 0:ext, :], NT, preferred_element_type=F32)
            ds = p * (dp - jnp.sum(p * dp, axis=-1, keepdims=True))
            dsb = ds.astype(BF16)
            dz_ref[0, rows, :] = (jnp.dot(dsb, k_ref[0, 0:ext, :], preferred_element_type=F32) * SCALE).astype(BF16)
            dk_acc[0:ext, :] += lax.dot_general(dsb, q_ref[0, rows, :], TN, preferred_element_type=F32) * SCALE
            dv_acc[0:ext, :] += lax.dot_general(p.astype(BF16), dob, TN, preferred_element_type=F32)
            dct_ref[:, 0:ext] -= jnp.sum(ds, axis=0, keepdims=True)
        dz_ref[1] = dk_acc[...].astype(BF16)
        dz_ref[2] = dv_acc[...].astype(BF16)

    qs, ks, vs, hb, cts = _fox_specs(S)
    zspec = BS((3, S, HEAD_DIM), lambda h: (0, 0, FOX_HEAD0 + h))
    dz, dct = pl.pallas_call(
        body, out_shape=[SDS((3, S, ATTN), BF16), SDS((8, 1, S), F32)], grid=(N_FOX,),
        in_specs=[qs, ks, vs, hb, cts, hb, hb, BS(memory_space=pl.ANY)], out_specs=[zspec, cts],
        scratch_shapes=[pltpu.VMEM((S, HEAD_DIM), F32)] * 2, input_output_aliases={7: 0}, name="fox_bwd",
        compiler_params=_cparams(("parallel",)))(z3, z3, z3, cb, ct.reshape(8, 1, S), dy, lse, dz3)
    return dz, dct.reshape(8, S)


TM = 1024


def _layer_fwd(h, p_bf, sm, W, tabs, after=(), mid=None):
    S, D = h.shape
    qkv_cols = 3 * ATTN
    u = _rms_fwd(h, sm["g_mix"], after=after)
    tn = 512
    per = ATTN // tn
    z3 = _mm_nn("mm_qkv", u, W["in"], [BF16], tm=TM, ncols=qkv_cols, tn=tn, out_shapes=[(3, S, ATTN)],
                out_specs=[BS((None, TM, tn), lambda i, j, k: (j // per, i, j % per))])[0]
    zf = _mm_nn("mm_f", u, W["in"], [F32], tm=TM, joff=qkv_cols // LANES, ncols=LANES, tn=LANES)[0]
    cb, ct, flt = _fgate_fwd(zf, sm["b_f"])
    os_, ls_ = [], []
    for grp in range(len(DIL_GROUPS)):
        o, l = _dil_fwd(z3, tabs, grp, S)
        os_.append(o)
        ls_.append(l)
    ya_h = _mix_fwd(os_, ls_)
    yb_h, lse_fox = _fox_fwd(z3, cb, ct, S)

    tg = _tile(W["gate"].shape[2], 512)
    gates = _mm_nn("mm_gate", u, W["gate"], [F32], tm=TM,
                   extras=[(sm["b_gate"], BS((1, tg), lambda i, j, k: (0, j)))],
                   epilogue=lambda acc, bias: (_sigmoid(acc + bias),))[0]
    ya = _mm_nn("mm_bra", ya_h, W["bra"], [F32], tm=TM)[0]
    tb = _tile(W["brb"].shape[2], 512)
    g2off = D // tb
    yb, merged = _mm_nn(
        "mm_brb", yb_h, W["brb"], [F32, BF16], tm=TM,
        extras=[(ya, BS((TM, tb), lambda i, j, k: (i, j))), (gates, BS((TM, tb), lambda i, j, k: (i, j))),
                (gates, BS((TM, tb), lambda i, j, k: (i, j + g2off)))],
        epilogue=lambda acc, a, g1, g2: (acc, g1 * a + g2 * acc))
    to = _tile(D, 512)
    res_spec = BS((TM, to), lambda i, j, k: (i, j))
    h1 = _mm_nn("mm_o", merged, W["o"], [F32], tm=TM, extras=[(h, res_spec)], epilogue=lambda acc, r: (r + acc,))[0]

    m = _rms_fwd(h1, sm["g_mlp"], after=() if mid is None else mid(h1))

    def up_epi(acc):
        r = jnp.maximum(acc, 0.0)
        return r, r * r

    relu_a, act = _mm_nn("mm_up", m, W["up"], [BF16, BF16], tm=TM, epilogue=up_epi)
    h2 = _mm_nn("mm_down", act, W["down"], [F32], tm=TM, tk=_tile(act.shape[1], 4096), extras=[(h1, res_spec)],
                epilogue=lambda acc, r: (r + acc,))[0]

    nrm = _rms_fwd(h2, sm["g_ple"])
    pp = _mm_nn("mm_ple", p_bf, W["ple"], [F32], tm=TM)[0]

    def pg_epi(acc, r, q):
        g = _sigmoid(acc)
        return g, r + g * q

    pg, h3 = _mm_nn("mm_pg", nrm, W["pg"], [F32, F32], tm=TM, extras=[(h2, res_spec), (pp, res_spec)], epilogue=pg_epi)
    saved = dict(h=h, u=u, z3=z3, cb=cb, ct=ct, flt=flt, os=os_, ls=ls_, ya_h=ya_h, yb_h=yb_h, lse_fox=lse_fox,
                 gates=gates, ya=ya, yb=yb, merged=merged, h1=h1, m=m, relu_a=relu_a, act=act, h2=h2, nrm=nrm,
                 pg=pg, pp=pp, p_bf=p_bf)
    return h3, saved


def _layer_bwd(dh3, sv, sm, W, tabs, after=(), ready=None):
    S, D = dh3.shape
    dW, dsm = {}, {}

    dpp, dpgz = _ple_bwd(dh3, sv["pg"], sv["pp"], after=after)
    dW["ple"] = _mm_tn("tn_ple", sv["p_bf"], dpp, W["ple"].shape, ts=S)
    dW["pg"] = _mm_tn("tn_pg", sv["nrm"], dpgz, W["pg"].shape, ts=S)
    dn = _mm_nt("nt_pg", dpgz, W["pg"], F32, tm=TM)
    dh2, dh2b, dsm["g_ple"] = _rms_bwd(sv["h2"], sm["g_ple"], dn, dh3)

    ta = _tile(W["up"].shape[1], 512)
    tff = _tile(W["down"].shape[1], 512)
    da = _mm_nt("nt_down", dh2b, W["down"], BF16, tm=TM,
                extras=[(sv["relu_a"], BS((TM, tff), lambda i, ko, cc: (i, ko)))],
                epilogue=lambda acc, r: (acc * (2.0 * r.astype(F32)),))
    dW["down"] = _mm_tn("tn_down", sv["act"], dh2b, W["down"].shape, ts=S)
    dm = _mm_nt("nt_up", da, W["up"], F32, tm=TM)
    dW["up"] = _mm_tn("tn_up", sv["m"], da, W["up"].shape, ts=S)
    dh1, dh1b, dsm["g_mlp"] = _rms_bwd(sv["h1"], sm["g_mlp"], dm, dh2, after=() if ready is None else ready("mlp", dW))

    dmerged = _mm_nt("nt_o", dh1b, W["o"], F32, tm=TM)
    dW["o"] = _mm_tn("tn_o", sv["merged"], dh1b, W["o"].shape, ts=S)
    dya, dyb, dgz, dsm["b_gate"] = _merge_bwd(dmerged, sv["ya"], sv["yb"], sv["gates"])
    dya_h = _mm_nt("nt_bra", dya, W["bra"], F32, tm=TM)
    dyb_h = _mm_nt("nt_brb", dyb, W["brb"], BF16, tm=TM)
    dW["bra"] = _mm_tn("tn_bra", sv["ya_h"], dya, W["bra"].shape, ts=S)
    dW["brb"] = _mm_tn("tn_brb", sv["yb_h"], dyb, W["brb"].shape, ts=S)
    dW["gate"] = _mm_tn("tn_gate", sv["u"], dgz, W["gate"].shape, ts=S)
    du = _mm_nt("nt_gate", dgz, W["gate"], F32, tm=TM)

    dos_, dpls_ = [], []
    mixed = _mix_bwd(dya_h, sv["os"], sv["ls"], after=() if ready is None else ready("merge", dW))
    dos_, dpls_ = mixed[:3], mixed[3:]
    dz3 = None
    for grp in range(len(DIL_GROUPS)):
        dz3 = _dil_bwd(sv["z3"], tabs, grp, S, dos_[grp], sv["ls"][grp], dpls_[grp], dz3)
    dz3, dct = _fox_bwd(sv["z3"], sv["cb"], sv["ct"], S, dyb_h, sv["lse_fox"], dz3)
    dzf, dsm["b_f"] = _fgate_bwd(dct, sv["flt"])

    tn = 1024
    per = ATTN // tn
    qkv_cols = 3 * ATTN
    dwin = _mm_tn("tn_qkv", sv["u"], dz3, W["in"].shape, ts=S, ncols=qkv_cols, tn=tn,
                  dy_spec=BS((None, S, tn), lambda ki, j, sc: (j // per, sc, j % per)))
    dW["in"] = _mm_tn("tn_f", sv["u"], dzf, W["in"].shape, ts=S, joff=qkv_cols // LANES, ncols=LANES, tn=LANES,
                      carry=dwin)
    tu = _tile(D, 512)
    prev_spec = BS((TM, tu), lambda i, ko, cc: (i, ko))
    du = _mm_nt("nt_qkv", dz3, W["in"], F32, tm=TM, ncontract=qkv_cols, tc=ATTN,
                dy_spec=BS((None, TM, ATTN), lambda i, ko, cc: (cc, i, 0)),
                extras=[(du, prev_spec)], epilogue=lambda acc, prev: (prev + acc,))
    du = _mm_nt("nt_f", dzf, W["in"], F32, tm=TM, coff=qkv_cols // LANES, ncontract=LANES, tc=LANES,
                extras=[(du, prev_spec)], epilogue=lambda acc, prev: (prev + acc,))
    dh0, _, dsm["g_mix"] = _rms_bwd(sv["h"], sm["g_mix"], du, dh1)
    return dh0, dW, dsm


def _shard_tiles(R, Cp):
    tc = Cp if Cp <= 2048 else _tile(Cp, 1024)
    tr = R
    while tr * tc > 256 * 1024 and tr % 16 == 0:
        tr //= 2
    return tr, tc


def _padded_cols(C):
    return -(-C // LANES) * LANES


def _cast_bf16(name, w, me):
    Lr, R, C = w.shape
    Cp = _padded_cols(C)
    tr, tc = _shard_tiles(R, Cp)

    def body(me_ref, w_ref, *o_refs):
        for l in range(Lr):
            x = w_ref[l]
            if Cp != C:
                col = pl.program_id(1) * tc + _iota2((tr, tc), 1)
                x = jnp.where(col < C, x, 0.0)
            o_refs[l][...] = x.astype(BF16)

    grid_spec = pltpu.PrefetchScalarGridSpec(
        num_scalar_prefetch=1, grid=(R // tr, Cp // tc),
        in_specs=[BS((Lr, tr, tc), lambda i, j, me_ref: (0, i, j))],
        out_specs=[BS((None, tr, tc), lambda i, j, me_ref: (me_ref[0], i, j))] * Lr)
    return pl.pallas_call(
        body, out_shape=[SDS((NDEV, R, Cp), BF16)] * Lr, grid_spec=grid_spec,
        name="cast_" + name, compiler_params=_cparams(("parallel", "parallel")))(me, w)


def _adam_math(w, g, m, v):
    m = ADAM_B1 * m + (1.0 - ADAM_B1) * g
    v = ADAM_B2 * v + (1.0 - ADAM_B2) * (g * g)
    m_hat = m / (1.0 - ADAM_B1 ** ADAM_STEP)
    v_hat = v / (1.0 - ADAM_B2 ** ADAM_STEP)
    delta = -ADAM_LR * (m_hat / (jnp.sqrt(v_hat) + ADAM_EPS) + ADAM_WD * w)
    return delta, m, v


def _pair_sum(name, dw, sib, core):
    _, R, Cp = dw.shape
    tr, tc = _shard_tiles(R, Cp)

    def body(core_ref, a_ref, b_ref, o_ref):
        o_ref[...] = (a_ref[...].astype(F32) + b_ref[...].astype(F32)).astype(BF16)

    grid_spec = pltpu.PrefetchScalarGridSpec(
        num_scalar_prefetch=1, grid=(4, R // tr, Cp // tc),
        in_specs=[BS((None, tr, tc), lambda q, i, j, core_ref: (2 * q + core_ref[0], i, j)),
                  BS((None, tr, tc), lambda q, i, j, core_ref: (q, i, j))],
        out_specs=BS((None, tr, tc), lambda q, i, j, core_ref: (q, i, j)))
    return pl.pallas_call(
        body, out_shape=SDS((4, R, Cp), BF16), grid_spec=grid_spec, name="pair_sum_" + name,
        compiler_params=_cparams(("parallel", "parallel", "parallel")))(core, dw, sib)


def _adamw(name, layer, w, m, v, recv, own, me, carried):
    Lr, R, C = w.shape
    nslot, _, Cp = recv.shape
    tr, tc = _shard_tiles(R, Cp)
    n_carry = 0 if carried is None else 4

    def body(me_ref, w_ref, m_ref, v_ref, r_ref, own_ref, *rest):
        g_out, d_out, m_out, v_out, token = rest[n_carry:]
        mine = own_ref[...].astype(F32)
        g = None
        for s in range(nslot):
            part = jnp.where(me_ref[0] == s, mine, r_ref[s].astype(F32))
            g = part if g is None else g + part
        delta, mn, vn = _adam_math(w_ref[...], g, m_ref[...], v_ref[...])
        g_out[...] = g
        d_out[...] = delta
        m_out[...] = mn
        v_out[...] = vn
        token[...] = jnp.zeros_like(token)

    spec = BS((None, tr, tc), lambda i, j, me_ref: (layer, i, j))
    rspec = BS((nslot, tr, tc), lambda i, j, me_ref: (0, i, j))
    ospec = BS((None, tr, tc), lambda i, j, me_ref: (me_ref[0], i, j))
    arrays, specs, alias = [w, m, v, recv, own], [spec, spec, spec, rspec, ospec], {}
    if carried is not None:
        arrays += list(carried)
        specs += [BS(memory_space=pl.ANY)] * 4
        alias = {6 + k: k for k in range(4)}
    grid_spec = pltpu.PrefetchScalarGridSpec(
        num_scalar_prefetch=1, grid=(R // tr, Cp // tc), in_specs=specs,
        out_specs=[spec] * 4 + [BS((8, LANES), lambda i, j, me_ref: (0, 0))])
    return pl.pallas_call(
        body, out_shape=[SDS(w.shape, F32)] * 4 + [SDS((8, LANES), F32)], grid_spec=grid_spec,
        input_output_aliases=alias, name="adamw_" + name,
        compiler_params=_cparams(("arbitrary", "arbitrary")))(me, *arrays)


def _adamw_small(w, g, m, v):
    vm = BS(memory_space=pltpu.VMEM)

    def body(w_ref, g_ref, m_ref, v_ref, d_out, m_out, v_out):
        delta, mn, vn = _adam_math(w_ref[...], g_ref[...], m_ref[...], v_ref[...])
        d_out[...] = delta
        m_out[...] = mn
        v_out[...] = vn

    return pl.pallas_call(body, out_shape=[SDS(w.shape, F32)] * 3, in_specs=[vm] * 4, out_specs=[vm] * 3,
                          name="adamw_small")(w, g, m, v)


def _place():
    return lax.axis_index("x"), lax.axis_index("y"), lax.axis_index("c")


def _slot(px, py, pc):
    return 4 * px + 2 * py + pc


def _other_chips(x, y):
    return [(1 - x, y), (x, 1 - y), (1 - x, 1 - y)]


def _peer(x, y, c, rel):
    return (1 - x if rel & 4 else x, 1 - y if rel & 2 else y, 1 - c if rel & 1 else c)


def _plan_gather_ici(src, land):
    x, y, c = _place()
    own = land.at[_slot(x, y, c)]
    return [(own, own, (x, y, 1 - c))] + [(own, own, (*chip, c)) for chip in _other_chips(x, y)]


def _plan_gather_relay(src, land):
    x, y, c = _place()
    return [(land.at[_slot(*chip, c)], land.at[_slot(*chip, c)], (x, y, 1 - c)) for chip in _other_chips(x, y)]


def _plan_pair(src, land):
    x, y, c = _place()
    return [(src.at[2 * q + (1 - c)], land.at[q], (x, y, 1 - c)) for q in range(4)]


def _plan_scatter_chips(src, land):
    x, y, c = _place()
    return [(src.at[2 * cx + cy], land.at[2 * x + y], (cx, cy, c)) for cx, cy in _other_chips(x, y)]


EFFECT = pltpu.SideEffectType.DATAFLOW_SIDE_EFFECTING
HBM_SPEC = BS(memory_space=pltpu.HBM)
SEM_SPEC = BS(memory_space=pltpu.SEMAPHORE)
ORDER_SPEC = BS(memory_space=pl.ANY)


def _exchange_start(name, plan, ncopy, lands, srcs=None, after=()):
    bufs = ([] if srcs is None else list(srcs)) + list(lands)
    n, nb, ns = len(lands), len(bufs), len(bufs) - len(lands)
    n_after = len(after)

    def body(*refs):
        src_refs = refs[:ns] if ns else [None] * n
        land_refs = refs[ns:nb]
        send_sems, recv_sems = refs[nb + n_after], refs[nb + n_after + 1]
        token = refs[-1]
        for i in range(n):
            for k, (src, dst, to) in enumerate(plan(src_refs[i], land_refs[i])):
                pltpu.make_async_remote_copy(src_ref=src, dst_ref=dst, send_sem=send_sems.at[i * ncopy + k],
                                             recv_sem=recv_sems.at[i * ncopy + k], device_id=to,
                                             device_id_type=MESH).start()
        token[...] = jnp.zeros_like(token)

    out_shape = [pltpu.SemaphoreType.DMA((n * ncopy,)), pltpu.SemaphoreType.DMA((n * ncopy,))]
    out_shape += [pltpu.HBM(b.shape, b.dtype) for b in bufs] + [SDS((8, LANES), F32)]
    out = pl.pallas_call(
        body, name=name, out_shape=out_shape, in_specs=[HBM_SPEC] * nb + [ORDER_SPEC] * n_after,
        out_specs=[SEM_SPEC, SEM_SPEC] + [HBM_SPEC] * nb + [BS(memory_space=pltpu.VMEM)],
        input_output_aliases={i: 2 + i for i in range(nb)},
        compiler_params=pltpu.CompilerParams(has_side_effects=EFFECT))(
            *[pltpu.with_memory_space_constraint(b, pltpu.HBM) for b in bufs], *after)
    return out[0], out[1], out[2:2 + ns], out[2 + ns:2 + nb], out[-1]


def _exchange_wait(name, plan, send_sems, recv_sems, lands, srcs=None, after=()):
    bufs = ([] if srcs is None else list(srcs)) + list(lands)
    n, nb, ns = len(lands), len(bufs), len(bufs) - len(lands)

    def body(*refs):
        src_refs = refs[:ns] if ns else [None] * n
        land_refs = refs[ns:nb]
        send_ref, recv_ref = refs[nb], refs[nb + 1]
        for i in range(n):
            copies = plan(src_refs[i], land_refs[i])
            for k, (src, dst, to) in enumerate(copies):
                cp = pltpu.make_async_remote_copy(src_ref=src, dst_ref=dst, send_sem=send_ref.at[i * len(copies) + k],
                                                  recv_sem=recv_ref.at[i * len(copies) + k], device_id=to,
                                                  device_id_type=MESH)
                cp.wait_send()
                cp.wait_recv()

    out = pl.pallas_call(
        body, name=name, out_shape=[pltpu.HBM(b.shape, b.dtype) for b in bufs],
        in_specs=[HBM_SPEC] * nb + [SEM_SPEC, SEM_SPEC] + [ORDER_SPEC] * len(after), out_specs=[HBM_SPEC] * nb,
        input_output_aliases={i: i for i in range(nb)},
        compiler_params=pltpu.CompilerParams(has_side_effects=EFFECT))(*bufs, send_sems, recv_sems, *after)
    return out[:ns], out[ns:]


def _allreduce_small(v, after=()):
    R, Wd = v.shape
    n_after = len(after)

    def body(v_ref, *rest):
        o_ref, buf, send_sems, recv_sems = rest[n_after:]
        x, y, c = _place()
        me = _slot(x, y, c)
        buf[me] = v_ref[...]
        started = []
        for rel in range(1, NDEV):
            peer = _peer(x, y, c, rel)
            cp = pltpu.make_async_remote_copy(
                src_ref=v_ref, dst_ref=buf.at[me], send_sem=send_sems.at[rel - 1], recv_sem=recv_sems.at[rel - 1],
                device_id=peer, device_id_type=MESH)
            cp.start()
            started.append(cp)
        for cp in started:
            cp.wait()
        acc = buf[0]
        for j in range(1, NDEV):
            acc = acc + buf[j]
        o_ref[...] = acc

    vm = BS(memory_space=pltpu.VMEM)
    return pl.pallas_call(
        body, out_shape=SDS((R, Wd), F32), in_specs=[vm] + [ORDER_SPEC] * n_after, out_specs=vm,
        scratch_shapes=[pltpu.VMEM((NDEV, R, Wd), F32), pltpu.SemaphoreType.DMA((7,)), pltpu.SemaphoreType.DMA((7,))],
        name="allreduce_small")(v, *after)


def _as_matrix(name, g):
    if name in ROW_SHARDED:
        return g.reshape(1, NDEV * g.shape[1], g.shape[2])
    return g


def _pad8(a):
    return jnp.pad(a, ((0, -a.shape[0] % 8), (0, 0)))


def _pack_small(g_mix, g_mlp, g_ple, b_gate, g_final, b_f, extra_row=None):
    Lr, D = g_mix.shape
    rows = [g_mix, g_mlp, g_ple, b_gate.reshape(2 * Lr, D), g_final.reshape(1, D),
            jnp.pad(b_f.reshape(1, -1), ((0, 0), (0, D - b_f.size)))]
    rows.append(jnp.zeros((1, D), F32) if extra_row is None else extra_row)
    return jnp.concatenate([_pad8(r) for r in rows], axis=0)


def _unpack_small(pk, Lr, D, nf):
    o = 0
    out = []
    for rows, shape in ((Lr, (Lr, D)), (Lr, (Lr, D)), (Lr, (Lr, D)), (2 * Lr, (Lr, 2 * D)), (1, (D,))):
        out.append(pk[o:o + rows].reshape(shape))
        o += rows + (-rows % 8)
    out.append(pk[o, :Lr * nf].reshape(Lr, nf))
    return out, pk[o + 8, 0]


def kernel(x, p, g_mix, w_in, b_f, w_gate, b_gate, w_br_a, w_br_b, w_o, g_mlp, w_up, w_down, g_ple, w_ple, w_ple_gate, g_final, loss_target, m_g_mix, m_w_in, m_b_f, m_w_gate, m_b_gate, m_w_br_a, m_w_br_b, m_w_o, m_g_mlp, m_w_up, m_w_down, m_g_ple, m_w_ple, m_w_ple_gate, m_g_final, v_g_mix, v_w_in, v_b_f, v_w_gate, v_b_gate, v_w_br_a, v_w_br_b, v_w_o, v_g_mlp, v_w_up, v_w_down, v_g_ple, v_w_ple, v_w_ple_gate, v_g_final):
    Lr, D = g_mix.shape
    S = x.shape[1]
    nf = b_f.shape[1]
    big_w = dict(zip(BIG, (w_in, w_gate, w_br_a, w_br_b, w_o, w_up, w_down, w_ple, w_ple_gate)))
    big_m = dict(zip(BIG, (m_w_in, m_w_gate, m_w_br_a, m_w_br_b, m_w_o, m_w_up, m_w_down, m_w_ple, m_w_ple_gate)))
    big_v = dict(zip(BIG, (v_w_in, v_w_gate, v_w_br_a, v_w_br_b, v_w_o, v_w_up, v_w_down, v_w_ple, v_w_ple_gate)))

    x_pos, y_pos, c_pos = _place()
    me = _slot(x_pos, y_pos, c_pos).astype(jnp.int32).reshape(1)
    core = c_pos.astype(jnp.int32).reshape(1)
    chip = (2 * x_pos + y_pos).astype(jnp.int32).reshape(1)
    wbf = {k: _cast_bf16(k, big_w[k], me) for k in BIG}
    tabs = _rope_tables(S)
    bf_pad = jnp.pad(b_f, ((0, 0), (0, LANES - nf)))

    smalls = [dict(g_mix=g_mix[l:l + 1], b_f=bf_pad[l:l + 1], b_gate=b_gate[l:l + 1], g_mlp=g_mlp[l:l + 1],
                   g_ple=g_ple[l:l + 1]) for l in range(Lr)]

    gather_groups = (("in", "gate", "bra", "brb", "o"), ("up", "down", "ple", "pg"))
    order = [(l, gi) for l in range(Lr) for gi in range(len(gather_groups))]
    on_links = {}

    def links_start(k, after=()):
        l, gi = order[k]
        on_links[k] = _exchange_start(f"gather_ici_start_{l}_{gi}", _plan_gather_ici, 4,
                                      [wbf[n][l] for n in gather_groups[gi]], None, after)
        return on_links[k][4]

    def arrive(k, after):
        l, gi = order[k]
        send_sems, recv_sems, _, lands, _ = on_links.pop(k)
        _, lands = _exchange_wait(f"gather_ici_wait_{l}_{gi}", _plan_gather_ici, send_sems, recv_sems, lands, None,
                                  after)
        send_sems, recv_sems, _, lands, token = _exchange_start(
            f"gather_relay_start_{l}_{gi}", _plan_gather_relay, 3, lands)
        tokens = [token]
        if k + 2 < len(order):
            tokens.append(links_start(k + 2, (token,)))
        _, lands = _exchange_wait(f"gather_relay_wait_{l}_{gi}", _plan_gather_relay, send_sems, recv_sems, lands)
        return {n: _as_matrix(n, g) for n, g in zip(gather_groups[gi], lands)}, tuple(tokens)

    h = x.reshape(S, D)
    Ws, saves = [], []
    queued = links_start(1, (links_start(0),))
    for l in range(Lr):
        W, tokens = arrive(2 * l, (h,))
        if l == 0:
            tokens += (queued,)

        def mid(h1, l=l, W=W):
            more, mid_tokens = arrive(2 * l + 1, (h1,))
            W.update(more)
            return mid_tokens

        h, sv = _layer_fwd(h, p[l, 0].astype(BF16), smalls[l], W, tabs, tokens, mid)
        Ws.append(W)
        saves.append(sv)

    dh, dg_final, loss_part = _final(h, g_final.reshape(1, D), loss_target.reshape(S, D))

    big_out = {k: None for k in BIG}
    dsmalls = [None] * Lr
    groups = dict(mlp=("ple", "pg", "down", "up"), merge=("o", "bra", "brb", "gate"), proj=("in",))
    in_flight = []
    links_steps = 3

    def pair_start(l, stage, dW):
        dws = [dW[k].reshape(wbf[k][l].shape) for k in groups[stage]]
        lands = [lax.empty((4,) + d.shape[1:], BF16) for d in dws]
        started = _exchange_start(f"pair_start_{l}_{stage}", _plan_pair, 4, lands, dws)
        in_flight.append(dict(tag=f"{l}_{stage}", layer=l, names=groups[stage], on_links=False, started=started,
                              age=0))
        return started[4]

    def to_links(ex, after):
        send_sems, recv_sems, dws, lands, _ = ex["started"]
        dws, sib = _exchange_wait(f"pair_wait_{ex['tag']}", _plan_pair, send_sems, recv_sems, lands, dws, after)
        sums = [_pair_sum(k, d, s, core) for k, d, s in zip(ex["names"], dws, sib)]
        ex["started"] = _exchange_start(f"scatter_start_{ex['tag']}", _plan_scatter_chips, 3,
                                        [lax.empty(s.shape, BF16) for s in sums], sums)
        ex["on_links"], ex["age"] = True, 0
        return ex["started"][4]

    def finish(ex, after):
        send_sems, recv_sems, sums, lands, _ = ex["started"]
        sums, landed = _exchange_wait(f"scatter_wait_{ex['tag']}", _plan_scatter_chips, send_sems, recv_sems, lands,
                                      sums, after)
        for k, own, recv in zip(ex["names"], sums, landed):
            carried = None if big_out[k] is None else big_out[k][:4]
            big_out[k] = _adamw(k, ex["layer"], big_w[k], big_m[k], big_v[k], recv, own, chip, carried)
        in_flight.remove(ex)
        return landed[0]

    def advance(after):
        tokens = []
        for ex in list(in_flight):
            if not ex["on_links"]:
                tokens.append(to_links(ex, after))
            elif ex["age"] + 1 >= links_steps:
                tokens.append(finish(ex, after))
            else:
                ex["age"] += 1
        return tokens

    after = ()
    for l in reversed(range(Lr)):
        def ready(stage, dW, l=l):
            tokens = advance((dW[groups[stage][-1]],))
            return tuple(tokens + [pair_start(l, stage, dW)])

        dh, dW, dsmalls[l] = _layer_bwd(dh, saves[l], smalls[l], Ws[l], tabs, after, ready)
        after = tuple(advance((dh,)) + [pair_start(l, "proj", dW)])
    last = after[-1]
    for ex in [ex for ex in in_flight if not ex["on_links"]]:
        last = to_links(ex, (last,))
    for ex in list(in_flight):
        updated = tuple(out[4] for out in big_out.values() if out is not None)
        last = finish(ex, updated + (last,))

    cat = lambda key: jnp.concatenate([_pad8(d[key]) for d in dsmalls], axis=0)[::8]
    dbf = jnp.concatenate([d["b_f"] for d in dsmalls], axis=0).reshape(Lr, 8, LANES)[:, :nf, 0]
    loss_row = jnp.concatenate([loss_part, g_final.reshape(1, D)[:, LANES:]], axis=1)
    g_pack = _allreduce_small(_pack_small(cat("g_mix"), cat("g_mlp"), cat("g_ple"), cat("b_gate"), dg_final, dbf,
                                          loss_row), (last,))
    w_pack = _pack_small(g_mix, g_mlp, g_ple, b_gate, g_final, b_f)
    m_pack = _pack_small(m_g_mix, m_g_mlp, m_g_ple, m_b_gate, m_g_final, m_b_f)
    v_pack = _pack_small(v_g_mix, v_g_mlp, v_g_ple, v_b_gate, v_g_final, v_b_f)
    d_pack, mn_pack, vn_pack = _adamw_small(w_pack, g_pack, m_pack, v_pack)

    small_names = ("g_mix", "g_mlp", "g_ple", "b_gate", "g_final", "b_f")
    sg, loss = _unpack_small(g_pack, Lr, D, nf)
    small_out = {}
    for kind, pk in (("grad", None), ("delta", d_pack), ("m", mn_pack), ("v", vn_pack)):
        vals = sg if pk is None else _unpack_small(pk, Lr, D, nf)[0]
        small_out[kind] = dict(zip(small_names, vals))

    order = ("g_mix", "in", "b_f", "gate", "b_gate", "bra", "brb", "o", "g_mlp", "up", "down", "g_ple", "ple", "pg",
             "g_final")
    outs = [loss, dh.reshape(x.shape)]
    for idx, kind in enumerate(("grad", "delta", "m", "v")):
        for name in order:
            outs.append(big_out[name][idx] if name in big_out else small_out[kind][name])
    return tuple(outs)
```

```python
import jax
import jax.numpy as jnp
from jax import lax
from jax.experimental import pallas as pl
from jax.experimental.pallas import tpu as pltpu

F32, BF16 = jnp.float32, jnp.bfloat16
SDS = jax.ShapeDtypeStruct
BS = pl.BlockSpec
MESH = pl.DeviceIdType.MESH
HI = lax.Precision.HIGHEST

HEAD_DIM = 128
N_HEADS = 16
ATTN = N_HEADS * HEAD_DIM
DIL_GROUPS = ((128, 1), (512, 4), (2048, 16))
HEADS_PER_GROUP = 4
FOX_HEAD0 = 12
N_FOX = 4
BRANCH = HEADS_PER_GROUP * HEAD_DIM
BLOCK = 128
ROPE_DIM = HEAD_DIM // 4
ROPE_HALF = ROPE_DIM // 2
ROPE_THETA = 500000.0
NORM_EPS = 1e-6
SCALE = HEAD_DIM ** -0.5
NDEV = 8
LANES = 128

ADAM_LR, ADAM_B1, ADAM_B2, ADAM_EPS, ADAM_WD, ADAM_STEP = 0.001, 0.9, 0.999, 1e-08, 0.01, 10

V7X_VMEM_BYTES = 64 * 1024 * 1024
VMEM_LIMIT = (V7X_VMEM_BYTES * 3) // 4

NT_CONTRACT = 4096
NN = (((1,), (0,)), ((), ()))
NT = (((1,), (1,)), ((), ()))
TN = (((0,), (0,)), ((), ()))

BIG = ("in", "gate", "bra", "brb", "o", "up", "down", "ple", "pg")
ROW_SHARDED = ("in", "o", "down", "pg")


def _tile(n, pref):
    t = min(n, pref)
    while n % t:
        t -= LANES
    return t


def _cparams(sem):
    return pltpu.CompilerParams(dimension_semantics=sem, vmem_limit_bytes=VMEM_LIMIT)


def _sigmoid(x):
    return 1.0 / (1.0 + jnp.exp(-x))


def _mm(name, grid, a, a_spec, b, b_spec, dims, outs, acc_shape, extras=(), epilogue=None, carry=None, blocks=1):
    nk = grid[2]
    n_ex, n_out = len(extras), len(outs)
    n_carry = 0 if carry is None else 1

    def body(*refs):
        a_ref, b_ref = refs[0], refs[1]
        ex_refs = refs[2:2 + n_ex]
        out_refs = refs[2 + n_ex + n_carry:2 + n_ex + n_carry + n_out]

        def product():
            if blocks == 1:
                return lax.dot_general(a_ref[...], b_ref[...], dims, preferred_element_type=F32)
            n = b_ref.shape[2]
            total = None
            for jb in range(blocks):
                part = lax.dot_general(a_ref[:, jb * n:(jb + 1) * n], b_ref[jb], dims, preferred_element_type=F32)
                total = part if total is None else total + part
            return total

        def finish(acc):
            res = (acc,) if epilogue is None else epilogue(acc, *[e[...] for e in ex_refs])
            for o_ref, r in zip(out_refs, res):
                o_ref[...] = r.astype(o_ref.dtype)

        if nk == 1:
            finish(product())
            return
        acc_ref = refs[-1]
        k = pl.program_id(2)

        @pl.when(k == 0)
        def _():
            acc_ref[...] = jnp.zeros_like(acc_ref)

        acc_ref[...] += product()

        @pl.when(k == nk - 1)
        def _():
            finish(acc_ref[...])

    arrays = [a, b] + [e[0] for e in extras]
    specs = [a_spec, b_spec] + [e[1] for e in extras]
    alias = {}
    if carry is not None:
        arrays.append(carry[0])
        specs.append(BS(memory_space=pl.ANY))
        alias = {len(arrays) - 1: carry[1]}
    return pl.pallas_call(
        body, out_shape=[o[0] for o in outs], grid=grid, in_specs=specs, out_specs=[o[1] for o in outs],
        scratch_shapes=[pltpu.VMEM(acc_shape, F32)] if nk > 1 else [], input_output_aliases=alias, name=name,
        compiler_params=_cparams(("parallel", "parallel", "arbitrary")))(*arrays)


def _mm_nn(name, a, w, out_dtypes, *, tm, joff=0, ncols=None, tn=None, tk=None, extras=(), epilogue=None,
           out_specs=None, out_shapes=None):
    M, K = a.shape
    J, _, n = w.shape
    tn = tn or _tile(n, 512)
    tk = tk or _tile(K, 2048)
    per = n // tn
    ncols = ncols or J * n
    grid = (M // tm, ncols // tn, K // tk)
    a_spec = BS((tm, tk), lambda i, j, k: (i, k))
    w_spec = BS((None, tk, tn), lambda i, j, k: ((j + joff) // per, k, (j + joff) % per))
    if out_specs is None:
        out_specs = [BS((tm, tn), lambda i, j, k: (i, j))] * len(out_dtypes)
        out_shapes = [(M, ncols)] * len(out_dtypes)
    outs = [(SDS(s, d), sp) for s, d, sp in zip(out_shapes, out_dtypes, out_specs)]
    return _mm(name, grid, a, a_spec, w, w_spec, NN, outs, (tm, tn), extras, epilogue)


def _mm_nt(name, dy, w, out_dtype, *, tm, coff=0, ncontract=None, tc=None, dy_spec=None, extras=(), epilogue=None):
    J, K, n = w.shape
    M = dy.shape[-2]
    tko = _tile(K, 512)
    outs = [(SDS((M, K), out_dtype), BS((tm, tko), lambda i, ko, cc: (i, ko)))]
    if J > 1 and tc is None and dy_spec is None:
        blocks = max(1, min(J, NT_CONTRACT // n))
        grid = (M // tm, K // tko, J // blocks)
        dy_spec = BS((tm, blocks * n), lambda i, ko, cc: (i, cc))
        w_spec = BS((blocks, tko, n), lambda i, ko, cc: (cc, ko, 0))
        return _mm(name, grid, dy, dy_spec, w, w_spec, NT, outs, (tm, tko), extras, epilogue, blocks=blocks)[0]
    tc = tc or _tile(n, 2048)
    per = n // tc
    ncontract = ncontract or J * n
    grid = (M // tm, K // tko, ncontract // tc)
    if dy_spec is None:
        dy_spec = BS((tm, tc), lambda i, ko, cc: (i, cc))
    w_spec = BS((None, tko, tc), lambda i, ko, cc: ((cc + coff) // per, ko, (cc + coff) % per))
    return _mm(name, grid, dy, dy_spec, w, w_spec, NT, outs, (tm, tko), extras, epilogue)[0]


def _mm_tn(name, x, dy, wshape, *, ts, joff=0, ncols=None, tn=None, dy_spec=None, carry=None):
    J, K, n = wshape
    S = x.shape[0]
    tn = tn or _tile(n, 1024)
    tkm = _tile(K, 512)
    per = n // tn
    ncols = ncols or J * n
    grid = (K // tkm, ncols // tn, S // ts)
    x_spec = BS((ts, tkm), lambda ki, j, sc: (sc, ki))
    if dy_spec is None:
        dy_spec = BS((ts, tn), lambda ki, j, sc: (sc, j))
    outs = [(SDS(wshape, BF16), BS((None, tkm, tn), lambda ki, j, sc: ((j + joff) // per, ki, (j + joff) % per)))]
    return _mm(name, grid, x, x_spec, dy, dy_spec, TN, outs, (tkm, tn), carry=None if carry is None else (carry, 0))[0]


def _rows(name, body, ins, outs, S, tr, sequential=False):
    def spec(shape, kind):
        if kind == "row":
            return BS((tr, shape[1]), lambda i: (i, 0))
        if kind == "order":
            return BS(memory_space=pl.ANY)
        return BS(tuple(shape), lambda i: (0,) * len(shape))
    return pl.pallas_call(
        body, out_shape=[o[0] for o in outs], grid=(S // tr,),
        in_specs=[spec(a.shape, k) for a, k in ins], out_specs=[spec(o[0].shape, o[1]) for o in outs],
        name=name, compiler_params=_cparams(("arbitrary" if sequential else "parallel",)))(*[a for a, _ in ins])


def _rms_fwd(h, g, tr=256, after=()):
    S, D = h.shape
    n_after = len(after)

    def body(h_ref, g_ref, *rest):
        u_ref = rest[n_after]
        x = h_ref[...]
        r = lax.rsqrt(jnp.mean(x * x, axis=-1, keepdims=True) + NORM_EPS)
        u_ref[...] = ((x * r) * g_ref[...]).astype(BF16)

    ins = [(h, "row"), (g, "full")] + [(t, "order") for t in after]
    return _rows("rms_fwd", body, ins, [(SDS((S, D), BF16), "row")], S, tr)[0]


def _rms_bwd(h, g, dy, dres, tr=256, after=()):
    S, D = h.shape
    n_after = len(after)

    def body(h_ref, g_ref, dy_ref, dres_ref, *rest):
        dh_ref, dhb_ref, dg_ref = rest[n_after:]
        x = h_ref[...]
        r = lax.rsqrt(jnp.mean(x * x, axis=-1, keepdims=True) + NORM_EPS)
        xhat = x * r
        dyv = dy_ref[...]
        gy = dyv * g_ref[...]
        dx = r * (gy - xhat * jnp.mean(gy * xhat, axis=-1, keepdims=True))
        dh = dres_ref[...] + dx
        dh_ref[...] = dh
        dhb_ref[...] = dh.astype(BF16)
        part = jnp.sum(dyv * xhat, axis=0, keepdims=True)

        @pl.when(pl.program_id(0) == 0)
        def _():
            dg_ref[...] = part

        @pl.when(pl.program_id(0) > 0)
        def _():
            dg_ref[...] += part

    ins = [(h, "row"), (g, "full"), (dy, "row"), (dres, "row")] + [(t, "order") for t in after]
    return _rows("rms_bwd", body, ins,
                 [(SDS((S, D), F32), "row"), (SDS((S, D), BF16), "row"), (SDS((1, D), F32), "full")], S, tr, True)


def _final(h, g, target, tr=256):
    S, D = h.shape

    def body(h_ref, g_ref, t_ref, dh_ref, dg_ref, loss_ref):
        x = h_ref[...]
        r = lax.rsqrt(jnp.mean(x * x, axis=-1, keepdims=True) + NORM_EPS)
        xhat = x * r
        gv = g_ref[...]
        err = xhat * gv - t_ref[...]
        lpart = 0.5 * jnp.sum(jnp.mean(err * err, axis=-1, keepdims=True), axis=0, keepdims=True)
        dyv = err * (1.0 / D)
        gy = dyv * gv
        dh_ref[...] = r * (gy - xhat * jnp.mean(gy * xhat, axis=-1, keepdims=True))
        part = jnp.sum(dyv * xhat, axis=0, keepdims=True)
        lrow = jnp.broadcast_to(lpart, (1, LANES))

        @pl.when(pl.program_id(0) == 0)
        def _():
            dg_ref[...] = part
            loss_ref[...] = lrow

        @pl.when(pl.program_id(0) > 0)
        def _():
            dg_ref[...] += part
            loss_ref[...] += lrow

    return _rows("final_head", body, [(h, "row"), (g, "full"), (target, "row")],
                 [(SDS((S, D), F32), "row"), (SDS((1, D), F32), "full"), (SDS((1, LANES), F32), "full")], S, tr, True)


def _ple_bwd(dh, pg, pp, tr=256, after=()):
    S, D = dh.shape
    n_after = len(after)

    def body(dh_ref, pg_ref, pp_ref, *rest):
        dpp_ref, dpgz_ref = rest[n_after:]
        d, g, q = dh_ref[...], pg_ref[...], pp_ref[...]
        dpp_ref[...] = (d * g).astype(BF16)
        dpgz_ref[...] = (d * q * g * (1.0 - g)).astype(BF16)

    ins = [(dh, "row"), (pg, "row"), (pp, "row")] + [(t, "order") for t in after]
    return _rows("ple_bwd", body, ins, [(SDS((S, D), BF16), "row"), (SDS((S, D), BF16), "row")], S, tr)


def _merge_bwd(dmerged, ya, yb, gates, tr=256):
    S, D = dmerged.shape

    def body(dm_ref, ya_ref, yb_ref, g_ref, dya_ref, dyb_ref, dgz_ref, dbg_ref):
        dm, a, b = dm_ref[...], ya_ref[...], yb_ref[...]
        g1, g2 = g_ref[:, :D], g_ref[:, D:]
        dya_ref[...] = (dm * g1).astype(BF16)
        dyb_ref[...] = (dm * g2).astype(BF16)
        dz1 = dm * a * g1 * (1.0 - g1)
        dz2 = dm * b * g2 * (1.0 - g2)
        dgz_ref[:, :D] = dz1.astype(BF16)
        dgz_ref[:, D:] = dz2.astype(BF16)
        p1 = jnp.sum(dz1, axis=0, keepdims=True)
        p2 = jnp.sum(dz2, axis=0, keepdims=True)

        @pl.when(pl.program_id(0) == 0)
        def _():
            dbg_ref[:, :D] = p1
            dbg_ref[:, D:] = p2

        @pl.when(pl.program_id(0) > 0)
        def _():
            dbg_ref[:, :D] += p1
            dbg_ref[:, D:] += p2

    return _rows("merge_bwd", body, [(dmerged, "row"), (ya, "row"), (yb, "row"), (gates, "row")],
                 [(SDS((S, D), BF16), "row"), (SDS((S, D), BF16), "row"), (SDS((S, 2 * D), BF16), "row"),
                  (SDS((1, 2 * D), F32), "full")], S, tr, True)


def _mix_weights(l0, l1, l2):
    mx = jnp.maximum(jnp.maximum(l0, l1), l2)
    e0, e1, e2 = jnp.exp(l0 - mx), jnp.exp(l1 - mx), jnp.exp(l2 - mx)
    inv = 1.0 / (e0 + e1 + e2)
    return e0 * inv, e1 * inv, e2 * inv


def _mix_fwd(os_, ls_, tr=512):
    S = os_[0].shape[0]

    def body(o0, o1, o2, l0, l1, l2, y_ref):
        w0, w1, w2 = _mix_weights(l0[...], l1[...], l2[...])
        y_ref[...] = (w0 * o0[...] + w1 * o1[...] + w2 * o2[...]).astype(BF16)

    return _rows("mix_fwd", body, [(a, "row") for a in (*os_, *ls_)], [(SDS((S, BRANCH), BF16), "row")], S, tr)[0]


def _mix_bwd(dy, os_, ls_, tr=512, after=()):
    S = dy.shape[0]
    n_after = len(after)

    def head_sums(t):
        return jnp.concatenate(
            [jnp.broadcast_to(jnp.sum(t[:, j * HEAD_DIM:(j + 1) * HEAD_DIM], axis=-1, keepdims=True), (tr, HEAD_DIM))
             for j in range(HEADS_PER_GROUP)], axis=1)

    def body(dy_ref, o0, o1, o2, l0, l1, l2, *rest):
        d0, d1, d2, p0, p1, p2 = rest[n_after:]
        ws = _mix_weights(l0[...], l1[...], l2[...])
        d = dy_ref[...]
        es = [head_sums(d * o[...]) for o in (o0, o1, o2)]
        ebar = ws[0] * es[0] + ws[1] * es[1] + ws[2] * es[2]
        for w, e, d_ref, p_ref in zip(ws, es, (d0, d1, d2), (p0, p1, p2)):
            d_ref[...] = (w * d).astype(BF16)
            p_ref[...] = w * (e - ebar)

    outs = [(SDS((S, BRANCH), BF16), "row")] * 3 + [(SDS((S, BRANCH), F32), "row")] * 3
    ins = [(a, "row") for a in (dy, *os_, *ls_)] + [(t, "order") for t in after]
    return _rows("mix_bwd", body, ins, outs, S, tr)


def _iota2(shape, dim):
    return lax.broadcasted_iota(jnp.int32, shape, dim)


def _fgate_fwd(zf, bf):
    S = zf.shape[0]
    nblk = S // BLOCK

    def body(zf_ref, bf_ref, cb_ref, ct_ref, flt_ref):
        tri = (_iota2((BLOCK, BLOCK), 0) >= _iota2((BLOCK, BLOCK), 1)).astype(F32)
        sel8 = (_iota2((8, LANES), 0) == _iota2((8, LANES), 1)).astype(F32)
        carry = jnp.zeros((1, LANES), F32)
        for blk in range(nblk):
            rows = pl.ds(blk * BLOCK, BLOCK)
            fl = zf_ref[rows, :] + bf_ref[...]
            ls = jnp.minimum(fl, 0.0) - jnp.log(1.0 + jnp.exp(-jnp.abs(fl)))
            c = jnp.dot(tri, ls, precision=HI, preferred_element_type=F32) + carry
            carry = carry + jnp.sum(ls, axis=0, keepdims=True)
            for hd in range(N_FOX):
                e_h = (_iota2((LANES, LANES), 0) == hd).astype(F32)
                cb_ref[rows, hd * LANES:(hd + 1) * LANES] = jnp.dot(c, e_h, precision=HI, preferred_element_type=F32)
            ct_ref[:, rows] = lax.dot_general(sel8, c, NT, precision=HI, preferred_element_type=F32)
            flt_ref[:, rows] = lax.dot_general(sel8, fl, NT, precision=HI, preferred_element_type=F32)

    vm = BS(memory_space=pltpu.VMEM)
    return pl.pallas_call(
        body, out_shape=[SDS((S, N_FOX * LANES), F32), SDS((8, S), F32), SDS((8, S), F32)],
        in_specs=[vm, vm], out_specs=[vm, vm, vm], name="fgate_fwd",
        compiler_params=pltpu.CompilerParams(vmem_limit_bytes=VMEM_LIMIT))(zf, bf)


def _fgate_bwd(dct, flt):
    S = dct.shape[1]
    nblk = S // BLOCK

    def body(dct_ref, flt_ref, dzf_ref, dbf_ref):
        rowid = _iota2((8, BLOCK), 0)
        tri_r = (_iota2((BLOCK, BLOCK), 0) >= _iota2((BLOCK, BLOCK), 1)).astype(F32)
        carry = jnp.zeros((8, 1), F32)
        dbf = jnp.zeros((8, 1), F32)
        for blk in reversed(range(nblk)):
            cols = pl.ds(blk * BLOCK, BLOCK)
            dc = jnp.where(rowid < N_FOX, dct_ref[:, cols], 0.0)
            dls = jnp.dot(dc, tri_r, precision=HI, preferred_element_type=F32) + carry
            carry = carry + jnp.sum(dc, axis=1, keepdims=True)
            dfl = jnp.where(rowid < N_FOX, dls * _sigmoid(-flt_ref[:, cols]), 0.0)
            dbf = dbf + jnp.sum(dfl, axis=1, keepdims=True)
            sq = jnp.concatenate([dfl, jnp.zeros((BLOCK - 8, BLOCK), F32)], axis=0)
            dzf_ref[cols, :] = sq.T.astype(BF16)
        dbf_ref[...] = jnp.broadcast_to(dbf, (8, LANES))

    vm = BS(memory_space=pltpu.VMEM)
    return pl.pallas_call(
        body, out_shape=[SDS((S, LANES), BF16), SDS((8, LANES), F32)],
        in_specs=[vm, vm], out_specs=[vm, vm], name="fgate_bwd",
        compiler_params=pltpu.CompilerParams(vmem_limit_bytes=VMEM_LIMIT))(dct, flt)


def _rope_tables(S):
    inv = ROPE_THETA ** (-jnp.arange(ROPE_HALF, dtype=F32) / ROPE_HALF)
    ang = jnp.arange(S, dtype=F32)[:, None] * inv[None, :]
    cos, sin = jnp.cos(ang), jnp.sin(ang)
    pad = HEAD_DIM - ROPE_DIM
    tc = jnp.concatenate([cos, cos, jnp.ones((S, pad), F32)], axis=1)
    ta = jnp.concatenate([-sin, jnp.zeros((S, HEAD_DIM - ROPE_HALF), F32)], axis=1)
    tb = jnp.concatenate([jnp.zeros((S, ROPE_HALF), F32), sin, jnp.zeros((S, pad), F32)], axis=1)
    return tc, ta, tb


def _rope(x, c, a, b):
    return x * c + pltpu.roll(x, HEAD_DIM - ROPE_HALF, 1) * a + pltpu.roll(x, ROPE_HALF, 1) * b


def _rope_t(g, c, a, b):
    return g * c - pltpu.roll(g, ROPE_HALF, 1) * b - pltpu.roll(g, HEAD_DIM - ROPE_HALF, 1) * a


DIL_TQ = 256


def _dil_extent(i, window):
    return max(0, i * DIL_TQ - window), (i + 1) * DIL_TQ


def _dil_mask(i, lo, ext, window, d):
    diff = (i * DIL_TQ - lo) + _iota2((DIL_TQ, ext), 0) - _iota2((DIL_TQ, ext), 1)
    mask = (diff >= 0) & (diff <= window)
    return mask & ((diff & (d - 1)) == 0) if d > 1 else mask


def _wide(t, w):
    return t if w == LANES else jnp.concatenate([t] * (w // LANES), axis=1)


def _dil_fwd(z3, tabs, grp, S):
    window, d = DIL_GROUPS[grp]
    nq = S // DIL_TQ

    def body(qkv_ref, c_ref, a_ref, b_ref, o_ref, lse_ref, qs, ks):
        c, a, b = c_ref[...], a_ref[...], b_ref[...]
        qs[...] = _rope(qkv_ref[0].astype(F32), c, a, b).astype(BF16)
        ks[...] = _rope(qkv_ref[1].astype(F32), c, a, b).astype(BF16)
        for i in range(nq):
            lo, hi = _dil_extent(i, window)
            rows = pl.ds(i * DIL_TQ, DIL_TQ)
            s = lax.dot_general(qs[rows, :], ks[lo:hi, :], NT, preferred_element_type=F32) * SCALE
            s = jnp.where(_dil_mask(i, lo, hi - lo, window, d), s, -jnp.inf)
            m = jnp.max(s, axis=-1, keepdims=True)
            p = jnp.exp(s - m)
            l = jnp.sum(p, axis=-1, keepdims=True)
            o = jnp.dot(p.astype(BF16), qkv_ref[2, lo:hi, :], preferred_element_type=F32)
            o_ref[rows, :] = o / l
            lse_ref[rows, :] = jnp.broadcast_to(m + jnp.log(l), (DIL_TQ, HEAD_DIM))

    hh0 = grp * HEADS_PER_GROUP
    tspec = BS((S, HEAD_DIM), lambda j: (0, 0))
    ospec = BS((S, HEAD_DIM), lambda j: (0, j))
    return pl.pallas_call(
        body, out_shape=[SDS((S, BRANCH), F32)] * 2, grid=(HEADS_PER_GROUP,),
        in_specs=[BS((3, S, HEAD_DIM), lambda j: (0, 0, hh0 + j)), tspec, tspec, tspec],
        out_specs=[ospec, ospec], scratch_shapes=[pltpu.VMEM((S, HEAD_DIM), BF16)] * 2,
        name="dil_fwd", compiler_params=_cparams(("parallel",)))(z3, *tabs)


def _dil_bwd(z3, tabs, grp, S, do, lse, dpl, dz3):
    window, d = DIL_GROUPS[grp]
    nq = S // DIL_TQ
    n_carry = 0 if dz3 is None else 1

    def body(*refs):
        qkv_ref, c_ref, a_ref, b_ref, do_ref, lse_ref, dpl_ref = refs[:7]
        dz_ref = refs[7 + n_carry]
        qs, ks, dq_acc, dk_acc, dv_acc = refs[8 + n_carry:]
        c, a, b = c_ref[...], a_ref[...], b_ref[...]
        qs[...] = _rope(qkv_ref[0].astype(F32), c, a, b).astype(BF16)
        ks[...] = _rope(qkv_ref[1].astype(F32), c, a, b).astype(BF16)
        dk_acc[...] = jnp.zeros_like(dk_acc)
        dv_acc[...] = jnp.zeros_like(dv_acc)
        for i in range(nq):
            lo, hi = _dil_extent(i, window)
            w = hi - lo
            rows = pl.ds(i * DIL_TQ, DIL_TQ)
            q, k, v = qs[rows, :], ks[lo:hi, :], qkv_ref[2, lo:hi, :]
            s = lax.dot_general(q, k, NT, preferred_element_type=F32) * SCALE
            p = jnp.where(_dil_mask(i, lo, w, window, d), jnp.exp(s - _wide(lse_ref[rows, :], w)), 0.0)
            dob = do_ref[rows, :]
            dp = lax.dot_general(dob, v, NT, preferred_element_type=F32)
            delta = jnp.sum(p * dp, axis=-1, keepdims=True)
            ds = (p * (dp - delta + _wide(dpl_ref[rows, :], w))).astype(BF16)
            dq_acc[rows, :] = jnp.dot(ds, k, preferred_element_type=F32) * SCALE
            dk_acc[lo:hi, :] += lax.dot_general(ds, q, TN, preferred_element_type=F32) * SCALE
            dv_acc[lo:hi, :] += lax.dot_general(p.astype(BF16), dob, TN, preferred_element_type=F32)
        dz_ref[0] = _rope_t(dq_acc[...], c, a, b).astype(BF16)
        dz_ref[1] = _rope_t(dk_acc[...], c, a, b).astype(BF16)
        dz_ref[2] = dv_acc[...].astype(BF16)

    hh0 = grp * HEADS_PER_GROUP
    tspec = BS((S, HEAD_DIM), lambda j: (0, 0))
    bspec = BS((S, HEAD_DIM), lambda j: (0, j))
    zspec = BS((3, S, HEAD_DIM), lambda j: (0, 0, hh0 + j))
    arrays = [z3, *tabs, do, lse, dpl]
    specs = [zspec, tspec, tspec, tspec, bspec, bspec, bspec]
    alias = {}
    if dz3 is not None:
        arrays.append(dz3)
        specs.append(BS(memory_space=pl.ANY))
        alias = {len(arrays) - 1: 0}
    return pl.pallas_call(
        body, out_shape=SDS((3, S, ATTN), BF16), grid=(HEADS_PER_GROUP,), in_specs=specs, out_specs=zspec,
        scratch_shapes=[pltpu.VMEM((S, HEAD_DIM), BF16)] * 2 + [pltpu.VMEM((S, HEAD_DIM), F32)] * 3,
        input_output_aliases=alias, name="dil_bwd", compiler_params=_cparams(("parallel",)))(*arrays)


FOX_TQ = 256


def _fox_scores(q_ref, k_ref, cb_ref, ct_ref, i, ext):
    rows = pl.ds(i * FOX_TQ, FOX_TQ)
    s = lax.dot_general(q_ref[0, rows, :], k_ref[0, 0:ext, :], NT, preferred_element_type=F32) * SCALE
    s = s + _wide(cb_ref[rows, :], ext) - ct_ref[:, 0:ext]
    qpos = i * FOX_TQ + _iota2((FOX_TQ, ext), 0)
    kpos = _iota2((FOX_TQ, ext), 1)
    return jnp.where(kpos <= qpos, s, -jnp.inf)


def _fox_specs(S):
    qs = BS((1, S, HEAD_DIM), lambda h: (0, 0, FOX_HEAD0 + h))
    ks = BS((1, S, HEAD_DIM), lambda h: (1, 0, FOX_HEAD0 + h))
    vs = BS((1, S, HEAD_DIM), lambda h: (2, 0, FOX_HEAD0 + h))
    hb = BS((S, HEAD_DIM), lambda h: (0, h))
    ct = BS((None, 1, S), lambda h: (h, 0, 0))
    return qs, ks, vs, hb, ct


def _fox_fwd(z3, cb, ct, S):
    nq = S // FOX_TQ

    def body(q_ref, k_ref, v_ref, cb_ref, ct_ref, y_ref, lse_ref):
        for i in range(nq):
            ext = (i + 1) * FOX_TQ
            rows = pl.ds(i * FOX_TQ, FOX_TQ)
            s = _fox_scores(q_ref, k_ref, cb_ref, ct_ref, i, ext)
            m = jnp.max(s, axis=-1, keepdims=True)
            p = jnp.exp(s - m)
            l = jnp.sum(p, axis=-1, keepdims=True)
            o = jnp.dot(p.astype(BF16), v_ref[0, 0:ext, :], preferred_element_type=F32)
            y_ref[rows, :] = (o / l).astype(BF16)
            lse_ref[rows, :] = jnp.broadcast_to(m + jnp.log(l), (FOX_TQ, HEAD_DIM))

    qs, ks, vs, hb, cts = _fox_specs(S)
    return pl.pallas_call(
        body, out_shape=[SDS((S, BRANCH), BF16), SDS((S, BRANCH), F32)], grid=(N_FOX,),
        in_specs=[qs, ks, vs, hb, cts], out_specs=[hb, hb], name="fox_fwd",
        compiler_params=_cparams(("parallel",)))(z3, z3, z3, cb, ct.reshape(8, 1, S))


def _fox_bwd(z3, cb, ct, S, dy, lse, dz3):
    nq = S // FOX_TQ

    def body(q_ref, k_ref, v_ref, cb_ref, ct_ref, dy_ref, lse_ref, _, dz_ref, dct_ref, dk_acc, dv_acc):
        dk_acc[...] = jnp.zeros_like(dk_acc)
        dv_acc[...] = jnp.zeros_like(dv_acc)
        dct_ref[...] = jnp.zeros_like(dct_ref)
        for i in range(nq):
            ext = (i + 1) * FOX_TQ
            rows = pl.ds(i * FOX_TQ, FOX_TQ)
            s = _fox_scores(q_ref, k_ref, cb_ref, ct_ref, i, ext)
            p = jnp.exp(s - _wide(lse_ref[rows, :], ext))
            dob = dy_ref[rows, :]
            dp = lax.dot_general(dob, v_ref[0, 0:ext, :], NT, preferred_element_type=F32)
            ds = p * (dp - jnp.sum(p * dp, axis=-1, keepdims=True))
            dsb = ds.astype(BF16)
            dz_ref[0, rows, :] = (jnp.dot(dsb, k_ref[0, 0:ext, :], preferred_element_type=F32) * SCALE).astype(BF16)
            dk_acc[0:ext, :] += lax.dot_general(dsb, q_ref[0, rows, :], TN, preferred_element_type=F32) * SCALE
            dv_acc[0:ext, :] += lax.dot_general(p.astype(BF16), dob, TN, preferred_element_type=F32)
            dct_ref[:, 0:ext] -= jnp.sum(ds, axis=0, keepdims=True)
        dz_ref[1] = dk_acc[...].astype(BF16)
        dz_ref[2] = dv_acc[...].astype(BF16)

    qs, ks, vs, hb, cts = _fox_specs(S)
    zspec = BS((3, S, HEAD_DIM), lambda h: (0, 0, FOX_HEAD0 + h))
    dz, dct = pl.pallas_call(
        body, out_shape=[SDS((3, S, ATTN), BF16), SDS((8, 1, S), F32)], grid=(N_FOX,),
        in_specs=[qs, ks, vs, hb, cts, hb, hb, BS(memory_space=pl.ANY)], out_specs=[zspec, cts],
        scratch_shapes=[pltpu.VMEM((S, HEAD_DIM), F32)] * 2, input_output_aliases={7: 0}, name="fox_bwd",
        compiler_params=_cparams(("parallel",)))(z3, z3, z3, cb, ct.reshape(8, 1, S), dy, lse, dz3)
    return dz, dct.reshape(8, S)


TM = 1024


def _layer_fwd(h, p_bf, sm, W, tabs, after=(), mid=None, late=None):
    S, D = h.shape
    qkv_cols = 3 * ATTN
    u = _rms_fwd(h, sm["g_mix"], after=after)
    tn = 512
    per = ATTN // tn
    z3 = _mm_nn("mm_qkv", u, W["in"], [BF16], tm=TM, ncols=qkv_cols, tn=tn, out_shapes=[(3, S, ATTN)],
                out_specs=[BS((None, TM, tn), lambda i, j, k: (j // per, i, j % per))])[0]
    zf = _mm_nn("mm_f", u, W["in"], [F32], tm=TM, joff=qkv_cols // LANES, ncols=LANES, tn=LANES)[0]
    cb, ct, flt = _fgate_fwd(zf, sm["b_f"])
    os_, ls_ = [], []
    for grp in range(len(DIL_GROUPS)):
        o, l = _dil_fwd(z3, tabs, grp, S)
        os_.append(o)
        ls_.append(l)
    ya_h = _mix_fwd(os_, ls_)
    yb_h, lse_fox = _fox_fwd(z3, cb, ct, S)

    tg = _tile(W["gate"].shape[2], 512)
    gates = _mm_nn("mm_gate", u, W["gate"], [F32], tm=TM,
                   extras=[(sm["b_gate"], BS((1, tg), lambda i, j, k: (0, j)))],
                   epilogue=lambda acc, bias: (_sigmoid(acc + bias),))[0]
    ya = _mm_nn("mm_bra", ya_h, W["bra"], [F32], tm=TM)[0]
    tb = _tile(W["brb"].shape[2], 512)
    g2off = D // tb
    yb, merged = _mm_nn(
        "mm_brb", yb_h, W["brb"], [F32, BF16], tm=TM,
        extras=[(ya, BS((TM, tb), lambda i, j, k: (i, j))), (gates, BS((TM, tb), lambda i, j, k: (i, j))),
                (gates, BS((TM, tb), lambda i, j, k: (i, j + g2off)))],
        epilogue=lambda acc, a, g1, g2: (acc, g1 * a + g2 * acc))
    to = _tile(D, 512)
    res_spec = BS((TM, to), lambda i, j, k: (i, j))
    h1 = _mm_nn("mm_o", merged, W["o"], [F32], tm=TM, extras=[(h, res_spec)], epilogue=lambda acc, r: (r + acc,))[0]

    m = _rms_fwd(h1, sm["g_mlp"], after=() if mid is None else mid(h1))

    def up_epi(acc):
        r = jnp.maximum(acc, 0.0)
        return r, r * r

    relu_a, act = _mm_nn("mm_up", m, W["up"], [BF16, BF16], tm=TM, epilogue=up_epi)
    h2 = _mm_nn("mm_down", act, W["down"], [F32], tm=TM, tk=_tile(act.shape[1], 4096), extras=[(h1, res_spec)],
                epilogue=lambda acc, r: (r + acc,))[0]

    nrm = _rms_fwd(h2, sm["g_ple"], after=() if late is None else late(h2))
    pp = _mm_nn("mm_ple", p_bf, W["ple"], [F32], tm=TM)[0]

    def pg_epi(acc, r, q):
        g = _sigmoid(acc)
        return g, r + g * q

    pg, h3 = _mm_nn("mm_pg", nrm, W["pg"], [F32, F32], tm=TM, extras=[(h2, res_spec), (pp, res_spec)], epilogue=pg_epi)
    saved = dict(h=h, u=u, z3=z3, cb=cb, ct=ct, flt=flt, os=os_, ls=ls_, ya_h=ya_h, yb_h=yb_h, lse_fox=lse_fox,
                 gates=gates, ya=ya, yb=yb, merged=merged, h1=h1, m=m, relu_a=relu_a, act=act, h2=h2, nrm=nrm,
                 pg=pg, pp=pp, p_bf=p_bf)
    return h3, saved


def _layer_bwd(dh3, sv, sm, W, tabs, after=(), ready=None):
    S, D = dh3.shape
    dW, dsm = {}, {}

    dpp, dpgz = _ple_bwd(dh3, sv["pg"], sv["pp"], after=after)
    dW["ple"] = _mm_tn("tn_ple", sv["p_bf"], dpp, W["ple"].shape, ts=S)
    dW["pg"] = _mm_tn("tn_pg", sv["nrm"], dpgz, W["pg"].shape, ts=S)
    dn = _mm_nt("nt_pg", dpgz, W["pg"], F32, tm=TM)
    dh2, dh2b, dsm["g_ple"] = _rms_bwd(sv["h2"], sm["g_ple"], dn, dh3)

    ta = _tile(W["up"].shape[1], 512)
    tff = _tile(W["down"].shape[1], 512)
    da = _mm_nt("nt_down", dh2b, W["down"], BF16, tm=TM,
                extras=[(sv["relu_a"], BS((TM, tff), lambda i, ko, cc: (i, ko)))],
                epilogue=lambda acc, r: (acc * (2.0 * r.astype(F32)),))
    dW["down"] = _mm_tn("tn_down", sv["act"], dh2b, W["down"].shape, ts=S)
    dm = _mm_nt("nt_up", da, W["up"], F32, tm=TM)
    dW["up"] = _mm_tn("tn_up", sv["m"], da, W["up"].shape, ts=S)
    dh1, dh1b, dsm["g_mlp"] = _rms_bwd(sv["h1"], sm["g_mlp"], dm, dh2, after=() if ready is None else ready("mlp", dW))

    dmerged = _mm_nt("nt_o", dh1b, W["o"], F32, tm=TM)
    dW["o"] = _mm_tn("tn_o", sv["merged"], dh1b, W["o"].shape, ts=S)
    dya, dyb, dgz, dsm["b_gate"] = _merge_bwd(dmerged, sv["ya"], sv["yb"], sv["gates"])
    dya_h = _mm_nt("nt_bra", dya, W["bra"], F32, tm=TM)
    dyb_h = _mm_nt("nt_brb", dyb, W["brb"], BF16, tm=TM)
    dW["bra"] = _mm_tn("tn_bra", sv["ya_h"], dya, W["bra"].shape, ts=S)
    dW["brb"] = _mm_tn("tn_brb", sv["yb_h"], dyb, W["brb"].shape, ts=S)
    dW["gate"] = _mm_tn("tn_gate", sv["u"], dgz, W["gate"].shape, ts=S)
    du = _mm_nt("nt_gate", dgz, W["gate"], F32, tm=TM)

    dos_, dpls_ = [], []
    mixed = _mix_bwd(dya_h, sv["os"], sv["ls"], after=() if ready is None else ready("merge", dW))
    dos_, dpls_ = mixed[:3], mixed[3:]
    dz3 = None
    for grp in range(len(DIL_GROUPS)):
        dz3 = _dil_bwd(sv["z3"], tabs, grp, S, dos_[grp], sv["ls"][grp], dpls_[grp], dz3)
    dz3, dct = _fox_bwd(sv["z3"], sv["cb"], sv["ct"], S, dyb_h, sv["lse_fox"], dz3)
    dzf, dsm["b_f"] = _fgate_bwd(dct, sv["flt"])

    tn = 1024
    per = ATTN // tn
    qkv_cols = 3 * ATTN
    dwin = _mm_tn("tn_qkv", sv["u"], dz3, W["in"].shape, ts=S, ncols=qkv_cols, tn=tn,
                  dy_spec=BS((None, S, tn), lambda ki, j, sc: (j // per, sc, j % per)))
    dW["in"] = _mm_tn("tn_f", sv["u"], dzf, W["in"].shape, ts=S, joff=qkv_cols // LANES, ncols=LANES, tn=LANES,
                      carry=dwin)
    tu = _tile(D, 512)
    prev_spec = BS((TM, tu), lambda i, ko, cc: (i, ko))
    du = _mm_nt("nt_qkv", dz3, W["in"], F32, tm=TM, ncontract=qkv_cols, tc=ATTN,
                dy_spec=BS((None, TM, ATTN), lambda i, ko, cc: (cc, i, 0)),
                extras=[(du, prev_spec)], epilogue=lambda acc, prev: (prev + acc,))
    du = _mm_nt("nt_f", dzf, W["in"], F32, tm=TM, coff=qkv_cols // LANES, ncontract=LANES, tc=LANES,
                extras=[(du, prev_spec)], epilogue=lambda acc, prev: (prev + acc,))
    dh0, _, dsm["g_mix"] = _rms_bwd(sv["h"], sm["g_mix"], du, dh1)
    return dh0, dW, dsm


def _shard_tiles(R, Cp):
    tc = Cp if Cp <= 2048 else _tile(Cp, 1024)
    tr = R
    while tr * tc > 256 * 1024 and tr % 16 == 0:
        tr //= 2
    return tr, tc


def _padded_cols(C):
    return -(-C // LANES) * LANES


def _cast_bf16(name, w, me):
    Lr, R, C = w.shape
    Cp = _padded_cols(C)
    tr, tc = _shard_tiles(R, Cp)

    def body(me_ref, w_ref, *o_refs):
        for l in range(Lr):
            x = w_ref[l]
            if Cp != C:
                col = pl.program_id(1) * tc + _iota2((tr, tc), 1)
                x = jnp.where(col < C, x, 0.0)
            o_refs[l][...] = x.astype(BF16)

    grid_spec = pltpu.PrefetchScalarGridSpec(
        num_scalar_prefetch=1, grid=(R // tr, Cp // tc),
        in_specs=[BS((Lr, tr, tc), lambda i, j, me_ref: (0, i, j))],
        out_specs=[BS((None, tr, tc), lambda i, j, me_ref: (me_ref[0], i, j))] * Lr)
    return pl.pallas_call(
        body, out_shape=[SDS((NDEV, R, Cp), BF16)] * Lr, grid_spec=grid_spec,
        name="cast_" + name, compiler_params=_cparams(("parallel", "parallel")))(me, w)


def _adam_math(w, g, m, v):
    m = ADAM_B1 * m + (1.0 - ADAM_B1) * g
    v = ADAM_B2 * v + (1.0 - ADAM_B2) * (g * g)
    m_hat = m / (1.0 - ADAM_B1 ** ADAM_STEP)
    v_hat = v / (1.0 - ADAM_B2 ** ADAM_STEP)
    delta = -ADAM_LR * (m_hat / (jnp.sqrt(v_hat) + ADAM_EPS) + ADAM_WD * w)
    return delta, m, v


def _pair_sum(name, dw, sib, core):
    _, R, Cp = dw.shape
    tr, tc = _shard_tiles(R, Cp)

    def body(core_ref, a_ref, b_ref, o_ref):
        o_ref[...] = (a_ref[...].astype(F32) + b_ref[...].astype(F32)).astype(BF16)

    grid_spec = pltpu.PrefetchScalarGridSpec(
        num_scalar_prefetch=1, grid=(4, R // tr, Cp // tc),
        in_specs=[BS((None, tr, tc), lambda q, i, j, core_ref: (2 * q + core_ref[0], i, j)),
                  BS((None, tr, tc), lambda q, i, j, core_ref: (q, i, j))],
        out_specs=BS((None, tr, tc), lambda q, i, j, core_ref: (q, i, j)))
    return pl.pallas_call(
        body, out_shape=SDS((4, R, Cp), BF16), grid_spec=grid_spec, name="pair_sum_" + name,
        compiler_params=_cparams(("parallel", "parallel", "parallel")))(core, dw, sib)


def _adamw(name, layer, w, m, v, recv, own, me, carried):
    Lr, R, C = w.shape
    nslot, _, Cp = recv.shape
    tr, tc = _shard_tiles(R, Cp)
    n_carry = 0 if carried is None else 4

    def body(me_ref, w_ref, m_ref, v_ref, r_ref, own_ref, *rest):
        g_out, d_out, m_out, v_out, token = rest[n_carry:]
        mine = own_ref[...].astype(F32)
        g = None
        for s in range(nslot):
            part = jnp.where(me_ref[0] == s, mine, r_ref[s].astype(F32))
            g = part if g is None else g + part
        delta, mn, vn = _adam_math(w_ref[...], g, m_ref[...], v_ref[...])
        g_out[...] = g
        d_out[...] = delta
        m_out[...] = mn
        v_out[...] = vn
        token[...] = jnp.zeros_like(token)

    spec = BS((None, tr, tc), lambda i, j, me_ref: (layer, i, j))
    rspec = BS((nslot, tr, tc), lambda i, j, me_ref: (0, i, j))
    ospec = BS((None, tr, tc), lambda i, j, me_ref: (me_ref[0], i, j))
    arrays, specs, alias = [w, m, v, recv, own], [spec, spec, spec, rspec, ospec], {}
    if carried is not None:
        arrays += list(carried)
        specs += [BS(memory_space=pl.ANY)] * 4
        alias = {6 + k: k for k in range(4)}
    grid_spec = pltpu.PrefetchScalarGridSpec(
        num_scalar_prefetch=1, grid=(R // tr, Cp // tc), in_specs=specs,
        out_specs=[spec] * 4 + [BS((8, LANES), lambda i, j, me_ref: (0, 0))])
    return pl.pallas_call(
        body, out_shape=[SDS(w.shape, F32)] * 4 + [SDS((8, LANES), F32)], grid_spec=grid_spec,
        input_output_aliases=alias, name="adamw_" + name,
        compiler_params=_cparams(("arbitrary", "arbitrary")))(me, *arrays)


def _adamw_small(w, g, m, v):
    vm = BS(memory_space=pltpu.VMEM)

    def body(w_ref, g_ref, m_ref, v_ref, d_out, m_out, v_out):
        delta, mn, vn = _adam_math(w_ref[...], g_ref[...], m_ref[...], v_ref[...])
        d_out[...] = delta
        m_out[...] = mn
        v_out[...] = vn

    return pl.pallas_call(body, out_shape=[SDS(w.shape, F32)] * 3, in_specs=[vm] * 4, out_specs=[vm] * 3,
                          name="adamw_small")(w, g, m, v)


def _place():
    return lax.axis_index("x"), lax.axis_index("y"), lax.axis_index("c")


def _slot(px, py, pc):
    return 4 * px + 2 * py + pc


def _other_chips(x, y):
    return [(1 - x, y), (x, 1 - y), (1 - x, 1 - y)]


def _peer(x, y, c, rel):
    return (1 - x if rel & 4 else x, 1 - y if rel & 2 else y, 1 - c if rel & 1 else c)


def _plan_gather_ici(src, land):
    x, y, c = _place()
    own = land.at[_slot(x, y, c)]
    return [(own, own, (x, y, 1 - c))] + [(own, own, (*chip, c)) for chip in _other_chips(x, y)]


def _plan_gather_relay(src, land):
    x, y, c = _place()
    return [(land.at[_slot(*chip, c)], land.at[_slot(*chip, c)], (x, y, 1 - c)) for chip in _other_chips(x, y)]


def _plan_pair(src, land):
    x, y, c = _place()
    return [(src.at[2 * q + (1 - c)], land.at[q], (x, y, 1 - c)) for q in range(4)]


def _plan_scatter_chips(src, land):
    x, y, c = _place()
    return [(src.at[2 * cx + cy], land.at[2 * x + y], (cx, cy, c)) for cx, cy in _other_chips(x, y)]


EFFECT = pltpu.SideEffectType.DATAFLOW_SIDE_EFFECTING
HBM_SPEC = BS(memory_space=pltpu.HBM)
SEM_SPEC = BS(memory_space=pltpu.SEMAPHORE)
ORDER_SPEC = BS(memory_space=pl.ANY)


def _exchange_start(name, plan, ncopy, lands, srcs=None, after=()):
    bufs = ([] if srcs is None else list(srcs)) + list(lands)
    n, nb, ns = len(lands), len(bufs), len(bufs) - len(lands)
    n_after = len(after)

    def body(*refs):
        src_refs = refs[:ns] if ns else [None] * n
        land_refs = refs[ns:nb]
        send_sems, recv_sems = refs[nb + n_after], refs[nb + n_after + 1]
        token = refs[-1]
        for i in range(n):
            for k, (src, dst, to) in enumerate(plan(src_refs[i], land_refs[i])):
                pltpu.make_async_remote_copy(src_ref=src, dst_ref=dst, send_sem=send_sems.at[i * ncopy + k],
                                             recv_sem=recv_sems.at[i * ncopy + k], device_id=to,
                                             device_id_type=MESH).start()
        token[...] = jnp.zeros_like(token)

    out_shape = [pltpu.SemaphoreType.DMA((n * ncopy,)), pltpu.SemaphoreType.DMA((n * ncopy,))]
    out_shape += [pltpu.HBM(b.shape, b.dtype) for b in bufs] + [SDS((8, LANES), F32)]
    out = pl.pallas_call(
        body, name=name, out_shape=out_shape, in_specs=[HBM_SPEC] * nb + [ORDER_SPEC] * n_after,
        out_specs=[SEM_SPEC, SEM_SPEC] + [HBM_SPEC] * nb + [BS(memory_space=pltpu.VMEM)],
        input_output_aliases={i: 2 + i for i in range(nb)},
        compiler_params=pltpu.CompilerParams(has_side_effects=EFFECT))(
            *[pltpu.with_memory_space_constraint(b, pltpu.HBM) for b in bufs], *after)
    return out[0], out[1], out[2:2 + ns], out[2 + ns:2 + nb], out[-1]


def _exchange_wait(name, plan, send_sems, recv_sems, lands, srcs=None, after=()):
    bufs = ([] if srcs is None else list(srcs)) + list(lands)
    n, nb, ns = len(lands), len(bufs), len(bufs) - len(lands)

    def body(*refs):
        src_refs = refs[:ns] if ns else [None] * n
        land_refs = refs[ns:nb]
        send_ref, recv_ref = refs[nb], refs[nb + 1]
        for i in range(n):
            copies = plan(src_refs[i], land_refs[i])
            for k, (src, dst, to) in enumerate(copies):
                cp = pltpu.make_async_remote_copy(src_ref=src, dst_ref=dst, send_sem=send_ref.at[i * len(copies) + k],
                                                  recv_sem=recv_ref.at[i * len(copies) + k], device_id=to,
                                                  device_id_type=MESH)
                cp.wait_send()
                cp.wait_recv()

    out = pl.pallas_call(
        body, name=name, out_shape=[pltpu.HBM(b.shape, b.dtype) for b in bufs],
        in_specs=[HBM_SPEC] * nb + [SEM_SPEC, SEM_SPEC] + [ORDER_SPEC] * len(after), out_specs=[HBM_SPEC] * nb,
        input_output_aliases={i: i for i in range(nb)},
        compiler_params=pltpu.CompilerParams(has_side_effects=EFFECT))(*bufs, send_sems, recv_sems, *after)
    return out[:ns], out[ns:]


def _allreduce_small(v, after=()):
    R, Wd = v.shape
    n_after = len(after)

    def body(v_ref, *rest):
        o_ref, buf, send_sems, recv_sems = rest[n_after:]
        x, y, c = _place()
        me = _slot(x, y, c)
        buf[me] = v_ref[...]
        started = []
        for rel in range(1, NDEV):
            peer = _peer(x, y, c, rel)
            cp = pltpu.make_async_remote_copy(
                src_ref=v_ref, dst_ref=buf.at[me], send_sem=send_sems.at[rel - 1], recv_sem=recv_sems.at[rel - 1],
                device_id=peer, device_id_type=MESH)
            cp.start()
            started.append(cp)
        for cp in started:
            cp.wait()
        acc = buf[0]
        for j in range(1, NDEV):
            acc = acc + buf[j]
        o_ref[...] = acc

    vm = BS(memory_space=pltpu.VMEM)
    return pl.pallas_call(
        body, out_shape=SDS((R, Wd), F32), in_specs=[vm] + [ORDER_SPEC] * n_after, out_specs=vm,
        scratch_shapes=[pltpu.VMEM((NDEV, R, Wd), F32), pltpu.SemaphoreType.DMA((7,)), pltpu.SemaphoreType.DMA((7,))],
        name="allreduce_small")(v, *after)


def _as_matrix(name, g):
    if name in ROW_SHARDED:
        return g.reshape(1, NDEV * g.shape[1], g.shape[2])
    return g


def _pad8(a):
    return jnp.pad(a, ((0, -a.shape[0] % 8), (0, 0)))


def _pack_small(g_mix, g_mlp, g_ple, b_gate, g_final, b_f, extra_row=None):
    Lr, D = g_mix.shape
    rows = [g_mix, g_mlp, g_ple, b_gate.reshape(2 * Lr, D), g_final.reshape(1, D),
            jnp.pad(b_f.reshape(1, -1), ((0, 0), (0, D - b_f.size)))]
    rows.append(jnp.zeros((1, D), F32) if extra_row is None else extra_row)
    return jnp.concatenate([_pad8(r) for r in rows], axis=0)


def _unpack_small(pk, Lr, D, nf):
    o = 0
    out = []
    for rows, shape in ((Lr, (Lr, D)), (Lr, (Lr, D)), (Lr, (Lr, D)), (2 * Lr, (Lr, 2 * D)), (1, (D,))):
        out.append(pk[o:o + rows].reshape(shape))
        o += rows + (-rows % 8)
    out.append(pk[o, :Lr * nf].reshape(Lr, nf))
    return out, pk[o + 8, 0]


def kernel(x, p, g_mix, w_in, b_f, w_gate, b_gate, w_br_a, w_br_b, w_o, g_mlp, w_up, w_down, g_ple, w_ple, w_ple_gate, g_final, loss_target, m_g_mix, m_w_in, m_b_f, m_w_gate, m_b_gate, m_w_br_a, m_w_br_b, m_w_o, m_g_mlp, m_w_up, m_w_down, m_g_ple, m_w_ple, m_w_ple_gate, m_g_final, v_g_mix, v_w_in, v_b_f, v_w_gate, v_b_gate, v_w_br_a, v_w_br_b, v_w_o, v_g_mlp, v_w_up, v_w_down, v_g_ple, v_w_ple, v_w_ple_gate, v_g_final):
    Lr, D = g_mix.shape
    S = x.shape[1]
    nf = b_f.shape[1]
    big_w = dict(zip(BIG, (w_in, w_gate, w_br_a, w_br_b, w_o, w_up, w_down, w_ple, w_ple_gate)))
    big_m = dict(zip(BIG, (m_w_in, m_w_gate, m_w_br_a, m_w_br_b, m_w_o, m_w_up, m_w_down, m_w_ple, m_w_ple_gate)))
    big_v = dict(zip(BIG, (v_w_in, v_w_gate, v_w_br_a, v_w_br_b, v_w_o, v_w_up, v_w_down, v_w_ple, v_w_ple_gate)))

    x_pos, y_pos, c_pos = _place()
    me = _slot(x_pos, y_pos, c_pos).astype(jnp.int32).reshape(1)
    core = c_pos.astype(jnp.int32).reshape(1)
    chip = (2 * x_pos + y_pos).astype(jnp.int32).reshape(1)
    wbf = {k: _cast_bf16(k, big_w[k], me) for k in BIG}
    tabs = _rope_tables(S)
    bf_pad = jnp.pad(b_f, ((0, 0), (0, LANES - nf)))

    smalls = [dict(g_mix=g_mix[l:l + 1], b_f=bf_pad[l:l + 1], b_gate=b_gate[l:l + 1], g_mlp=g_mlp[l:l + 1],
                   g_ple=g_ple[l:l + 1]) for l in range(Lr)]

    gather_groups = (("in", "gate", "bra", "brb", "o"), ("up", "down"), ("ple", "pg"))
    per_layer = len(gather_groups)
    order = [(l, gi) for l in range(Lr) for gi in range(len(gather_groups))]
    on_links = {}

    def links_start(k, after=()):
        l, gi = order[k]
        on_links[k] = _exchange_start(f"gather_ici_start_{l}_{gi}", _plan_gather_ici, 4,
                                      [wbf[n][l] for n in gather_groups[gi]], None, after)
        return on_links[k][4]

    def arrive(k, after):
        l, gi = order[k]
        send_sems, recv_sems, _, lands, _ = on_links.pop(k)
        _, lands = _exchange_wait(f"gather_ici_wait_{l}_{gi}", _plan_gather_ici, send_sems, recv_sems, lands, None,
                                  after)
        send_sems, recv_sems, _, lands, token = _exchange_start(
            f"gather_relay_start_{l}_{gi}", _plan_gather_relay, 3, lands)
        tokens = [token]
        if k + 2 < len(order):
            tokens.append(links_start(k + 2, (token,)))
        _, lands = _exchange_wait(f"gather_relay_wait_{l}_{gi}", _plan_gather_relay, send_sems, recv_sems, lands)
        return {n: _as_matrix(n, g) for n, g in zip(gather_groups[gi], lands)}, tuple(tokens)

    h = x.reshape(S, D)
    Ws, saves = [], []
    queued = links_start(1, (links_start(0),))
    for l in range(Lr):
        W, tokens = arrive(per_layer * l, (h, queued) if l == 0 else (h,))

        def hook(part, l=l, W=W):
            def at(h_now):
                more, hook_tokens = arrive(per_layer * l + part, (h_now,))
                W.update(more)
                return hook_tokens
            return at

        h, sv = _layer_fwd(h, p[l, 0].astype(BF16), smalls[l], W, tabs, tokens, hook(1), hook(2))
        Ws.append(W)
        saves.append(sv)

    dh, dg_final, loss_part = _final(h, g_final.reshape(1, D), loss_target.reshape(S, D))

    big_out = {k: None for k in BIG}
    dsmalls = [None] * Lr
    groups = dict(mlp=("ple", "pg", "down", "up"), merge=("o", "bra", "brb", "gate"), proj=("in",))
    in_flight = []
    links_steps = 3

    def pair_start(l, stage, dW):
        dws = [dW[k].reshape(wbf[k][l].shape) for k in groups[stage]]
        lands = [lax.empty((4,) + d.shape[1:], BF16) for d in dws]
        started = _exchange_start(f"pair_start_{l}_{stage}", _plan_pair, 4, lands, dws)
        in_flight.append(dict(tag=f"{l}_{stage}", layer=l, names=groups[stage], on_links=False, started=started,
                              age=0))
        return started[4]

    def to_links(ex, after):
        send_sems, recv_sems, dws, lands, _ = ex["started"]
        dws, sib = _exchange_wait(f"pair_wait_{ex['tag']}", _plan_pair, send_sems, recv_sems, lands, dws, after)
        sums = [_pair_sum(k, d, s, core) for k, d, s in zip(ex["names"], dws, sib)]
        ex["started"] = _exchange_start(f"scatter_start_{ex['tag']}", _plan_scatter_chips, 3,
                                        [lax.empty(s.shape, BF16) for s in sums], sums)
        ex["on_links"], ex["age"] = True, 0
        return ex["started"][4]

    def finish(ex, after):
        send_sems, recv_sems, sums, lands, _ = ex["started"]
        sums, landed = _exchange_wait(f"scatter_wait_{ex['tag']}", _plan_scatter_chips, send_sems, recv_sems, lands,
                                      sums, after)
        for k, own, recv in zip(ex["names"], sums, landed):
            carried = None if big_out[k] is None else big_out[k][:4]
            big_out[k] = _adamw(k, ex["layer"], big_w[k], big_m[k], big_v[k], recv, own, chip, carried)
        in_flight.remove(ex)
        return landed[0]

    def advance(after):
        tokens = []
        for ex in list(in_flight):
            if not ex["on_links"]:
                tokens.append(to_links(ex, after))
            elif ex["age"] + 1 >= links_steps:
                tokens.append(finish(ex, after))
            else:
                ex["age"] += 1
        return tokens

    after = ()
    for l in reversed(range(Lr)):
        def ready(stage, dW, l=l):
            tokens = advance((dW[groups[stage][-1]],))
            return tuple(tokens + [pair_start(l, stage, dW)])

        dh, dW, dsmalls[l] = _layer_bwd(dh, saves[l], smalls[l], Ws[l], tabs, after, ready)
        after = tuple(advance((dh,)) + [pair_start(l, "proj", dW)])
    last = after[-1]
    for ex in [ex for ex in in_flight if not ex["on_links"]]:
        last = to_links(ex, (last,))
    for ex in list(in_flight):
        updated = tuple(out[4] for out in big_out.values() if out is not None)
        last = finish(ex, updated + (last,))

    cat = lambda key: jnp.concatenate([_pad8(d[key]) for d in dsmalls], axis=0)[::8]
    dbf = jnp.concatenate([d["b_f"] for d in dsmalls], axis=0).reshape(Lr, 8, LANES)[:, :nf, 0]
    loss_row = jnp.concatenate([loss_part, g_final.reshape(1, D)[:, LANES:]], axis=1)
    g_pack = _allreduce_small(_pack_small(cat("g_mix"), cat("g_mlp"), cat("g_ple"), cat("b_gate"), dg_final, dbf,
                                          loss_row), (last,))
    w_pack = _pack_small(g_mix, g_mlp, g_ple, b_gate, g_final, b_f)
    m_pack = _pack_small(m_g_mix, m_g_mlp, m_g_ple, m_b_gate, m_g_final, m_b_f)
    v_pack = _pack_small(v_g_mix, v_g_mlp, v_g_ple, v_b_gate, v_g_final, v_b_f)
    d_pack, mn_pack, vn_pack = _adamw_small(w_pack, g_pack, m_pack, v_pack)

    small_names = ("g_mix", "g_mlp", "g_ple", "b_gate", "g_final", "b_f")
    sg, loss = _unpack_small(g_pack, Lr, D, nf)
    small_out = {}
    for kind, pk in (("grad", None), ("delta", d_pack), ("m", mn_pack), ("v", vn_pack)):
        vals = sg if pk is None else _unpack_small(pk, Lr, D, nf)[0]
        small_out[kind] = dict(zip(small_names, vals))

    order = ("g_mix", "in", "b_f", "gate", "b_gate", "bra", "brb", "o", "g_mlp", "up", "down", "g_ple", "ple", "pg",
             "g_final")
    outs = [loss, dh.reshape(x.shape)]
    for idx, kind in enumerate(("grad", "delta", "m", "v")):
        for name in order:
            outs.append(big_out[name][idx] if name in big_out else small_out[kind][name])
    return tuple(outs)
```

```python
import jax
import jax.numpy as jnp
from jax import lax
from jax.experimental import pallas as pl
from jax.experimental.pallas import tpu as pltpu

F32, BF16 = jnp.float32, jnp.bfloat16
SDS = jax.ShapeDtypeStruct
BS = pl.BlockSpec
MESH = pl.DeviceIdType.MESH
HI = lax.Precision.HIGHEST

HEAD_DIM = 128
N_HEADS = 16
ATTN = N_HEADS * HEAD_DIM
DIL_GROUPS = ((128, 1), (512, 4), (2048, 16))
HEADS_PER_GROUP = 4
FOX_HEAD0 = 12
N_FOX = 4
BRANCH = HEADS_PER_GROUP * HEAD_DIM
BLOCK = 128
ROPE_DIM = HEAD_DIM // 4
ROPE_HALF = ROPE_DIM // 2
ROPE_THETA = 500000.0
NORM_EPS = 1e-6
SCALE = HEAD_DIM ** -0.5
NDEV = 8
LANES = 128

ADAM_LR, ADAM_B1, ADAM_B2, ADAM_EPS, ADAM_WD, ADAM_STEP = 0.001, 0.9, 0.999, 1e-08, 0.01, 10

V7X_VMEM_BYTES = 64 * 1024 * 1024
VMEM_LIMIT = (V7X_VMEM_BYTES * 3) // 4

NT_CONTRACT = 4096
NN = (((1,), (0,)), ((), ()))
NT = (((1,), (1,)), ((), ()))
TN = (((0,), (0,)), ((), ()))

BIG = ("in", "gate", "bra", "brb", "o", "up", "down", "ple", "pg")
ROW_SHARDED = ("in", "o", "down", "pg")


def _tile(n, pref):
    t = min(n, pref)
    while n % t:
        t -= LANES
    return t


def _cparams(sem):
    return pltpu.CompilerParams(dimension_semantics=sem, vmem_limit_bytes=VMEM_LIMIT)


def _sigmoid(x):
    return 1.0 / (1.0 + jnp.exp(-x))


def _mm(name, grid, a, a_spec, b, b_spec, dims, outs, acc_shape, extras=(), epilogue=None, carry=None, blocks=1):
    nk = grid[2]
    n_ex, n_out = len(extras), len(outs)
    n_carry = 0 if carry is None else 1

    def body(*refs):
        a_ref, b_ref = refs[0], refs[1]
        ex_refs = refs[2:2 + n_ex]
        out_refs = refs[2 + n_ex + n_carry:2 + n_ex + n_carry + n_out]

        def product():
            if blocks == 1:
                return lax.dot_general(a_ref[...], b_ref[...], dims, preferred_element_type=F32)
            n = b_ref.shape[2]
            total = None
            for jb in range(blocks):
                part = lax.dot_general(a_ref[:, jb * n:(jb + 1) * n], b_ref[jb], dims, preferred_element_type=F32)
                total = part if total is None else total + part
            return total

        def finish(acc):
            res = (acc,) if epilogue is None else epilogue(acc, *[e[...] for e in ex_refs])
            for o_ref, r in zip(out_refs, res):
                o_ref[...] = r.astype(o_ref.dtype)

        if nk == 1:
            finish(product())
            return
        acc_ref = refs[-1]
        k = pl.program_id(2)

        @pl.when(k == 0)
        def _():
            acc_ref[...] = jnp.zeros_like(acc_ref)

        acc_ref[...] += product()

        @pl.when(k == nk - 1)
        def _():
            finish(acc_ref[...])

    arrays = [a, b] + [e[0] for e in extras]
    specs = [a_spec, b_spec] + [e[1] for e in extras]
    alias = {}
    if carry is not None:
        arrays.append(carry[0])
        specs.append(BS(memory_space=pl.ANY))
        alias = {len(arrays) - 1: carry[1]}
    return pl.pallas_call(
        body, out_shape=[o[0] for o in outs], grid=grid, in_specs=specs, out_specs=[o[1] for o in outs],
        scratch_shapes=[pltpu.VMEM(acc_shape, F32)] if nk > 1 else [], input_output_aliases=alias, name=name,
        compiler_params=_cparams(("parallel", "parallel", "arbitrary")))(*arrays)


def _mm_nn(name, a, w, out_dtypes, *, tm, joff=0, ncols=None, tn=None, tk=None, extras=(), epilogue=None,
           out_specs=None, out_shapes=None):
    M, K = a.shape
    J, _, n = w.shape
    tn = tn or _tile(n, 512)
    tk = tk or _tile(K, 2048)
    per = n // tn
    ncols = ncols or J * n
    grid = (M // tm, ncols // tn, K // tk)
    a_spec = BS((tm, tk), lambda i, j, k: (i, k))
    w_spec = BS((None, tk, tn), lambda i, j, k: ((j + joff) // per, k, (j + joff) % per))
    if out_specs is None:
        out_specs = [BS((tm, tn), lambda i, j, k: (i, j))] * len(out_dtypes)
        out_shapes = [(M, ncols)] * len(out_dtypes)
    outs = [(SDS(s, d), sp) for s, d, sp in zip(out_shapes, out_dtypes, out_specs)]
    return _mm(name, grid, a, a_spec, w, w_spec, NN, outs, (tm, tn), extras, epilogue)


def _mm_nt(name, dy, w, out_dtype, *, tm, coff=0, ncontract=None, tc=None, dy_spec=None, extras=(), epilogue=None):
    J, K, n = w.shape
    M = dy.shape[-2]
    tko = _tile(K, 512)
    outs = [(SDS((M, K), out_dtype), BS((tm, tko), lambda i, ko, cc: (i, ko)))]
    if J > 1 and tc is None and dy_spec is None:
        blocks = max(1, min(J, NT_CONTRACT // n))
        grid = (M // tm, K // tko, J // blocks)
        dy_spec = BS((tm, blocks * n), lambda i, ko, cc: (i, cc))
        w_spec = BS((blocks, tko, n), lambda i, ko, cc: (cc, ko, 0))
        return _mm(name, grid, dy, dy_spec, w, w_spec, NT, outs, (tm, tko), extras, epilogue, blocks=blocks)[0]
    tc = tc or _tile(n, 2048)
    per = n // tc
    ncontract = ncontract or J * n
    grid = (M // tm, K // tko, ncontract // tc)
    if dy_spec is None:
        dy_spec = BS((tm, tc), lambda i, ko, cc: (i, cc))
    w_spec = BS((None, tko, tc), lambda i, ko, cc: ((cc + coff) // per, ko, (cc + coff) % per))
    return _mm(name, grid, dy, dy_spec, w, w_spec, NT, outs, (tm, tko), extras, epilogue)[0]


def _mm_tn(name, x, dy, wshape, *, ts, joff=0, ncols=None, tn=None, dy_spec=None, carry=None):
    J, K, n = wshape
    S = x.shape[0]
    tn = tn or _tile(n, 1024)
    tkm = _tile(K, 1024)
    per = n // tn
    ncols = ncols or J * n
    grid = (K // tkm, ncols // tn, S // ts)
    x_spec = BS((ts, tkm), lambda ki, j, sc: (sc, ki))
    if dy_spec is None:
        dy_spec = BS((ts, tn), lambda ki, j, sc: (sc, j))
    outs = [(SDS(wshape, BF16), BS((None, tkm, tn), lambda ki, j, sc: ((j + joff) // per, ki, (j + joff) % per)))]
    return _mm(name, grid, x, x_spec, dy, dy_spec, TN, outs, (tkm, tn), carry=None if carry is None else (carry, 0))[0]


def _rows(name, body, ins, outs, S, tr, sequential=False):
    def spec(shape, kind):
        if kind == "row":
            return BS((tr, shape[1]), lambda i: (i, 0))
        if kind == "order":
            return BS(memory_space=pl.ANY)
        return BS(tuple(shape), lambda i: (0,) * len(shape))
    return pl.pallas_call(
        body, out_shape=[o[0] for o in outs], grid=(S // tr,),
        in_specs=[spec(a.shape, k) for a, k in ins], out_specs=[spec(o[0].shape, o[1]) for o in outs],
        name=name, compiler_params=_cparams(("arbitrary" if sequential else "parallel",)))(*[a for a, _ in ins])


def _rms_fwd(h, g, tr=256, after=()):
    S, D = h.shape
    n_after = len(after)

    def body(h_ref, g_ref, *rest):
        u_ref = rest[n_after]
        x = h_ref[...]
        r = lax.rsqrt(jnp.mean(x * x, axis=-1, keepdims=True) + NORM_EPS)
        u_ref[...] = ((x * r) * g_ref[...]).astype(BF16)

    ins = [(h, "row"), (g, "full")] + [(t, "order") for t in after]
    return _rows("rms_fwd", body, ins, [(SDS((S, D), BF16), "row")], S, tr)[0]


def _rms_bwd(h, g, dy, dres, tr=256, after=()):
    S, D = h.shape
    n_after = len(after)

    def body(h_ref, g_ref, dy_ref, dres_ref, *rest):
        dh_ref, dhb_ref, dg_ref = rest[n_after:]
        x = h_ref[...]
        r = lax.rsqrt(jnp.mean(x * x, axis=-1, keepdims=True) + NORM_EPS)
        xhat = x * r
        dyv = dy_ref[...]
        gy = dyv * g_ref[...]
        dx = r * (gy - xhat * jnp.mean(gy * xhat, axis=-1, keepdims=True))
        dh = dres_ref[...] + dx
        dh_ref[...] = dh
        dhb_ref[...] = dh.astype(BF16)
        part = jnp.sum(dyv * xhat, axis=0, keepdims=True)

        @pl.when(pl.program_id(0) == 0)
        def _():
            dg_ref[...] = part

        @pl.when(pl.program_id(0) > 0)
        def _():
            dg_ref[...] += part

    ins = [(h, "row"), (g, "full"), (dy, "row"), (dres, "row")] + [(t, "order") for t in after]
    return _rows("rms_bwd", body, ins,
                 [(SDS((S, D), F32), "row"), (SDS((S, D), BF16), "row"), (SDS((1, D), F32), "full")], S, tr, True)


def _final(h, g, target, tr=256):
    S, D = h.shape

    def body(h_ref, g_ref, t_ref, dh_ref, dg_ref, loss_ref):
        x = h_ref[...]
        r = lax.rsqrt(jnp.mean(x * x, axis=-1, keepdims=True) + NORM_EPS)
        xhat = x * r
        gv = g_ref[...]
        err = xhat * gv - t_ref[...]
        lpart = 0.5 * jnp.sum(jnp.mean(err * err, axis=-1, keepdims=True), axis=0, keepdims=True)
        dyv = err * (1.0 / D)
        gy = dyv * gv
        dh_ref[...] = r * (gy - xhat * jnp.mean(gy * xhat, axis=-1, keepdims=True))
        part = jnp.sum(dyv * xhat, axis=0, keepdims=True)
        lrow = jnp.broadcast_to(lpart, (1, LANES))

        @pl.when(pl.program_id(0) == 0)
        def _():
            dg_ref[...] = part
            loss_ref[...] = lrow

        @pl.when(pl.program_id(0) > 0)
        def _():
            dg_ref[...] += part
            loss_ref[...] += lrow

    return _rows("final_head", body, [(h, "row"), (g, "full"), (target, "row")],
                 [(SDS((S, D), F32), "row"), (SDS((1, D), F32), "full"), (SDS((1, LANES), F32), "full")], S, tr, True)


def _ple_bwd(dh, pg, pp, tr=256, after=()):
    S, D = dh.shape
    n_after = len(after)

    def body(dh_ref, pg_ref, pp_ref, *rest):
        dpp_ref, dpgz_ref = rest[n_after:]
        d, g, q = dh_ref[...], pg_ref[...], pp_ref[...]
        dpp_ref[...] = (d * g).astype(BF16)
        dpgz_ref[...] = (d * q * g * (1.0 - g)).astype(BF16)

    ins = [(dh, "row"), (pg, "row"), (pp, "row")] + [(t, "order") for t in after]
    return _rows("ple_bwd", body, ins, [(SDS((S, D), BF16), "row"), (SDS((S, D), BF16), "row")], S, tr)


def _merge_bwd(dmerged, ya, yb, gates, tr=256):
    S, D = dmerged.shape

    def body(dm_ref, ya_ref, yb_ref, g_ref, dya_ref, dyb_ref, dgz_ref, dbg_ref):
        dm, a, b = dm_ref[...], ya_ref[...], yb_ref[...]
        g1, g2 = g_ref[:, :D], g_ref[:, D:]
        dya_ref[...] = (dm * g1).astype(BF16)
        dyb_ref[...] = (dm * g2).astype(BF16)
        dz1 = dm * a * g1 * (1.0 - g1)
        dz2 = dm * b * g2 * (1.0 - g2)
        dgz_ref[:, :D] = dz1.astype(BF16)
        dgz_ref[:, D:] = dz2.astype(BF16)
        p1 = jnp.sum(dz1, axis=0, keepdims=True)
        p2 = jnp.sum(dz2, axis=0, keepdims=True)

        @pl.when(pl.program_id(0) == 0)
        def _():
            dbg_ref[:, :D] = p1
            dbg_ref[:, D:] = p2

        @pl.when(pl.program_id(0) > 0)
        def _():
            dbg_ref[:, :D] += p1
            dbg_ref[:, D:] += p2

    return _rows("merge_bwd", body, [(dmerged, "row"), (ya, "row"), (yb, "row"), (gates, "row")],
                 [(SDS((S, D), BF16), "row"), (SDS((S, D), BF16), "row"), (SDS((S, 2 * D), BF16), "row"),
                  (SDS((1, 2 * D), F32), "full")], S, tr, True)


def _mix_weights(l0, l1, l2):
    mx = jnp.maximum(jnp.maximum(l0, l1), l2)
    e0, e1, e2 = jnp.exp(l0 - mx), jnp.exp(l1 - mx), jnp.exp(l2 - mx)
    inv = 1.0 / (e0 + e1 + e2)
    return e0 * inv, e1 * inv, e2 * inv


def _mix_fwd(os_, ls_, tr=512):
    S = os_[0].shape[0]

    def body(o0, o1, o2, l0, l1, l2, y_ref):
        w0, w1, w2 = _mix_weights(l0[...], l1[...], l2[...])
        y_ref[...] = (w0 * o0[...] + w1 * o1[...] + w2 * o2[...]).astype(BF16)

    return _rows("mix_fwd", body, [(a, "row") for a in (*os_, *ls_)], [(SDS((S, BRANCH), BF16), "row")], S, tr)[0]


def _mix_bwd(dy, os_, ls_, tr=512, after=()):
    S = dy.shape[0]
    n_after = len(after)

    def head_sums(t):
        return jnp.concatenate(
            [jnp.broadcast_to(jnp.sum(t[:, j * HEAD_DIM:(j + 1) * HEAD_DIM], axis=-1, keepdims=True), (tr, HEAD_DIM))
             for j in range(HEADS_PER_GROUP)], axis=1)

    def body(dy_ref, o0, o1, o2, l0, l1, l2, *rest):
        d0, d1, d2, p0, p1, p2 = rest[n_after:]
        ws = _mix_weights(l0[...], l1[...], l2[...])
        d = dy_ref[...]
        es = [head_sums(d * o[...]) for o in (o0, o1, o2)]
        ebar = ws[0] * es[0] + ws[1] * es[1] + ws[2] * es[2]
        for w, e, d_ref, p_ref in zip(ws, es, (d0, d1, d2), (p0, p1, p2)):
            d_ref[...] = (w * d).astype(BF16)
            p_ref[...] = w * (e - ebar)

    outs = [(SDS((S, BRANCH), BF16), "row")] * 3 + [(SDS((S, BRANCH), F32), "row")] * 3
    ins = [(a, "row") for a in (dy, *os_, *ls_)] + [(t, "order") for t in after]
    return _rows("mix_bwd", body, ins, outs, S, tr)


def _iota2(shape, dim):
    return lax.broadcasted_iota(jnp.int32, shape, dim)


def _fgate_fwd(zf, bf):
    S = zf.shape[0]
    nblk = S // BLOCK

    def body(zf_ref, bf_ref, cb_ref, ct_ref, flt_ref):
        tri = (_iota2((BLOCK, BLOCK), 0) >= _iota2((BLOCK, BLOCK), 1)).astype(F32)
        sel8 = (_iota2((8, LANES), 0) == _iota2((8, LANES), 1)).astype(F32)
        carry = jnp.zeros((1, LANES), F32)
        for blk in range(nblk):
            rows = pl.ds(blk * BLOCK, BLOCK)
            fl = zf_ref[rows, :] + bf_ref[...]
            ls = jnp.minimum(fl, 0.0) - jnp.log(1.0 + jnp.exp(-jnp.abs(fl)))
            c = jnp.dot(tri, ls, precision=HI, preferred_element_type=F32) + carry
            carry = carry + jnp.sum(ls, axis=0, keepdims=True)
            for hd in range(N_FOX):
                e_h = (_iota2((LANES, LANES), 0) == hd).astype(F32)
                cb_ref[rows, hd * LANES:(hd + 1) * LANES] = jnp.dot(c, e_h, precision=HI, preferred_element_type=F32)
            ct_ref[:, rows] = lax.dot_general(sel8, c, NT, precision=HI, preferred_element_type=F32)
            flt_ref[:, rows] = lax.dot_general(sel8, fl, NT, precision=HI, preferred_element_type=F32)

    vm = BS(memory_space=pltpu.VMEM)
    return pl.pallas_call(
        body, out_shape=[SDS((S, N_FOX * LANES), F32), SDS((8, S), F32), SDS((8, S), F32)],
        in_specs=[vm, vm], out_specs=[vm, vm, vm], name="fgate_fwd",
        compiler_params=pltpu.CompilerParams(vmem_limit_bytes=VMEM_LIMIT))(zf, bf)


def _fgate_bwd(dct, flt):
    S = dct.shape[1]
    nblk = S // BLOCK

    def body(dct_ref, flt_ref, dzf_ref, dbf_ref):
        rowid = _iota2((8, BLOCK), 0)
        tri_r = (_iota2((BLOCK, BLOCK), 0) >= _iota2((BLOCK, BLOCK), 1)).astype(F32)
        carry = jnp.zeros((8, 1), F32)
        dbf = jnp.zeros((8, 1), F32)
        for blk in reversed(range(nblk)):
            cols = pl.ds(blk * BLOCK, BLOCK)
            dc = jnp.where(rowid < N_FOX, dct_ref[:, cols], 0.0)
            dls = jnp.dot(dc, tri_r, precision=HI, preferred_element_type=F32) + carry
            carry = carry + jnp.sum(dc, axis=1, keepdims=True)
            dfl = jnp.where(rowid < N_FOX, dls * _sigmoid(-flt_ref[:, cols]), 0.0)
            dbf = dbf + jnp.sum(dfl, axis=1, keepdims=True)
            sq = jnp.concatenate([dfl, jnp.zeros((BLOCK - 8, BLOCK), F32)], axis=0)
            dzf_ref[cols, :] = sq.T.astype(BF16)
        dbf_ref[...] = jnp.broadcast_to(dbf, (8, LANES))

    vm = BS(memory_space=pltpu.VMEM)
    return pl.pallas_call(
        body, out_shape=[SDS((S, LANES), BF16), SDS((8, LANES), F32)],
        in_specs=[vm, vm], out_specs=[vm, vm], name="fgate_bwd",
        compiler_params=pltpu.CompilerParams(vmem_limit_bytes=VMEM_LIMIT))(dct, flt)


def _rope_tables(S):
    inv = ROPE_THETA ** (-jnp.arange(ROPE_HALF, dtype=F32) / ROPE_HALF)
    ang = jnp.arange(S, dtype=F32)[:, None] * inv[None, :]
    cos, sin = jnp.cos(ang), jnp.sin(ang)
    pad = HEAD_DIM - ROPE_DIM
    tc = jnp.concatenate([cos, cos, jnp.ones((S, pad), F32)], axis=1)
    ta = jnp.concatenate([-sin, jnp.zeros((S, HEAD_DIM - ROPE_HALF), F32)], axis=1)
    tb = jnp.concatenate([jnp.zeros((S, ROPE_HALF), F32), sin, jnp.zeros((S, pad), F32)], axis=1)
    return tc, ta, tb


def _rope(x, c, a, b):
    return x * c + pltpu.roll(x, HEAD_DIM - ROPE_HALF, 1) * a + pltpu.roll(x, ROPE_HALF, 1) * b


def _rope_t(g, c, a, b):
    return g * c - pltpu.roll(g, ROPE_HALF, 1) * b - pltpu.roll(g, HEAD_DIM - ROPE_HALF, 1) * a


DIL_TQ = 256


def _dil_extent(i, window):
    return max(0, i * DIL_TQ - window), (i + 1) * DIL_TQ


def _dil_mask(i, lo, ext, window, d):
    diff = (i * DIL_TQ - lo) + _iota2((DIL_TQ, ext), 0) - _iota2((DIL_TQ, ext), 1)
    mask = (diff >= 0) & (diff <= window)
    return mask & ((diff & (d - 1)) == 0) if d > 1 else mask


def _wide(t, w):
    return t if w == LANES else jnp.concatenate([t] * (w // LANES), axis=1)


def _dil_fwd(z3, tabs, grp, S):
    window, d = DIL_GROUPS[grp]
    nq = S // DIL_TQ

    def body(qkv_ref, c_ref, a_ref, b_ref, o_ref, lse_ref, qs, ks):
        c, a, b = c_ref[...], a_ref[...], b_ref[...]
        qs[...] = _rope(qkv_ref[0].astype(F32), c, a, b).astype(BF16)
        ks[...] = _rope(qkv_ref[1].astype(F32), c, a, b).astype(BF16)
        for i in range(nq):
            lo, hi = _dil_extent(i, window)
            rows = pl.ds(i * DIL_TQ, DIL_TQ)
            s = lax.dot_general(qs[rows, :], ks[lo:hi, :], NT, preferred_element_type=F32) * SCALE
            s = jnp.where(_dil_mask(i, lo, hi - lo, window, d), s, -jnp.inf)
            m = jnp.max(s, axis=-1, keepdims=True)
            p = jnp.exp(s - m)
            l = jnp.sum(p, axis=-1, keepdims=True)
            o = jnp.dot(p.astype(BF16), qkv_ref[2, lo:hi, :], preferred_element_type=F32)
            o_ref[rows, :] = o / l
            lse_ref[rows, :] = jnp.broadcast_to(m + jnp.log(l), (DIL_TQ, HEAD_DIM))

    hh0 = grp * HEADS_PER_GROUP
    tspec = BS((S, HEAD_DIM), lambda j: (0, 0))
    ospec = BS((S, HEAD_DIM), lambda j: (0, j))
    return pl.pallas_call(
        body, out_shape=[SDS((S, BRANCH), F32)] * 2, grid=(HEADS_PER_GROUP,),
        in_specs=[BS((3, S, HEAD_DIM), lambda j: (0, 0, hh0 + j)), tspec, tspec, tspec],
        out_specs=[ospec, ospec], scratch_shapes=[pltpu.VMEM((S, HEAD_DIM), BF16)] * 2,
        name="dil_fwd", compiler_params=_cparams(("parallel",)))(z3, *tabs)


def _dil_bwd(z3, tabs, grp, S, do, lse, dpl, dz3):
    window, d = DIL_GROUPS[grp]
    nq = S // DIL_TQ
    n_carry = 0 if dz3 is None else 1

    def body(*refs):
        qkv_ref, c_ref, a_ref, b_ref, do_ref, lse_ref, dpl_ref = refs[:7]
        dz_ref = refs[7 + n_carry]
        qs, ks, dq_acc, dk_acc, dv_acc = refs[8 + n_carry:]
        c, a, b = c_ref[...], a_ref[...], b_ref[...]
        qs[...] = _rope(qkv_ref[0].astype(F32), c, a, b).astype(BF16)
        ks[...] = _rope(qkv_ref[1].astype(F32), c, a, b).astype(BF16)
        dk_acc[...] = jnp.zeros_like(dk_acc)
        dv_acc[...] = jnp.zeros_like(dv_acc)
        for i in range(nq):
            lo, hi = _dil_extent(i, window)
            w = hi - lo
            rows = pl.ds(i * DIL_TQ, DIL_TQ)
            q, k, v = qs[rows, :], ks[lo:hi, :], qkv_ref[2, lo:hi, :]
            s = lax.dot_general(q, k, NT, preferred_element_type=F32) * SCALE
            p = jnp.where(_dil_mask(i, lo, w, window, d), jnp.exp(s - _wide(lse_ref[rows, :], w)), 0.0)
            dob = do_ref[rows, :]
            dp = lax.dot_general(dob, v, NT, preferred_element_type=F32)
            delta = jnp.sum(p * dp, axis=-1, keepdims=True)
            ds = (p * (dp - delta + _wide(dpl_ref[rows, :], w))).astype(BF16)
            dq_acc[rows, :] = jnp.dot(ds, k, preferred_element_type=F32) * SCALE
            dk_acc[lo:hi, :] += lax.dot_general(ds, q, TN, preferred_element_type=F32) * SCALE
            dv_acc[lo:hi, :] += lax.dot_general(p.astype(BF16), dob, TN, preferred_element_type=F32)
        dz_ref[0] = _rope_t(dq_acc[...], c, a, b).astype(BF16)
        dz_ref[1] = _rope_t(dk_acc[...], c, a, b).astype(BF16)
        dz_ref[2] = dv_acc[...].astype(BF16)

    hh0 = grp * HEADS_PER_GROUP
    tspec = BS((S, HEAD_DIM), lambda j: (0, 0))
    bspec = BS((S, HEAD_DIM), lambda j: (0, j))
    zspec = BS((3, S, HEAD_DIM), lambda j: (0, 0, hh0 + j))
    arrays = [z3, *tabs, do, lse, dpl]
    specs = [zspec, tspec, tspec, tspec, bspec, bspec, bspec]
    alias = {}
    if dz3 is not None:
        arrays.append(dz3)
        specs.append(BS(memory_space=pl.ANY))
        alias = {len(arrays) - 1: 0}
    return pl.pallas_call(
        body, out_shape=SDS((3, S, ATTN), BF16), grid=(HEADS_PER_GROUP,), in_specs=specs, out_specs=zspec,
        scratch_shapes=[pltpu.VMEM((S, HEAD_DIM), BF16)] * 2 + [pltpu.VMEM((S, HEAD_DIM), F32)] * 3,
        input_output_aliases=alias, name="dil_bwd", compiler_params=_cparams(("parallel",)))(*arrays)


FOX_TQ = 256


def _fox_scores(q_ref, k_ref, cb_ref, ct_ref, i, ext):
    rows = pl.ds(i * FOX_TQ, FOX_TQ)
    s = lax.dot_general(q_ref[0, rows, :], k_ref[0, 0:ext, :], NT, preferred_element_type=F32) * SCALE
    s = s + _wide(cb_ref[rows, :], ext) - ct_ref[:, 0:ext]
    qpos = i * FOX_TQ + _iota2((FOX_TQ, ext), 0)
    kpos = _iota2((FOX_TQ, ext), 1)
    return jnp.where(kpos <= qpos, s, -jnp.inf)


def _fox_specs(S):
    qs = BS((1, S, HEAD_DIM), lambda h: (0, 0, FOX_HEAD0 + h))
    ks = BS((1, S, HEAD_DIM), lambda h: (1, 0, FOX_HEAD0 + h))
    vs = BS((1, S, HEAD_DIM), lambda h: (2, 0, FOX_HEAD0 + h))
    hb = BS((S, HEAD_DIM), lambda h: (0, h))
    ct = BS((None, 1, S), lambda h: (h, 0, 0))
    return qs, ks, vs, hb, ct


def _fox_fwd(z3, cb, ct, S):
    nq = S // FOX_TQ

    def body(q_ref, k_ref, v_ref, cb_ref, ct_ref, y_ref, lse_ref):
        for i in range(nq):
            ext = (i + 1) * FOX_TQ
            rows = pl.ds(i * FOX_TQ, FOX_TQ)
            s = _fox_scores(q_ref, k_ref, cb_ref, ct_ref, i, ext)
            m = jnp.max(s, axis=-1, keepdims=True)
            p = jnp.exp(s - m)
            l = jnp.sum(p, axis=-1, keepdims=True)
            o = jnp.dot(p.astype(BF16), v_ref[0, 0:ext, :], preferred_element_type=F32)
            y_ref[rows, :] = (o / l).astype(BF16)
            lse_ref[rows, :] = jnp.broadcast_to(m + jnp.log(l), (FOX_TQ, HEAD_DIM))

    qs, ks, vs, hb, cts = _fox_specs(S)
    return pl.pallas_call(
        body, out_shape=[SDS((S, BRANCH), BF16), SDS((S, BRANCH), F32)], grid=(N_FOX,),
        in_specs=[qs, ks, vs, hb, cts], out_specs=[hb, hb], name="fox_fwd",
        compiler_params=_cparams(("parallel",)))(z3, z3, z3, cb, ct.reshape(8, 1, S))


def _fox_bwd(z3, cb, ct, S, dy, lse, dz3):
    nq = S // FOX_TQ

    def body(q_ref, k_ref, v_ref, cb_ref, ct_ref, dy_ref, lse_ref, _, dz_ref, dct_ref, dk_acc, dv_acc):
        dk_acc[...] = jnp.zeros_like(dk_acc)
        dv_acc[...] = jnp.zeros_like(dv_acc)
        dct_ref[...] = jnp.zeros_like(dct_ref)
        for i in range(nq):
            ext = (i + 1) * FOX_TQ
            rows = pl.ds(i * FOX_TQ, FOX_TQ)
            s = _fox_scores(q_ref, k_ref, cb_ref, ct_ref, i, ext)
            p = jnp.exp(s - _wide(lse_ref[rows, :], ext))
            dob = dy_ref[rows, :]
            dp = lax.dot_general(dob, v_ref[0, 0:ext, :], NT, preferred_element_type=F32)
            ds = p * (dp - jnp.sum(p * dp, axis=-1, keepdims=True))
            dsb = ds.astype(BF16)
            dz_ref[0, rows, :] = (jnp.dot(dsb, k_ref[0, 0:ext, :], preferred_element_type=F32) * SCALE).astype(BF16)
            dk_acc[0:ext, :] += lax.dot_general(dsb, q_ref[0, rows, :], TN, preferred_element_type=F32) * SCALE
            dv_acc[0:ext, :] += lax.dot_general(p.astype(BF16), dob, TN, preferred_element_type=F32)
            dct_ref[:, 0:ext] -= jnp.sum(ds, axis=0, keepdims=True)
        dz_ref[1] = dk_acc[...].astype(BF16)
        dz_ref[2] = dv_acc[...].astype(BF16)

    qs, ks, vs, hb, cts = _fox_specs(S)
    zspec = BS((3, S, HEAD_DIM), lambda h: (0, 0, FOX_HEAD0 + h))
    dz, dct = pl.pallas_call(
        body, out_shape=[SDS((3, S, ATTN), BF16), SDS((8, 1, S), F32)], grid=(N_FOX,),
        in_specs=[qs, ks, vs, hb, cts, hb, hb, BS(memory_space=pl.ANY)], out_specs=[zspec, cts],
        scratch_shapes=[pltpu.VMEM((S, HEAD_DIM), F32)] * 2, input_output_aliases={7: 0}, name="fox_bwd",
        compiler_params=_cparams(("parallel",)))(z3, z3, z3, cb, ct.reshape(8, 1, S), dy, lse, dz3)
    return dz, dct.reshape(8, S)


TM = 1024


def _layer_fwd(h, p_bf, sm, W, tabs, after=(), mid=None, late=None):
    S, D = h.shape
    qkv_cols = 3 * ATTN
    u = _rms_fwd(h, sm["g_mix"], after=after)
    tn = 1024
    per = ATTN // tn
    z3 = _mm_nn("mm_qkv", u, W["in"], [BF16], tm=TM, ncols=qkv_cols, tn=tn, out_shapes=[(3, S, ATTN)],
                out_specs=[BS((None, TM, tn), lambda i, j, k: (j // per, i, j % per))])[0]
    zf = _mm_nn("mm_f", u, W["in"], [F32], tm=TM, joff=qkv_cols // LANES, ncols=LANES, tn=LANES)[0]
    cb, ct, flt = _fgate_fwd(zf, sm["b_f"])
    os_, ls_ = [], []
    for grp in range(len(DIL_GROUPS)):
        o, l = _dil_fwd(z3, tabs, grp, S)
        os_.append(o)
        ls_.append(l)
    ya_h = _mix_fwd(os_, ls_)
    yb_h, lse_fox = _fox_fwd(z3, cb, ct, S)

    tg = _tile(W["gate"].shape[2], 512)
    gates = _mm_nn("mm_gate", u, W["gate"], [F32], tm=TM,
                   extras=[(sm["b_gate"], BS((1, tg), lambda i, j, k: (0, j)))],
                   epilogue=lambda acc, bias: (_sigmoid(acc + bias),))[0]
    ya = _mm_nn("mm_bra", ya_h, W["bra"], [F32], tm=TM)[0]
    tb = _tile(W["brb"].shape[2], 512)
    g2off = D // tb
    yb, merged = _mm_nn(
        "mm_brb", yb_h, W["brb"], [F32, BF16], tm=TM,
        extras=[(ya, BS((TM, tb), lambda i, j, k: (i, j))), (gates, BS((TM, tb), lambda i, j, k: (i, j))),
                (gates, BS((TM, tb), lambda i, j, k: (i, j + g2off)))],
        epilogue=lambda acc, a, g1, g2: (acc, g1 * a + g2 * acc))
    to = _tile(D, 512)
    res_spec = BS((TM, to), lambda i, j, k: (i, j))
    h1 = _mm_nn("mm_o", merged, W["o"], [F32], tm=TM, extras=[(h, res_spec)], epilogue=lambda acc, r: (r + acc,))[0]

    m = _rms_fwd(h1, sm["g_mlp"], after=() if mid is None else mid(h1))

    def up_epi(acc):
        r = jnp.maximum(acc, 0.0)
        return r, r * r

    relu_a, act = _mm_nn("mm_up", m, W["up"], [BF16, BF16], tm=TM, tn=_tile(W["up"].shape[2], 1024), epilogue=up_epi)
    h2 = _mm_nn("mm_down", act, W["down"], [F32], tm=TM, tk=_tile(act.shape[1], 4096), extras=[(h1, res_spec)],
                epilogue=lambda acc, r: (r + acc,))[0]

    nrm = _rms_fwd(h2, sm["g_ple"], after=() if late is None else late(h2))
    pp = _mm_nn("mm_ple", p_bf, W["ple"], [F32], tm=TM)[0]

    def pg_epi(acc, r, q):
        g = _sigmoid(acc)
        return g, r + g * q

    pg, h3 = _mm_nn("mm_pg", nrm, W["pg"], [F32, F32], tm=TM, extras=[(h2, res_spec), (pp, res_spec)], epilogue=pg_epi)
    saved = dict(h=h, u=u, z3=z3, cb=cb, ct=ct, flt=flt, os=os_, ls=ls_, ya_h=ya_h, yb_h=yb_h, lse_fox=lse_fox,
                 gates=gates, ya=ya, yb=yb, merged=merged, h1=h1, m=m, relu_a=relu_a, act=act, h2=h2, nrm=nrm,
                 pg=pg, pp=pp, p_bf=p_bf)
    return h3, saved


def _layer_bwd(dh3, sv, sm, W, tabs, after=(), ready=None):
    S, D = dh3.shape
    dW, dsm = {}, {}

    dpp, dpgz = _ple_bwd(dh3, sv["pg"], sv["pp"], after=after)
    dW["ple"] = _mm_tn("tn_ple", sv["p_bf"], dpp, W["ple"].shape, ts=S)
    dW["pg"] = _mm_tn("tn_pg", sv["nrm"], dpgz, W["pg"].shape, ts=S)
    dn = _mm_nt("nt_pg", dpgz, W["pg"], F32, tm=TM)
    dh2, dh2b, dsm["g_ple"] = _rms_bwd(sv["h2"], sm["g_ple"], dn, dh3)

    ta = _tile(W["up"].shape[1], 512)
    tff = _tile(W["down"].shape[1], 512)
    da = _mm_nt("nt_down", dh2b, W["down"], BF16, tm=TM,
                extras=[(sv["relu_a"], BS((TM, tff), lambda i, ko, cc: (i, ko)))],
                epilogue=lambda acc, r: (acc * (2.0 * r.astype(F32)),))
    dW["down"] = _mm_tn("tn_down", sv["act"], dh2b, W["down"].shape, ts=S)
    dm = _mm_nt("nt_up", da, W["up"], F32, tm=TM)
    dW["up"] = _mm_tn("tn_up", sv["m"], da, W["up"].shape, ts=S)
    dh1, dh1b, dsm["g_mlp"] = _rms_bwd(sv["h1"], sm["g_mlp"], dm, dh2, after=() if ready is None else ready("mlp", dW))

    dmerged = _mm_nt("nt_o", dh1b, W["o"], F32, tm=TM)
    dW["o"] = _mm_tn("tn_o", sv["merged"], dh1b, W["o"].shape, ts=S)
    dya, dyb, dgz, dsm["b_gate"] = _merge_bwd(dmerged, sv["ya"], sv["yb"], sv["gates"])
    dya_h = _mm_nt("nt_bra", dya, W["bra"], F32, tm=TM)
    dyb_h = _mm_nt("nt_brb", dyb, W["brb"], BF16, tm=TM)
    dW["bra"] = _mm_tn("tn_bra", sv["ya_h"], dya, W["bra"].shape, ts=S)
    dW["brb"] = _mm_tn("tn_brb", sv["yb_h"], dyb, W["brb"].shape, ts=S)
    dW["gate"] = _mm_tn("tn_gate", sv["u"], dgz, W["gate"].shape, ts=S)
    du = _mm_nt("nt_gate", dgz, W["gate"], F32, tm=TM)

    dos_, dpls_ = [], []
    mixed = _mix_bwd(dya_h, sv["os"], sv["ls"], after=() if ready is None else ready("merge", dW))
    dos_, dpls_ = mixed[:3], mixed[3:]
    dz3 = None
    for grp in range(len(DIL_GROUPS)):
        dz3 = _dil_bwd(sv["z3"], tabs, grp, S, dos_[grp], sv["ls"][grp], dpls_[grp], dz3)
    dz3, dct = _fox_bwd(sv["z3"], sv["cb"], sv["ct"], S, dyb_h, sv["lse_fox"], dz3)
    dzf, dsm["b_f"] = _fgate_bwd(dct, sv["flt"])

    tn = 1024
    per = ATTN // tn
    qkv_cols = 3 * ATTN
    dwin = _mm_tn("tn_qkv", sv["u"], dz3, W["in"].shape, ts=S, ncols=qkv_cols, tn=tn,
                  dy_spec=BS((None, S, tn), lambda ki, j, sc: (j // per, sc, j % per)))
    dW["in"] = _mm_tn("tn_f", sv["u"], dzf, W["in"].shape, ts=S, joff=qkv_cols // LANES, ncols=LANES, tn=LANES,
                      carry=dwin)
    tu = _tile(D, 512)
    prev_spec = BS((TM, tu), lambda i, ko, cc: (i, ko))
    du = _mm_nt("nt_qkv", dz3, W["in"], F32, tm=TM, ncontract=qkv_cols, tc=ATTN,
                dy_spec=BS((None, TM, ATTN), lambda i, ko, cc: (cc, i, 0)),
                extras=[(du, prev_spec)], epilogue=lambda acc, prev: (prev + acc,))
    du = _mm_nt("nt_f", dzf, W["in"], F32, tm=TM, coff=qkv_cols // LANES, ncontract=LANES, tc=LANES,
                extras=[(du, prev_spec)], epilogue=lambda acc, prev: (prev + acc,))
    dh0, _, dsm["g_mix"] = _rms_bwd(sv["h"], sm["g_mix"], du, dh1)
    return dh0, dW, dsm


def _shard_tiles(R, Cp):
    tc = Cp if Cp <= 2048 else _tile(Cp, 1024)
    tr = R
    while tr * tc > 256 * 1024 and tr % 16 == 0:
        tr //= 2
    return tr, tc


def _padded_cols(C):
    return -(-C // LANES) * LANES


def _cast_bf16(name, w, me):
    Lr, R, C = w.shape
    Cp = _padded_cols(C)
    tr, tc = _shard_tiles(R, Cp)

    def body(me_ref, w_ref, *o_refs):
        for l in range(Lr):
            x = w_ref[l]
            if Cp != C:
                col = pl.program_id(1) * tc + _iota2((tr, tc), 1)
                x = jnp.where(col < C, x, 0.0)
            o_refs[l][...] = x.astype(BF16)

    grid_spec = pltpu.PrefetchScalarGridSpec(
        num_scalar_prefetch=1, grid=(R // tr, Cp // tc),
        in_specs=[BS((Lr, tr, tc), lambda i, j, me_ref: (0, i, j))],
        out_specs=[BS((None, tr, tc), lambda i, j, me_ref: (me_ref[0], i, j))] * Lr)
    return pl.pallas_call(
        body, out_shape=[SDS((NDEV, R, Cp), BF16)] * Lr, grid_spec=grid_spec,
        name="cast_" + name, compiler_params=_cparams(("parallel", "parallel")))(me, w)


def _adam_math(w, g, m, v):
    m = ADAM_B1 * m + (1.0 - ADAM_B1) * g
    v = ADAM_B2 * v + (1.0 - ADAM_B2) * (g * g)
    m_hat = m / (1.0 - ADAM_B1 ** ADAM_STEP)
    v_hat = v / (1.0 - ADAM_B2 ** ADAM_STEP)
    delta = -ADAM_LR * (m_hat / (jnp.sqrt(v_hat) + ADAM_EPS) + ADAM_WD * w)
    return delta, m, v


def _pair_sum(name, dw, sib, place):
    _, R, Cp = dw.shape
    tr, tc = _shard_tiles(R, Cp)

    def body(place_ref, a_ref, b_ref, o_ref):
        o_ref[...] = (a_ref[...].astype(F32) + b_ref[...].astype(F32)).astype(BF16)

    def other(r, place_ref):
        return jnp.bitwise_xor(place_ref[1], r + 1)

    grid_spec = pltpu.PrefetchScalarGridSpec(
        num_scalar_prefetch=1, grid=(3, R // tr, Cp // tc),
        in_specs=[BS((None, tr, tc), lambda r, i, j, place_ref: (2 * other(r, place_ref) + place_ref[0], i, j)),
                  BS((None, tr, tc), lambda r, i, j, place_ref: (other(r, place_ref), i, j))],
        out_specs=BS((None, tr, tc), lambda r, i, j, place_ref: (other(r, place_ref), i, j)))
    return pl.pallas_call(
        body, out_shape=SDS((4, R, Cp), BF16), grid_spec=grid_spec, name="pair_sum_" + name,
        compiler_params=_cparams(("parallel", "parallel", "parallel")))(place, dw, sib)


def _adamw(name, layer, w, m, v, recv, dw, sib, place, carried):
    Lr, R, C = w.shape
    nslot, _, Cp = recv.shape
    tr, tc = _shard_tiles(R, Cp)
    n_carry = 0 if carried is None else 4

    def body(place_ref, w_ref, m_ref, v_ref, r_ref, dw_ref, sib_ref, *rest):
        g_out, d_out, m_out, v_out, token = rest[n_carry:]
        mine = dw_ref[...].astype(F32) + sib_ref[...].astype(F32)
        g = None
        for s in range(nslot):
            part = jnp.where(place_ref[1] == s, mine, r_ref[s].astype(F32))
            g = part if g is None else g + part
        delta, mn, vn = _adam_math(w_ref[...], g, m_ref[...], v_ref[...])
        g_out[...] = g
        d_out[...] = delta
        m_out[...] = mn
        v_out[...] = vn
        token[...] = jnp.zeros_like(token)

    spec = BS((None, tr, tc), lambda i, j, place_ref: (layer, i, j))
    rspec = BS((nslot, tr, tc), lambda i, j, place_ref: (0, i, j))
    dspec = BS((None, tr, tc), lambda i, j, place_ref: (2 * place_ref[1] + place_ref[0], i, j))
    sspec = BS((None, tr, tc), lambda i, j, place_ref: (place_ref[1], i, j))
    arrays, specs, alias = [w, m, v, recv, dw, sib], [spec, spec, spec, rspec, dspec, sspec], {}
    if carried is not None:
        arrays += list(carried)
        specs += [BS(memory_space=pl.ANY)] * 4
        alias = {7 + k: k for k in range(4)}
    grid_spec = pltpu.PrefetchScalarGridSpec(
        num_scalar_prefetch=1, grid=(R // tr, Cp // tc), in_specs=specs,
        out_specs=[spec] * 4 + [BS((8, LANES), lambda i, j, place_ref: (0, 0))])
    return pl.pallas_call(
        body, out_shape=[SDS(w.shape, F32)] * 4 + [SDS((8, LANES), F32)], grid_spec=grid_spec,
        input_output_aliases=alias, name="adamw_" + name,
        compiler_params=_cparams(("arbitrary", "arbitrary")))(place, *arrays)


def _adamw_small(w, g, m, v):
    vm = BS(memory_space=pltpu.VMEM)

    def body(w_ref, g_ref, m_ref, v_ref, d_out, m_out, v_out):
        delta, mn, vn = _adam_math(w_ref[...], g_ref[...], m_ref[...], v_ref[...])
        d_out[...] = delta
        m_out[...] = mn
        v_out[...] = vn

    return pl.pallas_call(body, out_shape=[SDS(w.shape, F32)] * 3, in_specs=[vm] * 4, out_specs=[vm] * 3,
                          name="adamw_small")(w, g, m, v)


def _place():
    return lax.axis_index("x"), lax.axis_index("y"), lax.axis_index("c")


def _slot(px, py, pc):
    return 4 * px + 2 * py + pc


def _other_chips(x, y):
    return [(1 - x, y), (x, 1 - y), (1 - x, 1 - y)]


def _peer(x, y, c, rel):
    return (1 - x if rel & 4 else x, 1 - y if rel & 2 else y, 1 - c if rel & 1 else c)


def _plan_gather_ici(src, land):
    x, y, c = _place()
    own = land.at[_slot(x, y, c)]
    return [(own, own, (x, y, 1 - c))] + [(own, own, (*chip, c)) for chip in _other_chips(x, y)]


def _plan_gather_relay(src, land):
    x, y, c = _place()
    return [(land.at[_slot(*chip, c)], land.at[_slot(*chip, c)], (x, y, 1 - c)) for chip in _other_chips(x, y)]


def _plan_pair(src, land):
    x, y, c = _place()
    return [(src.at[2 * q + (1 - c)], land.at[q], (x, y, 1 - c)) for q in range(4)]


def _plan_scatter_chips(src, land):
    x, y, c = _place()
    return [(src.at[2 * cx + cy], land.at[2 * x + y], (cx, cy, c)) for cx, cy in _other_chips(x, y)]


EFFECT = pltpu.SideEffectType.DATAFLOW_SIDE_EFFECTING
HBM_SPEC = BS(memory_space=pltpu.HBM)
SEM_SPEC = BS(memory_space=pltpu.SEMAPHORE)
ORDER_SPEC = BS(memory_space=pl.ANY)


def _exchange_start(name, plan, ncopy, lands, srcs=None, after=()):
    bufs = ([] if srcs is None else list(srcs)) + list(lands)
    n, nb, ns = len(lands), len(bufs), len(bufs) - len(lands)
    n_after = len(after)

    def body(*refs):
        src_refs = refs[:ns] if ns else [None] * n
        land_refs = refs[ns:nb]
        send_sems, recv_sems = refs[nb + n_after], refs[nb + n_after + 1]
        token = refs[-1]
        for i in range(n):
            for k, (src, dst, to) in enumerate(plan(src_refs[i], land_refs[i])):
                pltpu.make_async_remote_copy(src_ref=src, dst_ref=dst, send_sem=send_sems.at[i * ncopy + k],
                                             recv_sem=recv_sems.at[i * ncopy + k], device_id=to,
                                             device_id_type=MESH).start()
        token[...] = jnp.zeros_like(token)

    out_shape = [pltpu.SemaphoreType.DMA((n * ncopy,)), pltpu.SemaphoreType.DMA((n * ncopy,))]
    out_shape += [pltpu.HBM(b.shape, b.dtype) for b in bufs] + [SDS((8, LANES), F32)]
    out = pl.pallas_call(
        body, name=name, out_shape=out_shape, in_specs=[HBM_SPEC] * nb + [ORDER_SPEC] * n_after,
        out_specs=[SEM_SPEC, SEM_SPEC] + [HBM_SPEC] * nb + [BS(memory_space=pltpu.VMEM)],
        input_output_aliases={i: 2 + i for i in range(nb)},
        compiler_params=pltpu.CompilerParams(has_side_effects=EFFECT))(
            *[pltpu.with_memory_space_constraint(b, pltpu.HBM) for b in bufs], *after)
    return out[0], out[1], out[2:2 + ns], out[2 + ns:2 + nb], out[-1]


def _exchange_wait(name, plan, send_sems, recv_sems, lands, srcs=None, after=()):
    bufs = ([] if srcs is None else list(srcs)) + list(lands)
    n, nb, ns = len(lands), len(bufs), len(bufs) - len(lands)

    def body(*refs):
        src_refs = refs[:ns] if ns else [None] * n
        land_refs = refs[ns:nb]
        send_ref, recv_ref = refs[nb], refs[nb + 1]
        for i in range(n):
            copies = plan(src_refs[i], land_refs[i])
            for k, (src, dst, to) in enumerate(copies):
                cp = pltpu.make_async_remote_copy(src_ref=src, dst_ref=dst, send_sem=send_ref.at[i * len(copies) + k],
                                                  recv_sem=recv_ref.at[i * len(copies) + k], device_id=to,
                                                  device_id_type=MESH)
                cp.wait_send()
                cp.wait_recv()

    out = pl.pallas_call(
        body, name=name, out_shape=[pltpu.HBM(b.shape, b.dtype) for b in bufs],
        in_specs=[HBM_SPEC] * nb + [SEM_SPEC, SEM_SPEC] + [ORDER_SPEC] * len(after), out_specs=[HBM_SPEC] * nb,
        input_output_aliases={i: i for i in range(nb)},
        compiler_params=pltpu.CompilerParams(has_side_effects=EFFECT))(*bufs, send_sems, recv_sems, *after)
    return out[:ns], out[ns:]


def _allreduce_small(v, after=()):
    R, Wd = v.shape
    n_after = len(after)

    def body(v_ref, *rest):
        o_ref, buf, send_sems, recv_sems = rest[n_after:]
        x, y, c = _place()
        me = _slot(x, y, c)
        buf[me] = v_ref[...]
        started = []
        for rel in range(1, NDEV):
            peer = _peer(x, y, c, rel)
            cp = pltpu.make_async_remote_copy(
                src_ref=v_ref, dst_ref=buf.at[me], send_sem=send_sems.at[rel - 1], recv_sem=recv_sems.at[rel - 1],
                device_id=peer, device_id_type=MESH)
            cp.start()
            started.append(cp)
        for cp in started:
            cp.wait()
        acc = buf[0]
        for j in range(1, NDEV):
            acc = acc + buf[j]
        o_ref[...] = acc

    vm = BS(memory_space=pltpu.VMEM)
    return pl.pallas_call(
        body, out_shape=SDS((R, Wd), F32), in_specs=[vm] + [ORDER_SPEC] * n_after, out_specs=vm,
        scratch_shapes=[pltpu.VMEM((NDEV, R, Wd), F32), pltpu.SemaphoreType.DMA((7,)), pltpu.SemaphoreType.DMA((7,))],
        name="allreduce_small")(v, *after)


def _as_matrix(name, g):
    if name in ROW_SHARDED:
        return g.reshape(1, NDEV * g.shape[1], g.shape[2])
    return g


def _pad8(a):
    return jnp.pad(a, ((0, -a.shape[0] % 8), (0, 0)))


def _pack_small(g_mix, g_mlp, g_ple, b_gate, g_final, b_f, extra_row=None):
    Lr, D = g_mix.shape
    rows = [g_mix, g_mlp, g_ple, b_gate.reshape(2 * Lr, D), g_final.reshape(1, D),
            jnp.pad(b_f.reshape(1, -1), ((0, 0), (0, D - b_f.size)))]
    rows.append(jnp.zeros((1, D), F32) if extra_row is None else extra_row)
    return jnp.concatenate([_pad8(r) for r in rows], axis=0)


def _unpack_small(pk, Lr, D, nf):
    o = 0
    out = []
    for rows, shape in ((Lr, (Lr, D)), (Lr, (Lr, D)), (Lr, (Lr, D)), (2 * Lr, (Lr, 2 * D)), (1, (D,))):
        out.append(pk[o:o + rows].reshape(shape))
        o += rows + (-rows % 8)
    out.append(pk[o, :Lr * nf].reshape(Lr, nf))
    return out, pk[o + 8, 0]


def kernel(x, p, g_mix, w_in, b_f, w_gate, b_gate, w_br_a, w_br_b, w_o, g_mlp, w_up, w_down, g_ple, w_ple, w_ple_gate, g_final, loss_target, m_g_mix, m_w_in, m_b_f, m_w_gate, m_b_gate, m_w_br_a, m_w_br_b, m_w_o, m_g_mlp, m_w_up, m_w_down, m_g_ple, m_w_ple, m_w_ple_gate, m_g_final, v_g_mix, v_w_in, v_b_f, v_w_gate, v_b_gate, v_w_br_a, v_w_br_b, v_w_o, v_g_mlp, v_w_up, v_w_down, v_g_ple, v_w_ple, v_w_ple_gate, v_g_final):
    Lr, D = g_mix.shape
    S = x.shape[1]
    nf = b_f.shape[1]
    big_w = dict(zip(BIG, (w_in, w_gate, w_br_a, w_br_b, w_o, w_up, w_down, w_ple, w_ple_gate)))
    big_m = dict(zip(BIG, (m_w_in, m_w_gate, m_w_br_a, m_w_br_b, m_w_o, m_w_up, m_w_down, m_w_ple, m_w_ple_gate)))
    big_v = dict(zip(BIG, (v_w_in, v_w_gate, v_w_br_a, v_w_br_b, v_w_o, v_w_up, v_w_down, v_w_ple, v_w_ple_gate)))

    x_pos, y_pos, c_pos = _place()
    me = _slot(x_pos, y_pos, c_pos).astype(jnp.int32).reshape(1)
    place = jnp.stack([c_pos, 2 * x_pos + y_pos]).astype(jnp.int32)
    wbf = {k: _cast_bf16(k, big_w[k], me) for k in BIG}
    tabs = _rope_tables(S)
    bf_pad = jnp.pad(b_f, ((0, 0), (0, LANES - nf)))

    smalls = [dict(g_mix=g_mix[l:l + 1], b_f=bf_pad[l:l + 1], b_gate=b_gate[l:l + 1], g_mlp=g_mlp[l:l + 1],
                   g_ple=g_ple[l:l + 1]) for l in range(Lr)]

    gather_groups = (("in", "gate", "bra", "brb", "o"), ("up", "down"), ("ple", "pg"))
    per_layer = len(gather_groups)
    order = [(l, gi) for l in range(Lr) for gi in range(len(gather_groups))]
    on_links = {}

    def links_start(k, after=()):
        l, gi = order[k]
        on_links[k] = _exchange_start(f"gather_ici_start_{l}_{gi}", _plan_gather_ici, 4,
                                      [wbf[n][l] for n in gather_groups[gi]], None, after)
        return on_links[k][4]

    def arrive(k, after):
        l, gi = order[k]
        send_sems, recv_sems, _, lands, _ = on_links.pop(k)
        _, lands = _exchange_wait(f"gather_ici_wait_{l}_{gi}", _plan_gather_ici, send_sems, recv_sems, lands, None,
                                  after)
        send_sems, recv_sems, _, lands, token = _exchange_start(
            f"gather_relay_start_{l}_{gi}", _plan_gather_relay, 3, lands)
        tokens = [token]
        if k + 2 < len(order):
            tokens.append(links_start(k + 2, (token,)))
        _, lands = _exchange_wait(f"gather_relay_wait_{l}_{gi}", _plan_gather_relay, send_sems, recv_sems, lands)
        return {n: _as_matrix(n, g) for n, g in zip(gather_groups[gi], lands)}, tuple(tokens)

    h = x.reshape(S, D)
    Ws, saves = [], []
    queued = links_start(1, (links_start(0),))
    for l in range(Lr):
        W, tokens = arrive(per_layer * l, (h, queued) if l == 0 else (h,))

        def hook(part, l=l, W=W):
            def at(h_now):
                more, hook_tokens = arrive(per_layer * l + part, (h_now,))
                W.update(more)
                return hook_tokens
            return at

        h, sv = _layer_fwd(h, p[l, 0].astype(BF16), smalls[l], W, tabs, tokens, hook(1), hook(2))
        Ws.append(W)
        saves.append(sv)

    dh, dg_final, loss_part = _final(h, g_final.reshape(1, D), loss_target.reshape(S, D))

    big_out = {k: None for k in BIG}
    dsmalls = [None] * Lr
    groups = dict(mlp=("ple", "pg", "down", "up"), merge=("o", "bra", "brb", "gate"), proj=("in",))
    in_flight = []
    links_steps = 3

    def pair_start(l, stage, dW):
        dws = [dW[k].reshape(wbf[k][l].shape) for k in groups[stage]]
        lands = [lax.empty((4,) + d.shape[1:], BF16) for d in dws]
        started = _exchange_start(f"pair_start_{l}_{stage}", _plan_pair, 4, lands, dws)
        in_flight.append(dict(tag=f"{l}_{stage}", layer=l, names=groups[stage], on_links=False, started=started,
                              age=0))
        return started[4]

    def to_links(ex, after):
        send_sems, recv_sems, dws, lands, _ = ex["started"]
        dws, sib = _exchange_wait(f"pair_wait_{ex['tag']}", _plan_pair, send_sems, recv_sems, lands, dws, after)
        sums = [_pair_sum(k, d, s, place) for k, d, s in zip(ex["names"], dws, sib)]
        ex["halves"] = (dws, sib)
        ex["started"] = _exchange_start(f"scatter_start_{ex['tag']}", _plan_scatter_chips, 3,
                                        [lax.empty(s.shape, BF16) for s in sums], sums)
        ex["on_links"], ex["age"] = True, 0
        return ex["started"][4]

    def finish(ex, after):
        send_sems, recv_sems, sums, lands, _ = ex["started"]
        _, landed = _exchange_wait(f"scatter_wait_{ex['tag']}", _plan_scatter_chips, send_sems, recv_sems, lands, sums,
                                   after)
        for k, dw, sib, recv in zip(ex["names"], *ex["halves"], landed):
            carried = None if big_out[k] is None else big_out[k][:4]
            big_out[k] = _adamw(k, ex["layer"], big_w[k], big_m[k], big_v[k], recv, dw, sib, place, carried)
        in_flight.remove(ex)
        return landed[0]

    def advance(after):
        tokens = []
        for ex in list(in_flight):
            if not ex["on_links"]:
                tokens.append(to_links(ex, after))
            elif ex["age"] + 1 >= links_steps:
                tokens.append(finish(ex, after))
            else:
                ex["age"] += 1
        return tokens

    after = ()
    for l in reversed(range(Lr)):
        def ready(stage, dW, l=l):
            tokens = advance((dW[groups[stage][-1]],))
            return tuple(tokens + [pair_start(l, stage, dW)])

        dh, dW, dsmalls[l] = _layer_bwd(dh, saves[l], smalls[l], Ws[l], tabs, after, ready)
        after = tuple(advance((dh,)) + [pair_start(l, "proj", dW)])
    last = after[-1]
    for ex in [ex for ex in in_flight if not ex["on_links"]]:
        last = to_links(ex, (last,))
    for ex in list(in_flight):
        updated = tuple(out[4] for out in big_out.values() if out is not None)
        last = finish(ex, updated + (last,))

    cat = lambda key: jnp.concatenate([_pad8(d[key]) for d in dsmalls], axis=0)[::8]
    dbf = jnp.concatenate([d["b_f"] for d in dsmalls], axis=0).reshape(Lr, 8, LANES)[:, :nf, 0]
    loss_row = jnp.concatenate([loss_part, g_final.reshape(1, D)[:, LANES:]], axis=1)
    g_pack = _allreduce_small(_pack_small(cat("g_mix"), cat("g_mlp"), cat("g_ple"), cat("b_gate"), dg_final, dbf,
                                          loss_row), (last,))
    w_pack = _pack_small(g_mix, g_mlp, g_ple, b_gate, g_final, b_f)
    m_pack = _pack_small(m_g_mix, m_g_mlp, m_g_ple, m_b_gate, m_g_final, m_b_f)
    v_pack = _pack_small(v_g_mix, v_g_mlp, v_g_ple, v_b_gate, v_g_final, v_b_f)
    d_pack, mn_pack, vn_pack = _adamw_small(w_pack, g_pack, m_pack, v_pack)

    small_names = ("g_mix", "g_mlp", "g_ple", "b_gate", "g_final", "b_f")
    sg, loss = _unpack_small(g_pack, Lr, D, nf)
    small_out = {}
    for kind, pk in (("grad", None), ("delta", d_pack), ("m", mn_pack), ("v", vn_pack)):
        vals = sg if pk is None else _unpack_small(pk, Lr, D, nf)[0]
        small_out[kind] = dict(zip(small_names, vals))

    order = ("g_mix", "in", "b_f", "gate", "b_gate", "bra", "brb", "o", "g_mlp", "up", "down", "g_ple", "ple", "pg",
             "g_final")
    outs = [loss, dh.reshape(x.shape)]
    for idx, kind in enumerate(("grad", "delta", "m", "v")):
        for name in order:
            outs.append(big_out[name][idx] if name in big_out else small_out[kind][name])
    return tuple(outs)
```

```python
import jax
import jax.numpy as jnp
from jax import lax
from jax.experimental import pallas as pl
from jax.experimental.pallas import tpu as pltpu

F32, BF16 = jnp.float32, jnp.bfloat16
SDS = jax.ShapeDtypeStruct
BS = pl.BlockSpec
MESH = pl.DeviceIdType.MESH
HI = lax.Precision.HIGHEST

HEAD_DIM = 128
N_HEADS = 16
ATTN = N_HEADS * HEAD_DIM
DIL_GROUPS = ((128, 1), (512, 4), (2048, 16))
HEADS_PER_GROUP = 4
FOX_HEAD0 = 12
N_FOX = 4
BRANCH = HEADS_PER_GROUP * HEAD_DIM
BLOCK = 128
ROPE_DIM = HEAD_DIM // 4
ROPE_HALF = ROPE_DIM // 2
ROPE_THETA = 500000.0
NORM_EPS = 1e-6
SCALE = HEAD_DIM ** -0.5
NDEV = 8
LANES = 128

ADAM_LR, ADAM_B1, ADAM_B2, ADAM_EPS, ADAM_WD, ADAM_STEP = 0.001, 0.9, 0.999, 1e-08, 0.01, 10

V7X_VMEM_BYTES = 64 * 1024 * 1024
VMEM_LIMIT = (V7X_VMEM_BYTES * 3) // 4

NT_CONTRACT = 4096
NN = (((1,), (0,)), ((), ()))
NT = (((1,), (1,)), ((), ()))
TN = (((0,), (0,)), ((), ()))

BIG = ("in", "gate", "bra", "brb", "o", "up", "down", "ple", "pg")
ROW_SHARDED = ("in", "o", "down", "pg")


def _tile(n, pref):
    t = min(n, pref)
    while n % t:
        t -= LANES
    return t


def _cparams(sem):
    return pltpu.CompilerParams(dimension_semantics=sem, vmem_limit_bytes=VMEM_LIMIT)


def _sigmoid(x):
    return 1.0 / (1.0 + jnp.exp(-x))


def _mm(name, grid, a, a_spec, b, b_spec, dims, outs, acc_shape, extras=(), epilogue=None, carry=None, blocks=1):
    nk = grid[2]
    n_ex, n_out = len(extras), len(outs)
    n_carry = 0 if carry is None else 1

    def body(*refs):
        a_ref, b_ref = refs[0], refs[1]
        ex_refs = refs[2:2 + n_ex]
        out_refs = refs[2 + n_ex + n_carry:2 + n_ex + n_carry + n_out]

        def product():
            if blocks == 1:
                return lax.dot_general(a_ref[...], b_ref[...], dims, preferred_element_type=F32)
            n = b_ref.shape[2]
            total = None
            for jb in range(blocks):
                part = lax.dot_general(a_ref[:, jb * n:(jb + 1) * n], b_ref[jb], dims, preferred_element_type=F32)
                total = part if total is None else total + part
            return total

        def finish(acc):
            res = (acc,) if epilogue is None else epilogue(acc, *[e[...] for e in ex_refs])
            for o_ref, r in zip(out_refs, res):
                o_ref[...] = r.astype(o_ref.dtype)

        if nk == 1:
            finish(product())
            return
        acc_ref = refs[-1]
        k = pl.program_id(2)

        @pl.when(k == 0)
        def _():
            acc_ref[...] = jnp.zeros_like(acc_ref)

        acc_ref[...] += product()

        @pl.when(k == nk - 1)
        def _():
            finish(acc_ref[...])

    arrays = [a, b] + [e[0] for e in extras]
    specs = [a_spec, b_spec] + [e[1] for e in extras]
    alias = {}
    if carry is not None:
        arrays.append(carry[0])
        specs.append(BS(memory_space=pl.ANY))
        alias = {len(arrays) - 1: carry[1]}
    return pl.pallas_call(
        body, out_shape=[o[0] for o in outs], grid=grid, in_specs=specs, out_specs=[o[1] for o in outs],
        scratch_shapes=[pltpu.VMEM(acc_shape, F32)] if nk > 1 else [], input_output_aliases=alias, name=name,
        compiler_params=_cparams(("parallel", "parallel", "arbitrary")))(*arrays)


def _mm_nn(name, a, w, out_dtypes, *, tm, joff=0, ncols=None, tn=None, tk=None, extras=(), epilogue=None,
           out_specs=None, out_shapes=None):
    M, K = a.shape
    J, _, n = w.shape
    tn = tn or _tile(n, 512)
    tk = tk or _tile(K, 2048)
    per = n // tn
    ncols = ncols or J * n
    grid = (M // tm, ncols // tn, K // tk)
    a_spec = BS((tm, tk), lambda i, j, k: (i, k))
    w_spec = BS((None, tk, tn), lambda i, j, k: ((j + joff) // per, k, (j + joff) % per))
    if out_specs is None:
        out_specs = [BS((tm, tn), lambda i, j, k: (i, j))] * len(out_dtypes)
        out_shapes = [(M, ncols)] * len(out_dtypes)
    outs = [(SDS(s, d), sp) for s, d, sp in zip(out_shapes, out_dtypes, out_specs)]
    return _mm(name, grid, a, a_spec, w, w_spec, NN, outs, (tm, tn), extras, epilogue)


def _mm_nt(name, dy, w, out_dtype, *, tm, coff=0, ncontract=None, tc=None, dy_spec=None, extras=(), epilogue=None):
    J, K, n = w.shape
    M = dy.shape[-2]
    tko = _tile(K, 1024 if J == 1 else 512)
    outs = [(SDS((M, K), out_dtype), BS((tm, tko), lambda i, ko, cc: (i, ko)))]
    if J > 1 and tc is None and dy_spec is None:
        blocks = max(1, min(J, NT_CONTRACT // n))
        grid = (M // tm, K // tko, J // blocks)
        dy_spec = BS((tm, blocks * n), lambda i, ko, cc: (i, cc))
        w_spec = BS((blocks, tko, n), lambda i, ko, cc: (cc, ko, 0))
        return _mm(name, grid, dy, dy_spec, w, w_spec, NT, outs, (tm, tko), extras, epilogue, blocks=blocks)[0]
    tc = tc or _tile(n, 2048)
    per = n // tc
    ncontract = ncontract or J * n
    grid = (M // tm, K // tko, ncontract // tc)
    if dy_spec is None:
        dy_spec = BS((tm, tc), lambda i, ko, cc: (i, cc))
    w_spec = BS((None, tko, tc), lambda i, ko, cc: ((cc + coff) // per, ko, (cc + coff) % per))
    return _mm(name, grid, dy, dy_spec, w, w_spec, NT, outs, (tm, tko), extras, epilogue)[0]


def _mm_tn(name, x, dy, wshape, *, ts, joff=0, ncols=None, tn=None, dy_spec=None, carry=None):
    J, K, n = wshape
    S = x.shape[0]
    tn = tn or _tile(n, 1024)
    tkm = _tile(K, 1024)
    per = n // tn
    ncols = ncols or J * n
    grid = (K // tkm, ncols // tn, S // ts)
    x_spec = BS((ts, tkm), lambda ki, j, sc: (sc, ki))
    if dy_spec is None:
        dy_spec = BS((ts, tn), lambda ki, j, sc: (sc, j))
    outs = [(SDS(wshape, BF16), BS((None, tkm, tn), lambda ki, j, sc: ((j + joff) // per, ki, (j + joff) % per)))]
    return _mm(name, grid, x, x_spec, dy, dy_spec, TN, outs, (tkm, tn), carry=None if carry is None else (carry, 0))[0]


def _rows(name, body, ins, outs, S, tr, sequential=False):
    def spec(shape, kind):
        if kind == "row":
            return BS((tr, shape[1]), lambda i: (i, 0))
        if kind == "order":
            return BS(memory_space=pl.ANY)
        return BS(tuple(shape), lambda i: (0,) * len(shape))
    return pl.pallas_call(
        body, out_shape=[o[0] for o in outs], grid=(S // tr,),
        in_specs=[spec(a.shape, k) for a, k in ins], out_specs=[spec(o[0].shape, o[1]) for o in outs],
        name=name, compiler_params=_cparams(("arbitrary" if sequential else "parallel",)))(*[a for a, _ in ins])


def _rms_fwd(h, g, tr=256, after=()):
    S, D = h.shape
    n_after = len(after)

    def body(h_ref, g_ref, *rest):
        u_ref = rest[n_after]
        x = h_ref[...]
        r = lax.rsqrt(jnp.mean(x * x, axis=-1, keepdims=True) + NORM_EPS)
        u_ref[...] = ((x * r) * g_ref[...]).astype(BF16)

    ins = [(h, "row"), (g, "full")] + [(t, "order") for t in after]
    return _rows("rms_fwd", body, ins, [(SDS((S, D), BF16), "row")], S, tr)[0]


def _rms_bwd(h, g, dy, dres, tr=256, after=()):
    S, D = h.shape
    n_after = len(after)

    def body(h_ref, g_ref, dy_ref, dres_ref, *rest):
        dh_ref, dhb_ref, dg_ref = rest[n_after:]
        x = h_ref[...]
        r = lax.rsqrt(jnp.mean(x * x, axis=-1, keepdims=True) + NORM_EPS)
        xhat = x * r
        dyv = dy_ref[...]
        gy = dyv * g_ref[...]
        dx = r * (gy - xhat * jnp.mean(gy * xhat, axis=-1, keepdims=True))
        dh = dres_ref[...] + dx
        dh_ref[...] = dh
        dhb_ref[...] = dh.astype(BF16)
        part = jnp.sum(dyv * xhat, axis=0, keepdims=True)

        @pl.when(pl.program_id(0) == 0)
        def _():
            dg_ref[...] = part

        @pl.when(pl.program_id(0) > 0)
        def _():
            dg_ref[...] += part

    ins = [(h, "row"), (g, "full"), (dy, "row"), (dres, "row")] + [(t, "order") for t in after]
    return _rows("rms_bwd", body, ins,
                 [(SDS((S, D), F32), "row"), (SDS((S, D), BF16), "row"), (SDS((1, D), F32), "full")], S, tr, True)


def _final(h, g, target, tr=256):
    S, D = h.shape

    def body(h_ref, g_ref, t_ref, dh_ref, dg_ref, loss_ref):
        x = h_ref[...]
        r = lax.rsqrt(jnp.mean(x * x, axis=-1, keepdims=True) + NORM_EPS)
        xhat = x * r
        gv = g_ref[...]
        err = xhat * gv - t_ref[...]
        lpart = 0.5 * jnp.sum(jnp.mean(err * err, axis=-1, keepdims=True), axis=0, keepdims=True)
        dyv = err * (1.0 / D)
        gy = dyv * gv
        dh_ref[...] = r * (gy - xhat * jnp.mean(gy * xhat, axis=-1, keepdims=True))
        part = jnp.sum(dyv * xhat, axis=0, keepdims=True)
        lrow = jnp.broadcast_to(lpart, (1, LANES))

        @pl.when(pl.program_id(0) == 0)
        def _():
            dg_ref[...] = part
            loss_ref[...] = lrow

        @pl.when(pl.program_id(0) > 0)
        def _():
            dg_ref[...] += part
            loss_ref[...] += lrow

    return _rows("final_head", body, [(h, "row"), (g, "full"), (target, "row")],
                 [(SDS((S, D), F32), "row"), (SDS((1, D), F32), "full"), (SDS((1, LANES), F32), "full")], S, tr, True)


def _ple_bwd(dh, pg, pp, tr=256, after=()):
    S, D = dh.shape
    n_after = len(after)

    def body(dh_ref, pg_ref, pp_ref, *rest):
        dpp_ref, dpgz_ref = rest[n_after:]
        d, g, q = dh_ref[...], pg_ref[...], pp_ref[...]
        dpp_ref[...] = (d * g).astype(BF16)
        dpgz_ref[...] = (d * q * g * (1.0 - g)).astype(BF16)

    ins = [(dh, "row"), (pg, "row"), (pp, "row")] + [(t, "order") for t in after]
    return _rows("ple_bwd", body, ins, [(SDS((S, D), BF16), "row"), (SDS((S, D), BF16), "row")], S, tr)


def _merge_bwd(dmerged, ya, yb, gates, tr=256):
    S, D = dmerged.shape

    def body(dm_ref, ya_ref, yb_ref, g_ref, dya_ref, dyb_ref, dgz_ref, dbg_ref):
        dm, a, b = dm_ref[...], ya_ref[...], yb_ref[...]
        g1, g2 = g_ref[:, :D], g_ref[:, D:]
        dya_ref[...] = (dm * g1).astype(BF16)
        dyb_ref[...] = (dm * g2).astype(BF16)
        dz1 = dm * a * g1 * (1.0 - g1)
        dz2 = dm * b * g2 * (1.0 - g2)
        dgz_ref[:, :D] = dz1.astype(BF16)
        dgz_ref[:, D:] = dz2.astype(BF16)
        p1 = jnp.sum(dz1, axis=0, keepdims=True)
        p2 = jnp.sum(dz2, axis=0, keepdims=True)

        @pl.when(pl.program_id(0) == 0)
        def _():
            dbg_ref[:, :D] = p1
            dbg_ref[:, D:] = p2

        @pl.when(pl.program_id(0) > 0)
        def _():
            dbg_ref[:, :D] += p1
            dbg_ref[:, D:] += p2

    return _rows("merge_bwd", body, [(dmerged, "row"), (ya, "row"), (yb, "row"), (gates, "row")],
                 [(SDS((S, D), BF16), "row"), (SDS((S, D), BF16), "row"), (SDS((S, 2 * D), BF16), "row"),
                  (SDS((1, 2 * D), F32), "full")], S, tr, True)


def _mix_weights(l0, l1, l2):
    mx = jnp.maximum(jnp.maximum(l0, l1), l2)
    e0, e1, e2 = jnp.exp(l0 - mx), jnp.exp(l1 - mx), jnp.exp(l2 - mx)
    inv = 1.0 / (e0 + e1 + e2)
    return e0 * inv, e1 * inv, e2 * inv


def _mix_fwd(os_, ls_, tr=512):
    S = os_[0].shape[0]

    def body(o0, o1, o2, l0, l1, l2, y_ref):
        w0, w1, w2 = _mix_weights(l0[...], l1[...], l2[...])
        y_ref[...] = (w0 * o0[...] + w1 * o1[...] + w2 * o2[...]).astype(BF16)

    return _rows("mix_fwd", body, [(a, "row") for a in (*os_, *ls_)], [(SDS((S, BRANCH), BF16), "row")], S, tr)[0]


def _mix_bwd(dy, os_, ls_, tr=512, after=()):
    S = dy.shape[0]
    n_after = len(after)

    def head_sums(t):
        return jnp.concatenate(
            [jnp.broadcast_to(jnp.sum(t[:, j * HEAD_DIM:(j + 1) * HEAD_DIM], axis=-1, keepdims=True), (tr, HEAD_DIM))
             for j in range(HEADS_PER_GROUP)], axis=1)

    def body(dy_ref, o0, o1, o2, l0, l1, l2, *rest):
        d0, d1, d2, p0, p1, p2 = rest[n_after:]
        ws = _mix_weights(l0[...], l1[...], l2[...])
        d = dy_ref[...]
        es = [head_sums(d * o[...]) for o in (o0, o1, o2)]
        ebar = ws[0] * es[0] + ws[1] * es[1] + ws[2] * es[2]
        for w, e, d_ref, p_ref in zip(ws, es, (d0, d1, d2), (p0, p1, p2)):
            d_ref[...] = (w * d).astype(BF16)
            p_ref[...] = w * (e - ebar)

    outs = [(SDS((S, BRANCH), BF16), "row")] * 3 + [(SDS((S, BRANCH), F32), "row")] * 3
    ins = [(a, "row") for a in (dy, *os_, *ls_)] + [(t, "order") for t in after]
    return _rows("mix_bwd", body, ins, outs, S, tr)


def _iota2(shape, dim):
    return lax.broadcasted_iota(jnp.int32, shape, dim)


def _fgate_fwd(zf, bf):
    S = zf.shape[0]
    nblk = S // BLOCK

    def body(zf_ref, bf_ref, cb_ref, ct_ref, flt_ref):
        tri = (_iota2((BLOCK, BLOCK), 0) >= _iota2((BLOCK, BLOCK), 1)).astype(F32)
        sel8 = (_iota2((8, LANES), 0) == _iota2((8, LANES), 1)).astype(F32)
        carry = jnp.zeros((1, LANES), F32)
        for blk in range(nblk):
            rows = pl.ds(blk * BLOCK, BLOCK)
            fl = zf_ref[rows, :] + bf_ref[...]
            ls = jnp.minimum(fl, 0.0) - jnp.log(1.0 + jnp.exp(-jnp.abs(fl)))
            c = jnp.dot(tri, ls, precision=HI, preferred_element_type=F32) + carry
            carry = carry + jnp.sum(ls, axis=0, keepdims=True)
            for hd in range(N_FOX):
                e_h = (_iota2((LANES, LANES), 0) == hd).astype(F32)
                cb_ref[rows, hd * LANES:(hd + 1) * LANES] = jnp.dot(c, e_h, precision=HI, preferred_element_type=F32)
            ct_ref[:, rows] = lax.dot_general(sel8, c, NT, precision=HI, preferred_element_type=F32)
            flt_ref[:, rows] = lax.dot_general(sel8, fl, NT, precision=HI, preferred_element_type=F32)

    vm = BS(memory_space=pltpu.VMEM)
    return pl.pallas_call(
        body, out_shape=[SDS((S, N_FOX * LANES), F32), SDS((8, S), F32), SDS((8, S), F32)],
        in_specs=[vm, vm], out_specs=[vm, vm, vm], name="fgate_fwd",
        compiler_params=pltpu.CompilerParams(vmem_limit_bytes=VMEM_LIMIT))(zf, bf)


def _fgate_bwd(dct, flt):
    S = dct.shape[1]
    nblk = S // BLOCK

    def body(dct_ref, flt_ref, dzf_ref, dbf_ref):
        rowid = _iota2((8, BLOCK), 0)
        tri_r = (_iota2((BLOCK, BLOCK), 0) >= _iota2((BLOCK, BLOCK), 1)).astype(F32)
        carry = jnp.zeros((8, 1), F32)
        dbf = jnp.zeros((8, 1), F32)
        for blk in reversed(range(nblk)):
            cols = pl.ds(blk * BLOCK, BLOCK)
            dc = jnp.where(rowid < N_FOX, dct_ref[:, cols], 0.0)
            dls = jnp.dot(dc, tri_r, precision=HI, preferred_element_type=F32) + carry
            carry = carry + jnp.sum(dc, axis=1, keepdims=True)
            dfl = jnp.where(rowid < N_FOX, dls * _sigmoid(-flt_ref[:, cols]), 0.0)
            dbf = dbf + jnp.sum(dfl, axis=1, keepdims=True)
            sq = jnp.concatenate([dfl, jnp.zeros((BLOCK - 8, BLOCK), F32)], axis=0)
            dzf_ref[cols, :] = sq.T.astype(BF16)
        dbf_ref[...] = jnp.broadcast_to(dbf, (8, LANES))

    vm = BS(memory_space=pltpu.VMEM)
    return pl.pallas_call(
        body, out_shape=[SDS((S, LANES), BF16), SDS((8, LANES), F32)],
        in_specs=[vm, vm], out_specs=[vm, vm], name="fgate_bwd",
        compiler_params=pltpu.CompilerParams(vmem_limit_bytes=VMEM_LIMIT))(dct, flt)


def _rope_tables(S):
    inv = ROPE_THETA ** (-jnp.arange(ROPE_HALF, dtype=F32) / ROPE_HALF)
    ang = jnp.arange(S, dtype=F32)[:, None] * inv[None, :]
    cos, sin = jnp.cos(ang), jnp.sin(ang)
    pad = HEAD_DIM - ROPE_DIM
    tc = jnp.concatenate([cos, cos, jnp.ones((S, pad), F32)], axis=1)
    ta = jnp.concatenate([-sin, jnp.zeros((S, HEAD_DIM - ROPE_HALF), F32)], axis=1)
    tb = jnp.concatenate([jnp.zeros((S, ROPE_HALF), F32), sin, jnp.zeros((S, pad), F32)], axis=1)
    return tc, ta, tb


def _rope(x, c, a, b):
    return x * c + pltpu.roll(x, HEAD_DIM - ROPE_HALF, 1) * a + pltpu.roll(x, ROPE_HALF, 1) * b


def _rope_t(g, c, a, b):
    return g * c - pltpu.roll(g, ROPE_HALF, 1) * b - pltpu.roll(g, HEAD_DIM - ROPE_HALF, 1) * a


DIL_TQ = 256


def _dil_extent(i, window):
    return max(0, i * DIL_TQ - window), (i + 1) * DIL_TQ


def _dil_mask(i, lo, ext, window, d):
    diff = (i * DIL_TQ - lo) + _iota2((DIL_TQ, ext), 0) - _iota2((DIL_TQ, ext), 1)
    mask = (diff >= 0) & (diff <= window)
    return mask & ((diff & (d - 1)) == 0) if d > 1 else mask


def _wide(t, w):
    return t if w == LANES else jnp.concatenate([t] * (w // LANES), axis=1)


def _dil_fwd(z3, tabs, grp, S):
    window, d = DIL_GROUPS[grp]
    nq = S // DIL_TQ

    def body(qkv_ref, c_ref, a_ref, b_ref, o_ref, lse_ref, qs, ks):
        c, a, b = c_ref[...], a_ref[...], b_ref[...]
        qs[...] = _rope(qkv_ref[0].astype(F32), c, a, b).astype(BF16)
        ks[...] = _rope(qkv_ref[1].astype(F32), c, a, b).astype(BF16)
        for i in range(nq):
            lo, hi = _dil_extent(i, window)
            rows = pl.ds(i * DIL_TQ, DIL_TQ)
            s = lax.dot_general(qs[rows, :], ks[lo:hi, :], NT, preferred_element_type=F32) * SCALE
            s = jnp.where(_dil_mask(i, lo, hi - lo, window, d), s, -jnp.inf)
            m = jnp.max(s, axis=-1, keepdims=True)
            p = jnp.exp(s - m)
            l = jnp.sum(p, axis=-1, keepdims=True)
            o = jnp.dot(p.astype(BF16), qkv_ref[2, lo:hi, :], preferred_element_type=F32)
            o_ref[rows, :] = o / l
            lse_ref[rows, :] = jnp.broadcast_to(m + jnp.log(l), (DIL_TQ, HEAD_DIM))

    hh0 = grp * HEADS_PER_GROUP
    tspec = BS((S, HEAD_DIM), lambda j: (0, 0))
    ospec = BS((S, HEAD_DIM), lambda j: (0, j))
    return pl.pallas_call(
        body, out_shape=[SDS((S, BRANCH), F32)] * 2, grid=(HEADS_PER_GROUP,),
        in_specs=[BS((3, S, HEAD_DIM), lambda j: (0, 0, hh0 + j)), tspec, tspec, tspec],
        out_specs=[ospec, ospec], scratch_shapes=[pltpu.VMEM((S, HEAD_DIM), BF16)] * 2,
        name="dil_fwd", compiler_params=_cparams(("parallel",)))(z3, *tabs)


def _dil_bwd(z3, tabs, grp, S, do, lse, dpl, dz3):
    window, d = DIL_GROUPS[grp]
    nq = S // DIL_TQ
    n_carry = 0 if dz3 is None else 1

    def body(*refs):
        qkv_ref, c_ref, a_ref, b_ref, do_ref, lse_ref, dpl_ref = refs[:7]
        dz_ref = refs[7 + n_carry]
        qs, ks, dq_acc, dk_acc, dv_acc = refs[8 + n_carry:]
        c, a, b = c_ref[...], a_ref[...], b_ref[...]
        qs[...] = _rope(qkv_ref[0].astype(F32), c, a, b).astype(BF16)
        ks[...] = _rope(qkv_ref[1].astype(F32), c, a, b).astype(BF16)
        dk_acc[...] = jnp.zeros_like(dk_acc)
        dv_acc[...] = jnp.zeros_like(dv_acc)
        for i in range(nq):
            lo, hi = _dil_extent(i, window)
            w = hi - lo
            rows = pl.ds(i * DIL_TQ, DIL_TQ)
            q, k, v = qs[rows, :], ks[lo:hi, :], qkv_ref[2, lo:hi, :]
            s = lax.dot_general(q, k, NT, preferred_element_type=F32) * SCALE
            p = jnp.where(_dil_mask(i, lo, w, window, d), jnp.exp(s - _wide(lse_ref[rows, :], w)), 0.0)
            dob = do_ref[rows, :]
            dp = lax.dot_general(dob, v, NT, preferred_element_type=F32)
            delta = jnp.sum(p * dp, axis=-1, keepdims=True)
            ds = (p * (dp - delta + _wide(dpl_ref[rows, :], w))).astype(BF16)
            dq_acc[rows, :] = jnp.dot(ds, k, preferred_element_type=F32) * SCALE
            dk_acc[lo:hi, :] += lax.dot_general(ds, q, TN, preferred_element_type=F32) * SCALE
            dv_acc[lo:hi, :] += lax.dot_general(p.astype(BF16), dob, TN, preferred_element_type=F32)
        dz_ref[0] = _rope_t(dq_acc[...], c, a, b).astype(BF16)
        dz_ref[1] = _rope_t(dk_acc[...], c, a, b).astype(BF16)
        dz_ref[2] = dv_acc[...].astype(BF16)

    hh0 = grp * HEADS_PER_GROUP
    tspec = BS((S, HEAD_DIM), lambda j: (0, 0))
    bspec = BS((S, HEAD_DIM), lambda j: (0, j))
    zspec = BS((3, S, HEAD_DIM), lambda j: (0, 0, hh0 + j))
    arrays = [z3, *tabs, do, lse, dpl]
    specs = [zspec, tspec, tspec, tspec, bspec, bspec, bspec]
    alias = {}
    if dz3 is not None:
        arrays.append(dz3)
        specs.append(BS(memory_space=pl.ANY))
        alias = {len(arrays) - 1: 0}
    return pl.pallas_call(
        body, out_shape=SDS((3, S, ATTN), BF16), grid=(HEADS_PER_GROUP,), in_specs=specs, out_specs=zspec,
        scratch_shapes=[pltpu.VMEM((S, HEAD_DIM), BF16)] * 2 + [pltpu.VMEM((S, HEAD_DIM), F32)] * 3,
        input_output_aliases=alias, name="dil_bwd", compiler_params=_cparams(("parallel",)))(*arrays)


FOX_TQ = 256


def _fox_scores(q_ref, k_ref, cb_ref, ct_ref, i, ext):
    rows = pl.ds(i * FOX_TQ, FOX_TQ)
    s = lax.dot_general(q_ref[0, rows, :], k_ref[0, 0:ext, :], NT, preferred_element_type=F32) * SCALE
    s = s + _wide(cb_ref[rows, :], ext) - ct_ref[:, 0:ext]
    qpos = i * FOX_TQ + _iota2((FOX_TQ, ext), 0)
    kpos = _iota2((FOX_TQ, ext), 1)
    return jnp.where(kpos <= qpos, s, -jnp.inf)


def _fox_specs(S):
    qs = BS((1, S, HEAD_DIM), lambda h: (0, 0, FOX_HEAD0 + h))
    ks = BS((1, S, HEAD_DIM), lambda h: (1, 0, FOX_HEAD0 + h))
    vs = BS((1, S, HEAD_DIM), lambda h: (2, 0, FOX_HEAD0 + h))
    hb = BS((S, HEAD_DIM), lambda h: (0, h))
    ct = BS((None, 1, S), lambda h: (h, 0, 0))
    return qs, ks, vs, hb, ct


def _fox_fwd(z3, cb, ct, S):
    nq = S // FOX_TQ

    def body(q_ref, k_ref, v_ref, cb_ref, ct_ref, y_ref, lse_ref):
        for i in range(nq):
            ext = (i + 1) * FOX_TQ
            rows = pl.ds(i * FOX_TQ, FOX_TQ)
            s = _fox_scores(q_ref, k_ref, cb_ref, ct_ref, i, ext)
            m = jnp.max(s, axis=-1, keepdims=True)
            p = jnp.exp(s - m)
            l = jnp.sum(p, axis=-1, keepdims=True)
            o = jnp.dot(p.astype(BF16), v_ref[0, 0:ext, :], preferred_element_type=F32)
            y_ref[rows, :] = (o / l).astype(BF16)
            lse_ref[rows, :] = jnp.broadcast_to(m + jnp.log(l), (FOX_TQ, HEAD_DIM))

    qs, ks, vs, hb, cts = _fox_specs(S)
    return pl.pallas_call(
        body, out_shape=[SDS((S, BRANCH), BF16), SDS((S, BRANCH), F32)], grid=(N_FOX,),
        in_specs=[qs, ks, vs, hb, cts], out_specs=[hb, hb], name="fox_fwd",
        compiler_params=_cparams(("parallel",)))(z3, z3, z3, cb, ct.reshape(8, 1, S))


def _fox_bwd(z3, cb, ct, S, dy, lse, dz3):
    nq = S // FOX_TQ

    def body(q_ref, k_ref, v_ref, cb_ref, ct_ref, dy_ref, lse_ref, _, dz_ref, dct_ref, dk_acc, dv_acc):
        dk_acc[...] = jnp.zeros_like(dk_acc)
        dv_acc[...] = jnp.zeros_like(dv_acc)
        dct_ref[...] = jnp.zeros_like(dct_ref)
        for i in range(nq):
            ext = (i + 1) * FOX_TQ
            rows = pl.ds(i * FOX_TQ, FOX_TQ)
            s = _fox_scores(q_ref, k_ref, cb_ref, ct_ref, i, ext)
            p = jnp.exp(s - _wide(lse_ref[rows, :], ext))
            dob = dy_ref[rows, :]
            dp = lax.dot_general(dob, v_ref[0, 0:ext, :], NT, preferred_element_type=F32)
            ds = p * (dp - jnp.sum(p * dp, axis=-1, keepdims=True))
            dsb = ds.astype(BF16)
            dz_ref[0, rows, :] = (jnp.dot(dsb, k_ref[0, 0:ext, :], preferred_element_type=F32) * SCALE).astype(BF16)
            dk_acc[0:ext, :] += lax.dot_general(dsb, q_ref[0, rows, :], TN, preferred_element_type=F32) * SCALE
            dv_acc[0:ext, :] += lax.dot_general(p.astype(BF16), dob, TN, preferred_element_type=F32)
            dct_ref[:, 0:ext] -= jnp.sum(ds, axis=0, keepdims=True)
        dz_ref[1] = dk_acc[...].astype(BF16)
        dz_ref[2] = dv_acc[...].astype(BF16)

    qs, ks, vs, hb, cts = _fox_specs(S)
    zspec = BS((3, S, HEAD_DIM), lambda h: (0, 0, FOX_HEAD0 + h))
    dz, dct = pl.pallas_call(
        body, out_shape=[SDS((3, S, ATTN), BF16), SDS((8, 1, S), F32)], grid=(N_FOX,),
        in_specs=[qs, ks, vs, hb, cts, hb, hb, BS(memory_space=pl.ANY)], out_specs=[zspec, cts],
        scratch_shapes=[pltpu.VMEM((S, HEAD_DIM), F32)] * 2, input_output_aliases={7: 0}, name="fox_bwd",
        compiler_params=_cparams(("parallel",)))(z3, z3, z3, cb, ct.reshape(8, 1, S), dy, lse, dz3)
    return dz, dct.reshape(8, S)


TM = 1024


def _layer_fwd(h, p_bf, sm, W, tabs, after=(), mid=None, late=None):
    S, D = h.shape
    qkv_cols = 3 * ATTN
    u = _rms_fwd(h, sm["g_mix"], after=after)
    tn = 1024
    per = ATTN // tn
    z3 = _mm_nn("mm_qkv", u, W["in"], [BF16], tm=TM, ncols=qkv_cols, tn=tn, out_shapes=[(3, S, ATTN)],
                out_specs=[BS((None, TM, tn), lambda i, j, k: (j // per, i, j % per))])[0]
    zf = _mm_nn("mm_f", u, W["in"], [F32], tm=TM, joff=qkv_cols // LANES, ncols=LANES, tn=LANES)[0]
    cb, ct, flt = _fgate_fwd(zf, sm["b_f"])
    os_, ls_ = [], []
    for grp in range(len(DIL_GROUPS)):
        o, l = _dil_fwd(z3, tabs, grp, S)
        os_.append(o)
        ls_.append(l)
    ya_h = _mix_fwd(os_, ls_)
    yb_h, lse_fox = _fox_fwd(z3, cb, ct, S)

    tg = _tile(W["gate"].shape[2], 512)
    gates = _mm_nn("mm_gate", u, W["gate"], [F32], tm=TM,
                   extras=[(sm["b_gate"], BS((1, tg), lambda i, j, k: (0, j)))],
                   epilogue=lambda acc, bias: (_sigmoid(acc + bias),))[0]
    ya = _mm_nn("mm_bra", ya_h, W["bra"], [F32], tm=TM)[0]
    tb = _tile(W["brb"].shape[2], 512)
    g2off = D // tb
    yb, merged = _mm_nn(
        "mm_brb", yb_h, W["brb"], [F32, BF16], tm=TM,
        extras=[(ya, BS((TM, tb), lambda i, j, k: (i, j))), (gates, BS((TM, tb), lambda i, j, k: (i, j))),
                (gates, BS((TM, tb), lambda i, j, k: (i, j + g2off)))],
        epilogue=lambda acc, a, g1, g2: (acc, g1 * a + g2 * acc))
    to, wide = _tile(D, 512), _tile(D, 1024)
    res_spec = BS((TM, to), lambda i, j, k: (i, j))
    wide_spec = BS((TM, wide), lambda i, j, k: (i, j))
    h1 = _mm_nn("mm_o", merged, W["o"], [F32], tm=TM, tn=wide, extras=[(h, wide_spec)],
                epilogue=lambda acc, r: (r + acc,))[0]

    m = _rms_fwd(h1, sm["g_mlp"], after=() if mid is None else mid(h1))

    def up_epi(acc):
        r = jnp.maximum(acc, 0.0)
        return r, r * r

    relu_a, act = _mm_nn("mm_up", m, W["up"], [BF16, BF16], tm=TM, tn=_tile(W["up"].shape[2], 1024), epilogue=up_epi)
    h2 = _mm_nn("mm_down", act, W["down"], [F32], tm=TM, tn=wide, extras=[(h1, wide_spec)],
                epilogue=lambda acc, r: (r + acc,))[0]

    nrm = _rms_fwd(h2, sm["g_ple"], after=() if late is None else late(h2))
    pp = _mm_nn("mm_ple", p_bf, W["ple"], [F32], tm=TM)[0]

    def pg_epi(acc, r, q):
        g = _sigmoid(acc)
        return g, r + g * q

    pg, h3 = _mm_nn("mm_pg", nrm, W["pg"], [F32, F32], tm=TM, extras=[(h2, res_spec), (pp, res_spec)], epilogue=pg_epi)
    saved = dict(h=h, u=u, z3=z3, cb=cb, ct=ct, flt=flt, os=os_, ls=ls_, ya_h=ya_h, yb_h=yb_h, lse_fox=lse_fox,
                 gates=gates, ya=ya, yb=yb, merged=merged, h1=h1, m=m, relu_a=relu_a, act=act, h2=h2, nrm=nrm,
                 pg=pg, pp=pp, p_bf=p_bf)
    return h3, saved


def _layer_bwd(dh3, sv, sm, W, tabs, after=(), ready=None):
    S, D = dh3.shape
    dW, dsm = {}, {}

    dpp, dpgz = _ple_bwd(dh3, sv["pg"], sv["pp"], after=after)
    dW["ple"] = _mm_tn("tn_ple", sv["p_bf"], dpp, W["ple"].shape, ts=S)
    dW["pg"] = _mm_tn("tn_pg", sv["nrm"], dpgz, W["pg"].shape, ts=S)
    dn = _mm_nt("nt_pg", dpgz, W["pg"], F32, tm=TM)
    dh2, dh2b, dsm["g_ple"] = _rms_bwd(sv["h2"], sm["g_ple"], dn, dh3)

    ta = _tile(W["up"].shape[1], 512)
    tff = _tile(W["down"].shape[1], 1024)
    da = _mm_nt("nt_down", dh2b, W["down"], BF16, tm=TM,
                extras=[(sv["relu_a"], BS((TM, tff), lambda i, ko, cc: (i, ko)))],
                epilogue=lambda acc, r: (acc * (2.0 * r.astype(F32)),))
    dW["down"] = _mm_tn("tn_down", sv["act"], dh2b, W["down"].shape, ts=S)
    dm = _mm_nt("nt_up", da, W["up"], F32, tm=TM)
    dW["up"] = _mm_tn("tn_up", sv["m"], da, W["up"].shape, ts=S)
    dh1, dh1b, dsm["g_mlp"] = _rms_bwd(sv["h1"], sm["g_mlp"], dm, dh2, after=() if ready is None else ready("mlp", dW))

    dmerged = _mm_nt("nt_o", dh1b, W["o"], F32, tm=TM)
    dW["o"] = _mm_tn("tn_o", sv["merged"], dh1b, W["o"].shape, ts=S)
    dya, dyb, dgz, dsm["b_gate"] = _merge_bwd(dmerged, sv["ya"], sv["yb"], sv["gates"])
    dya_h = _mm_nt("nt_bra", dya, W["bra"], F32, tm=TM)
    dyb_h = _mm_nt("nt_brb", dyb, W["brb"], BF16, tm=TM)
    dW["bra"] = _mm_tn("tn_bra", sv["ya_h"], dya, W["bra"].shape, ts=S)
    dW["brb"] = _mm_tn("tn_brb", sv["yb_h"], dyb, W["brb"].shape, ts=S)
    dW["gate"] = _mm_tn("tn_gate", sv["u"], dgz, W["gate"].shape, ts=S)
    du = _mm_nt("nt_gate", dgz, W["gate"], F32, tm=TM)

    dos_, dpls_ = [], []
    mixed = _mix_bwd(dya_h, sv["os"], sv["ls"], after=() if ready is None else ready("merge", dW))
    dos_, dpls_ = mixed[:3], mixed[3:]
    dz3 = None
    for grp in range(len(DIL_GROUPS)):
        dz3 = _dil_bwd(sv["z3"], tabs, grp, S, dos_[grp], sv["ls"][grp], dpls_[grp], dz3)
    dz3, dct = _fox_bwd(sv["z3"], sv["cb"], sv["ct"], S, dyb_h, sv["lse_fox"], dz3)
    dzf, dsm["b_f"] = _fgate_bwd(dct, sv["flt"])

    tn = 1024
    per = ATTN // tn
    qkv_cols = 3 * ATTN
    dwin = _mm_tn("tn_qkv", sv["u"], dz3, W["in"].shape, ts=S, ncols=qkv_cols, tn=tn,
                  dy_spec=BS((None, S, tn), lambda ki, j, sc: (j // per, sc, j % per)))
    dW["in"] = _mm_tn("tn_f", sv["u"], dzf, W["in"].shape, ts=S, joff=qkv_cols // LANES, ncols=LANES, tn=LANES,
                      carry=dwin)
    tu = _tile(D, 1024)
    prev_spec = BS((TM, tu), lambda i, ko, cc: (i, ko))
    du = _mm_nt("nt_qkv", dz3, W["in"], F32, tm=TM, ncontract=qkv_cols, tc=ATTN,
                dy_spec=BS((None, TM, ATTN), lambda i, ko, cc: (cc, i, 0)),
                extras=[(du, prev_spec)], epilogue=lambda acc, prev: (prev + acc,))
    du = _mm_nt("nt_f", dzf, W["in"], F32, tm=TM, coff=qkv_cols // LANES, ncontract=LANES, tc=LANES,
                extras=[(du, prev_spec)], epilogue=lambda acc, prev: (prev + acc,))
    dh0, _, dsm["g_mix"] = _rms_bwd(sv["h"], sm["g_mix"], du, dh1)
    return dh0, dW, dsm


def _shard_tiles(R, Cp):
    tc = Cp if Cp <= 2048 else _tile(Cp, 1024)
    tr = R
    while tr * tc > 256 * 1024 and tr % 16 == 0:
        tr //= 2
    return tr, tc


def _padded_cols(C):
    return -(-C // LANES) * LANES


def _cast_bf16(name, w, me):
    Lr, R, C = w.shape
    Cp = _padded_cols(C)
    tr, tc = _shard_tiles(R, Cp)

    def body(me_ref, w_ref, *o_refs):
        for l in range(Lr):
            x = w_ref[l]
            if Cp != C:
                col = pl.program_id(1) * tc + _iota2((tr, tc), 1)
                x = jnp.where(col < C, x, 0.0)
            o_refs[l][...] = x.astype(BF16)

    grid_spec = pltpu.PrefetchScalarGridSpec(
        num_scalar_prefetch=1, grid=(R // tr, Cp // tc),
        in_specs=[BS((Lr, tr, tc), lambda i, j, me_ref: (0, i, j))],
        out_specs=[BS((None, tr, tc), lambda i, j, me_ref: (me_ref[0], i, j))] * Lr)
    return pl.pallas_call(
        body, out_shape=[SDS((NDEV, R, Cp), BF16)] * Lr, grid_spec=grid_spec,
        name="cast_" + name, compiler_params=_cparams(("parallel", "parallel")))(me, w)


def _adam_math(w, g, m, v):
    m = ADAM_B1 * m + (1.0 - ADAM_B1) * g
    v = ADAM_B2 * v + (1.0 - ADAM_B2) * (g * g)
    m_hat = m / (1.0 - ADAM_B1 ** ADAM_STEP)
    v_hat = v / (1.0 - ADAM_B2 ** ADAM_STEP)
    delta = -ADAM_LR * (m_hat / (jnp.sqrt(v_hat) + ADAM_EPS) + ADAM_WD * w)
    return delta, m, v


def _pair_sum(name, dw, sib, place):
    _, R, Cp = dw.shape
    tr, tc = _shard_tiles(R, Cp)

    def body(place_ref, a_ref, b_ref, o_ref):
        o_ref[...] = (a_ref[...].astype(F32) + b_ref[...].astype(F32)).astype(BF16)

    def other(r, place_ref):
        return jnp.bitwise_xor(place_ref[1], r + 1)

    grid_spec = pltpu.PrefetchScalarGridSpec(
        num_scalar_prefetch=1, grid=(3, R // tr, Cp // tc),
        in_specs=[BS((None, tr, tc), lambda r, i, j, place_ref: (2 * other(r, place_ref) + place_ref[0], i, j)),
                  BS((None, tr, tc), lambda r, i, j, place_ref: (other(r, place_ref), i, j))],
        out_specs=BS((None, tr, tc), lambda r, i, j, place_ref: (other(r, place_ref), i, j)))
    return pl.pallas_call(
        body, out_shape=SDS((4, R, Cp), BF16), grid_spec=grid_spec, name="pair_sum_" + name,
        compiler_params=_cparams(("parallel", "parallel", "parallel")))(place, dw, sib)


def _adamw(name, layer, w, m, v, recv, dw, sib, place, carried):
    Lr, R, C = w.shape
    nslot, _, Cp = recv.shape
    tr, tc = _shard_tiles(R, Cp)
    n_carry = 0 if carried is None else 4

    def body(place_ref, w_ref, m_ref, v_ref, r_ref, dw_ref, sib_ref, *rest):
        g_out, d_out, m_out, v_out, token = rest[n_carry:]
        mine = dw_ref[...].astype(F32) + sib_ref[...].astype(F32)
        g = None
        for s in range(nslot):
            part = jnp.where(place_ref[1] == s, mine, r_ref[s].astype(F32))
            g = part if g is None else g + part
        delta, mn, vn = _adam_math(w_ref[...], g, m_ref[...], v_ref[...])
        g_out[...] = g
        d_out[...] = delta
        m_out[...] = mn
        v_out[...] = vn
        token[...] = jnp.zeros_like(token)

    spec = BS((None, tr, tc), lambda i, j, place_ref: (layer, i, j))
    rspec = BS((nslot, tr, tc), lambda i, j, place_ref: (0, i, j))
    dspec = BS((None, tr, tc), lambda i, j, place_ref: (2 * place_ref[1] + place_ref[0], i, j))
    sspec = BS((None, tr, tc), lambda i, j, place_ref: (place_ref[1], i, j))
    arrays, specs, alias = [w, m, v, recv, dw, sib], [spec, spec, spec, rspec, dspec, sspec], {}
    if carried is not None:
        arrays += list(carried)
        specs += [BS(memory_space=pl.ANY)] * 4
        alias = {7 + k: k for k in range(4)}
    grid_spec = pltpu.PrefetchScalarGridSpec(
        num_scalar_prefetch=1, grid=(R // tr, Cp // tc), in_specs=specs,
        out_specs=[spec] * 4 + [BS((8, LANES), lambda i, j, place_ref: (0, 0))])
    return pl.pallas_call(
        body, out_shape=[SDS(w.shape, F32)] * 4 + [SDS((8, LANES), F32)], grid_spec=grid_spec,
        input_output_aliases=alias, name="adamw_" + name,
        compiler_params=_cparams(("arbitrary", "arbitrary")))(place, *arrays)


def _adamw_small(w, g, m, v):
    vm = BS(memory_space=pltpu.VMEM)

    def body(w_ref, g_ref, m_ref, v_ref, d_out, m_out, v_out):
        delta, mn, vn = _adam_math(w_ref[...], g_ref[...], m_ref[...], v_ref[...])
        d_out[...] = delta
        m_out[...] = mn
        v_out[...] = vn

    return pl.pallas_call(body, out_shape=[SDS(w.shape, F32)] * 3, in_specs=[vm] * 4, out_specs=[vm] * 3,
                          name="adamw_small")(w, g, m, v)


def _place():
    return lax.axis_index("x"), lax.axis_index("y"), lax.axis_index("c")


def _slot(px, py, pc):
    return 4 * px + 2 * py + pc


def _other_chips(x, y):
    return [(1 - x, y), (x, 1 - y), (1 - x, 1 - y)]


def _peer(x, y, c, rel):
    return (1 - x if rel & 4 else x, 1 - y if rel & 2 else y, 1 - c if rel & 1 else c)


def _plan_gather_ici(src, land):
    x, y, c = _place()
    own = land.at[_slot(x, y, c)]
    return [(own, own, (x, y, 1 - c))] + [(own, own, (*chip, c)) for chip in _other_chips(x, y)]


def _plan_gather_relay(src, land):
    x, y, c = _place()
    return [(land.at[_slot(*chip, c)], land.at[_slot(*chip, c)], (x, y, 1 - c)) for chip in _other_chips(x, y)]


def _plan_pair(src, land):
    x, y, c = _place()
    return [(src.at[2 * q + (1 - c)], land.at[q], (x, y, 1 - c)) for q in range(4)]


def _plan_scatter_chips(src, land):
    x, y, c = _place()
    return [(src.at[2 * cx + cy], land.at[2 * x + y], (cx, cy, c)) for cx, cy in _other_chips(x, y)]


EFFECT = pltpu.SideEffectType.DATAFLOW_SIDE_EFFECTING
HBM_SPEC = BS(memory_space=pltpu.HBM)
SEM_SPEC = BS(memory_space=pltpu.SEMAPHORE)
ORDER_SPEC = BS(memory_space=pl.ANY)


def _exchange_start(name, plan, ncopy, lands, srcs=None, after=()):
    bufs = ([] if srcs is None else list(srcs)) + list(lands)
    n, nb, ns = len(lands), len(bufs), len(bufs) - len(lands)
    n_after = len(after)

    def body(*refs):
        src_refs = refs[:ns] if ns else [None] * n
        land_refs = refs[ns:nb]
        send_sems, recv_sems = refs[nb + n_after], refs[nb + n_after + 1]
        token = refs[-1]
        for i in range(n):
            for k, (src, dst, to) in enumerate(plan(src_refs[i], land_refs[i])):
                pltpu.make_async_remote_copy(src_ref=src, dst_ref=dst, send_sem=send_sems.at[i * ncopy + k],
                                             recv_sem=recv_sems.at[i * ncopy + k], device_id=to,
                                             device_id_type=MESH).start()
        token[...] = jnp.zeros_like(token)

    out_shape = [pltpu.SemaphoreType.DMA((n * ncopy,)), pltpu.SemaphoreType.DMA((n * ncopy,))]
    out_shape += [pltpu.HBM(b.shape, b.dtype) for b in bufs] + [SDS((8, LANES), F32)]
    out = pl.pallas_call(
        body, name=name, out_shape=out_shape, in_specs=[HBM_SPEC] * nb + [ORDER_SPEC] * n_after,
        out_specs=[SEM_SPEC, SEM_SPEC] + [HBM_SPEC] * nb + [BS(memory_space=pltpu.VMEM)],
        input_output_aliases={i: 2 + i for i in range(nb)},
        compiler_params=pltpu.CompilerParams(has_side_effects=EFFECT))(
            *[pltpu.with_memory_space_constraint(b, pltpu.HBM) for b in bufs], *after)
    return out[0], out[1], out[2:2 + ns], out[2 + ns:2 + nb], out[-1]


def _exchange_wait(name, plan, send_sems, recv_sems, lands, srcs=None, after=()):
    bufs = ([] if srcs is None else list(srcs)) + list(lands)
    n, nb, ns = len(lands), len(bufs), len(bufs) - len(lands)

    def body(*refs):
        src_refs = refs[:ns] if ns else [None] * n
        land_refs = refs[ns:nb]
        send_ref, recv_ref = refs[nb], refs[nb + 1]
        for i in range(n):
            copies = plan(src_refs[i], land_refs[i])
            for k, (src, dst, to) in enumerate(copies):
                cp = pltpu.make_async_remote_copy(src_ref=src, dst_ref=dst, send_sem=send_ref.at[i * len(copies) + k],
                                                  recv_sem=recv_ref.at[i * len(copies) + k], device_id=to,
                                                  device_id_type=MESH)
                cp.wait_send()
                cp.wait_recv()

    out = pl.pallas_call(
        body, name=name, out_shape=[pltpu.HBM(b.shape, b.dtype) for b in bufs],
        in_specs=[HBM_SPEC] * nb + [SEM_SPEC, SEM_SPEC] + [ORDER_SPEC] * len(after), out_specs=[HBM_SPEC] * nb,
        input_output_aliases={i: i for i in range(nb)},
        compiler_params=pltpu.CompilerParams(has_side_effects=EFFECT))(*bufs, send_sems, recv_sems, *after)
    return out[:ns], out[ns:]


def _allreduce_small(v, after=()):
    R, Wd = v.shape
    n_after = len(after)

    def body(v_ref, *rest):
        o_ref, buf, send_sems, recv_sems = rest[n_after:]
        x, y, c = _place()
        me = _slot(x, y, c)
        buf[me] = v_ref[...]
        started = []
        for rel in range(1, NDEV):
            peer = _peer(x, y, c, rel)
            cp = pltpu.make_async_remote_copy(
                src_ref=v_ref, dst_ref=buf.at[me], send_sem=send_sems.at[rel - 1], recv_sem=recv_sems.at[rel - 1],
                device_id=peer, device_id_type=MESH)
            cp.start()
            started.append(cp)
        for cp in started:
            cp.wait()
        acc = buf[0]
        for j in range(1, NDEV):
            acc = acc + buf[j]
        o_ref[...] = acc

    vm = BS(memory_space=pltpu.VMEM)
    return pl.pallas_call(
        body, out_shape=SDS((R, Wd), F32), in_specs=[vm] + [ORDER_SPEC] * n_after, out_specs=vm,
        scratch_shapes=[pltpu.VMEM((NDEV, R, Wd), F32), pltpu.SemaphoreType.DMA((7,)), pltpu.SemaphoreType.DMA((7,))],
        name="allreduce_small")(v, *after)


def _as_matrix(name, g):
    if name in ROW_SHARDED:
        return g.reshape(1, NDEV * g.shape[1], g.shape[2])
    return g


def _pad8(a):
    return jnp.pad(a, ((0, -a.shape[0] % 8), (0, 0)))


def _pack_small(g_mix, g_mlp, g_ple, b_gate, g_final, b_f, extra_row=None):
    Lr, D = g_mix.shape
    rows = [g_mix, g_mlp, g_ple, b_gate.reshape(2 * Lr, D), g_final.reshape(1, D),
            jnp.pad(b_f.reshape(1, -1), ((0, 0), (0, D - b_f.size)))]
    rows.append(jnp.zeros((1, D), F32) if extra_row is None else extra_row)
    return jnp.concatenate([_pad8(r) for r in rows], axis=0)


def _unpack_small(pk, Lr, D, nf):
    o = 0
    out = []
    for rows, shape in ((Lr, (Lr, D)), (Lr, (Lr, D)), (Lr, (Lr, D)), (2 * Lr, (Lr, 2 * D)), (1, (D,))):
        out.append(pk[o:o + rows].reshape(shape))
        o += rows + (-rows % 8)
    out.append(pk[o, :Lr * nf].reshape(Lr, nf))
    return out, pk[o + 8, 0]


def kernel(x, p, g_mix, w_in, b_f, w_gate, b_gate, w_br_a, w_br_b, w_o, g_mlp, w_up, w_down, g_ple, w_ple, w_ple_gate, g_final, loss_target, m_g_mix, m_w_in, m_b_f, m_w_gate, m_b_gate, m_w_br_a, m_w_br_b, m_w_o, m_g_mlp, m_w_up, m_w_down, m_g_ple, m_w_ple, m_w_ple_gate, m_g_final, v_g_mix, v_w_in, v_b_f, v_w_gate, v_b_gate, v_w_br_a, v_w_br_b, v_w_o, v_g_mlp, v_w_up, v_w_down, v_g_ple, v_w_ple, v_w_ple_gate, v_g_final):
    Lr, D = g_mix.shape
    S = x.shape[1]
    nf = b_f.shape[1]
    big_w = dict(zip(BIG, (w_in, w_gate, w_br_a, w_br_b, w_o, w_up, w_down, w_ple, w_ple_gate)))
    big_m = dict(zip(BIG, (m_w_in, m_w_gate, m_w_br_a, m_w_br_b, m_w_o, m_w_up, m_w_down, m_w_ple, m_w_ple_gate)))
    big_v = dict(zip(BIG, (v_w_in, v_w_gate, v_w_br_a, v_w_br_b, v_w_o, v_w_up, v_w_down, v_w_ple, v_w_ple_gate)))

    x_pos, y_pos, c_pos = _place()
    me = _slot(x_pos, y_pos, c_pos).astype(jnp.int32).reshape(1)
    place = jnp.stack([c_pos, 2 * x_pos + y_pos]).astype(jnp.int32)
    wbf = {k: _cast_bf16(k, big_w[k], me) for k in BIG}
    tabs = _rope_tables(S)
    bf_pad = jnp.pad(b_f, ((0, 0), (0, LANES - nf)))

    smalls = [dict(g_mix=g_mix[l:l + 1], b_f=bf_pad[l:l + 1], b_gate=b_gate[l:l + 1], g_mlp=g_mlp[l:l + 1],
                   g_ple=g_ple[l:l + 1]) for l in range(Lr)]

    gather_groups = (("in", "gate", "bra", "brb", "o"), ("up", "down"), ("ple", "pg"))
    per_layer = len(gather_groups)
    order = [(l, gi) for l in range(Lr) for gi in range(len(gather_groups))]
    on_links = {}

    def links_start(k, after=()):
        l, gi = order[k]
        on_links[k] = _exchange_start(f"gather_ici_start_{l}_{gi}", _plan_gather_ici, 4,
                                      [wbf[n][l] for n in gather_groups[gi]], None, after)
        return on_links[k][4]

    def arrive(k, after):
        l, gi = order[k]
        send_sems, recv_sems, _, lands, _ = on_links.pop(k)
        _, lands = _exchange_wait(f"gather_ici_wait_{l}_{gi}", _plan_gather_ici, send_sems, recv_sems, lands, None,
                                  after)
        send_sems, recv_sems, _, lands, token = _exchange_start(
            f"gather_relay_start_{l}_{gi}", _plan_gather_relay, 3, lands)
        tokens = [token]
        if k + 2 < len(order):
            tokens.append(links_start(k + 2, (token,)))
        _, lands = _exchange_wait(f"gather_relay_wait_{l}_{gi}", _plan_gather_relay, send_sems, recv_sems, lands)
        return {n: _as_matrix(n, g) for n, g in zip(gather_groups[gi], lands)}, tuple(tokens)

    h = x.reshape(S, D)
    Ws, saves = [], []
    queued = links_start(1, (links_start(0),))
    for l in range(Lr):
        W, tokens = arrive(per_layer * l, (h, queued) if l == 0 else (h,))

        def hook(part, l=l, W=W):
            def at(h_now):
                more, hook_tokens = arrive(per_layer * l + part, (h_now,))
                W.update(more)
                return hook_tokens
            return at

        h, sv = _layer_fwd(h, p[l, 0].astype(BF16), smalls[l], W, tabs, tokens, hook(1), hook(2))
        Ws.append(W)
        saves.append(sv)

    dh, dg_final, loss_part = _final(h, g_final.reshape(1, D), loss_target.reshape(S, D))

    big_out = {k: None for k in BIG}
    dsmalls = [None] * Lr
    groups = dict(mlp=("ple", "pg", "down", "up"), merge=("o", "bra", "brb", "gate"), proj=("in",))
    in_flight = []
    links_steps = 3

    def pair_start(l, stage, dW):
        dws = [dW[k].reshape(wbf[k][l].shape) for k in groups[stage]]
        lands = [lax.empty((4,) + d.shape[1:], BF16) for d in dws]
        started = _exchange_start(f"pair_start_{l}_{stage}", _plan_pair, 4, lands, dws)
        in_flight.append(dict(tag=f"{l}_{stage}", layer=l, names=groups[stage], on_links=False, started=started,
                              age=0))
        return started[4]

    def to_links(ex, after):
        send_sems, recv_sems, dws, lands, _ = ex["started"]
        dws, sib = _exchange_wait(f"pair_wait_{ex['tag']}", _plan_pair, send_sems, recv_sems, lands, dws, after)
        sums = [_pair_sum(k, d, s, place) for k, d, s in zip(ex["names"], dws, sib)]
        ex["halves"] = (dws, sib)
        ex["started"] = _exchange_start(f"scatter_start_{ex['tag']}", _plan_scatter_chips, 3,
                                        [lax.empty(s.shape, BF16) for s in sums], sums)
        ex["on_links"], ex["age"] = True, 0
        return ex["started"][4]

    def finish(ex, after):
        send_sems, recv_sems, sums, lands, _ = ex["started"]
        _, landed = _exchange_wait(f"scatter_wait_{ex['tag']}", _plan_scatter_chips, send_sems, recv_sems, lands, sums,
                                   after)
        for k, dw, sib, recv in zip(ex["names"], *ex["halves"], landed):
            carried = None if big_out[k] is None else big_out[k][:4]
            big_out[k] = _adamw(k, ex["layer"], big_w[k], big_m[k], big_v[k], recv, dw, sib, place, carried)
        in_flight.remove(ex)
        return landed[0]

    def advance(after):
        tokens = []
        for ex in list(in_flight):
            if not ex["on_links"]:
                tokens.append(to_links(ex, after))
            elif ex["age"] + 1 >= links_steps:
                tokens.append(finish(ex, after))
            else:
                ex["age"] += 1
        return tokens

    after = ()
    for l in reversed(range(Lr)):
        def ready(stage, dW, l=l):
            tokens = advance((dW[groups[stage][-1]],))
            return tuple(tokens + [pair_start(l, stage, dW)])

        dh, dW, dsmalls[l] = _layer_bwd(dh, saves[l], smalls[l], Ws[l], tabs, after, ready)
        after = tuple(advance((dh,)) + [pair_start(l, "proj", dW)])
    last = after[-1]
    for ex in [ex for ex in in_flight if not ex["on_links"]]:
        last = to_links(ex, (last,))
    for ex in list(in_flight):
        updated = tuple(out[4] for out in big_out.values() if out is not None)
        last = finish(ex, updated + (last,))

    cat = lambda key: jnp.concatenate([_pad8(d[key]) for d in dsmalls], axis=0)[::8]
    dbf = jnp.concatenate([d["b_f"] for d in dsmalls], axis=0).reshape(Lr, 8, LANES)[:, :nf, 0]
    loss_row = jnp.concatenate([loss_part, g_final.reshape(1, D)[:, LANES:]], axis=1)
    g_pack = _allreduce_small(_pack_small(cat("g_mix"), cat("g_mlp"), cat("g_ple"), cat("b_gate"), dg_final, dbf,
                                          loss_row), (last,))
    w_pack = _pack_small(g_mix, g_mlp, g_ple, b_gate, g_final, b_f)
    m_pack = _pack_small(m_g_mix, m_g_mlp, m_g_ple, m_b_gate, m_g_final, m_b_f)
    v_pack = _pack_small(v_g_mix, v_g_mlp, v_g_ple, v_b_gate, v_g_final, v_b_f)
    d_pack, mn_pack, vn_pack = _adamw_small(w_pack, g_pack, m_pack, v_pack)

    small_names = ("g_mix", "g_mlp", "g_ple", "b_gate", "g_final", "b_f")
    sg, loss = _unpack_small(g_pack, Lr, D, nf)
    small_out = {}
    for kind, pk in (("grad", None), ("delta", d_pack), ("m", mn_pack), ("v", vn_pack)):
        vals = sg if pk is None else _unpack_small(pk, Lr, D, nf)[0]
        small_out[kind] = dict(zip(small_names, vals))

    order = ("g_mix", "in", "b_f", "gate", "b_gate", "bra", "brb", "o", "g_mlp", "up", "down", "g_ple", "ple", "pg",
             "g_final")
    outs = [loss, dh.reshape(x.shape)]
    for idx, kind in enumerate(("grad", "delta", "m", "v")):
        for name in order:
            outs.append(big_out[name][idx] if name in big_out else small_out[kind][name])
    return tuple(outs)
```

```python
import jax
import jax.numpy as jnp
from jax import lax
from jax.experimental import pallas as pl
from jax.experimental.pallas import tpu as pltpu

F32, BF16 = jnp.float32, jnp.bfloat16
SDS = jax.ShapeDtypeStruct
BS = pl.BlockSpec
MESH = pl.DeviceIdType.MESH
HI = lax.Precision.HIGHEST

HEAD_DIM = 128
N_HEADS = 16
ATTN = N_HEADS * HEAD_DIM
DIL_GROUPS = ((128, 1), (512, 4), (2048, 16))
HEADS_PER_GROUP = 4
FOX_HEAD0 = 12
N_FOX = 4
BRANCH = HEADS_PER_GROUP * HEAD_DIM
BLOCK = 128
ROPE_DIM = HEAD_DIM // 4
ROPE_HALF = ROPE_DIM // 2
ROPE_THETA = 500000.0
NORM_EPS = 1e-6
SCALE = HEAD_DIM ** -0.5
NDEV = 8
LANES = 128

ADAM_LR, ADAM_B1, ADAM_B2, ADAM_EPS, ADAM_WD, ADAM_STEP = 0.001, 0.9, 0.999, 1e-08, 0.01, 10

V7X_VMEM_BYTES = 64 * 1024 * 1024
VMEM_LIMIT = (V7X_VMEM_BYTES * 3) // 4

NT_CONTRACT = 4096
NN = (((1,), (0,)), ((), ()))
NT = (((1,), (1,)), ((), ()))
TN = (((0,), (0,)), ((), ()))

BIG = ("in", "gate", "bra", "brb", "o", "up", "down", "ple", "pg")
ROW_SHARDED = ("in", "o", "down", "pg")


def _tile(n, pref):
    t = min(n, pref)
    while n % t:
        t -= LANES
    return t


def _cparams(sem):
    return pltpu.CompilerParams(dimension_semantics=sem, vmem_limit_bytes=VMEM_LIMIT)


def _sigmoid(x):
    return 1.0 / (1.0 + jnp.exp(-x))


def _mm(name, grid, a, a_spec, b, b_spec, dims, outs, acc_shape, extras=(), epilogue=None, carry=None, blocks=1):
    nk = grid[2]
    n_ex, n_out = len(extras), len(outs)
    n_carry = 0 if carry is None else 1

    def body(*refs):
        a_ref, b_ref = refs[0], refs[1]
        ex_refs = refs[2:2 + n_ex]
        out_refs = refs[2 + n_ex + n_carry:2 + n_ex + n_carry + n_out]

        def product():
            if blocks == 1:
                return lax.dot_general(a_ref[...], b_ref[...], dims, preferred_element_type=F32)
            n = b_ref.shape[2]
            total = None
            for jb in range(blocks):
                part = lax.dot_general(a_ref[:, jb * n:(jb + 1) * n], b_ref[jb], dims, preferred_element_type=F32)
                total = part if total is None else total + part
            return total

        def finish(acc):
            res = (acc,) if epilogue is None else epilogue(acc, *[e[...] for e in ex_refs])
            for o_ref, r in zip(out_refs, res):
                o_ref[...] = r.astype(o_ref.dtype)

        if nk == 1:
            finish(product())
            return
        acc_ref = refs[-1]
        k = pl.program_id(2)

        @pl.when(k == 0)
        def _():
            acc_ref[...] = jnp.zeros_like(acc_ref)

        acc_ref[...] += product()

        @pl.when(k == nk - 1)
        def _():
            finish(acc_ref[...])

    arrays = [a, b] + [e[0] for e in extras]
    specs = [a_spec, b_spec] + [e[1] for e in extras]
    alias = {}
    if carry is not None:
        arrays.append(carry[0])
        specs.append(BS(memory_space=pl.ANY))
        alias = {len(arrays) - 1: carry[1]}
    return pl.pallas_call(
        body, out_shape=[o[0] for o in outs], grid=grid, in_specs=specs, out_specs=[o[1] for o in outs],
        scratch_shapes=[pltpu.VMEM(acc_shape, F32)] if nk > 1 else [], input_output_aliases=alias, name=name,
        compiler_params=_cparams(("parallel", "parallel", "arbitrary")))(*arrays)


def _mm_nn(name, a, w, out_dtypes, *, tm, joff=0, ncols=None, tn=None, tk=None, extras=(), epilogue=None,
           out_specs=None, out_shapes=None):
    M, K = a.shape
    J, _, n = w.shape
    tn = tn or _tile(n, 512)
    tk = tk or _tile(K, 2048)
    per = n // tn
    ncols = ncols or J * n
    grid = (M // tm, ncols // tn, K // tk)
    a_spec = BS((tm, tk), lambda i, j, k: (i, k))
    w_spec = BS((None, tk, tn), lambda i, j, k: ((j + joff) // per, k, (j + joff) % per))
    if out_specs is None:
        out_specs = [BS((tm, tn), lambda i, j, k: (i, j))] * len(out_dtypes)
        out_shapes = [(M, ncols)] * len(out_dtypes)
    outs = [(SDS(s, d), sp) for s, d, sp in zip(out_shapes, out_dtypes, out_specs)]
    return _mm(name, grid, a, a_spec, w, w_spec, NN, outs, (tm, tn), extras, epilogue)


def _mm_nt(name, dy, w, out_dtype, *, tm, coff=0, ncontract=None, tc=None, dy_spec=None, extras=(), epilogue=None):
    J, K, n = w.shape
    M = dy.shape[-2]
    tko = _tile(K, 1024 if J == 1 else 512)
    outs = [(SDS((M, K), out_dtype), BS((tm, tko), lambda i, ko, cc: (i, ko)))]
    if J > 1 and tc is None and dy_spec is None:
        blocks = max(1, min(J, NT_CONTRACT // n))
        grid = (M // tm, K // tko, J // blocks)
        dy_spec = BS((tm, blocks * n), lambda i, ko, cc: (i, cc))
        w_spec = BS((blocks, tko, n), lambda i, ko, cc: (cc, ko, 0))
        return _mm(name, grid, dy, dy_spec, w, w_spec, NT, outs, (tm, tko), extras, epilogue, blocks=blocks)[0]
    tc = tc or _tile(n, 2048)
    per = n // tc
    ncontract = ncontract or J * n
    grid = (M // tm, K // tko, ncontract // tc)
    if dy_spec is None:
        dy_spec = BS((tm, tc), lambda i, ko, cc: (i, cc))
    w_spec = BS((None, tko, tc), lambda i, ko, cc: ((cc + coff) // per, ko, (cc + coff) % per))
    return _mm(name, grid, dy, dy_spec, w, w_spec, NT, outs, (tm, tko), extras, epilogue)[0]


def _mm_tn(name, x, dy, wshape, *, ts, joff=0, ncols=None, tn=None, dy_spec=None, carry=None):
    J, K, n = wshape
    S = x.shape[0]
    tn = tn or _tile(n, 1024)
    tkm = _tile(K, 1024)
    per = n // tn
    ncols = ncols or J * n
    grid = (K // tkm, ncols // tn, S // ts)
    x_spec = BS((ts, tkm), lambda ki, j, sc: (sc, ki))
    if dy_spec is None:
        dy_spec = BS((ts, tn), lambda ki, j, sc: (sc, j))
    outs = [(SDS(wshape, BF16), BS((None, tkm, tn), lambda ki, j, sc: ((j + joff) // per, ki, (j + joff) % per)))]
    return _mm(name, grid, x, x_spec, dy, dy_spec, TN, outs, (tkm, tn), carry=None if carry is None else (carry, 0))[0]


def _rows(name, body, ins, outs, S, tr, sequential=False):
    def spec(shape, kind):
        if kind == "row":
            return BS((tr, shape[1]), lambda i: (i, 0))
        if kind == "order":
            return BS(memory_space=pl.ANY)
        return BS(tuple(shape), lambda i: (0,) * len(shape))
    return pl.pallas_call(
        body, out_shape=[o[0] for o in outs], grid=(S // tr,),
        in_specs=[spec(a.shape, k) for a, k in ins], out_specs=[spec(o[0].shape, o[1]) for o in outs],
        name=name, compiler_params=_cparams(("arbitrary" if sequential else "parallel",)))(*[a for a, _ in ins])


def _rms_fwd(h, g, tr=256, after=()):
    S, D = h.shape
    n_after = len(after)

    def body(h_ref, g_ref, *rest):
        u_ref = rest[n_after]
        x = h_ref[...]
        r = lax.rsqrt(jnp.mean(x * x, axis=-1, keepdims=True) + NORM_EPS)
        u_ref[...] = ((x * r) * g_ref[...]).astype(BF16)

    ins = [(h, "row"), (g, "full")] + [(t, "order") for t in after]
    return _rows("rms_fwd", body, ins, [(SDS((S, D), BF16), "row")], S, tr)[0]


def _rms_bwd(h, g, dy, dres, tr=256, after=()):
    S, D = h.shape
    n_after = len(after)

    def body(h_ref, g_ref, dy_ref, dres_ref, *rest):
        dh_ref, dhb_ref, dg_ref = rest[n_after:]
        x = h_ref[...]
        r = lax.rsqrt(jnp.mean(x * x, axis=-1, keepdims=True) + NORM_EPS)
        xhat = x * r
        dyv = dy_ref[...]
        gy = dyv * g_ref[...]
        dx = r * (gy - xhat * jnp.mean(gy * xhat, axis=-1, keepdims=True))
        dh = dres_ref[...] + dx
        dh_ref[...] = dh
        dhb_ref[...] = dh.astype(BF16)
        part = jnp.sum(dyv * xhat, axis=0, keepdims=True)

        @pl.when(pl.program_id(0) == 0)
        def _():
            dg_ref[...] = part

        @pl.when(pl.program_id(0) > 0)
        def _():
            dg_ref[...] += part

    ins = [(h, "row"), (g, "full"), (dy, "row"), (dres, "row")] + [(t, "order") for t in after]
    return _rows("rms_bwd", body, ins,
                 [(SDS((S, D), F32), "row"), (SDS((S, D), BF16), "row"), (SDS((1, D), F32), "full")], S, tr, True)


def _final(h, g, target, tr=256):
    S, D = h.shape

    def body(h_ref, g_ref, t_ref, dh_ref, dg_ref, loss_ref):
        x = h_ref[...]
        r = lax.rsqrt(jnp.mean(x * x, axis=-1, keepdims=True) + NORM_EPS)
        xhat = x * r
        gv = g_ref[...]
        err = xhat * gv - t_ref[...]
        lpart = 0.5 * jnp.sum(jnp.mean(err * err, axis=-1, keepdims=True), axis=0, keepdims=True)
        dyv = err * (1.0 / D)
        gy = dyv * gv
        dh_ref[...] = r * (gy - xhat * jnp.mean(gy * xhat, axis=-1, keepdims=True))
        part = jnp.sum(dyv * xhat, axis=0, keepdims=True)
        lrow = jnp.broadcast_to(lpart, (1, LANES))

        @pl.when(pl.program_id(0) == 0)
        def _():
            dg_ref[...] = part
            loss_ref[...] = lrow

        @pl.when(pl.program_id(0) > 0)
        def _():
            dg_ref[...] += part
            loss_ref[...] += lrow

    return _rows("final_head", body, [(h, "row"), (g, "full"), (target, "row")],
                 [(SDS((S, D), F32), "row"), (SDS((1, D), F32), "full"), (SDS((1, LANES), F32), "full")], S, tr, True)


def _ple_bwd(dh, pg, pp, tr=256, after=()):
    S, D = dh.shape
    n_after = len(after)

    def body(dh_ref, pg_ref, pp_ref, *rest):
        dpp_ref, dpgz_ref = rest[n_after:]
        d, g, q = dh_ref[...], pg_ref[...], pp_ref[...]
        dpp_ref[...] = (d * g).astype(BF16)
        dpgz_ref[...] = (d * q * g * (1.0 - g)).astype(BF16)

    ins = [(dh, "row"), (pg, "row"), (pp, "row")] + [(t, "order") for t in after]
    return _rows("ple_bwd", body, ins, [(SDS((S, D), BF16), "row"), (SDS((S, D), BF16), "row")], S, tr)


def _merge_bwd(dmerged, ya, yb, gates, tr=256):
    S, D = dmerged.shape

    def body(dm_ref, ya_ref, yb_ref, g_ref, dya_ref, dyb_ref, dgz_ref, dbg_ref):
        dm, a, b = dm_ref[...], ya_ref[...], yb_ref[...]
        g1, g2 = g_ref[:, :D], g_ref[:, D:]
        dya_ref[...] = (dm * g1).astype(BF16)
        dyb_ref[...] = (dm * g2).astype(BF16)
        dz1 = dm * a * g1 * (1.0 - g1)
        dz2 = dm * b * g2 * (1.0 - g2)
        dgz_ref[:, :D] = dz1.astype(BF16)
        dgz_ref[:, D:] = dz2.astype(BF16)
        p1 = jnp.sum(dz1, axis=0, keepdims=True)
        p2 = jnp.sum(dz2, axis=0, keepdims=True)

        @pl.when(pl.program_id(0) == 0)
        def _():
            dbg_ref[:, :D] = p1
            dbg_ref[:, D:] = p2

        @pl.when(pl.program_id(0) > 0)
        def _():
            dbg_ref[:, :D] += p1
            dbg_ref[:, D:] += p2

    return _rows("merge_bwd", body, [(dmerged, "row"), (ya, "row"), (yb, "row"), (gates, "row")],
                 [(SDS((S, D), BF16), "row"), (SDS((S, D), BF16), "row"), (SDS((S, 2 * D), BF16), "row"),
                  (SDS((1, 2 * D), F32), "full")], S, tr, True)


def _mix_weights(l0, l1, l2):
    mx = jnp.maximum(jnp.maximum(l0, l1), l2)
    e0, e1, e2 = jnp.exp(l0 - mx), jnp.exp(l1 - mx), jnp.exp(l2 - mx)
    inv = 1.0 / (e0 + e1 + e2)
    return e0 * inv, e1 * inv, e2 * inv


def _mix_fwd(os_, ls_, tr=512):
    S = os_[0].shape[0]

    def body(o0, o1, o2, l0, l1, l2, y_ref):
        w0, w1, w2 = _mix_weights(l0[...], l1[...], l2[...])
        y_ref[...] = (w0 * o0[...] + w1 * o1[...] + w2 * o2[...]).astype(BF16)

    return _rows("mix_fwd", body, [(a, "row") for a in (*os_, *ls_)], [(SDS((S, BRANCH), BF16), "row")], S, tr)[0]


def _mix_bwd(dy, os_, ls_, tr=512, after=()):
    S = dy.shape[0]
    n_after = len(after)

    def head_sums(t):
        return jnp.concatenate(
            [jnp.broadcast_to(jnp.sum(t[:, j * HEAD_DIM:(j + 1) * HEAD_DIM], axis=-1, keepdims=True), (tr, HEAD_DIM))
             for j in range(HEADS_PER_GROUP)], axis=1)

    def body(dy_ref, o0, o1, o2, l0, l1, l2, *rest):
        d0, d1, d2, p0, p1, p2 = rest[n_after:]
        ws = _mix_weights(l0[...], l1[...], l2[...])
        d = dy_ref[...]
        es = [head_sums(d * o[...]) for o in (o0, o1, o2)]
        ebar = ws[0] * es[0] + ws[1] * es[1] + ws[2] * es[2]
        for w, e, d_ref, p_ref in zip(ws, es, (d0, d1, d2), (p0, p1, p2)):
            d_ref[...] = (w * d).astype(BF16)
            p_ref[...] = w * (e - ebar)

    outs = [(SDS((S, BRANCH), BF16), "row")] * 3 + [(SDS((S, BRANCH), F32), "row")] * 3
    ins = [(a, "row") for a in (dy, *os_, *ls_)] + [(t, "order") for t in after]
    return _rows("mix_bwd", body, ins, outs, S, tr)


def _iota2(shape, dim):
    return lax.broadcasted_iota(jnp.int32, shape, dim)


def _fgate_fwd(zf, bf):
    S = zf.shape[0]
    nblk = S // BLOCK

    def body(zf_ref, bf_ref, cb_ref, ct_ref, flt_ref):
        tri = (_iota2((BLOCK, BLOCK), 0) >= _iota2((BLOCK, BLOCK), 1)).astype(F32)
        sel8 = (_iota2((8, LANES), 0) == _iota2((8, LANES), 1)).astype(F32)
        carry = jnp.zeros((1, LANES), F32)
        for blk in range(nblk):
            rows = pl.ds(blk * BLOCK, BLOCK)
            fl = zf_ref[rows, :] + bf_ref[...]
            ls = jnp.minimum(fl, 0.0) - jnp.log(1.0 + jnp.exp(-jnp.abs(fl)))
            c = jnp.dot(tri, ls, precision=HI, preferred_element_type=F32) + carry
            carry = carry + jnp.sum(ls, axis=0, keepdims=True)
            for hd in range(N_FOX):
                e_h = (_iota2((LANES, LANES), 0) == hd).astype(F32)
                cb_ref[rows, hd * LANES:(hd + 1) * LANES] = jnp.dot(c, e_h, precision=HI, preferred_element_type=F32)
            ct_ref[:, rows] = lax.dot_general(sel8, c, NT, precision=HI, preferred_element_type=F32)
            flt_ref[:, rows] = lax.dot_general(sel8, fl, NT, precision=HI, preferred_element_type=F32)

    vm = BS(memory_space=pltpu.VMEM)
    return pl.pallas_call(
        body, out_shape=[SDS((S, N_FOX * LANES), F32), SDS((8, S), F32), SDS((8, S), F32)],
        in_specs=[vm, vm], out_specs=[vm, vm, vm], name="fgate_fwd",
        compiler_params=pltpu.CompilerParams(vmem_limit_bytes=VMEM_LIMIT))(zf, bf)


def _fgate_bwd(dct, flt):
    S = dct.shape[1]
    nblk = S // BLOCK

    def body(dct_ref, flt_ref, dzf_ref, dbf_ref):
        rowid = _iota2((8, BLOCK), 0)
        tri_r = (_iota2((BLOCK, BLOCK), 0) >= _iota2((BLOCK, BLOCK), 1)).astype(F32)
        carry = jnp.zeros((8, 1), F32)
        dbf = jnp.zeros((8, 1), F32)
        for blk in reversed(range(nblk)):
            cols = pl.ds(blk * BLOCK, BLOCK)
            dc = jnp.where(rowid < N_FOX, dct_ref[:, cols], 0.0)
            dls = jnp.dot(dc, tri_r, precision=HI, preferred_element_type=F32) + carry
            carry = carry + jnp.sum(dc, axis=1, keepdims=True)
            dfl = jnp.where(rowid < N_FOX, dls * _sigmoid(-flt_ref[:, cols]), 0.0)
            dbf = dbf + jnp.sum(dfl, axis=1, keepdims=True)
            sq = jnp.concatenate([dfl, jnp.zeros((BLOCK - 8, BLOCK), F32)], axis=0)
            dzf_ref[cols, :] = sq.T.astype(BF16)
        dbf_ref[...] = jnp.broadcast_to(dbf, (8, LANES))

    vm = BS(memory_space=pltpu.VMEM)
    return pl.pallas_call(
        body, out_shape=[SDS((S, LANES), BF16), SDS((8, LANES), F32)],
        in_specs=[vm, vm], out_specs=[vm, vm], name="fgate_bwd",
        compiler_params=pltpu.CompilerParams(vmem_limit_bytes=VMEM_LIMIT))(dct, flt)


def _rope_tables(S):
    inv = ROPE_THETA ** (-jnp.arange(ROPE_HALF, dtype=F32) / ROPE_HALF)
    ang = jnp.arange(S, dtype=F32)[:, None] * inv[None, :]
    cos, sin = jnp.cos(ang), jnp.sin(ang)
    pad = HEAD_DIM - ROPE_DIM
    tc = jnp.concatenate([cos, cos, jnp.ones((S, pad), F32)], axis=1)
    ta = jnp.concatenate([-sin, jnp.zeros((S, HEAD_DIM - ROPE_HALF), F32)], axis=1)
    tb = jnp.concatenate([jnp.zeros((S, ROPE_HALF), F32), sin, jnp.zeros((S, pad), F32)], axis=1)
    return tc, ta, tb


def _rope(x, c, a, b):
    return x * c + pltpu.roll(x, HEAD_DIM - ROPE_HALF, 1) * a + pltpu.roll(x, ROPE_HALF, 1) * b


def _rope_t(g, c, a, b):
    return g * c - pltpu.roll(g, ROPE_HALF, 1) * b - pltpu.roll(g, HEAD_DIM - ROPE_HALF, 1) * a


DIL_TQ = 512


def _dil_extent(i, window):
    return max(0, i * DIL_TQ - window), (i + 1) * DIL_TQ


def _dil_mask(i, lo, ext, window, d):
    diff = (i * DIL_TQ - lo) + _iota2((DIL_TQ, ext), 0) - _iota2((DIL_TQ, ext), 1)
    mask = (diff >= 0) & (diff <= window)
    return mask & ((diff & (d - 1)) == 0) if d > 1 else mask


def _wide(t, w):
    return t if w == LANES else jnp.concatenate([t] * (w // LANES), axis=1)


def _dil_fwd(z3, tabs, grp, S):
    window, d = DIL_GROUPS[grp]
    nq = S // DIL_TQ

    def body(qkv_ref, c_ref, a_ref, b_ref, o_ref, lse_ref, qs, ks):
        c, a, b = c_ref[...], a_ref[...], b_ref[...]
        qs[...] = _rope(qkv_ref[0].astype(F32), c, a, b).astype(BF16)
        ks[...] = _rope(qkv_ref[1].astype(F32), c, a, b).astype(BF16)
        for i in range(nq):
            lo, hi = _dil_extent(i, window)
            rows = pl.ds(i * DIL_TQ, DIL_TQ)
            s = lax.dot_general(qs[rows, :], ks[lo:hi, :], NT, preferred_element_type=F32) * SCALE
            s = jnp.where(_dil_mask(i, lo, hi - lo, window, d), s, -jnp.inf)
            m = jnp.max(s, axis=-1, keepdims=True)
            p = jnp.exp(s - m)
            l = jnp.sum(p, axis=-1, keepdims=True)
            o = jnp.dot(p.astype(BF16), qkv_ref[2, lo:hi, :], preferred_element_type=F32)
            o_ref[rows, :] = o / l
            lse_ref[rows, :] = jnp.broadcast_to(m + jnp.log(l), (DIL_TQ, HEAD_DIM))

    hh0 = grp * HEADS_PER_GROUP
    tspec = BS((S, HEAD_DIM), lambda j: (0, 0))
    ospec = BS((S, HEAD_DIM), lambda j: (0, j))
    return pl.pallas_call(
        body, out_shape=[SDS((S, BRANCH), F32)] * 2, grid=(HEADS_PER_GROUP,),
        in_specs=[BS((3, S, HEAD_DIM), lambda j: (0, 0, hh0 + j)), tspec, tspec, tspec],
        out_specs=[ospec, ospec], scratch_shapes=[pltpu.VMEM((S, HEAD_DIM), BF16)] * 2,
        name="dil_fwd", compiler_params=_cparams(("parallel",)))(z3, *tabs)


def _dil_bwd(z3, tabs, grp, S, do, lse, dpl, dz3):
    window, d = DIL_GROUPS[grp]
    nq = S // DIL_TQ
    n_carry = 0 if dz3 is None else 1

    def body(*refs):
        qkv_ref, c_ref, a_ref, b_ref, do_ref, lse_ref, dpl_ref = refs[:7]
        dz_ref = refs[7 + n_carry]
        qs, ks, dq_acc, dk_acc, dv_acc = refs[8 + n_carry:]
        c, a, b = c_ref[...], a_ref[...], b_ref[...]
        qs[...] = _rope(qkv_ref[0].astype(F32), c, a, b).astype(BF16)
        ks[...] = _rope(qkv_ref[1].astype(F32), c, a, b).astype(BF16)
        dk_acc[...] = jnp.zeros_like(dk_acc)
        dv_acc[...] = jnp.zeros_like(dv_acc)
        for i in range(nq):
            lo, hi = _dil_extent(i, window)
            w = hi - lo
            rows = pl.ds(i * DIL_TQ, DIL_TQ)
            q, k, v = qs[rows, :], ks[lo:hi, :], qkv_ref[2, lo:hi, :]
            s = lax.dot_general(q, k, NT, preferred_element_type=F32) * SCALE
            p = jnp.where(_dil_mask(i, lo, w, window, d), jnp.exp(s - _wide(lse_ref[rows, :], w)), 0.0)
            dob = do_ref[rows, :]
            dp = lax.dot_general(dob, v, NT, preferred_element_type=F32)
            delta = jnp.sum(p * dp, axis=-1, keepdims=True)
            ds = (p * (dp - delta + _wide(dpl_ref[rows, :], w))).astype(BF16)
            dq_acc[rows, :] = jnp.dot(ds, k, preferred_element_type=F32) * SCALE
            dk_acc[lo:hi, :] += lax.dot_general(ds, q, TN, preferred_element_type=F32) * SCALE
            dv_acc[lo:hi, :] += lax.dot_general(p.astype(BF16), dob, TN, preferred_element_type=F32)
        dz_ref[0] = _rope_t(dq_acc[...], c, a, b).astype(BF16)
        dz_ref[1] = _rope_t(dk_acc[...], c, a, b).astype(BF16)
        dz_ref[2] = dv_acc[...].astype(BF16)

    hh0 = grp * HEADS_PER_GROUP
    tspec = BS((S, HEAD_DIM), lambda j: (0, 0))
    bspec = BS((S, HEAD_DIM), lambda j: (0, j))
    zspec = BS((3, S, HEAD_DIM), lambda j: (0, 0, hh0 + j))
    arrays = [z3, *tabs, do, lse, dpl]
    specs = [zspec, tspec, tspec, tspec, bspec, bspec, bspec]
    alias = {}
    if dz3 is not None:
        arrays.append(dz3)
        specs.append(BS(memory_space=pl.ANY))
        alias = {len(arrays) - 1: 0}
    return pl.pallas_call(
        body, out_shape=SDS((3, S, ATTN), BF16), grid=(HEADS_PER_GROUP,), in_specs=specs, out_specs=zspec,
        scratch_shapes=[pltpu.VMEM((S, HEAD_DIM), BF16)] * 2 + [pltpu.VMEM((S, HEAD_DIM), F32)] * 3,
        input_output_aliases=alias, name="dil_bwd", compiler_params=_cparams(("parallel",)))(*arrays)


FOX_TQ = 512


def _fox_scores(q_ref, k_ref, cb_ref, ct_ref, i, ext):
    rows = pl.ds(i * FOX_TQ, FOX_TQ)
    s = lax.dot_general(q_ref[0, rows, :], k_ref[0, 0:ext, :], NT, preferred_element_type=F32) * SCALE
    s = s + _wide(cb_ref[rows, :], ext) - ct_ref[:, 0:ext]
    qpos = i * FOX_TQ + _iota2((FOX_TQ, ext), 0)
    kpos = _iota2((FOX_TQ, ext), 1)
    return jnp.where(kpos <= qpos, s, -jnp.inf)


def _fox_specs(S):
    qs = BS((1, S, HEAD_DIM), lambda h: (0, 0, FOX_HEAD0 + h))
    ks = BS((1, S, HEAD_DIM), lambda h: (1, 0, FOX_HEAD0 + h))
    vs = BS((1, S, HEAD_DIM), lambda h: (2, 0, FOX_HEAD0 + h))
    hb = BS((S, HEAD_DIM), lambda h: (0, h))
    ct = BS((None, 1, S), lambda h: (h, 0, 0))
    return qs, ks, vs, hb, ct


def _fox_fwd(z3, cb, ct, S):
    nq = S // FOX_TQ

    def body(q_ref, k_ref, v_ref, cb_ref, ct_ref, y_ref, lse_ref):
        for i in range(nq):
            ext = (i + 1) * FOX_TQ
            rows = pl.ds(i * FOX_TQ, FOX_TQ)
            s = _fox_scores(q_ref, k_ref, cb_ref, ct_ref, i, ext)
            m = jnp.max(s, axis=-1, keepdims=True)
            p = jnp.exp(s - m)
            l = jnp.sum(p, axis=-1, keepdims=True)
            o = jnp.dot(p.astype(BF16), v_ref[0, 0:ext, :], preferred_element_type=F32)
            y_ref[rows, :] = (o / l).astype(BF16)
            lse_ref[rows, :] = jnp.broadcast_to(m + jnp.log(l), (FOX_TQ, HEAD_DIM))

    qs, ks, vs, hb, cts = _fox_specs(S)
    return pl.pallas_call(
        body, out_shape=[SDS((S, BRANCH), BF16), SDS((S, BRANCH), F32)], grid=(N_FOX,),
        in_specs=[qs, ks, vs, hb, cts], out_specs=[hb, hb], name="fox_fwd",
        compiler_params=_cparams(("parallel",)))(z3, z3, z3, cb, ct.reshape(8, 1, S))


def _fox_bwd(z3, cb, ct, S, dy, lse, dz3):
    nq = S // FOX_TQ

    def body(q_ref, k_ref, v_ref, cb_ref, ct_ref, dy_ref, lse_ref, _, dz_ref, dct_ref, dk_acc, dv_acc):
        dk_acc[...] = jnp.zeros_like(dk_acc)
        dv_acc[...] = jnp.zeros_like(dv_acc)
        dct_ref[...] = jnp.zeros_like(dct_ref)
        for i in range(nq):
            ext = (i + 1) * FOX_TQ
            rows = pl.ds(i * FOX_TQ, FOX_TQ)
            s = _fox_scores(q_ref, k_ref, cb_ref, ct_ref, i, ext)
            p = jnp.exp(s - _wide(lse_ref[rows, :], ext))
            dob = dy_ref[rows, :]
            dp = lax.dot_general(dob, v_ref[0, 0:ext, :], NT, preferred_element_type=F32)
            ds = p * (dp - jnp.sum(p * dp, axis=-1, keepdims=True))
            dsb = ds.astype(BF16)
            dz_ref[0, rows, :] = (jnp.dot(dsb, k_ref[0, 0:ext, :], preferred_element_type=F32) * SCALE).astype(BF16)
            dk_acc[0:ext, :] += lax.dot_general(dsb, q_ref[0, rows, :], TN, preferred_element_type=F32) * SCALE
            dv_acc[0:ext, :] += lax.dot_general(p.astype(BF16), dob, TN, preferred_element_type=F32)
            dct_ref[:, 0:ext] -= jnp.sum(ds, axis=0, keepdims=True)
        dz_ref[1] = dk_acc[...].astype(BF16)
        dz_ref[2] = dv_acc[...].astype(BF16)

    qs, ks, vs, hb, cts = _fox_specs(S)
    zspec = BS((3, S, HEAD_DIM), lambda h: (0, 0, FOX_HEAD0 + h))
    dz, dct = pl.pallas_call(
        body, out_shape=[SDS((3, S, ATTN), BF16), SDS((8, 1, S), F32)], grid=(N_FOX,),
        in_specs=[qs, ks, vs, hb, cts, hb, hb, BS(memory_space=pl.ANY)], out_specs=[zspec, cts],
        scratch_shapes=[pltpu.VMEM((S, HEAD_DIM), F32)] * 2, input_output_aliases={7: 0}, name="fox_bwd",
        compiler_params=_cparams(("parallel",)))(z3, z3, z3, cb, ct.reshape(8, 1, S), dy, lse, dz3)
    return dz, dct.reshape(8, S)


TM = 1024


def _layer_fwd(h, p_bf, sm, W, tabs, after=(), mid=None, late=None):
    S, D = h.shape
    qkv_cols = 3 * ATTN
    u = _rms_fwd(h, sm["g_mix"], after=after)
    tn = 1024
    per = ATTN // tn
    z3 = _mm_nn("mm_qkv", u, W["in"], [BF16], tm=TM, ncols=qkv_cols, tn=tn, out_shapes=[(3, S, ATTN)],
                out_specs=[BS((None, TM, tn), lambda i, j, k: (j // per, i, j % per))])[0]
    zf = _mm_nn("mm_f", u, W["in"], [F32], tm=TM, joff=qkv_cols // LANES, ncols=LANES, tn=LANES)[0]
    cb, ct, flt = _fgate_fwd(zf, sm["b_f"])
    os_, ls_ = [], []
    for grp in range(len(DIL_GROUPS)):
        o, l = _dil_fwd(z3, tabs, grp, S)
        os_.append(o)
        ls_.append(l)
    ya_h = _mix_fwd(os_, ls_)
    yb_h, lse_fox = _fox_fwd(z3, cb, ct, S)

    tg = _tile(W["gate"].shape[2], 512)
    gates = _mm_nn("mm_gate", u, W["gate"], [F32], tm=TM,
                   extras=[(sm["b_gate"], BS((1, tg), lambda i, j, k: (0, j)))],
                   epilogue=lambda acc, bias: (_sigmoid(acc + bias),))[0]
    ya = _mm_nn("mm_bra", ya_h, W["bra"], [F32], tm=TM)[0]
    tb = _tile(W["brb"].shape[2], 512)
    g2off = D // tb
    yb, merged = _mm_nn(
        "mm_brb", yb_h, W["brb"], [F32, BF16], tm=TM,
        extras=[(ya, BS((TM, tb), lambda i, j, k: (i, j))), (gates, BS((TM, tb), lambda i, j, k: (i, j))),
                (gates, BS((TM, tb), lambda i, j, k: (i, j + g2off)))],
        epilogue=lambda acc, a, g1, g2: (acc, g1 * a + g2 * acc))
    to, wide = _tile(D, 512), _tile(D, 1024)
    res_spec = BS((TM, to), lambda i, j, k: (i, j))
    wide_spec = BS((TM, wide), lambda i, j, k: (i, j))
    h1 = _mm_nn("mm_o", merged, W["o"], [F32], tm=TM, tn=wide, extras=[(h, wide_spec)],
                epilogue=lambda acc, r: (r + acc,))[0]

    m = _rms_fwd(h1, sm["g_mlp"], after=() if mid is None else mid(h1))

    def up_epi(acc):
        r = jnp.maximum(acc, 0.0)
        return r, r * r

    relu_a, act = _mm_nn("mm_up", m, W["up"], [BF16, BF16], tm=TM, tn=_tile(W["up"].shape[2], 1024), epilogue=up_epi)
    h2 = _mm_nn("mm_down", act, W["down"], [F32], tm=TM, tn=wide, extras=[(h1, wide_spec)],
                epilogue=lambda acc, r: (r + acc,))[0]

    nrm = _rms_fwd(h2, sm["g_ple"], after=() if late is None else late(h2))
    pp = _mm_nn("mm_ple", p_bf, W["ple"], [F32], tm=TM)[0]

    def pg_epi(acc, r, q):
        g = _sigmoid(acc)
        return g, r + g * q

    pg, h3 = _mm_nn("mm_pg", nrm, W["pg"], [F32, F32], tm=TM, extras=[(h2, res_spec), (pp, res_spec)], epilogue=pg_epi)
    saved = dict(h=h, u=u, z3=z3, cb=cb, ct=ct, flt=flt, os=os_, ls=ls_, ya_h=ya_h, yb_h=yb_h, lse_fox=lse_fox,
                 gates=gates, ya=ya, yb=yb, merged=merged, h1=h1, m=m, relu_a=relu_a, act=act, h2=h2, nrm=nrm,
                 pg=pg, pp=pp, p_bf=p_bf)
    return h3, saved


def _layer_bwd(dh3, sv, sm, W, tabs, after=(), ready=None):
    S, D = dh3.shape
    dW, dsm = {}, {}

    dpp, dpgz = _ple_bwd(dh3, sv["pg"], sv["pp"], after=after)
    dW["ple"] = _mm_tn("tn_ple", sv["p_bf"], dpp, W["ple"].shape, ts=S)
    dW["pg"] = _mm_tn("tn_pg", sv["nrm"], dpgz, W["pg"].shape, ts=S)
    dn = _mm_nt("nt_pg", dpgz, W["pg"], F32, tm=TM)
    dh2, dh2b, dsm["g_ple"] = _rms_bwd(sv["h2"], sm["g_ple"], dn, dh3)

    ta = _tile(W["up"].shape[1], 512)
    tff = _tile(W["down"].shape[1], 1024)
    da = _mm_nt("nt_down", dh2b, W["down"], BF16, tm=TM,
                extras=[(sv["relu_a"], BS((TM, tff), lambda i, ko, cc: (i, ko)))],
                epilogue=lambda acc, r: (acc * (2.0 * r.astype(F32)),))
    dW["down"] = _mm_tn("tn_down", sv["act"], dh2b, W["down"].shape, ts=S)
    dm = _mm_nt("nt_up", da, W["up"], F32, tm=TM)
    dW["up"] = _mm_tn("tn_up", sv["m"], da, W["up"].shape, ts=S)
    dh1, dh1b, dsm["g_mlp"] = _rms_bwd(sv["h1"], sm["g_mlp"], dm, dh2, after=() if ready is None else ready("mlp", dW))

    dmerged = _mm_nt("nt_o", dh1b, W["o"], F32, tm=TM)
    dW["o"] = _mm_tn("tn_o", sv["merged"], dh1b, W["o"].shape, ts=S)
    dya, dyb, dgz, dsm["b_gate"] = _merge_bwd(dmerged, sv["ya"], sv["yb"], sv["gates"])
    dya_h = _mm_nt("nt_bra", dya, W["bra"], F32, tm=TM)
    dyb_h = _mm_nt("nt_brb", dyb, W["brb"], BF16, tm=TM)
    dW["bra"] = _mm_tn("tn_bra", sv["ya_h"], dya, W["bra"].shape, ts=S)
    dW["brb"] = _mm_tn("tn_brb", sv["yb_h"], dyb, W["brb"].shape, ts=S)
    dW["gate"] = _mm_tn("tn_gate", sv["u"], dgz, W["gate"].shape, ts=S)
    du = _mm_nt("nt_gate", dgz, W["gate"], F32, tm=TM)

    dos_, dpls_ = [], []
    mixed = _mix_bwd(dya_h, sv["os"], sv["ls"], after=() if ready is None else ready("merge", dW))
    dos_, dpls_ = mixed[:3], mixed[3:]
    dz3 = None
    for grp in range(len(DIL_GROUPS)):
        dz3 = _dil_bwd(sv["z3"], tabs, grp, S, dos_[grp], sv["ls"][grp], dpls_[grp], dz3)
    dz3, dct = _fox_bwd(sv["z3"], sv["cb"], sv["ct"], S, dyb_h, sv["lse_fox"], dz3)
    dzf, dsm["b_f"] = _fgate_bwd(dct, sv["flt"])

    tn = 1024
    per = ATTN // tn
    qkv_cols = 3 * ATTN
    dwin = _mm_tn("tn_qkv", sv["u"], dz3, W["in"].shape, ts=S, ncols=qkv_cols, tn=tn,
                  dy_spec=BS((None, S, tn), lambda ki, j, sc: (j // per, sc, j % per)))
    dW["in"] = _mm_tn("tn_f", sv["u"], dzf, W["in"].shape, ts=S, joff=qkv_cols // LANES, ncols=LANES, tn=LANES,
                      carry=dwin)
    tu = _tile(D, 1024)
    prev_spec = BS((TM, tu), lambda i, ko, cc: (i, ko))
    du = _mm_nt("nt_qkv", dz3, W["in"], F32, tm=TM, ncontract=qkv_cols, tc=ATTN,
                dy_spec=BS((None, TM, ATTN), lambda i, ko, cc: (cc, i, 0)),
                extras=[(du, prev_spec)], epilogue=lambda acc, prev: (prev + acc,))
    du = _mm_nt("nt_f", dzf, W["in"], F32, tm=TM, coff=qkv_cols // LANES, ncontract=LANES, tc=LANES,
                extras=[(du, prev_spec)], epilogue=lambda acc, prev: (prev + acc,))
    dh0, _, dsm["g_mix"] = _rms_bwd(sv["h"], sm["g_mix"], du, dh1)
    return dh0, dW, dsm


def _shard_tiles(R, Cp):
    tc = Cp if Cp <= 2048 else _tile(Cp, 1024)
    tr = R
    while tr * tc > 256 * 1024 and tr % 16 == 0:
        tr //= 2
    return tr, tc


def _padded_cols(C):
    return -(-C // LANES) * LANES


def _cast_bf16(name, w, me):
    Lr, R, C = w.shape
    Cp = _padded_cols(C)
    tr, tc = _shard_tiles(R, Cp)

    def body(me_ref, w_ref, *o_refs):
        for l in range(Lr):
            x = w_ref[l]
            if Cp != C:
                col = pl.program_id(1) * tc + _iota2((tr, tc), 1)
                x = jnp.where(col < C, x, 0.0)
            o_refs[l][...] = x.astype(BF16)

    grid_spec = pltpu.PrefetchScalarGridSpec(
        num_scalar_prefetch=1, grid=(R // tr, Cp // tc),
        in_specs=[BS((Lr, tr, tc), lambda i, j, me_ref: (0, i, j))],
        out_specs=[BS((None, tr, tc), lambda i, j, me_ref: (me_ref[0], i, j))] * Lr)
    return pl.pallas_call(
        body, out_shape=[SDS((NDEV, R, Cp), BF16)] * Lr, grid_spec=grid_spec,
        name="cast_" + name, compiler_params=_cparams(("parallel", "parallel")))(me, w)


def _adam_math(w, g, m, v):
    m = ADAM_B1 * m + (1.0 - ADAM_B1) * g
    v = ADAM_B2 * v + (1.0 - ADAM_B2) * (g * g)
    m_hat = m / (1.0 - ADAM_B1 ** ADAM_STEP)
    v_hat = v / (1.0 - ADAM_B2 ** ADAM_STEP)
    delta = -ADAM_LR * (m_hat / (jnp.sqrt(v_hat) + ADAM_EPS) + ADAM_WD * w)
    return delta, m, v


def _pair_sum(name, dw, sib, place):
    _, R, Cp = dw.shape
    tr, tc = _shard_tiles(R, Cp)

    def body(place_ref, a_ref, b_ref, o_ref):
        o_ref[...] = (a_ref[...].astype(F32) + b_ref[...].astype(F32)).astype(BF16)

    def other(r, place_ref):
        return jnp.bitwise_xor(place_ref[1], r + 1)

    grid_spec = pltpu.PrefetchScalarGridSpec(
        num_scalar_prefetch=1, grid=(3, R // tr, Cp // tc),
        in_specs=[BS((None, tr, tc), lambda r, i, j, place_ref: (2 * other(r, place_ref) + place_ref[0], i, j)),
                  BS((None, tr, tc), lambda r, i, j, place_ref: (other(r, place_ref), i, j))],
        out_specs=BS((None, tr, tc), lambda r, i, j, place_ref: (other(r, place_ref), i, j)))
    return pl.pallas_call(
        body, out_shape=SDS((4, R, Cp), BF16), grid_spec=grid_spec, name="pair_sum_" + name,
        compiler_params=_cparams(("parallel", "parallel", "parallel")))(place, dw, sib)


def _adamw(name, layer, w, m, v, recv, dw, sib, place, carried):
    Lr, R, C = w.shape
    nslot, _, Cp = recv.shape
    tr, tc = _shard_tiles(R, Cp)
    n_carry = 0 if carried is None else 4

    def body(place_ref, w_ref, m_ref, v_ref, r_ref, dw_ref, sib_ref, *rest):
        g_out, d_out, m_out, v_out, token = rest[n_carry:]
        mine = dw_ref[...].astype(F32) + sib_ref[...].astype(F32)
        g = None
        for s in range(nslot):
            part = jnp.where(place_ref[1] == s, mine, r_ref[s].astype(F32))
            g = part if g is None else g + part
        delta, mn, vn = _adam_math(w_ref[...], g, m_ref[...], v_ref[...])
        g_out[...] = g
        d_out[...] = delta
        m_out[...] = mn
        v_out[...] = vn
        token[...] = jnp.zeros_like(token)

    spec = BS((None, tr, tc), lambda i, j, place_ref: (layer, i, j))
    rspec = BS((nslot, tr, tc), lambda i, j, place_ref: (0, i, j))
    dspec = BS((None, tr, tc), lambda i, j, place_ref: (2 * place_ref[1] + place_ref[0], i, j))
    sspec = BS((None, tr, tc), lambda i, j, place_ref: (place_ref[1], i, j))
    arrays, specs, alias = [w, m, v, recv, dw, sib], [spec, spec, spec, rspec, dspec, sspec], {}
    if carried is not None:
        arrays += list(carried)
        specs += [BS(memory_space=pl.ANY)] * 4
        alias = {7 + k: k for k in range(4)}
    grid_spec = pltpu.PrefetchScalarGridSpec(
        num_scalar_prefetch=1, grid=(R // tr, Cp // tc), in_specs=specs,
        out_specs=[spec] * 4 + [BS((8, LANES), lambda i, j, place_ref: (0, 0))])
    return pl.pallas_call(
        body, out_shape=[SDS(w.shape, F32)] * 4 + [SDS((8, LANES), F32)], grid_spec=grid_spec,
        input_output_aliases=alias, name="adamw_" + name,
        compiler_params=_cparams(("arbitrary", "arbitrary")))(place, *arrays)


def _adamw_small(w, g, m, v):
    vm = BS(memory_space=pltpu.VMEM)

    def body(w_ref, g_ref, m_ref, v_ref, d_out, m_out, v_out):
        delta, mn, vn = _adam_math(w_ref[...], g_ref[...], m_ref[...], v_ref[...])
        d_out[...] = delta
        m_out[...] = mn
        v_out[...] = vn

    return pl.pallas_call(body, out_shape=[SDS(w.shape, F32)] * 3, in_specs=[vm] * 4, out_specs=[vm] * 3,
                          name="adamw_small")(w, g, m, v)


def _place():
    return lax.axis_index("x"), lax.axis_index("y"), lax.axis_index("c")


def _slot(px, py, pc):
    return 4 * px + 2 * py + pc


def _other_chips(x, y):
    return [(1 - x, y), (x, 1 - y), (1 - x, 1 - y)]


def _peer(x, y, c, rel):
    return (1 - x if rel & 4 else x, 1 - y if rel & 2 else y, 1 - c if rel & 1 else c)


def _plan_gather_ici(src, land):
    x, y, c = _place()
    own = land.at[_slot(x, y, c)]
    return [(own, own, (x, y, 1 - c))] + [(own, own, (*chip, c)) for chip in _other_chips(x, y)]


def _plan_gather_relay(src, land):
    x, y, c = _place()
    return [(land.at[_slot(*chip, c)], land.at[_slot(*chip, c)], (x, y, 1 - c)) for chip in _other_chips(x, y)]


def _plan_pair(src, land):
    x, y, c = _place()
    return [(src.at[2 * q + (1 - c)], land.at[q], (x, y, 1 - c)) for q in range(4)]


def _plan_scatter_chips(src, land):
    x, y, c = _place()
    return [(src.at[2 * cx + cy], land.at[2 * x + y], (cx, cy, c)) for cx, cy in _other_chips(x, y)]


EFFECT = pltpu.SideEffectType.DATAFLOW_SIDE_EFFECTING
HBM_SPEC = BS(memory_space=pltpu.HBM)
SEM_SPEC = BS(memory_space=pltpu.SEMAPHORE)
ORDER_SPEC = BS(memory_space=pl.ANY)


def _exchange_start(name, plan, ncopy, lands, srcs=None, after=()):
    bufs = ([] if srcs is None else list(srcs)) + list(lands)
    n, nb, ns = len(lands), len(bufs), len(bufs) - len(lands)
    n_after = len(after)

    def body(*refs):
        src_refs = refs[:ns] if ns else [None] * n
        land_refs = refs[ns:nb]
        send_sems, recv_sems = refs[nb + n_after], refs[nb + n_after + 1]
        token = refs[-1]
        for i in range(n):
            for k, (src, dst, to) in enumerate(plan(src_refs[i], land_refs[i])):
                pltpu.make_async_remote_copy(src_ref=src, dst_ref=dst, send_sem=send_sems.at[i * ncopy + k],
                                             recv_sem=recv_sems.at[i * ncopy + k], device_id=to,
                                             device_id_type=MESH).start()
        token[...] = jnp.zeros_like(token)

    out_shape = [pltpu.SemaphoreType.DMA((n * ncopy,)), pltpu.SemaphoreType.DMA((n * ncopy,))]
    out_shape += [pltpu.HBM(b.shape, b.dtype) for b in bufs] + [SDS((8, LANES), F32)]
    out = pl.pallas_call(
        body, name=name, out_shape=out_shape, in_specs=[HBM_SPEC] * nb + [ORDER_SPEC] * n_after,
        out_specs=[SEM_SPEC, SEM_SPEC] + [HBM_SPEC] * nb + [BS(memory_space=pltpu.VMEM)],
        input_output_aliases={i: 2 + i for i in range(nb)},
        compiler_params=pltpu.CompilerParams(has_side_effects=EFFECT))(
            *[pltpu.with_memory_space_constraint(b, pltpu.HBM) for b in bufs], *after)
    return out[0], out[1], out[2:2 + ns], out[2 + ns:2 + nb], out[-1]


def _exchange_wait(name, plan, send_sems, recv_sems, lands, srcs=None, after=()):
    bufs = ([] if srcs is None else list(srcs)) + list(lands)
    n, nb, ns = len(lands), len(bufs), len(bufs) - len(lands)

    def body(*refs):
        src_refs = refs[:ns] if ns else [None] * n
        land_refs = refs[ns:nb]
        send_ref, recv_ref = refs[nb], refs[nb + 1]
        for i in range(n):
            copies = plan(src_refs[i], land_refs[i])
            for k, (src, dst, to) in enumerate(copies):
                cp = pltpu.make_async_remote_copy(src_ref=src, dst_ref=dst, send_sem=send_ref.at[i * len(copies) + k],
                                                  recv_sem=recv_ref.at[i * len(copies) + k], device_id=to,
                                                  device_id_type=MESH)
                cp.wait_send()
                cp.wait_recv()

    out = pl.pallas_call(
        body, name=name, out_shape=[pltpu.HBM(b.shape, b.dtype) for b in bufs],
        in_specs=[HBM_SPEC] * nb + [SEM_SPEC, SEM_SPEC] + [ORDER_SPEC] * len(after), out_specs=[HBM_SPEC] * nb,
        input_output_aliases={i: i for i in range(nb)},
        compiler_params=pltpu.CompilerParams(has_side_effects=EFFECT))(*bufs, send_sems, recv_sems, *after)
    return out[:ns], out[ns:]


def _allreduce_small(v, after=()):
    R, Wd = v.shape
    n_after = len(after)

    def body(v_ref, *rest):
        o_ref, buf, send_sems, recv_sems = rest[n_after:]
        x, y, c = _place()
        me = _slot(x, y, c)
        buf[me] = v_ref[...]
        started = []
        for rel in range(1, NDEV):
            peer = _peer(x, y, c, rel)
            cp = pltpu.make_async_remote_copy(
                src_ref=v_ref, dst_ref=buf.at[me], send_sem=send_sems.at[rel - 1], recv_sem=recv_sems.at[rel - 1],
                device_id=peer, device_id_type=MESH)
            cp.start()
            started.append(cp)
        for cp in started:
            cp.wait()
        acc = buf[0]
        for j in range(1, NDEV):
            acc = acc + buf[j]
        o_ref[...] = acc

    vm = BS(memory_space=pltpu.VMEM)
    return pl.pallas_call(
        body, out_shape=SDS((R, Wd), F32), in_specs=[vm] + [ORDER_SPEC] * n_after, out_specs=vm,
        scratch_shapes=[pltpu.VMEM((NDEV, R, Wd), F32), pltpu.SemaphoreType.DMA((7,)), pltpu.SemaphoreType.DMA((7,))],
        name="allreduce_small")(v, *after)


def _as_matrix(name, g):
    if name in ROW_SHARDED:
        return g.reshape(1, NDEV * g.shape[1], g.shape[2])
    return g


def _pad8(a):
    return jnp.pad(a, ((0, -a.shape[0] % 8), (0, 0)))


def _pack_small(g_mix, g_mlp, g_ple, b_gate, g_final, b_f, extra_row=None):
    Lr, D = g_mix.shape
    rows = [g_mix, g_mlp, g_ple, b_gate.reshape(2 * Lr, D), g_final.reshape(1, D),
            jnp.pad(b_f.reshape(1, -1), ((0, 0), (0, D - b_f.size)))]
    rows.append(jnp.zeros((1, D), F32) if extra_row is None else extra_row)
    return jnp.concatenate([_pad8(r) for r in rows], axis=0)


def _unpack_small(pk, Lr, D, nf):
    o = 0
    out = []
    for rows, shape in ((Lr, (Lr, D)), (Lr, (Lr, D)), (Lr, (Lr, D)), (2 * Lr, (Lr, 2 * D)), (1, (D,))):
        out.append(pk[o:o + rows].reshape(shape))
        o += rows + (-rows % 8)
    out.append(pk[o, :Lr * nf].reshape(Lr, nf))
    return out, pk[o + 8, 0]


def kernel(x, p, g_mix, w_in, b_f, w_gate, b_gate, w_br_a, w_br_b, w_o, g_mlp, w_up, w_down, g_ple, w_ple, w_ple_gate, g_final, loss_target, m_g_mix, m_w_in, m_b_f, m_w_gate, m_b_gate, m_w_br_a, m_w_br_b, m_w_o, m_g_mlp, m_w_up, m_w_down, m_g_ple, m_w_ple, m_w_ple_gate, m_g_final, v_g_mix, v_w_in, v_b_f, v_w_gate, v_b_gate, v_w_br_a, v_w_br_b, v_w_o, v_g_mlp, v_w_up, v_w_down, v_g_ple, v_w_ple, v_w_ple_gate, v_g_final):
    Lr, D = g_mix.shape
    S = x.shape[1]
    nf = b_f.shape[1]
    big_w = dict(zip(BIG, (w_in, w_gate, w_br_a, w_br_b, w_o, w_up, w_down, w_ple, w_ple_gate)))
    big_m = dict(zip(BIG, (m_w_in, m_w_gate, m_w_br_a, m_w_br_b, m_w_o, m_w_up, m_w_down, m_w_ple, m_w_ple_gate)))
    big_v = dict(zip(BIG, (v_w_in, v_w_gate, v_w_br_a, v_w_br_b, v_w_o, v_w_up, v_w_down, v_w_ple, v_w_ple_gate)))

    x_pos, y_pos, c_pos = _place()
    me = _slot(x_pos, y_pos, c_pos).astype(jnp.int32).reshape(1)
    place = jnp.stack([c_pos, 2 * x_pos + y_pos]).astype(jnp.int32)
    wbf = {k: _cast_bf16(k, big_w[k], me) for k in BIG}
    tabs = _rope_tables(S)
    bf_pad = jnp.pad(b_f, ((0, 0), (0, LANES - nf)))

    smalls = [dict(g_mix=g_mix[l:l + 1], b_f=bf_pad[l:l + 1], b_gate=b_gate[l:l + 1], g_mlp=g_mlp[l:l + 1],
                   g_ple=g_ple[l:l + 1]) for l in range(Lr)]

    gather_groups = (("in", "gate", "bra", "brb", "o"), ("up", "down"), ("ple", "pg"))
    per_layer = len(gather_groups)
    order = [(l, gi) for l in range(Lr) for gi in range(len(gather_groups))]
    on_links = {}

    def links_start(k, after=()):
        l, gi = order[k]
        on_links[k] = _exchange_start(f"gather_ici_start_{l}_{gi}", _plan_gather_ici, 4,
                                      [wbf[n][l] for n in gather_groups[gi]], None, after)
        return on_links[k][4]

    def arrive(k, after):
        l, gi = order[k]
        send_sems, recv_sems, _, lands, _ = on_links.pop(k)
        _, lands = _exchange_wait(f"gather_ici_wait_{l}_{gi}", _plan_gather_ici, send_sems, recv_sems, lands, None,
                                  after)
        send_sems, recv_sems, _, lands, token = _exchange_start(
            f"gather_relay_start_{l}_{gi}", _plan_gather_relay, 3, lands)
        tokens = [token]
        if k + 2 < len(order):
            tokens.append(links_start(k + 2, (token,)))
        _, lands = _exchange_wait(f"gather_relay_wait_{l}_{gi}", _plan_gather_relay, send_sems, recv_sems, lands)
        return {n: _as_matrix(n, g) for n, g in zip(gather_groups[gi], lands)}, tuple(tokens)

    h = x.reshape(S, D)
    Ws, saves = [], []
    queued = links_start(1, (links_start(0),))
    for l in range(Lr):
        W, tokens = arrive(per_layer * l, (h, queued) if l == 0 else (h,))

        def hook(part, l=l, W=W):
            def at(h_now):
                more, hook_tokens = arrive(per_layer * l + part, (h_now,))
                W.update(more)
                return hook_tokens
            return at

        h, sv = _layer_fwd(h, p[l, 0].astype(BF16), smalls[l], W, tabs, tokens, hook(1), hook(2))
        Ws.append(W)
        saves.append(sv)

    dh, dg_final, loss_part = _final(h, g_final.reshape(1, D), loss_target.reshape(S, D))

    big_out = {k: None for k in BIG}
    dsmalls = [None] * Lr
    groups = dict(mlp=("ple", "pg", "down", "up"), merge=("o", "bra", "brb", "gate"), proj=("in",))
    in_flight = []
    links_steps = 3

    def pair_start(l, stage, dW):
        dws = [dW[k].reshape(wbf[k][l].shape) for k in groups[stage]]
        lands = [lax.empty((4,) + d.shape[1:], BF16) for d in dws]
        started = _exchange_start(f"pair_start_{l}_{stage}", _plan_pair, 4, lands, dws)
        in_flight.append(dict(tag=f"{l}_{stage}", layer=l, names=groups[stage], on_links=False, started=started,
                              age=0))
        return started[4]

    def to_links(ex, after):
        send_sems, recv_sems, dws, lands, _ = ex["started"]
        dws, sib = _exchange_wait(f"pair_wait_{ex['tag']}", _plan_pair, send_sems, recv_sems, lands, dws, after)
        sums = [_pair_sum(k, d, s, place) for k, d, s in zip(ex["names"], dws, sib)]
        ex["halves"] = (dws, sib)
        ex["started"] = _exchange_start(f"scatter_start_{ex['tag']}", _plan_scatter_chips, 3,
                                        [lax.empty(s.shape, BF16) for s in sums], sums)
        ex["on_links"], ex["age"] = True, 0
        return ex["started"][4]

    def finish(ex, after):
        send_sems, recv_sems, sums, lands, _ = ex["started"]
        _, landed = _exchange_wait(f"scatter_wait_{ex['tag']}", _plan_scatter_chips, send_sems, recv_sems, lands, sums,
                                   after)
        for k, dw, sib, recv in zip(ex["names"], *ex["halves"], landed):
            carried = None if big_out[k] is None else big_out[k][:4]
            big_out[k] = _adamw(k, ex["layer"], big_w[k], big_m[k], big_v[k], recv, dw, sib, place, carried)
        in_flight.remove(ex)
        return landed[0]

    def advance(after):
        tokens = []
        for ex in list(in_flight):
            if not ex["on_links"]:
                tokens.append(to_links(ex, after))
            elif ex["age"] + 1 >= links_steps:
                tokens.append(finish(ex, after))
            else:
                ex["age"] += 1
        return tokens

    after = ()
    for l in reversed(range(Lr)):
        def ready(stage, dW, l=l):
            tokens = advance((dW[groups[stage][-1]],))
            return tuple(tokens + [pair_start(l, stage, dW)])

        dh, dW, dsmalls[l] = _layer_bwd(dh, saves[l], smalls[l], Ws[l], tabs, after, ready)
        after = tuple(advance((dh,)) + [pair_start(l, "proj", dW)])
    last = after[-1]
    for ex in [ex for ex in in_flight if not ex["on_links"]]:
        last = to_links(ex, (last,))
    for ex in list(in_flight):
        updated = tuple(out[4] for out in big_out.values() if out is not None)
        last = finish(ex, updated + (last,))

    cat = lambda key: jnp.concatenate([_pad8(d[key]) for d in dsmalls], axis=0)[::8]
    dbf = jnp.concatenate([d["b_f"] for d in dsmalls], axis=0).reshape(Lr, 8, LANES)[:, :nf, 0]
    loss_row = jnp.concatenate([loss_part, g_final.reshape(1, D)[:, LANES:]], axis=1)
    g_pack = _allreduce_small(_pack_small(cat("g_mix"), cat("g_mlp"), cat("g_ple"), cat("b_gate"), dg_final, dbf,
                                          loss_row), (last,))
    w_pack = _pack_small(g_mix, g_mlp, g_ple, b_gate, g_final, b_f)
    m_pack = _pack_small(m_g_mix, m_g_mlp, m_g_ple, m_b_gate, m_g_final, m_b_f)
    v_pack = _pack_small(v_g_mix, v_g_mlp, v_g_ple, v_b_gate, v_g_final, v_b_f)
    d_pack, mn_pack, vn_pack = _adamw_small(w_pack, g_pack, m_pack, v_pack)

    small_names = ("g_mix", "g_mlp", "g_ple", "b_gate", "g_final", "b_f")
    sg, loss = _unpack_small(g_pack, Lr, D, nf)
    small_out = {}
    for kind, pk in (("grad", None), ("delta", d_pack), ("m", mn_pack), ("v", vn_pack)):
        vals = sg if pk is None else _unpack_small(pk, Lr, D, nf)[0]
        small_out[kind] = dict(zip(small_names, vals))

    order = ("g_mix", "in", "b_f", "gate", "b_gate", "bra", "brb", "o", "g_mlp", "up", "down", "g_ple", "ple", "pg",
             "g_final")
    outs = [loss, dh.reshape(x.shape)]
    for idx, kind in enumerate(("grad", "delta", "m", "v")):
        for name in order:
            outs.append(big_out[name][idx] if name in big_out else small_out[kind][name])
    return tuple(outs)
```
